```python
import math
import jax
import jax.numpy as jnp
from jax import lax
import numpy as np

D_MODEL = 1024
BATCH = 32
SEQ = 256
DEPTH = 2
DEC_BATCH = 2
DEC_SEQ = 2048
PAST_LEN = 256

GRID_W = 64
HEAD_DIM = 64
H_NA = D_MODEL // 128
H_DIFF = D_MODEL // 256
H_RET = D_MODEL // 128
H_RWKV = D_MODEL // 128
WIN_R_MAX = 8
WIN_W = 16
Q_BLOCK = 128
RET_CHUNK = 128
D_LORA_W = 64
D_LORA_A = 64
D_LORA_G = 128
FF_RAW = -(-8 * D_MODEL // 3)
D_FF = -(-FF_RAW // 256) * 256
N_ATTN_LAYERS = (DEPTH + 1) // 2
N_REC_LAYERS = DEPTH // 2
D_NA = H_NA * HEAD_DIM
D_DIFF = H_DIFF * 2 * HEAD_DIM
D_RET = H_RET * HEAD_DIM
D_RWKV = H_RWKV * HEAD_DIM
ATTN_IN_SIZES = (D_NA, D_NA, D_NA, D_DIFF, D_DIFF, D_DIFF)
REC_IN_SIZES = (D_RET, D_RET, D_RET, D_RET, D_RWKV, D_RWKV, D_RWKV, D_LORA_W, D_LORA_A, D_LORA_G)
D_IN_ATTN = 3 * D_NA + 3 * D_DIFF
D_MIX_ATTN = D_NA + D_DIFF
D_IN_REC = 4 * D_RET + 3 * D_RWKV + D_LORA_W + D_LORA_A + D_LORA_G
D_MIX_REC = D_RET + D_RWKV
ROPE_BASE = 10000.0
RMS_EPS = 1e-6
RWKV_GN_EPS = 64e-5
NEG_INF = -1e30

kernel_name = 'hybrid_diffusion_prefix_step'


def rms_norm(x, gain=None, eps=RMS_EPS):
    xf = x.astype(jnp.float32)
    y = xf * lax.rsqrt(jnp.mean(xf * xf, axis=-1, keepdims=True) + eps)
    if gain is not None:
        y = y * gain.astype(jnp.float32)
    return y.astype(x.dtype)


def ada_modulation(cond, w, b):
    m = (jax.nn.silu(cond) @ w + b)[..., None, :]
    return jnp.split(m, 6, axis=-1)


def modulate(h, shift, scale):
    return h * (1.0 + scale) + shift


def split_cols(x, sizes):
    out, start = [], 0
    for s in sizes:
        out.append(x[..., start:start + s])
        start += s
    return out


def axial_rope(x):
    n = x.shape[1]
    pos = jnp.arange(n)
    half = HEAD_DIM // 2
    quarter = half // 2
    inv_freq = ROPE_BASE ** (-jnp.arange(quarter, dtype=jnp.float32) / quarter)
    bshape = (n,) + (1,) * (x.ndim - 3) + (quarter,)

    def rot(xh, p):
        ang = (p.astype(jnp.float32)[:, None] * inv_freq[None, :]).reshape(bshape)
        cos, sin = jnp.cos(ang), jnp.sin(ang)
        x1, x2 = xh[..., :quarter], xh[..., quarter:]
        return jnp.concatenate([x1 * cos - x2 * sin, x1 * sin + x2 * cos], axis=-1)

    xf = x.astype(jnp.float32)
    out = jnp.concatenate([rot(xf[..., :half], pos // GRID_W), rot(xf[..., half:], pos % GRID_W)], axis=-1)
    return out.astype(x.dtype)


def sweep_query_blocks(fn, q):
    b, n = q.shape[0], q.shape[1]
    nb = n // Q_BLOCK
    qb = jnp.moveaxis(q.reshape((b, nb, Q_BLOCK) + q.shape[2:]), 1, 0)
    o = jnp.moveaxis(lax.map(fn, qb), 0, 1)
    return o.reshape((b, n) + o.shape[3:])


def softmax_attention(q, k, v):
    scale = HEAD_DIM ** -0.5

    def block(qb):
        s = jnp.einsum('bqhd,bkhd->bhqk', qb, k).astype(jnp.float32) * scale
        p = jax.nn.softmax(s, axis=-1).astype(v.dtype)
        return jnp.einsum('bhqk,bkhd->bqhd', p, v)

    return sweep_query_blocks(block, q)


def neighbourhood_attention(q, k, v, k_ctx, v_ctx, rpb):
    b, n, h, d = q.shape
    rows = n // GRID_W
    win_r = min(WIN_R_MAX, rows)
    r_idx = jnp.arange(rows)
    r_start = jnp.clip(r_idx - win_r // 2, 0, rows - win_r)
    key_rows = r_start[:, None] + jnp.arange(win_r)[None, :]
    col = jnp.arange(GRID_W)
    c_start = jnp.clip(col - WIN_W // 2, 0, GRID_W - WIN_W)
    col_in = (col[None, :] >= c_start[:, None]) & (col[None, :] < c_start[:, None] + WIN_W)
    dr_idx = key_rows - r_idx[:, None] + (WIN_R_MAX - 1)
    dc_idx = jnp.clip(col[None, :] - col[:, None] + (WIN_W - 1), 0, 2 * WIN_W - 2)
    bias = rpb[:, dr_idx[:, None, :, None], dc_idx[None, :, None, :]]
    qg = q.reshape(b, rows, GRID_W, h, d)
    kg = k.reshape(b, rows, GRID_W, h, d)[:, key_rows]
    vg = v.reshape(b, rows, GRID_W, h, d)[:, key_rows]
    scale = d ** -0.5
    s_win = jnp.einsum('brqhd,brikhd->bhrqik', qg, kg).astype(jnp.float32) * scale + bias.astype(jnp.float32)
    s_win = jnp.where(col_in[:, None, :], s_win, NEG_INF).reshape(b, h, rows, GRID_W, win_r * GRID_W)
    s_ctx = jnp.einsum('brqhd,blhd->bhrql', qg, k_ctx).astype(jnp.float32) * scale
    p = jax.nn.softmax(jnp.concatenate([s_win, s_ctx], axis=-1), axis=-1).astype(v.dtype)
    p_win = p[..., :win_r * GRID_W].reshape(b, h, rows, GRID_W, win_r, GRID_W)
    p_ctx = p[..., win_r * GRID_W:]
    o = jnp.einsum('bhrqik,brikhd->brqhd', p_win, vg) + jnp.einsum('bhrql,blhd->brqhd', p_ctx, v_ctx)
    return o.reshape(b, n, h, d)


def diff_attention(q, k, v, lam, lam_init):
    scale = HEAD_DIM ** -0.5

    def block(qb):
        s = jnp.einsum('bqhcd,bkhcd->bhcqk', qb, k).astype(jnp.float32) * scale
        p = jax.nn.softmax(s, axis=-1)
        p = (p[:, :, 0] - lam * p[:, :, 1]).astype(v.dtype)
        return jnp.einsum('bhqk,bkhe->bqhe', p, v)

    o = sweep_query_blocks(block, q)
    return rms_norm(o) * (1.0 - lam_init)


def attn_projections(h, w_in):
    b, n, _ = h.shape
    nq, nk, nv, dq, dk, dv = split_cols(h @ w_in, ATTN_IN_SIZES)
    na = lambda t: t.reshape(b, n, H_NA, HEAD_DIM)
    dqk = lambda t: t.reshape(b, n, H_DIFF, 2, HEAD_DIM)
    return na(nq), na(nk), na(nv), dqk(dq), dqk(dk), dv.reshape(b, n, H_DIFF, 2 * HEAD_DIM)


def retention_scan(q, k, v, log_g, r0):
    b, n, h, d = q.shape
    nc = n // RET_CHUNK
    idx = jnp.arange(RET_CHUNK, dtype=jnp.float32)
    diff = idx[:, None] - idx[None, :]
    decay_mask = jnp.where(diff[None] >= 0, jnp.exp(jnp.maximum(diff, 0.0)[None] * log_g[:, None, None]), 0.0)
    q_decay = jnp.exp((idx[:, None] + 1.0) * log_g[None, :])
    k_decay = jnp.exp((RET_CHUNK - 1.0 - idx)[:, None] * log_g[None, :])
    chunk_decay = jnp.exp(RET_CHUNK * log_g)

    def to_chunks(x):
        return jnp.moveaxis(x.reshape(b, nc, RET_CHUNK, h, d), 1, 0)

    def step(state, qkv):
        qc, kc, vc = qkv
        s = jnp.einsum('bihd,bjhd->bhij', qc, kc) * decay_mask[None]
        inner = jnp.einsum('bhij,bjhe->bihe', s, vc)
        cross = jnp.einsum('bihd,bhde->bihe', qc, state) * q_decay[None, :, :, None]
        new_state = state * chunk_decay[None, :, None, None] + jnp.einsum('bjhd,bjhe->bhde', kc * k_decay[None, :, :, None], vc)
        return new_state, inner + cross

    final, out = lax.scan(step, r0, (to_chunks(q), to_chunks(k), to_chunks(v)))
    return jnp.moveaxis(out, 0, 1).reshape(b, n, h, d), final


def retention_mixer(x_q, x_k, x_v, x_g, decay_logit, state0):
    b, n, _ = x_q.shape
    heads = lambda t: t.reshape(b, n, H_RET, HEAD_DIM).astype(jnp.float32)
    q, k, v = heads(x_q), heads(x_k) * (HEAD_DIM ** -0.5), heads(x_v)
    log_g = jax.nn.log_sigmoid(decay_logit.astype(jnp.float32))
    s0 = state0.astype(jnp.float32)
    o_f, s_f = retention_scan(q, k, v, log_g[0], s0[:, 0])
    o_b, s_b = retention_scan(q[:, ::-1], k[:, ::-1], v[:, ::-1], log_g[1], s0[:, 1])
    o = rms_norm(o_f + o_b[:, ::-1]).reshape(b, n, D_RET)
    out = jax.nn.silu(x_g.astype(jnp.float32)) * o
    return out.astype(x_q.dtype), jnp.stack([s_f, s_b], axis=1).astype(x_q.dtype)


def rwkv7_scan(r, w, k, v, a_vec, b_vec, s0):
    def step(S, inp):
        rt, wt, kt, vt, at, bt = inp
        sa = jnp.einsum('bhvk,bhk->bhv', S, at)
        S = S * wt[:, :, None, :] + sa[..., None] * bt[:, :, None, :] + vt[..., None] * kt[:, :, None, :]
        return S, jnp.einsum('bhvk,bhk->bhv', S, rt)

    xs = tuple(jnp.moveaxis(t, 1, 0) for t in (r, w, k, v, a_vec, b_vec))
    S, ys = lax.scan(step, s0, xs)
    return jnp.moveaxis(ys, 0, 1), S


def rwkv7_mixer(x_r, x_k, x_v, x_wd, x_ad, x_gd, w0, w_up, a0, a_up, g_up, k_k, k_a, r_k, ln_g, ln_b, state0):
    b, n, _ = x_r.shape
    heads = lambda t: t.reshape(b, n, H_RWKV, HEAD_DIM).astype(jnp.float32)
    r, k, v = heads(x_r), heads(x_k), heads(x_v)
    kk = heads(x_k * k_k)
    kk = kk * lax.rsqrt(jnp.sum(kk * kk, axis=-1, keepdims=True) + 1e-12)
    k_a_h = k_a.reshape(H_RWKV, HEAD_DIM).astype(jnp.float32)
    s0 = state0.astype(jnp.float32)
    ys, states = [], []
    for d in range(2):
        w_log = -math.exp(-0.5) * jax.nn.sigmoid(w0[d] + jnp.tanh(x_wd) @ w_up[d])
        decay = heads(jnp.exp(w_log))
        a = heads(jax.nn.sigmoid(a0[d] + x_ad @ a_up[d]))
        k_eff = k * (1.0 + (a - 1.0) * k_a_h)
        seq = (r, decay, k_eff, v, -kk, kk * a)
        if d == 1:
            seq = tuple(t[:, ::-1] for t in seq)
        y, s = rwkv7_scan(*seq, s0[:, d])
        if d == 1:
            y = y[:, ::-1]
        ys.append(y)
        states.append(s)
    y = ys[0] + ys[1]
    mu = jnp.mean(y, axis=-1, keepdims=True)
    var = jnp.mean(jnp.square(y - mu), axis=-1, keepdims=True)
    y = ((y - mu) * lax.rsqrt(var + RWKV_GN_EPS)).reshape(b, n, D_RWKV) * ln_g.astype(jnp.float32) + ln_b.astype(jnp.float32)
    bonus = (jnp.sum(r * r_k.astype(jnp.float32) * k, axis=-1, keepdims=True) * v).reshape(b, n, D_RWKV)
    g = (jax.nn.sigmoid(x_gd) @ g_up).astype(jnp.float32)
    out = (y + bonus) * g
    return out.astype(x_r.dtype), jnp.stack(states, axis=1).astype(x_r.dtype)


def rec_mixer(h, w_in, w_out, decay_logit, w0, w_up, a0, a_up, g_up, k_k, k_a, r_k, ln_g, ln_b, ret_state0, rwkv_state0):
    rq, rk, rv, rg, wr, wk, wv, wd, ad, gd = split_cols(h @ w_in, REC_IN_SIZES)
    o_ret, s_ret = retention_mixer(rq, rk, rv, rg, decay_logit, ret_state0)
    o_rw, s_rw = rwkv7_mixer(wr, wk, wv, wd, ad, gd, w0, w_up, a0, a_up, g_up, k_k, k_a, r_k, ln_g, ln_b, rwkv_state0)
    return jnp.concatenate([o_ret, o_rw], axis=-1) @ w_out, s_ret, s_rw


def swiglu(h, w_in, w_out):
    g, u = jnp.split(h @ w_in, 2, axis=-1)
    return (jax.nn.silu(g) * u) @ w_out


def setup_inputs(seed: int = 0) -> dict:
    key = jax.random.key(seed)
    ks = iter(jax.random.split(key, 48))
    nrm = lambda shape, scale: jax.random.normal(next(ks), shape, jnp.float32) * scale
    base_logit = jnp.log(2.0 ** (5.0 + jnp.arange(H_RET, dtype=jnp.float32)) - 1.0)
    return {
        'x_prompt': nrm((BATCH, SEQ, D_MODEL), 1.0),
        'x_sample': nrm((DEC_BATCH, DEC_SEQ, D_MODEL), 1.0),
        'cache_na_k': nrm((DEC_BATCH, N_ATTN_LAYERS, PAST_LEN, H_NA, HEAD_DIM), 1.0),
        'cache_na_v': nrm((DEC_BATCH, N_ATTN_LAYERS, PAST_LEN, H_NA, HEAD_DIM), 1.0),
        'cache_diff_k': nrm((DEC_BATCH, N_ATTN_LAYERS, PAST_LEN, H_DIFF, 2, HEAD_DIM), 1.0),
        'cache_diff_v': nrm((DEC_BATCH, N_ATTN_LAYERS, PAST_LEN, H_DIFF, 2 * HEAD_DIM), 1.0),
        'state_ret': nrm((DEC_BATCH, N_REC_LAYERS, 2, H_RET, HEAD_DIM, HEAD_DIM), 0.5),
        'state_rwkv': nrm((DEC_BATCH, N_REC_LAYERS, 2, H_RWKV, HEAD_DIM, HEAD_DIM), 0.5),
        'c': nrm((DEC_BATCH, D_MODEL), 1.0),
        'c_ctx': nrm((D_MODEL,), 1.0),
        'norm_mix_g': 1.0 + nrm((DEPTH, D_MODEL), 0.05),
        'norm_ffn_g': 1.0 + nrm((DEPTH, D_MODEL), 0.05),
        'norm_final_g': 1.0 + nrm((D_MODEL,), 0.05),
        'w_ada': nrm((DEPTH, D_MODEL, 6 * D_MODEL), 0.5 * D_MODEL ** -0.5),
        'b_ada': nrm((DEPTH, 6 * D_MODEL), 0.02),
        'w_in_attn': nrm((N_ATTN_LAYERS, D_MODEL, D_IN_ATTN), D_MODEL ** -0.5),
        'w_out_attn': nrm((N_ATTN_LAYERS, D_MIX_ATTN, D_MODEL), D_MIX_ATTN ** -0.5),
        'na_rpb': nrm((N_ATTN_LAYERS, H_NA, 2 * WIN_R_MAX - 1, 2 * WIN_W - 1), 0.1),
        'diff_lq1': nrm((N_ATTN_LAYERS, HEAD_DIM), 0.1),
        'diff_lk1': nrm((N_ATTN_LAYERS, HEAD_DIM), 0.1),
        'diff_lq2': nrm((N_ATTN_LAYERS, HEAD_DIM), 0.1),
        'diff_lk2': nrm((N_ATTN_LAYERS, HEAD_DIM), 0.1),
        'w_in_rec': nrm((N_REC_LAYERS, D_MODEL, D_IN_REC), D_MODEL ** -0.5),
        'w_out_rec': nrm((N_REC_LAYERS, D_MIX_REC, D_MODEL), D_MIX_REC ** -0.5),
        'ret_decay_logit': jnp.broadcast_to(base_logit, (N_REC_LAYERS, 2, H_RET)) + nrm((N_REC_LAYERS, 2, H_RET), 0.05),
        'rw_w0': nrm((N_REC_LAYERS, 2, D_RWKV), 0.5),
        'rw_w_up': nrm((N_REC_LAYERS, 2, D_LORA_W, D_RWKV), 0.5 * D_LORA_W ** -0.5),
        'rw_a0': nrm((N_REC_LAYERS, 2, D_RWKV), 0.5),
        'rw_a_up': nrm((N_REC_LAYERS, 2, D_LORA_A, D_RWKV), 0.5 * D_LORA_A ** -0.5),
        'rw_g_up': nrm((N_REC_LAYERS, D_LORA_G, D_RWKV), D_LORA_G ** -0.5),
        'rw_k_k': 0.85 + nrm((N_REC_LAYERS, D_RWKV), 0.05),
        'rw_k_a': 1.0 + nrm((N_REC_LAYERS, D_RWKV), 0.05),
        'rw_r_k': nrm((N_REC_LAYERS, H_RWKV, HEAD_DIM), 0.1),
        'rw_ln_g': 1.0 + nrm((N_REC_LAYERS, D_RWKV), 0.05),
        'rw_ln_b': nrm((N_REC_LAYERS, D_RWKV), 0.02),
        'w_ffn_in': nrm((DEPTH, D_MODEL, 2 * D_FF), D_MODEL ** -0.5),
        'w_ffn_out': nrm((DEPTH, D_FF, D_MODEL), D_FF ** -0.5),
    }


def reference(x_prompt, x_sample, cache_na_k, cache_na_v, cache_diff_k, cache_diff_v, state_ret, state_rwkv,
              c, c_ctx, norm_mix_g, norm_ffn_g, norm_final_g, w_ada, b_ada,
              w_in_attn, w_out_attn, na_rpb, diff_lq1, diff_lk1, diff_lq2, diff_lk2,
              w_in_rec, w_out_rec, ret_decay_logit, rw_w0, rw_w_up, rw_a0, rw_a_up, rw_g_up,
              rw_k_k, rw_k_a, rw_r_k, rw_ln_g, rw_ln_b, w_ffn_in, w_ffn_out):
    ctx, lat = x_prompt, x_sample
    bp = ctx.shape[0]
    na_k_list, na_v_list, df_k_list, df_v_list, ret_list, rwkv_list = [], [], [], [], [], []
    for layer in range(DEPTH):
        m_ctx = ada_modulation(c_ctx, w_ada[layer], b_ada[layer])
        m_lat = ada_modulation(c, w_ada[layer], b_ada[layer])
        h_ctx = modulate(rms_norm(ctx, norm_mix_g[layer]), m_ctx[0], m_ctx[1])
        h_lat = modulate(rms_norm(lat, norm_mix_g[layer]), m_lat[0], m_lat[1])
        if layer % 2 == 0:
            i = layer // 2
            lam_init = 0.8 - 0.6 * math.exp(-0.3 * layer)
            lam = (jnp.exp(jnp.sum(diff_lq1[i].astype(jnp.float32) * diff_lk1[i].astype(jnp.float32)))
                   - jnp.exp(jnp.sum(diff_lq2[i].astype(jnp.float32) * diff_lk2[i].astype(jnp.float32))) + lam_init)
            nq, nk, nv, dq, dk, dv = attn_projections(h_ctx, w_in_attn[i])
            o_na = softmax_attention(nq, nk, nv)
            o_df = diff_attention(dq, dk, dv, lam, lam_init)
            mix_ctx = jnp.concatenate([o_na.reshape(o_na.shape[:2] + (D_NA,)), o_df.reshape(o_df.shape[:2] + (D_DIFF,))], axis=-1) @ w_out_attn[i]
            na_k_list.append(nk)
            na_v_list.append(nv)
            df_k_list.append(dk)
            df_v_list.append(dv)
            nq, nk, nv, dq, dk, dv = attn_projections(h_lat, w_in_attn[i])
            dq, dk = axial_rope(dq), axial_rope(dk)
            o_na = neighbourhood_attention(nq, nk, nv, cache_na_k[:, i], cache_na_v[:, i], na_rpb[i])
            k_all = jnp.concatenate([dk, cache_diff_k[:, i]], axis=1)
            v_all = jnp.concatenate([dv, cache_diff_v[:, i]], axis=1)
            o_df = diff_attention(dq, k_all, v_all, lam, lam_init)
            mix_lat = jnp.concatenate([o_na.reshape(o_na.shape[:2] + (D_NA,)), o_df.reshape(o_df.shape[:2] + (D_DIFF,))], axis=-1) @ w_out_attn[i]
        else:
            j = layer // 2
            params = (w_in_rec[j], w_out_rec[j], ret_decay_logit[j], rw_w0[j], rw_w_up[j], rw_a0[j], rw_a_up[j],
                      rw_g_up[j], rw_k_k[j], rw_k_a[j], rw_r_k[j], rw_ln_g[j], rw_ln_b[j])
            zero_ret = jnp.zeros((bp, 2, H_RET, HEAD_DIM, HEAD_DIM), ctx.dtype)
            zero_rwkv = jnp.zeros((bp, 2, H_RWKV, HEAD_DIM, HEAD_DIM), ctx.dtype)
            mix_ctx, s_ret, s_rw = rec_mixer(h_ctx, *params, zero_ret, zero_rwkv)
            ret_list.append(s_ret)
            rwkv_list.append(s_rw)
            mix_lat, _, _ = rec_mixer(h_lat, *params, state_ret[:, j], state_rwkv[:, j])
        ctx = ctx + m_ctx[2] * mix_ctx
        lat = lat + m_lat[2] * mix_lat
        ctx = ctx + m_ctx[5] * swiglu(modulate(rms_norm(ctx, norm_ffn_g[layer]), m_ctx[3], m_ctx[4]), w_ffn_in[layer], w_ffn_out[layer])
        lat = lat + m_lat[5] * swiglu(modulate(rms_norm(lat, norm_ffn_g[layer]), m_lat[3], m_lat[4]), w_ffn_in[layer], w_ffn_out[layer])
    y_prompt = rms_norm(ctx, norm_final_g)
    y_sample = rms_norm(lat, norm_final_g)
    new_cache_na_k = jnp.stack(na_k_list, axis=1)
    new_cache_na_v = jnp.stack(na_v_list, axis=1)
    new_cache_diff_k = jnp.stack(df_k_list, axis=1)
    new_cache_diff_v = jnp.stack(df_v_list, axis=1)
    new_state_ret = jnp.stack(ret_list, axis=1)
    new_state_rwkv = jnp.stack(rwkv_list, axis=1)
    return (y_prompt, y_sample, new_cache_na_k, new_cache_na_v, new_cache_diff_k, new_cache_diff_v, new_state_ret, new_state_rwkv)
```

```python
import functools
import math

import jax
import jax.numpy as jnp
from jax import lax
from jax.experimental import pallas as pl
from jax.experimental.pallas import tpu as pltpu

F32 = jnp.float32
BF16 = jnp.bfloat16

D_MODEL = 1024
HEAD_DIM = 64
N_HEADS = 8
D_GROUP = 512
GRID_W = 64
GRID_ROWS = 32
WIN_R = 8
WIN_W = 16
RPB_R = 15
RPB_C = 31
PAST_LEN = 256
D_FF = 2816
D_LORA = 64
D_LORA_G = 128
ROPE_BASE = 10000.0
RMS_EPS = 1e-6
RWKV_GN_EPS = 64e-5
NEG_INF = -1e30
ATTN_SCALE = HEAD_DIM ** -0.5
RET_CHUNK = 128
RWKV_CHUNK = 64
RWKV_DOUBLINGS = 5
VMEM_LIMIT = 56 * 1024 * 1024


def _cparams(*sem):
    return pltpu.CompilerParams(dimension_semantics=sem, vmem_limit_bytes=VMEM_LIMIT)


def _sigmoid(x):
    return 1.0 / (1.0 + jnp.exp(-x))


def _silu(x):
    return x * _sigmoid(x)


def _rms(x, eps=RMS_EPS):
    return x * lax.rsqrt(jnp.mean(x * x, axis=-1, keepdims=True) + eps)


def _bdot(a, b):
    return jnp.dot(a.astype(BF16), b.astype(BF16), preferred_element_type=F32)


def _bdot_nt(a, b):
    return lax.dot_general(a.astype(BF16), b.astype(BF16), (((1,), (1,)), ((), ())),
                           preferred_element_type=F32)


def _bdot_tn(a, b):
    return lax.dot_general(a.astype(BF16), b.astype(BF16), (((0,), (0,)), ((), ())),
                           preferred_element_type=F32)


_DN = {"nn": (((1,), (0,)), ((), ())), "nt": (((1,), (1,)), ((), ())), "tn": (((0,), (0,)), ((), ()))}


def _split2(x):
    hi = x.astype(BF16)
    lo = (x - hi.astype(F32)).astype(BF16)
    return hi, lo


def _mm3(a, b, mode="nn"):
    dn = _DN[mode]
    ah, al = _split2(a)
    bh, bl = _split2(b)
    d = functools.partial(lax.dot_general, dimension_numbers=dn, preferred_element_type=F32)
    return d(ah, bh) + (d(ah, bl) + d(al, bh))


def _softmax_rows(s):
    m = jnp.max(s, axis=-1, keepdims=True)
    e = jnp.exp(s - m)
    return e / jnp.sum(e, axis=-1, keepdims=True)


def _diff_lambda(lp, lam_init):
    s1 = jnp.sum(lp[0:1, :] * lp[1:2, :], axis=-1, keepdims=True)
    s2 = jnp.sum(lp[2:3, :] * lp[3:4, :], axis=-1, keepdims=True)
    return jnp.exp(s1) - jnp.exp(s2) + lam_init


def _ada_kernel(c_ref, w_ref, b_ref, o_ref):
    s = _silu(c_ref[...])
    o_ref[0] = _bdot(s, w_ref[0]) + b_ref[0]


def _ada_modulation(cond8, w_ada, b_ada):
    depth, d, n = w_ada.shape
    tn = 1536
    return pl.pallas_call(
        _ada_kernel,
        grid=(depth, n // tn),
        in_specs=[pl.BlockSpec((8, d), lambda l, j: (0, 0)),
                  pl.BlockSpec((1, d, tn), lambda l, j: (l, 0, j)),
                  pl.BlockSpec((1, 1, tn), lambda l, j: (l, 0, j))],
        out_specs=pl.BlockSpec((1, 8, tn), lambda l, j: (l, 0, j)),
        out_shape=jax.ShapeDtypeStruct((depth, 8, n), F32),
        compiler_params=_cparams("arbitrary", "arbitrary"),
        name="ada_modulation",
    )(cond8, w_ada, b_ada.reshape(depth, 1, n))


def _nmm_kernel(x_ref, g_ref, sh_ref, sc_ref, w_ref, *o_refs, widths):
    h = _rms(x_ref[...]) * g_ref[...]
    h = h * (1.0 + sc_ref[0]) + sh_ref[0]
    hb = h.astype(BF16)
    off = 0
    for o_ref, w in zip(o_refs, widths):
        o_ref[...] = jnp.dot(hb, w_ref[:, off:off + w], preferred_element_type=F32)
        off += w


def _norm_mod_matmul(x, gain, shift, scale, w_bf16, widths, rows_per_mod, tm):
    n, d = x.shape
    tiles_per_mod = rows_per_mod // tm
    mod_spec = pl.BlockSpec((1, 1, d), lambda i: (i // tiles_per_mod, 0, 0))
    return pl.pallas_call(
        functools.partial(_nmm_kernel, widths=widths),
        grid=(n // tm,),
        in_specs=[pl.BlockSpec((tm, d), lambda i: (i, 0)),
                  pl.BlockSpec((1, d), lambda i: (0, 0)),
                  mod_spec, mod_spec,
                  pl.BlockSpec(w_bf16.shape, lambda i: (0, 0))],
        out_specs=[pl.BlockSpec((tm, w), lambda i: (i, 0)) for w in widths],
        out_shape=[jax.ShapeDtypeStruct((n, w), F32) for w in widths],
        compiler_params=_cparams("arbitrary"),
        name="norm_mod_matmul",
    )(x, gain.reshape(1, d), shift, scale, w_bf16)


def _ctx_attn_kernel(lp_ref, qn_ref, kn_ref, vn_ref, qd_ref, kd_ref, vd_ref, on_ref, od_ref, *, lam_init):
    for h in range(N_HEADS):
        sl = slice(h * HEAD_DIM, (h + 1) * HEAD_DIM)
        s = _bdot_nt(qn_ref[:, sl], kn_ref[:, sl]) * ATTN_SCALE
        on_ref[:, sl] = _bdot(_softmax_rows(s), vn_ref[:, sl])
    lam = _diff_lambda(lp_ref[...], lam_init)
    for h in range(N_HEADS // 2):
        s1 = slice(2 * h * HEAD_DIM, (2 * h + 1) * HEAD_DIM)
        s2 = slice((2 * h + 1) * HEAD_DIM, (2 * h + 2) * HEAD_DIM)
        sv = slice(2 * h * HEAD_DIM, (2 * h + 2) * HEAD_DIM)
        p1 = _softmax_rows(_bdot_nt(qd_ref[:, s1], kd_ref[:, s1]) * ATTN_SCALE)
        p2 = _softmax_rows(_bdot_nt(qd_ref[:, s2], kd_ref[:, s2]) * ATTN_SCALE)
        o = _bdot(p1 - lam * p2, vd_ref[:, sv])
        od_ref[:, sv] = _rms(o) * (1.0 - lam_init)


def _ctx_attention(lam_params, qn, kn, vn, qd, kd, vd, seq, lam_init):
    n = qn.shape[0]
    blk = pl.BlockSpec((seq, D_GROUP), lambda b: (b, 0))
    return pl.pallas_call(
        functools.partial(_ctx_attn_kernel, lam_init=lam_init),
        grid=(n // seq,),
        in_specs=[pl.BlockSpec(lam_params.shape, lambda b: (0, 0))] + [blk] * 6,
        out_specs=[blk, blk],
        out_shape=[jax.ShapeDtypeStruct((n, D_GROUP), F32)] * 2,
        compiler_params=_cparams("arbitrary"),
        name="ctx_attention",
    )(lam_params, qn, kn, vn, qd, kd, vd)


def _rope_tables(n):
    quarter = HEAD_DIM // 4
    pos = jnp.arange(n)
    inv_freq = ROPE_BASE ** (-jnp.arange(quarter, dtype=F32) / quarter)
    lane = jnp.arange(HEAD_DIM)
    p = jnp.where(lane[None, :] < HEAD_DIM // 2, (pos // GRID_W)[:, None], (pos % GRID_W)[:, None]).astype(F32)
    ang = p * inv_freq[lane % quarter][None, :]
    sign = jnp.where((lane % (2 * quarter)) < quarter, -1.0, 1.0).astype(F32)
    cos = jnp.tile(jnp.cos(ang), (1, N_HEADS))
    sin = jnp.tile(jnp.sin(ang) * sign[None, :], (1, N_HEADS))
    return cos, sin


def _rope(x, cos, sin_signed):
    quarter = HEAD_DIM // 4
    width = x.shape[-1]
    lane = lax.broadcasted_iota(jnp.int32, x.shape, 1)
    first = (lane % (2 * quarter)) < quarter
    partner = jnp.where(first, pltpu.roll(x, width - quarter, 1), pltpu.roll(x, quarter, 1))
    return x * cos + partner * sin_signed


def _lat_diff_kernel(lp_ref, q_ref, k_ref, v_ref, ck_ref, cv_ref, cosq_ref, sinq_ref, cosk_ref, sink_ref,
                     o_ref, kall, vall, *, lam_init, n_lat):
    @pl.when(pl.program_id(1) == 0)
    def _():
        kall[0:n_lat, :] = _rope(k_ref[...], cosk_ref[...], sink_ref[...]).astype(BF16)
        kall[n_lat:, :] = ck_ref[...].astype(BF16)
        vall[0:n_lat, :] = v_ref[...].astype(BF16)
        vall[n_lat:, :] = cv_ref[...].astype(BF16)

    lam = _diff_lambda(lp_ref[...], lam_init)
    q = _rope(q_ref[...], cosq_ref[...], sinq_ref[...]).astype(BF16)
    for h in range(N_HEADS // 2):
        s1 = slice(2 * h * HEAD_DIM, (2 * h + 1) * HEAD_DIM)
        s2 = slice((2 * h + 1) * HEAD_DIM, (2 * h + 2) * HEAD_DIM)
        sv = slice(2 * h * HEAD_DIM, (2 * h + 2) * HEAD_DIM)
        p1 = _softmax_rows(_bdot_nt(q[:, s1], kall[:, s1]) * ATTN_SCALE)
        p2 = _softmax_rows(_bdot_nt(q[:, s2], kall[:, s2]) * ATTN_SCALE)
        o = _bdot(p1 - lam * p2, vall[:, sv])
        o_ref[:, sv] = _rms(o) * (1.0 - lam_init)


def _lat_diff_attention(lam_params, qd, kd, vd, cache_k, cache_v, n_lat, lam_init, tq=256):
    n = qd.shape[0]
    nb = n // n_lat
    nq = n_lat // tq
    past = cache_k.shape[1]
    cos, sin = _rope_tables(n_lat)
    qblk = pl.BlockSpec((tq, D_GROUP), lambda b, i: (b * nq + i, 0))
    kvblk = pl.BlockSpec((n_lat, D_GROUP), lambda b, i: (b, 0))
    cblk = pl.BlockSpec((None, past, D_GROUP), lambda b, i: (b, 0, 0))
    return pl.pallas_call(
        functools.partial(_lat_diff_kernel, lam_init=lam_init, n_lat=n_lat),
        grid=(nb, nq),
        in_specs=[pl.BlockSpec(lam_params.shape, lambda b, i: (0, 0)),
                  qblk, kvblk, kvblk, cblk, cblk,
                  pl.BlockSpec((tq, D_GROUP), lambda b, i: (i, 0)),
                  pl.BlockSpec((tq, D_GROUP), lambda b, i: (i, 0)),
                  pl.BlockSpec((n_lat, D_GROUP), lambda b, i: (0, 0)),
                  pl.BlockSpec((n_lat, D_GROUP), lambda b, i: (0, 0))],
        out_specs=qblk,
        out_shape=jax.ShapeDtypeStruct((n, D_GROUP), F32),
        scratch_shapes=[pltpu.VMEM((n_lat + past, D_GROUP), BF16),
                        pltpu.VMEM((n_lat + past, D_GROUP), BF16)],
        compiler_params=_cparams("arbitrary", "arbitrary"),
        name="lat_diff_attention",
    )(lam_params, qd, kd, vd, cache_k, cache_v, cos, sin, cos, sin)


def _na_bias_kernel(rpb_ref, tt_ref):
    h = pl.program_id(0)
    lane = lax.broadcasted_iota(jnp.int32, (GRID_W, 2 * GRID_W), 1)
    qc = lax.broadcasted_iota(jnp.int32, (GRID_W, 2 * GRID_W), 0)
    dc = jnp.clip(lane % GRID_W - qc + (WIN_W - 1), 0, RPB_C - 1)
    first = lane < GRID_W
    for dr in range(RPB_R - 1):
        acc = jnp.zeros((GRID_W, 2 * GRID_W), F32)
        for d in range(RPB_C):
            val = jnp.where(first, rpb_ref[h * RPB_R + dr, d], rpb_ref[h * RPB_R + dr + 1, d])
            acc = jnp.where(dc == d, val, acc)
        tt_ref[0, dr] = acc


def _na_bias_table(rpb):
    return pl.pallas_call(
        _na_bias_kernel,
        grid=(N_HEADS,),
        in_specs=[pl.BlockSpec(memory_space=pltpu.SMEM)],
        out_specs=pl.BlockSpec((1, RPB_R - 1, GRID_W, 2 * GRID_W), lambda h: (h, 0, 0, 0)),
        out_shape=jax.ShapeDtypeStruct((N_HEADS, RPB_R - 1, GRID_W, 2 * GRID_W), F32),
        compiler_params=_cparams("arbitrary"),
        name="na_bias_table",
    )(rpb.reshape(N_HEADS * RPB_R, RPB_C))


def _lat_na_kernel(q_ref, k_ref, v_ref, ck_ref, cv_ref, tt_ref, o_ref):
    r = pl.program_id(1)
    r_start = jnp.clip(r - WIN_R // 2, 0, GRID_ROWS - WIN_R)
    dr0 = r_start - r + (WIN_R - 1)
    row0 = pl.multiple_of(r_start * GRID_W, GRID_W)
    nkeys = WIN_R * GRID_W
    kc = lax.broadcasted_iota(jnp.int32, (GRID_W, nkeys), 1) % GRID_W
    qc = lax.broadcasted_iota(jnp.int32, (GRID_W, nkeys), 0)
    c_start = jnp.clip(qc - WIN_W // 2, 0, GRID_W - WIN_W)
    col_in = (kc >= c_start) & (kc < c_start + WIN_W)
    for h in range(N_HEADS):
        sl = slice(h * HEAD_DIM, (h + 1) * HEAD_DIM)
        q = q_ref[:, sl].astype(BF16)
        kw = k_ref[pl.ds(row0, nkeys), sl]
        vw = v_ref[pl.ds(row0, nkeys), sl]
        bias = jnp.concatenate([tt_ref[h, dr0 + 2 * j] for j in range(WIN_R // 2)], axis=1)
        s = _bdot_nt(q, kw) * ATTN_SCALE + bias
        s = jnp.where(col_in, s, NEG_INF)
        sc = _bdot_nt(q, ck_ref[:, sl]) * ATTN_SCALE
        m = jnp.maximum(jnp.max(s, axis=-1, keepdims=True), jnp.max(sc, axis=-1, keepdims=True))
        e1 = jnp.exp(s - m)
        e2 = jnp.exp(sc - m)
        l = jnp.sum(e1, axis=-1, keepdims=True) + jnp.sum(e2, axis=-1, keepdims=True)
        o_ref[:, sl] = _bdot(e1 / l, vw) + _bdot(e2 / l, cv_ref[:, sl])


def _lat_na_attention(qn, kn, vn, cache_k, cache_v, tt, n_lat):
    n = qn.shape[0]
    nb = n // n_lat
    past = cache_k.shape[1]
    qblk = pl.BlockSpec((GRID_W, D_GROUP), lambda b, r: (b * GRID_ROWS + r, 0))
    kvblk = pl.BlockSpec((n_lat, D_GROUP), lambda b, r: (b, 0))
    cblk = pl.BlockSpec((None, past, D_GROUP), lambda b, r: (b, 0, 0))
    return pl.pallas_call(
        _lat_na_kernel,
        grid=(nb, GRID_ROWS),
        in_specs=[qblk, kvblk, kvblk, cblk, cblk,
                  pl.BlockSpec(tt.shape, lambda b, r: (0, 0, 0, 0))],
        out_specs=qblk,
        out_shape=jax.ShapeDtypeStruct((n, D_GROUP), F32),
        compiler_params=_cparams("arbitrary", "arbitrary"),
        name="lat_na_attention",
    )(qn, kn, vn, cache_k, cache_v, tt)


def _tail_kernel(x_ref, ma_ref, mb_ref, wo_ref, gm_ref, g_ref, sh_ref, sc_ref, gf_ref, wi_ref, wf_ref, fg_ref,
                 o_ref, *, final, ff_chunk):
    mix = _bdot(ma_ref[...], wo_ref[0:D_GROUP, :]) + _bdot(mb_ref[...], wo_ref[D_GROUP:, :])
    x1 = x_ref[...] + gm_ref[0] * mix
    h = _rms(x1) * g_ref[...]
    hb = (h * (1.0 + sc_ref[0]) + sh_ref[0]).astype(BF16)
    acc = jnp.zeros_like(x1)
    for c0 in range(0, D_FF, ff_chunk):
        gate = jnp.dot(hb, wi_ref[:, c0:c0 + ff_chunk], preferred_element_type=F32)
        up = jnp.dot(hb, wi_ref[:, D_FF + c0:D_FF + c0 + ff_chunk], preferred_element_type=F32)
        acc = acc + jnp.dot((_silu(gate) * up).astype(BF16), wf_ref[c0:c0 + ff_chunk, :],
                            preferred_element_type=F32)
    x2 = x1 + gf_ref[0] * acc
    if final:
        x2 = _rms(x2) * fg_ref[...]
    o_ref[...] = x2


def _layer_tail(x, mix_a, mix_b, w_out, gate_mix, gain, shift, scale, gate_ffn, w_in, w_ffn_out, final_gain,
                rows_per_mod, final, tm=256, ff_chunk=1408):
    n, d = x.shape
    tiles_per_mod = rows_per_mod // tm
    mod_spec = pl.BlockSpec((1, 1, d), lambda i: (i // tiles_per_mod, 0, 0))
    vec_spec = pl.BlockSpec((1, d), lambda i: (0, 0))
    resident = lambda a: pl.BlockSpec(a.shape, lambda i: (0, 0), pipeline_mode=pl.Buffered(1))
    return pl.pallas_call(
        functools.partial(_tail_kernel, final=final, ff_chunk=ff_chunk),
        grid=(n // tm,),
        in_specs=[pl.BlockSpec((tm, d), lambda i: (i, 0)),
                  pl.BlockSpec((tm, D_GROUP), lambda i: (i, 0)),
                  pl.BlockSpec((tm, D_GROUP), lambda i: (i, 0)),
                  resident(w_out), mod_spec, vec_spec, mod_spec, mod_spec, mod_spec,
                  resident(w_in), resident(w_ffn_out), vec_spec],
        out_specs=pl.BlockSpec((tm, d), lambda i: (i, 0)),
        out_shape=jax.ShapeDtypeStruct((n, d), F32),
        compiler_params=_cparams("arbitrary"),
        name="layer_tail",
    )(x, mix_a, mix_b, w_out, gate_mix, gain.reshape(1, d), shift, scale, gate_ffn, w_in, w_ffn_out,
      final_gain.reshape(1, d))


def _log_sigmoid(x):
    return jnp.minimum(x, 0.0) - jnp.log(1.0 + jnp.exp(-jnp.abs(x)))


def _ret_kernel(dl_ref, q_ref, k_ref, v_ref, s0_ref, o_ref, sf_ref, st):
    d = pl.program_id(1)
    c = pl.program_id(2)
    cs = RET_CHUNK

    @pl.when(c == 0)
    def _():
        st[...] = s0_ref[0, 0]

    fwd = d == 0
    row = lax.broadcasted_iota(jnp.int32, (cs, cs), 0)
    col = lax.broadcasted_iota(jnp.int32, (cs, cs), 1)
    dist = jnp.where(fwd, row - col, col - row)
    causal = dist >= 0
    distf = jnp.maximum(dist, 0).astype(F32)
    tq = lax.broadcasted_iota(jnp.int32, (cs, HEAD_DIM), 0)
    tq = jnp.where(fwd, tq, cs - 1 - tq).astype(F32)
    for h in range(N_HEADS):
        sl = slice(h * HEAD_DIM, (h + 1) * HEAD_DIM)
        lg = _log_sigmoid(dl_ref[0, h])[0:1, :]
        lgh = lg[:, :HEAD_DIM]
        mask = jnp.where(causal, jnp.exp(distf * lg), 0.0)
        q = q_ref[:, sl]
        k = k_ref[:, sl] * ATTN_SCALE
        v = v_ref[:, sl]
        s = _bdot_nt(q, k) * mask
        state = st[h]
        o = _bdot(s, v) + _bdot(q, state) * jnp.exp((tq + 1.0) * lgh)
        o_ref[0, :, sl] = o
        st[h] = state * jnp.exp(cs * lgh) + _bdot_tn(k * jnp.exp((cs - 1.0 - tq) * lgh), v)

    @pl.when(c == pl.num_programs(2) - 1)
    def _():
        sf_ref[0, 0] = st[...]


def _retention_scan(decay_logit, q, k, v, state0, seq):
    n = q.shape[0]
    nb = n // seq
    nc = seq // RET_CHUNK
    dl = jnp.broadcast_to(decay_logit.astype(F32)[:, :, None, None], (2, N_HEADS, 8, 128))

    def chunk_idx(b, d, c):
        return (b * nc + jnp.where(d == 0, c, nc - 1 - c), 0)

    blk = pl.BlockSpec((RET_CHUNK, D_GROUP), chunk_idx)
    st_blk = pl.BlockSpec((1, 1, N_HEADS, HEAD_DIM, HEAD_DIM), lambda b, d, c: (b, d, 0, 0, 0))
    return pl.pallas_call(
        _ret_kernel,
        grid=(nb, 2, nc),
        in_specs=[pl.BlockSpec((1, N_HEADS, 8, 128), lambda b, d, c: (d, 0, 0, 0)), blk, blk, blk, st_blk],
        out_specs=[pl.BlockSpec((1, RET_CHUNK, D_GROUP), lambda b, d, c: (d,) + chunk_idx(b, d, c)), st_blk],
        out_shape=[jax.ShapeDtypeStruct((2, n, D_GROUP), F32),
                   jax.ShapeDtypeStruct((nb, 2, N_HEADS, HEAD_DIM, HEAD_DIM), F32)],
        scratch_shapes=[pltpu.VMEM((N_HEADS, HEAD_DIM, HEAD_DIM), F32)],
        compiler_params=_cparams("arbitrary", "arbitrary", "arbitrary"),
        name="retention_scan",
    )(dl, q, k, v, state0)


def _rwkv_kernel(xr_ref, xk_ref, xv_ref, lo_ref, w0_ref, wup_ref, a0_ref, aup_ref, kk_ref, ka_ref, s0_ref,
                 y_ref, sf_ref, st):
    d = pl.program_id(1)
    c = pl.program_id(2)
    cs = RWKV_CHUNK

    @pl.when(c == 0)
    def _():
        st[...] = s0_ref[0, 0]

    fwd = d == 0
    row = lax.broadcasted_iota(jnp.int32, (cs, cs), 0)
    col = lax.broadcasted_iota(jnp.int32, (cs, cs), 1)
    dist = jnp.where(fwd, row - col, col - row)
    earlier = dist > 0
    upto = dist >= 0
    eye = jnp.where(col == row, 1.0, 0.0).astype(F32)
    tri = jnp.where(upto, 1.0, 0.0).astype(BF16)

    xr = xr_ref[...]
    xk = xk_ref[...]
    xv = xv_ref[...]
    lo = lo_ref[...]
    w_log = -math.exp(-0.5) * _sigmoid(w0_ref[0] + _bdot(jnp.tanh(lo[:, 0:D_LORA]), wup_ref[0]))
    a_all = _sigmoid(a0_ref[0] + _bdot(lo[:, D_LORA:2 * D_LORA], aup_ref[0]))
    w1 = w_log.astype(BF16)
    r1 = w_log - w1.astype(F32)
    w2 = r1.astype(BF16)
    w3 = (r1 - w2.astype(F32)).astype(BF16)
    cum = (jnp.dot(tri, w1, preferred_element_type=F32) + jnp.dot(tri, w2, preferred_element_type=F32)
           + jnp.dot(tri, w3, preferred_element_type=F32))
    cum_end = jnp.where(fwd, cum[cs - 1:cs, :], cum[0:1, :])
    e_incl = jnp.exp(cum)
    e_excl = jnp.exp(cum - w_log)
    e_neg = jnp.exp(-cum)
    e_end = jnp.exp(cum_end - cum)
    w_end = jnp.exp(cum_end)
    kk_all = xk * kk_ref[...]
    keff_all = xk * (1.0 + (a_all - 1.0) * ka_ref[...])

    for h in range(N_HEADS):
        sl = slice(h * HEAD_DIM, (h + 1) * HEAD_DIM)
        kk = kk_all[:, sl]
        kk = kk * lax.rsqrt(jnp.sum(kk * kk, axis=-1, keepdims=True) + 1e-12)
        a = a_all[:, sl]
        keff = keff_all[:, sl]
        b = kk * a
        v = xv[:, sl]
        a_t = -kk * e_excl[:, sl]
        r_t = xr[:, sl] * e_incl[:, sl]
        b_t = b * e_neg[:, sl]
        k_t = keff * e_neg[:, sl]
        n_mat = jnp.where(earlier, _mm3(a_t, b_t, "nt"), 0.0)
        a_ak = jnp.where(earlier, _mm3(a_t, k_t, "nt"), 0.0)
        m_rb = jnp.where(upto, _mm3(r_t, b_t, "nt"), 0.0)
        m_rk = jnp.where(upto, _mm3(r_t, k_t, "nt"), 0.0)
        t_inv = eye + n_mat
        p = n_mat
        for _ in range(RWKV_DOUBLINGS):
            p = _mm3(p, p)
            t_inv = t_inv + _mm3(t_inv, p)
        a_hat = _mm3(t_inv, a_t)
        u0 = _mm3(t_inv, _mm3(a_ak, v))
        y0 = _mm3(m_rk, v)
        s_prev = st[h]
        u = _mm3(a_hat, s_prev, "nt") + u0
        y_ref[0, :, sl] = _mm3(r_t, s_prev, "nt") + _mm3(m_rb, u) + y0
        st[h] = (s_prev * w_end[:, sl] + _mm3(u, b * e_end[:, sl], "tn")
                 + _mm3(v, keff * e_end[:, sl], "tn"))

    @pl.when(c == pl.num_programs(2) - 1)
    def _():
        sf_ref[0, 0] = st[...]


def _rwkv_scan(xr, xk, xv, lora, w0, w_up, a0, a_up, k_k, k_a, state0, seq):
    n = xr.shape[0]
    nb = n // seq
    nc = seq // RWKV_CHUNK

    def chunk_idx(b, d, c):
        return (b * nc + jnp.where(d == 0, c, nc - 1 - c), 0)

    blk = pl.BlockSpec((RWKV_CHUNK, D_GROUP), chunk_idx)
    lblk = pl.BlockSpec((RWKV_CHUNK, lora.shape[1]), chunk_idx)
    dvec = pl.BlockSpec((1, 1, D_GROUP), lambda b, d, c: (d, 0, 0))
    dmat = pl.BlockSpec((1, D_LORA, D_GROUP), lambda b, d, c: (d, 0, 0))
    vec = pl.BlockSpec((1, D_GROUP), lambda b, d, c: (0, 0))
    st_blk = pl.BlockSpec((1, 1, N_HEADS, HEAD_DIM, HEAD_DIM), lambda b, d, c: (b, d, 0, 0, 0))
    return pl.pallas_call(
        _rwkv_kernel,
        grid=(nb, 2, nc),
        in_specs=[blk, blk, blk, lblk, dvec, dmat, dvec, dmat, vec, vec, st_blk],
        out_specs=[pl.BlockSpec((1, RWKV_CHUNK, D_GROUP), lambda b, d, c: (d,) + chunk_idx(b, d, c)), st_blk],
        out_shape=[jax.ShapeDtypeStruct((2, n, D_GROUP), F32),
                   jax.ShapeDtypeStruct((nb, 2, N_HEADS, HEAD_DIM, HEAD_DIM), F32)],
        scratch_shapes=[pltpu.VMEM((N_HEADS, HEAD_DIM, HEAD_DIM), F32)],
        compiler_params=_cparams("arbitrary", "arbitrary", "arbitrary"),
        name="rwkv7_scan",
    )(xr, xk, xv, lora, w0.reshape(2, 1, D_GROUP), w_up, a0.reshape(2, 1, D_GROUP), a_up,
      k_k.reshape(1, D_GROUP), k_a.reshape(1, D_GROUP), state0)


def _rec_combine_kernel(oret_ref, rg_ref, yrw_ref, xr_ref, xk_ref, xv_ref, lo_ref, rk_ref, lng_ref, lnb_ref,
                        gup_ref, oret_out, orw_out):
    o = oret_ref[0] + oret_ref[1]
    y = yrw_ref[0] + yrw_ref[1]
    g_ret = _silu(rg_ref[...])
    xr = xr_ref[...]
    xk = xk_ref[...]
    xv = xv_ref[...]
    rk = rk_ref[...]
    lng = lng_ref[...]
    lnb = lnb_ref[...]
    g_rw = _bdot(_sigmoid(lo_ref[:, 2 * D_LORA:]), gup_ref[...])
    for h in range(N_HEADS):
        sl = slice(h * HEAD_DIM, (h + 1) * HEAD_DIM)
        oret_out[:, sl] = g_ret[:, sl] * _rms(o[:, sl])
        yh = y[:, sl]
        mu = jnp.mean(yh, axis=-1, keepdims=True)
        var = jnp.mean(jnp.square(yh - mu), axis=-1, keepdims=True)
        yn = (yh - mu) * lax.rsqrt(var + RWKV_GN_EPS) * lng[:, sl] + lnb[:, sl]
        bonus = jnp.sum(xr[:, sl] * rk[:, sl] * xk[:, sl], axis=-1, keepdims=True) * xv[:, sl]
        orw_out[:, sl] = (yn + bonus) * g_rw[:, sl]


def _rec_combine(o_ret2, rg, y_rw2, xr, xk, xv, lora, r_k, ln_g, ln_b, g_up, tm=256):
    n = rg.shape[0]
    blk = pl.BlockSpec((tm, D_GROUP), lambda i: (i, 0))
    blk2 = pl.BlockSpec((2, tm, D_GROUP), lambda i: (0, i, 0))
    vec = pl.BlockSpec((1, D_GROUP), lambda i: (0, 0))
    return pl.pallas_call(
        _rec_combine_kernel,
        grid=(n // tm,),
        in_specs=[blk2, blk, blk2, blk, blk, blk, pl.BlockSpec((tm, lora.shape[1]), lambda i: (i, 0)),
                  vec, vec, vec, pl.BlockSpec(g_up.shape, lambda i: (0, 0))],
        out_specs=[blk, blk],
        out_shape=[jax.ShapeDtypeStruct((n, D_GROUP), F32)] * 2,
        compiler_params=_cparams("arbitrary"),
        name="rec_combine",
    )(o_ret2, rg, y_rw2, xr, xk, xv, lora, r_k.reshape(1, D_GROUP), ln_g.reshape(1, D_GROUP),
      ln_b.reshape(1, D_GROUP), g_up)


ATTN_WIDTHS = (D_GROUP,) * 6
REC_WIDTHS = (D_GROUP,) * 7 + (2 * D_LORA + D_LORA_G,)


def kernel(x_prompt, x_sample, cache_na_k, cache_na_v, cache_diff_k, cache_diff_v, state_ret, state_rwkv, c, c_ctx, norm_mix_g, norm_ffn_g, norm_final_g, w_ada, b_ada, w_in_attn, w_out_attn, na_rpb, diff_lq1, diff_lk1, diff_lq2, diff_lk2, w_in_rec, w_out_rec, ret_decay_logit, rw_w0, rw_w_up, rw_a0, rw_a_up, rw_g_up, rw_k_k, rw_k_a, rw_r_k, rw_ln_g, rw_ln_b, w_ffn_in, w_ffn_out):
    bp, seq, d = x_prompt.shape
    bl, n_lat, _ = x_sample.shape
    depth = w_ada.shape[0]
    ctx = x_prompt.reshape(bp * seq, d)
    lat = x_sample.reshape(bl * n_lat, d)

    cond8 = jnp.zeros((8, d), F32).at[0].set(c_ctx).at[1:1 + bl].set(c)
    mods = _ada_modulation(cond8, w_ada, b_ada).reshape(depth, 8, 6, d)

    outs = {}
    for layer in range(depth):
        m_ctx = [mods[layer, 0:1, j].reshape(1, 1, d) for j in range(6)]
        m_lat = [mods[layer, 1:1 + bl, j].reshape(bl, 1, d) for j in range(6)]
        if layer % 2 == 0:
            i = layer // 2
            lam_init = 0.8 - 0.6 * math.exp(-0.3 * layer)
            lam_params = jnp.stack([diff_lq1[i], diff_lk1[i], diff_lq2[i], diff_lk2[i]]).astype(F32)
            w_in = w_in_attn[i].astype(BF16)
            w_out = w_out_attn[i].astype(BF16)
            pc = _norm_mod_matmul(ctx, norm_mix_g[layer], m_ctx[0], m_ctx[1], w_in, ATTN_WIDTHS, bp * seq, 256)
            plat = _norm_mod_matmul(lat, norm_mix_g[layer], m_lat[0], m_lat[1], w_in, ATTN_WIDTHS, n_lat, 256)
            mix_ctx = _ctx_attention(lam_params, *pc, seq, lam_init)
            outs["na_k"], outs["na_v"], outs["df_k"], outs["df_v"] = pc[1], pc[2], pc[4], pc[5]
            past = cache_na_k.shape[2]
            tt = _na_bias_table(na_rpb[i])
            o_na = _lat_na_attention(plat[0], plat[1], plat[2],
                                     cache_na_k[:, i].reshape(bl, past, D_GROUP),
                                     cache_na_v[:, i].reshape(bl, past, D_GROUP), tt, n_lat)
            o_df = _lat_diff_attention(lam_params, plat[3], plat[4], plat[5],
                                       cache_diff_k[:, i].reshape(bl, past, D_GROUP),
                                       cache_diff_v[:, i].reshape(bl, past, D_GROUP), n_lat, lam_init)
            mix_lat = (o_na, o_df)
        else:
            j = layer // 2
            w_in = w_in_rec[j].astype(BF16)
            w_out = w_out_rec[j].astype(BF16)
            pc = _norm_mod_matmul(ctx, norm_mix_g[layer], m_ctx[0], m_ctx[1], w_in, REC_WIDTHS, bp * seq, 256)
            plat = _norm_mod_matmul(lat, norm_mix_g[layer], m_lat[0], m_lat[1], w_in, REC_WIDTHS, n_lat, 256)
            zero_state = jnp.zeros((bp, 2, N_HEADS, HEAD_DIM, HEAD_DIM), F32)
            mixes = []
            for p, s_ret0, s_rw0, sq in ((pc, zero_state, zero_state, seq),
                                         (plat, state_ret[:, j], state_rwkv[:, j], n_lat)):
                rq, rk, rv, rg, wr, wk, wv, lora = p
                o_ret2, s_ret = _retention_scan(ret_decay_logit[j], rq, rk, rv, s_ret0, sq)
                y_rw2, s_rw = _rwkv_scan(wr, wk, wv, lora, rw_w0[j], rw_w_up[j].astype(BF16), rw_a0[j],
                                         rw_a_up[j].astype(BF16), rw_k_k[j], rw_k_a[j], s_rw0, sq)
                mixes.append(_rec_combine(o_ret2, rg, y_rw2, wr, wk, wv, lora, rw_r_k[j], rw_ln_g[j],
                                          rw_ln_b[j], rw_g_up[j].astype(BF16)))
                if sq == seq:
                    outs["ret"], outs["rwkv"] = s_ret, s_rw
            mix_ctx, mix_lat = mixes
        final = layer == depth - 1
        w_fi = w_ffn_in[layer].astype(BF16)
        w_fo = w_ffn_out[layer].astype(BF16)
        ctx = _layer_tail(ctx, mix_ctx[0], mix_ctx[1], w_out, m_ctx[2], norm_ffn_g[layer], m_ctx[3], m_ctx[4],
                          m_ctx[5], w_fi, w_fo, norm_final_g, bp * seq, final)
        lat = _layer_tail(lat, mix_lat[0], mix_lat[1], w_out, m_lat[2], norm_ffn_g[layer], m_lat[3], m_lat[4],
                          m_lat[5], w_fi, w_fo, norm_final_g, n_lat, final)

    y_prompt = ctx.reshape(bp, seq, d)
    y_sample = lat.reshape(bl, n_lat, d)
    new_cache_na_k = outs["na_k"].reshape(bp, 1, seq, N_HEADS, HEAD_DIM)
    new_cache_na_v = outs["na_v"].reshape(bp, 1, seq, N_HEADS, HEAD_DIM)
    new_cache_diff_k = outs["df_k"].reshape(bp, 1, seq, N_HEADS // 2, 2, HEAD_DIM)
    new_cache_diff_v = outs["df_v"].reshape(bp, 1, seq, N_HEADS // 2, 2 * HEAD_DIM)
    new_state_ret = outs["ret"].reshape(bp, 1, 2, N_HEADS, HEAD_DIM, HEAD_DIM)
    new_state_rwkv = outs["rwkv"].reshape(bp, 1, 2, N_HEADS, HEAD_DIM, HEAD_DIM)
    return (y_prompt, y_sample, new_cache_na_k, new_cache_na_v, new_cache_diff_k, new_cache_diff_v,
            new_state_ret, new_state_rwkv)
```

```python
import functools
import math

import jax
import jax.numpy as jnp
from jax import lax
from jax.experimental import pallas as pl
from jax.experimental.pallas import tpu as pltpu

F32 = jnp.float32
BF16 = jnp.bfloat16

D_MODEL = 1024
HEAD_DIM = 64
N_HEADS = 8
D_GROUP = 512
GRID_W = 64
GRID_ROWS = 32
WIN_R = 8
WIN_W = 16
RPB_R = 15
RPB_C = 31
PAST_LEN = 256
D_FF = 2816
D_LORA = 64
D_LORA_G = 128
ROPE_BASE = 10000.0
RMS_EPS = 1e-6
RWKV_GN_EPS = 64e-5
NEG_INF = -1e30
ATTN_SCALE = HEAD_DIM ** -0.5
RET_CHUNK = 128
RWKV_CHUNK = 64
RWKV_DOUBLINGS = 5
VMEM_LIMIT = 56 * 1024 * 1024


def _cparams(*sem):
    return pltpu.CompilerParams(dimension_semantics=sem, vmem_limit_bytes=VMEM_LIMIT)


def _sigmoid(x):
    return 1.0 / (1.0 + jnp.exp(-x))


def _silu(x):
    return x * _sigmoid(x)


def _rms(x, eps=RMS_EPS):
    return x * lax.rsqrt(jnp.mean(x * x, axis=-1, keepdims=True) + eps)


def _bdot(a, b):
    return jnp.dot(a.astype(BF16), b.astype(BF16), preferred_element_type=F32)


def _bdot_nt(a, b):
    return lax.dot_general(a.astype(BF16), b.astype(BF16), (((1,), (1,)), ((), ())),
                           preferred_element_type=F32)


def _bdot_tn(a, b):
    return lax.dot_general(a.astype(BF16), b.astype(BF16), (((0,), (0,)), ((), ())),
                           preferred_element_type=F32)


_DN = {"nn": (((1,), (0,)), ((), ())), "nt": (((1,), (1,)), ((), ())), "tn": (((0,), (0,)), ((), ()))}


def _split2(x):
    hi = x.astype(BF16)
    lo = (x - hi.astype(F32)).astype(BF16)
    return hi, lo


def _mm3s(a_split, b_split, mode="nn"):
    ah, al = a_split
    bh, bl = b_split
    d = functools.partial(lax.dot_general, dimension_numbers=_DN[mode], preferred_element_type=F32)
    return d(ah, bh) + (d(ah, bl) + d(al, bh))


def _mm3(a, b, mode="nn"):
    return _mm3s(_split2(a), _split2(b), mode)


def _softmax_rows(s):
    m = jnp.max(s, axis=-1, keepdims=True)
    e = jnp.exp(s - m)
    return e / jnp.sum(e, axis=-1, keepdims=True)


def _diff_lambda(lp, lam_init):
    s1 = jnp.sum(lp[0:1, :] * lp[1:2, :], axis=-1, keepdims=True)
    s2 = jnp.sum(lp[2:3, :] * lp[3:4, :], axis=-1, keepdims=True)
    return jnp.exp(s1) - jnp.exp(s2) + lam_init


def _ada_kernel(c_ref, w_ref, b_ref, o_ref):
    s = _silu(c_ref[...])
    o_ref[0] = _bdot(s, w_ref[0]) + b_ref[0]


def _ada_modulation(cond8, w_ada, b_ada):
    depth, d, n = w_ada.shape
    tn = 1536
    return pl.pallas_call(
        _ada_kernel,
        grid=(depth, n // tn),
        in_specs=[pl.BlockSpec((8, d), lambda l, j: (0, 0)),
                  pl.BlockSpec((1, d, tn), lambda l, j: (l, 0, j)),
                  pl.BlockSpec((1, 1, tn), lambda l, j: (l, 0, j))],
        out_specs=pl.BlockSpec((1, 8, tn), lambda l, j: (l, 0, j)),
        out_shape=jax.ShapeDtypeStruct((depth, 8, n), F32),
        compiler_params=_cparams("arbitrary", "arbitrary"),
        name="ada_modulation",
    )(cond8, w_ada, b_ada.reshape(depth, 1, n))


def _nmm_kernel(x_ref, g_ref, sh_ref, sc_ref, w_ref, *o_refs, widths):
    h = _rms(x_ref[...]) * g_ref[...]
    h = h * (1.0 + sc_ref[0]) + sh_ref[0]
    hb = h.astype(BF16)
    off = 0
    for o_ref, w in zip(o_refs, widths):
        o_ref[...] = jnp.dot(hb, w_ref[:, off:off + w], preferred_element_type=F32)
        off += w


def _norm_mod_matmul(x, gain, shift, scale, w_bf16, widths, rows_per_mod, tm):
    n, d = x.shape
    tiles_per_mod = rows_per_mod // tm
    mod_spec = pl.BlockSpec((1, 1, d), lambda i: (i // tiles_per_mod, 0, 0))
    return pl.pallas_call(
        functools.partial(_nmm_kernel, widths=widths),
        grid=(n // tm,),
        in_specs=[pl.BlockSpec((tm, d), lambda i: (i, 0)),
                  pl.BlockSpec((1, d), lambda i: (0, 0)),
                  mod_spec, mod_spec,
                  pl.BlockSpec(w_bf16.shape, lambda i: (0, 0))],
        out_specs=[pl.BlockSpec((tm, w), lambda i: (i, 0)) for w in widths],
        out_shape=[jax.ShapeDtypeStruct((n, w), F32) for w in widths],
        compiler_params=_cparams("arbitrary"),
        name="norm_mod_matmul",
    )(x, gain.reshape(1, d), shift, scale, w_bf16)


def _ctx_attn_kernel(lp_ref, qn_ref, kn_ref, vn_ref, qd_ref, kd_ref, vd_ref, on_ref, od_ref, *, lam_init):
    for h in range(N_HEADS):
        sl = slice(h * HEAD_DIM, (h + 1) * HEAD_DIM)
        s = _bdot_nt(qn_ref[:, sl], kn_ref[:, sl]) * ATTN_SCALE
        on_ref[:, sl] = _bdot(_softmax_rows(s), vn_ref[:, sl])
    lam = _diff_lambda(lp_ref[...], lam_init)
    for h in range(N_HEADS // 2):
        s1 = slice(2 * h * HEAD_DIM, (2 * h + 1) * HEAD_DIM)
        s2 = slice((2 * h + 1) * HEAD_DIM, (2 * h + 2) * HEAD_DIM)
        sv = slice(2 * h * HEAD_DIM, (2 * h + 2) * HEAD_DIM)
        p1 = _softmax_rows(_bdot_nt(qd_ref[:, s1], kd_ref[:, s1]) * ATTN_SCALE)
        p2 = _softmax_rows(_bdot_nt(qd_ref[:, s2], kd_ref[:, s2]) * ATTN_SCALE)
        o = _bdot(p1 - lam * p2, vd_ref[:, sv])
        od_ref[:, sv] = _rms(o) * (1.0 - lam_init)


def _ctx_attention(lam_params, qn, kn, vn, qd, kd, vd, seq, lam_init):
    n = qn.shape[0]
    blk = pl.BlockSpec((seq, D_GROUP), lambda b: (b, 0))
    return pl.pallas_call(
        functools.partial(_ctx_attn_kernel, lam_init=lam_init),
        grid=(n // seq,),
        in_specs=[pl.BlockSpec(lam_params.shape, lambda b: (0, 0))] + [blk] * 6,
        out_specs=[blk, blk],
        out_shape=[jax.ShapeDtypeStruct((n, D_GROUP), F32)] * 2,
        compiler_params=_cparams("arbitrary"),
        name="ctx_attention",
    )(lam_params, qn, kn, vn, qd, kd, vd)


def _rope_tables(n):
    quarter = HEAD_DIM // 4
    pos = jnp.arange(n)
    inv_freq = ROPE_BASE ** (-jnp.arange(quarter, dtype=F32) / quarter)
    lane = jnp.arange(HEAD_DIM)
    p = jnp.where(lane[None, :] < HEAD_DIM // 2, (pos // GRID_W)[:, None], (pos % GRID_W)[:, None]).astype(F32)
    ang = p * inv_freq[lane % quarter][None, :]
    sign = jnp.where((lane % (2 * quarter)) < quarter, -1.0, 1.0).astype(F32)
    cos = jnp.tile(jnp.cos(ang), (1, N_HEADS))
    sin = jnp.tile(jnp.sin(ang) * sign[None, :], (1, N_HEADS))
    return cos, sin


def _rope(x, cos, sin_signed):
    quarter = HEAD_DIM // 4
    width = x.shape[-1]
    lane = lax.broadcasted_iota(jnp.int32, x.shape, 1)
    first = (lane % (2 * quarter)) < quarter
    partner = jnp.where(first, pltpu.roll(x, width - quarter, 1), pltpu.roll(x, quarter, 1))
    return x * cos + partner * sin_signed


def _lat_diff_kernel(lp_ref, q_ref, k_ref, v_ref, ck_ref, cv_ref, cosq_ref, sinq_ref, cosk_ref, sink_ref,
                     o_ref, kall, vall, *, lam_init, n_lat):
    @pl.when(pl.program_id(1) == 0)
    def _():
        kall[0:n_lat, :] = _rope(k_ref[...], cosk_ref[...], sink_ref[...]).astype(BF16)
        kall[n_lat:, :] = ck_ref[...].astype(BF16)
        vall[0:n_lat, :] = v_ref[...].astype(BF16)
        vall[n_lat:, :] = cv_ref[...].astype(BF16)

    lam = _diff_lambda(lp_ref[...], lam_init)
    q = _rope(q_ref[...], cosq_ref[...], sinq_ref[...]).astype(BF16)
    for h in range(N_HEADS // 2):
        s1 = slice(2 * h * HEAD_DIM, (2 * h + 1) * HEAD_DIM)
        s2 = slice((2 * h + 1) * HEAD_DIM, (2 * h + 2) * HEAD_DIM)
        sv = slice(2 * h * HEAD_DIM, (2 * h + 2) * HEAD_DIM)
        p1 = _softmax_rows(_bdot_nt(q[:, s1], kall[:, s1]) * ATTN_SCALE)
        p2 = _softmax_rows(_bdot_nt(q[:, s2], kall[:, s2]) * ATTN_SCALE)
        o = _bdot(p1 - lam * p2, vall[:, sv])
        o_ref[:, sv] = _rms(o) * (1.0 - lam_init)


def _lat_diff_attention(lam_params, qd, kd, vd, cache_k, cache_v, n_lat, lam_init, tq=256):
    n = qd.shape[0]
    nb = n // n_lat
    nq = n_lat // tq
    past = cache_k.shape[1]
    cos, sin = _rope_tables(n_lat)
    qblk = pl.BlockSpec((tq, D_GROUP), lambda b, i: (b * nq + i, 0))
    kvblk = pl.BlockSpec((n_lat, D_GROUP), lambda b, i: (b, 0))
    cblk = pl.BlockSpec((None, past, D_GROUP), lambda b, i: (b, 0, 0))
    return pl.pallas_call(
        functools.partial(_lat_diff_kernel, lam_init=lam_init, n_lat=n_lat),
        grid=(nb, nq),
        in_specs=[pl.BlockSpec(lam_params.shape, lambda b, i: (0, 0)),
                  qblk, kvblk, kvblk, cblk, cblk,
                  pl.BlockSpec((tq, D_GROUP), lambda b, i: (i, 0)),
                  pl.BlockSpec((tq, D_GROUP), lambda b, i: (i, 0)),
                  pl.BlockSpec((n_lat, D_GROUP), lambda b, i: (0, 0)),
                  pl.BlockSpec((n_lat, D_GROUP), lambda b, i: (0, 0))],
        out_specs=qblk,
        out_shape=jax.ShapeDtypeStruct((n, D_GROUP), F32),
        scratch_shapes=[pltpu.VMEM((n_lat + past, D_GROUP), BF16),
                        pltpu.VMEM((n_lat + past, D_GROUP), BF16)],
        compiler_params=_cparams("arbitrary", "arbitrary"),
        name="lat_diff_attention",
    )(lam_params, qd, kd, vd, cache_k, cache_v, cos, sin, cos, sin)


def _na_bias_kernel(rpb_ref, tt_ref):
    h = pl.program_id(0)
    lane = lax.broadcasted_iota(jnp.int32, (GRID_W, 2 * GRID_W), 1)
    qc = lax.broadcasted_iota(jnp.int32, (GRID_W, 2 * GRID_W), 0)
    dc = jnp.clip(lane % GRID_W - qc + (WIN_W - 1), 0, RPB_C - 1)
    first = lane < GRID_W
    for dr in range(RPB_R - 1):
        acc = jnp.zeros((GRID_W, 2 * GRID_W), F32)
        for d in range(RPB_C):
            val = jnp.where(first, rpb_ref[h * RPB_R + dr, d], rpb_ref[h * RPB_R + dr + 1, d])
            acc = jnp.where(dc == d, val, acc)
        tt_ref[0, dr] = acc


def _na_bias_table(rpb):
    return pl.pallas_call(
        _na_bias_kernel,
        grid=(N_HEADS,),
        in_specs=[pl.BlockSpec(memory_space=pltpu.SMEM)],
        out_specs=pl.BlockSpec((1, RPB_R - 1, GRID_W, 2 * GRID_W), lambda h: (h, 0, 0, 0)),
        out_shape=jax.ShapeDtypeStruct((N_HEADS, RPB_R - 1, GRID_W, 2 * GRID_W), F32),
        compiler_params=_cparams("arbitrary"),
        name="na_bias_table",
    )(rpb.reshape(N_HEADS * RPB_R, RPB_C))


def _lat_na_kernel(q_ref, k_ref, v_ref, ck_ref, cv_ref, tt_ref, o_ref):
    r = pl.program_id(1)
    r_start = jnp.clip(r - WIN_R // 2, 0, GRID_ROWS - WIN_R)
    dr0 = r_start - r + (WIN_R - 1)
    row0 = pl.multiple_of(r_start * GRID_W, GRID_W)
    nkeys = WIN_R * GRID_W
    kc = lax.broadcasted_iota(jnp.int32, (GRID_W, nkeys), 1) % GRID_W
    qc = lax.broadcasted_iota(jnp.int32, (GRID_W, nkeys), 0)
    c_start = jnp.clip(qc - WIN_W // 2, 0, GRID_W - WIN_W)
    col_in = (kc >= c_start) & (kc < c_start + WIN_W)
    for h in range(N_HEADS):
        sl = slice(h * HEAD_DIM, (h + 1) * HEAD_DIM)
        q = q_ref[:, sl].astype(BF16)
        kw = k_ref[pl.ds(row0, nkeys), sl]
        vw = v_ref[pl.ds(row0, nkeys), sl]
        bias = jnp.concatenate([tt_ref[h, dr0 + 2 * j] for j in range(WIN_R // 2)], axis=1)
        s = _bdot_nt(q, kw) * ATTN_SCALE + bias
        s = jnp.where(col_in, s, NEG_INF)
        sc = _bdot_nt(q, ck_ref[:, sl]) * ATTN_SCALE
        m = jnp.maximum(jnp.max(s, axis=-1, keepdims=True), jnp.max(sc, axis=-1, keepdims=True))
        e1 = jnp.exp(s - m)
        e2 = jnp.exp(sc - m)
        l = jnp.sum(e1, axis=-1, keepdims=True) + jnp.sum(e2, axis=-1, keepdims=True)
        o_ref[:, sl] = _bdot(e1 / l, vw) + _bdot(e2 / l, cv_ref[:, sl])


def _lat_na_attention(qn, kn, vn, cache_k, cache_v, tt, n_lat):
    n = qn.shape[0]
    nb = n // n_lat
    past = cache_k.shape[1]
    qblk = pl.BlockSpec((GRID_W, D_GROUP), lambda b, r: (b * GRID_ROWS + r, 0))
    kvblk = pl.BlockSpec((n_lat, D_GROUP), lambda b, r: (b, 0))
    cblk = pl.BlockSpec((None, past, D_GROUP), lambda b, r: (b, 0, 0))
    return pl.pallas_call(
        _lat_na_kernel,
        grid=(nb, GRID_ROWS),
        in_specs=[qblk, kvblk, kvblk, cblk, cblk,
                  pl.BlockSpec(tt.shape, lambda b, r: (0, 0, 0, 0))],
        out_specs=qblk,
        out_shape=jax.ShapeDtypeStruct((n, D_GROUP), F32),
        compiler_params=_cparams("arbitrary", "arbitrary"),
        name="lat_na_attention",
    )(qn, kn, vn, cache_k, cache_v, tt)


def _tail_kernel(x_ref, ma_ref, mb_ref, wo_ref, gm_ref, g_ref, sh_ref, sc_ref, gf_ref, wi_ref, wf_ref, fg_ref,
                 o_ref, *, final, ff_chunk):
    mix = _bdot(ma_ref[...], wo_ref[0:D_GROUP, :]) + _bdot(mb_ref[...], wo_ref[D_GROUP:, :])
    x1 = x_ref[...] + gm_ref[0] * mix
    h = _rms(x1) * g_ref[...]
    hb = (h * (1.0 + sc_ref[0]) + sh_ref[0]).astype(BF16)
    acc = jnp.zeros_like(x1)
    for c0 in range(0, D_FF, ff_chunk):
        gate = jnp.dot(hb, wi_ref[:, c0:c0 + ff_chunk], preferred_element_type=F32)
        up = jnp.dot(hb, wi_ref[:, D_FF + c0:D_FF + c0 + ff_chunk], preferred_element_type=F32)
        acc = acc + jnp.dot((_silu(gate) * up).astype(BF16), wf_ref[c0:c0 + ff_chunk, :],
                            preferred_element_type=F32)
    x2 = x1 + gf_ref[0] * acc
    if final:
        x2 = _rms(x2) * fg_ref[...]
    o_ref[...] = x2


def _layer_tail(x, mix_a, mix_b, w_out, gate_mix, gain, shift, scale, gate_ffn, w_in, w_ffn_out, final_gain,
                rows_per_mod, final, tm=256, ff_chunk=1408):
    n, d = x.shape
    tiles_per_mod = rows_per_mod // tm
    mod_spec = pl.BlockSpec((1, 1, d), lambda i: (i // tiles_per_mod, 0, 0))
    vec_spec = pl.BlockSpec((1, d), lambda i: (0, 0))
    resident = lambda a: pl.BlockSpec(a.shape, lambda i: (0, 0), pipeline_mode=pl.Buffered(1))
    return pl.pallas_call(
        functools.partial(_tail_kernel, final=final, ff_chunk=ff_chunk),
        grid=(n // tm,),
        in_specs=[pl.BlockSpec((tm, d), lambda i: (i, 0)),
                  pl.BlockSpec((tm, D_GROUP), lambda i: (i, 0)),
                  pl.BlockSpec((tm, D_GROUP), lambda i: (i, 0)),
                  resident(w_out), mod_spec, vec_spec, mod_spec, mod_spec, mod_spec,
                  resident(w_in), resident(w_ffn_out), vec_spec],
        out_specs=pl.BlockSpec((tm, d), lambda i: (i, 0)),
        out_shape=jax.ShapeDtypeStruct((n, d), F32),
        compiler_params=_cparams("arbitrary"),
        name="layer_tail",
    )(x, mix_a, mix_b, w_out, gate_mix, gain.reshape(1, d), shift, scale, gate_ffn, w_in, w_ffn_out,
      final_gain.reshape(1, d))


def _log_sigmoid(x):
    return jnp.minimum(x, 0.0) - jnp.log(1.0 + jnp.exp(-jnp.abs(x)))


def _ret_kernel(dl_ref, q_ref, k_ref, v_ref, s0_ref, o_ref, sf_ref, st):
    d = pl.program_id(1)
    c = pl.program_id(2)
    cs = RET_CHUNK

    @pl.when(c == 0)
    def _():
        st[...] = s0_ref[0, 0]

    fwd = d == 0
    row = lax.broadcasted_iota(jnp.int32, (cs, cs), 0)
    col = lax.broadcasted_iota(jnp.int32, (cs, cs), 1)
    dist = jnp.where(fwd, row - col, col - row)
    causal = dist >= 0
    distf = jnp.maximum(dist, 0).astype(F32)
    tq = lax.broadcasted_iota(jnp.int32, (cs, HEAD_DIM), 0)
    tq = jnp.where(fwd, tq, cs - 1 - tq).astype(F32)
    for h in range(N_HEADS):
        sl = slice(h * HEAD_DIM, (h + 1) * HEAD_DIM)
        lg = _log_sigmoid(dl_ref[0, h])[0:1, :]
        lgh = lg[:, :HEAD_DIM]
        mask = jnp.where(causal, jnp.exp(distf * lg), 0.0)
        q = q_ref[:, sl]
        k = k_ref[:, sl] * ATTN_SCALE
        v = v_ref[:, sl]
        s = _bdot_nt(q, k) * mask
        state = st[h]
        o = _bdot(s, v) + _bdot(q, state) * jnp.exp((tq + 1.0) * lgh)
        o_ref[0, :, sl] = o
        st[h] = state * jnp.exp(cs * lgh) + _bdot_tn(k * jnp.exp((cs - 1.0 - tq) * lgh), v)

    @pl.when(c == pl.num_programs(2) - 1)
    def _():
        sf_ref[0, 0] = st[...]


def _retention_scan(decay_logit, q, k, v, state0, seq):
    n = q.shape[0]
    nb = n // seq
    nc = seq // RET_CHUNK
    dl = jnp.broadcast_to(decay_logit.astype(F32)[:, :, None, None], (2, N_HEADS, 8, 128))

    def chunk_idx(b, d, c):
        return (b * nc + jnp.where(d == 0, c, nc - 1 - c), 0)

    blk = pl.BlockSpec((RET_CHUNK, D_GROUP), chunk_idx)
    st_blk = pl.BlockSpec((1, 1, N_HEADS, HEAD_DIM, HEAD_DIM), lambda b, d, c: (b, d, 0, 0, 0))
    return pl.pallas_call(
        _ret_kernel,
        grid=(nb, 2, nc),
        in_specs=[pl.BlockSpec((1, N_HEADS, 8, 128), lambda b, d, c: (d, 0, 0, 0)), blk, blk, blk, st_blk],
        out_specs=[pl.BlockSpec((1, RET_CHUNK, D_GROUP), lambda b, d, c: (d,) + chunk_idx(b, d, c)), st_blk],
        out_shape=[jax.ShapeDtypeStruct((2, n, D_GROUP), F32),
                   jax.ShapeDtypeStruct((nb, 2, N_HEADS, HEAD_DIM, HEAD_DIM), F32)],
        scratch_shapes=[pltpu.VMEM((N_HEADS, HEAD_DIM, HEAD_DIM), F32)],
        compiler_params=_cparams("arbitrary", "arbitrary", "arbitrary"),
        name="retention_scan",
    )(dl, q, k, v, state0)


def _rwkv_kernel(xrf_ref, xkf_ref, xvf_ref, lof_ref, xrb_ref, xkb_ref, xvb_ref, lob_ref,
                 w0_ref, wup_ref, a0_ref, aup_ref, kk_ref, ka_ref, s0_ref, yf_ref, yb_ref, sf_ref, st):
    c = pl.program_id(1)
    cs = RWKV_CHUNK
    npair = N_HEADS // 2

    @pl.when(c == 0)
    def _():
        st[...] = s0_ref[0]

    row = lax.broadcasted_iota(jnp.int32, (cs, cs), 0)
    col = lax.broadcasted_iota(jnp.int32, (cs, cs), 1)
    pw = 2 * HEAD_DIM
    prow = lax.broadcasted_iota(jnp.int32, (cs, pw), 0)
    pcol = lax.broadcasted_iota(jnp.int32, (cs, pw), 1) % HEAD_DIM
    eye = jnp.where(pcol == prow, 1.0, 0.0).astype(F32)
    brow = lax.broadcasted_iota(jnp.int32, (pw, pw), 0) // HEAD_DIM
    bcol = lax.broadcasted_iota(jnp.int32, (pw, pw), 1) // HEAD_DIM
    same_head = brow == bcol
    ones_bd = jnp.where(same_head, 1.0, 0.0).astype(BF16)

    def bd(xp):
        left = lax.broadcasted_iota(jnp.int32, xp.shape, 1) < HEAD_DIM
        zero = jnp.zeros_like(xp)
        return jnp.concatenate([jnp.where(left, xp, zero), jnp.where(left, zero, xp)], axis=0)

    def direction(d, xr_ref, xk_ref, xv_ref, lo_ref):
        tri = jnp.where((row >= col) if d == 0 else (col >= row), 1.0, 0.0).astype(BF16)
        dist = (prow - pcol) if d == 0 else (pcol - prow)
        xk = xk_ref[...]
        lo = lo_ref[...]
        w_log = -math.exp(-0.5) * _sigmoid(w0_ref[d] + _bdot(jnp.tanh(lo[:, 0:D_LORA]), wup_ref[d]))
        a_all = _sigmoid(a0_ref[d] + _bdot(lo[:, D_LORA:2 * D_LORA], aup_ref[d]))
        w1 = w_log.astype(BF16)
        r1 = w_log - w1.astype(F32)
        w2 = r1.astype(BF16)
        w3 = (r1 - w2.astype(F32)).astype(BF16)
        cum = (jnp.dot(tri, w1, preferred_element_type=F32) + jnp.dot(tri, w2, preferred_element_type=F32)
               + jnp.dot(tri, w3, preferred_element_type=F32))
        cum_end = cum[cs - 1:cs, :] if d == 0 else cum[0:1, :]
        return dict(earlier=dist > 0, upto=dist >= 0, xr=xr_ref[...], xv=xv_ref[...], a_all=a_all,
                    e_incl=jnp.exp(cum), e_excl=jnp.exp(cum - w_log), e_neg=jnp.exp(-cum),
                    e_end=jnp.exp(cum_end - cum), w_end=jnp.exp(cum_end), kk_all=xk * kk_ref[...],
                    keff_all=xk * (1.0 + (a_all - 1.0) * ka_ref[...]))

    dirs = [direction(0, xrf_ref, xkf_ref, xvf_ref, lof_ref), direction(1, xrb_ref, xkb_ref, xvb_ref, lob_ref)]
    y_refs = [yf_ref, yb_ref]

    chains = [(d, p) for d in range(2) for p in range(npair)]
    pairs = range(len(chains))
    sls = [slice(p * pw, (p + 1) * pw) for _, p in chains]
    pick = lambda name: [dirs[d][name][:, sls[i]] for i, (d, _) in enumerate(chains)]
    earlier = [dirs[d]["earlier"] for d, _ in chains]
    upto = [dirs[d]["upto"] for d, _ in chains]
    cat = lambda x, y: jnp.concatenate([x, y], axis=0)

    def head_sum(x):
        x1 = x.astype(BF16)
        r1 = x - x1.astype(F32)
        x2 = r1.astype(BF16)
        x3 = (r1 - x2.astype(F32)).astype(BF16)
        f = functools.partial(jnp.dot, preferred_element_type=F32)
        return f(x1, ones_bd) + (f(x2, ones_bd) + f(x3, ones_bd))

    e_incl, e_excl, e_neg, e_end, w_end = (pick(k) for k in ("e_incl", "e_excl", "e_neg", "e_end", "w_end"))
    kk = [x * lax.rsqrt(head_sum(x * x) + 1e-12) for x in pick("kk_all")]
    b = [kk[p] * a for p, a in zip(pairs, pick("a_all"))]
    keff = pick("keff_all")
    v = pick("xv")
    a_t = [-kk[p] * e_excl[p] for p in pairs]
    r_t = [xr * e_incl[p] for p, xr in zip(pairs, pick("xr"))]
    ar = [cat(a_t[p], r_t[p]).astype(BF16) for p in pairs]
    g_b = [_bdot_nt(ar[p], bd(b[p] * e_neg[p])) for p in pairs]
    g_k = [_bdot_nt(ar[p], bd(keff[p] * e_neg[p])) for p in pairs]
    n_mat = [jnp.where(earlier[p], g_b[p][:cs], 0.0) for p in pairs]
    m_rb = [jnp.where(upto[p], g_b[p][cs:], 0.0) for p in pairs]
    a_ak = [jnp.where(earlier[p], g_k[p][:cs], 0.0) for p in pairs]
    m_rk = [jnp.where(upto[p], g_k[p][cs:], 0.0) for p in pairs]
    bd_s = lambda xs: (bd(xs[0]), bd(xs[1]))
    n_s = [_split2(n) for n in n_mat]
    x = [eye + n for n in n_mat]
    pk = [_mm3s(ns, bd_s(ns)) for ns in n_s]
    for _ in range(RWKV_DOUBLINGS - 1):
        xp = [_mm3s(_split2(cat(x[p], pk[p])), bd_s(_split2(pk[p]))) for p in pairs]
        x = [x[p] + xp[p][:cs] for p in pairs]
        pk = [r[cs:] for r in xp]
    t_inv = [(x[p] + _mm3s(_split2(x[p]), bd_s(_split2(pk[p])))).astype(BF16) for p in pairs]
    av = [_bdot(cat(a_ak[p], m_rk[p]), bd(v[p])) for p in pairs]
    a_hat = [_bdot(t_inv[p], bd(a_t[p])) for p in pairs]
    u0 = [_bdot(t_inv[p], bd(av[p][:cs])) for p in pairs]
    s_prev = [st[d, p] for d, p in chains]
    hs = [_bdot_nt(cat(a_hat[p], r_t[p]), s_prev[p]) for p in pairs]
    u = [hs[p][:cs] + u0[p] for p in pairs]
    y = [hs[p][cs:] + _bdot(m_rb[p], bd(u[p])) + av[p][cs:] for p in pairs]
    for i, (d, p) in enumerate(chains):
        y_refs[d][:, sls[i]] = y[i]
        bk_end = cat(b[i] * e_end[i], keff[i] * e_end[i])
        st[d, p] = s_prev[i] * w_end[i] + jnp.where(same_head, _bdot_tn(cat(u[i], v[i]), bk_end), 0.0)

    @pl.when(c == pl.num_programs(1) - 1)
    def _():
        sf_ref[0] = st[...]


def _rwkv_scan(xr, xk, xv, lora, w0, w_up, a0, a_up, k_k, k_a, state0, seq):
    n = xr.shape[0]
    nb = n // seq
    nc = seq // RWKV_CHUNK

    fwd_idx = lambda b, c: (b * nc + c, 0)
    bwd_idx = lambda b, c: (b * nc + nc - 1 - c, 0)
    blk_f = pl.BlockSpec((RWKV_CHUNK, D_GROUP), fwd_idx)
    blk_b = pl.BlockSpec((RWKV_CHUNK, D_GROUP), bwd_idx)
    lblk_f = pl.BlockSpec((RWKV_CHUNK, lora.shape[1]), fwd_idx)
    lblk_b = pl.BlockSpec((RWKV_CHUNK, lora.shape[1]), bwd_idx)
    dvec = pl.BlockSpec((2, 1, D_GROUP), lambda b, c: (0, 0, 0))
    dmat = pl.BlockSpec((2, D_LORA, D_GROUP), lambda b, c: (0, 0, 0))
    vec = pl.BlockSpec((1, D_GROUP), lambda b, c: (0, 0))
    npair, pw = N_HEADS // 2, 2 * HEAD_DIM
    st_blk = pl.BlockSpec((1, 2, npair, pw, pw), lambda b, c: (b, 0, 0, 0, 0))
    s0 = state0.reshape(nb, 2, npair, 2, HEAD_DIM, HEAD_DIM)
    zero = jnp.zeros_like(s0[:, :, :, 0])
    s0_bd = jnp.concatenate([jnp.concatenate([s0[:, :, :, 0], zero], axis=-1),
                             jnp.concatenate([zero, s0[:, :, :, 1]], axis=-1)], axis=-2)
    y_f, y_b, sf_bd = pl.pallas_call(
        _rwkv_kernel,
        grid=(nb, nc),
        in_specs=[blk_f, blk_f, blk_f, lblk_f, blk_b, blk_b, blk_b, lblk_b,
                  dvec, dmat, dvec, dmat, vec, vec, st_blk],
        out_specs=[blk_f, blk_b, st_blk],
        out_shape=[jax.ShapeDtypeStruct((n, D_GROUP), F32), jax.ShapeDtypeStruct((n, D_GROUP), F32),
                   jax.ShapeDtypeStruct((nb, 2, npair, pw, pw), F32)],
        scratch_shapes=[pltpu.VMEM((2, npair, pw, pw), F32)],
        compiler_params=_cparams("arbitrary", "arbitrary"),
        name="rwkv7_scan",
    )(xr, xk, xv, lora, xr, xk, xv, lora, w0.reshape(2, 1, D_GROUP), w_up, a0.reshape(2, 1, D_GROUP), a_up,
      k_k.reshape(1, D_GROUP), k_a.reshape(1, D_GROUP), s0_bd)
    sf = jnp.stack([sf_bd[..., :HEAD_DIM, :HEAD_DIM], sf_bd[..., HEAD_DIM:, HEAD_DIM:]], axis=3)
    return y_f, y_b, sf.reshape(nb, 2, N_HEADS, HEAD_DIM, HEAD_DIM)


def _rec_combine_kernel(oret_ref, rg_ref, yf_ref, yb_ref, xr_ref, xk_ref, xv_ref, lo_ref, rk_ref, lng_ref, lnb_ref,
                        gup_ref, oret_out, orw_out):
    o = oret_ref[0] + oret_ref[1]
    y = yf_ref[...] + yb_ref[...]
    g_ret = _silu(rg_ref[...])
    xr = xr_ref[...]
    xk = xk_ref[...]
    xv = xv_ref[...]
    rk = rk_ref[...]
    lng = lng_ref[...]
    lnb = lnb_ref[...]
    g_rw = _bdot(_sigmoid(lo_ref[:, 2 * D_LORA:]), gup_ref[...])
    for h in range(N_HEADS):
        sl = slice(h * HEAD_DIM, (h + 1) * HEAD_DIM)
        oret_out[:, sl] = g_ret[:, sl] * _rms(o[:, sl])
        yh = y[:, sl]
        mu = jnp.mean(yh, axis=-1, keepdims=True)
        var = jnp.mean(jnp.square(yh - mu), axis=-1, keepdims=True)
        yn = (yh - mu) * lax.rsqrt(var + RWKV_GN_EPS) * lng[:, sl] + lnb[:, sl]
        bonus = jnp.sum(xr[:, sl] * rk[:, sl] * xk[:, sl], axis=-1, keepdims=True) * xv[:, sl]
        orw_out[:, sl] = (yn + bonus) * g_rw[:, sl]


def _rec_combine(o_ret2, rg, y_f, y_b, xr, xk, xv, lora, r_k, ln_g, ln_b, g_up, tm=256):
    n = rg.shape[0]
    blk = pl.BlockSpec((tm, D_GROUP), lambda i: (i, 0))
    blk2 = pl.BlockSpec((2, tm, D_GROUP), lambda i: (0, i, 0))
    vec = pl.BlockSpec((1, D_GROUP), lambda i: (0, 0))
    return pl.pallas_call(
        _rec_combine_kernel,
        grid=(n // tm,),
        in_specs=[blk2, blk, blk, blk, blk, blk, blk, pl.BlockSpec((tm, lora.shape[1]), lambda i: (i, 0)),
                  vec, vec, vec, pl.BlockSpec(g_up.shape, lambda i: (0, 0))],
        out_specs=[blk, blk],
        out_shape=[jax.ShapeDtypeStruct((n, D_GROUP), F32)] * 2,
        compiler_params=_cparams("arbitrary"),
        name="rec_combine",
    )(o_ret2, rg, y_f, y_b, xr, xk, xv, lora, r_k.reshape(1, D_GROUP), ln_g.reshape(1, D_GROUP),
      ln_b.reshape(1, D_GROUP), g_up)


ATTN_WIDTHS = (D_GROUP,) * 6
REC_WIDTHS = (D_GROUP,) * 7 + (2 * D_LORA + D_LORA_G,)


def kernel(x_prompt, x_sample, cache_na_k, cache_na_v, cache_diff_k, cache_diff_v, state_ret, state_rwkv, c, c_ctx, norm_mix_g, norm_ffn_g, norm_final_g, w_ada, b_ada, w_in_attn, w_out_attn, na_rpb, diff_lq1, diff_lk1, diff_lq2, diff_lk2, w_in_rec, w_out_rec, ret_decay_logit, rw_w0, rw_w_up, rw_a0, rw_a_up, rw_g_up, rw_k_k, rw_k_a, rw_r_k, rw_ln_g, rw_ln_b, w_ffn_in, w_ffn_out):
    bp, seq, d = x_prompt.shape
    bl, n_lat, _ = x_sample.shape
    depth = w_ada.shape[0]
    ctx = x_prompt.reshape(bp * seq, d)
    lat = x_sample.reshape(bl * n_lat, d)

    cond8 = jnp.zeros((8, d), F32).at[0].set(c_ctx).at[1:1 + bl].set(c)
    mods = _ada_modulation(cond8, w_ada, b_ada).reshape(depth, 8, 6, d)

    outs = {}
    for layer in range(depth):
        m_ctx = [mods[layer, 0:1, j].reshape(1, 1, d) for j in range(6)]
        m_lat = [mods[layer, 1:1 + bl, j].reshape(bl, 1, d) for j in range(6)]
        if layer % 2 == 0:
            i = layer // 2
            lam_init = 0.8 - 0.6 * math.exp(-0.3 * layer)
            lam_params = jnp.stack([diff_lq1[i], diff_lk1[i], diff_lq2[i], diff_lk2[i]]).astype(F32)
            w_in = w_in_attn[i].astype(BF16)
            w_out = w_out_attn[i].astype(BF16)
            pc = _norm_mod_matmul(ctx, norm_mix_g[layer], m_ctx[0], m_ctx[1], w_in, ATTN_WIDTHS, bp * seq, 256)
            plat = _norm_mod_matmul(lat, norm_mix_g[layer], m_lat[0], m_lat[1], w_in, ATTN_WIDTHS, n_lat, 256)
            mix_ctx = _ctx_attention(lam_params, *pc, seq, lam_init)
            outs["na_k"], outs["na_v"], outs["df_k"], outs["df_v"] = pc[1], pc[2], pc[4], pc[5]
            past = cache_na_k.shape[2]
            tt = _na_bias_table(na_rpb[i])
            o_na = _lat_na_attention(plat[0], plat[1], plat[2],
                                     cache_na_k[:, i].reshape(bl, past, D_GROUP),
                                     cache_na_v[:, i].reshape(bl, past, D_GROUP), tt, n_lat)
            o_df = _lat_diff_attention(lam_params, plat[3], plat[4], plat[5],
                                       cache_diff_k[:, i].reshape(bl, past, D_GROUP),
                                       cache_diff_v[:, i].reshape(bl, past, D_GROUP), n_lat, lam_init)
            mix_lat = (o_na, o_df)
        else:
            j = layer // 2
            w_in = w_in_rec[j].astype(BF16)
            w_out = w_out_rec[j].astype(BF16)
            pc = _norm_mod_matmul(ctx, norm_mix_g[layer], m_ctx[0], m_ctx[1], w_in, REC_WIDTHS, bp * seq, 256)
            plat = _norm_mod_matmul(lat, norm_mix_g[layer], m_lat[0], m_lat[1], w_in, REC_WIDTHS, n_lat, 256)
            zero_state = jnp.zeros((bp, 2, N_HEADS, HEAD_DIM, HEAD_DIM), F32)
            mixes = []
            for is_ctx, p, s_ret0, s_rw0, sq in ((True, pc, zero_state, zero_state, seq),
                                                 (False, plat, state_ret[:, j], state_rwkv[:, j], n_lat)):
                rq, rk, rv, rg, wr, wk, wv, lora = p
                o_ret2, s_ret = _retention_scan(ret_decay_logit[j], rq, rk, rv, s_ret0, sq)
                y_f, y_b, s_rw = _rwkv_scan(wr, wk, wv, lora, rw_w0[j], rw_w_up[j].astype(BF16), rw_a0[j],
                                            rw_a_up[j].astype(BF16), rw_k_k[j], rw_k_a[j], s_rw0, sq)
                mixes.append(_rec_combine(o_ret2, rg, y_f, y_b, wr, wk, wv, lora, rw_r_k[j], rw_ln_g[j],
                                          rw_ln_b[j], rw_g_up[j].astype(BF16)))
                if is_ctx:
                    outs["ret"], outs["rwkv"] = s_ret, s_rw
            mix_ctx, mix_lat = mixes
        final = layer == depth - 1
        w_fi = w_ffn_in[layer].astype(BF16)
        w_fo = w_ffn_out[layer].astype(BF16)
        ctx = _layer_tail(ctx, mix_ctx[0], mix_ctx[1], w_out, m_ctx[2], norm_ffn_g[layer], m_ctx[3], m_ctx[4],
                          m_ctx[5], w_fi, w_fo, norm_final_g, bp * seq, final)
        lat = _layer_tail(lat, mix_lat[0], mix_lat[1], w_out, m_lat[2], norm_ffn_g[layer], m_lat[3], m_lat[4],
                          m_lat[5], w_fi, w_fo, norm_final_g, n_lat, final)

    y_prompt = ctx.reshape(bp, seq, d)
    y_sample = lat.reshape(bl, n_lat, d)
    new_cache_na_k = outs["na_k"].reshape(bp, 1, seq, N_HEADS, HEAD_DIM)
    new_cache_na_v = outs["na_v"].reshape(bp, 1, seq, N_HEADS, HEAD_DIM)
    new_cache_diff_k = outs["df_k"].reshape(bp, 1, seq, N_HEADS // 2, 2, HEAD_DIM)
    new_cache_diff_v = outs["df_v"].reshape(bp, 1, seq, N_HEADS // 2, 2 * HEAD_DIM)
    new_state_ret = outs["ret"].reshape(bp, 1, 2, N_HEADS, HEAD_DIM, HEAD_DIM)
    new_state_rwkv = outs["rwkv"].reshape(bp, 1, 2, N_HEADS, HEAD_DIM, HEAD_DIM)
    return (y_prompt, y_sample, new_cache_na_k, new_cache_na_v, new_cache_diff_k, new_cache_diff_v,
            new_state_ret, new_state_rwkv)
```

```python
import functools
import math

import jax
import jax.numpy as jnp
from jax import lax
from jax.experimental import pallas as pl
from jax.experimental.pallas import tpu as pltpu

F32 = jnp.float32
BF16 = jnp.bfloat16

D_MODEL = 1024
HEAD_DIM = 64
N_HEADS = 8
D_GROUP = 512
GRID_W = 64
GRID_ROWS = 32
WIN_R = 8
WIN_W = 16
RPB_R = 15
RPB_C = 31
PAST_LEN = 256
D_FF = 2816
D_LORA = 64
D_LORA_G = 128
ROPE_BASE = 10000.0
RMS_EPS = 1e-6
RWKV_GN_EPS = 64e-5
NEG_INF = -1e30
ATTN_SCALE = HEAD_DIM ** -0.5
RET_CHUNK = 128
RWKV_CHUNK = 64
RWKV_DOUBLINGS = 5
VMEM_LIMIT = 56 * 1024 * 1024


def _cparams(*sem):
    return pltpu.CompilerParams(dimension_semantics=sem, vmem_limit_bytes=VMEM_LIMIT)


def _sigmoid(x):
    return 1.0 / (1.0 + jnp.exp(-x))


def _silu(x):
    return x * _sigmoid(x)


def _rms(x, eps=RMS_EPS):
    return x * lax.rsqrt(jnp.mean(x * x, axis=-1, keepdims=True) + eps)


def _bdot(a, b):
    return jnp.dot(a.astype(BF16), b.astype(BF16), preferred_element_type=F32)


def _bdot_nt(a, b):
    return lax.dot_general(a.astype(BF16), b.astype(BF16), (((1,), (1,)), ((), ())),
                           preferred_element_type=F32)


def _bdot_tn(a, b):
    return lax.dot_general(a.astype(BF16), b.astype(BF16), (((0,), (0,)), ((), ())),
                           preferred_element_type=F32)


_DN = {"nn": (((1,), (0,)), ((), ())), "nt": (((1,), (1,)), ((), ())), "tn": (((0,), (0,)), ((), ()))}


def _split2(x):
    hi = x.astype(BF16)
    lo = (x - hi.astype(F32)).astype(BF16)
    return hi, lo


def _mm3s(a_split, b_split, mode="nn"):
    ah, al = a_split
    bh, bl = b_split
    d = functools.partial(lax.dot_general, dimension_numbers=_DN[mode], preferred_element_type=F32)
    return d(ah, bh) + (d(ah, bl) + d(al, bh))


def _mm3(a, b, mode="nn"):
    return _mm3s(_split2(a), _split2(b), mode)


def _softmax_rows(s):
    m = jnp.max(s, axis=-1, keepdims=True)
    e = jnp.exp(s - m)
    return e / jnp.sum(e, axis=-1, keepdims=True)


def _diff_lambda(lp, lam_init):
    s1 = jnp.sum(lp[0:1, :] * lp[1:2, :], axis=-1, keepdims=True)
    s2 = jnp.sum(lp[2:3, :] * lp[3:4, :], axis=-1, keepdims=True)
    return jnp.exp(s1) - jnp.exp(s2) + lam_init


def _ada_kernel(c_ref, w_ref, b_ref, o_ref):
    s = _silu(c_ref[...])
    o_ref[0] = _bdot(s, w_ref[0]) + b_ref[0]


def _ada_modulation(cond8, w_ada, b_ada):
    depth, d, n = w_ada.shape
    tn = 1536
    return pl.pallas_call(
        _ada_kernel,
        grid=(depth, n // tn),
        in_specs=[pl.BlockSpec((8, d), lambda l, j: (0, 0)),
                  pl.BlockSpec((1, d, tn), lambda l, j: (l, 0, j)),
                  pl.BlockSpec((1, 1, tn), lambda l, j: (l, 0, j))],
        out_specs=pl.BlockSpec((1, 8, tn), lambda l, j: (l, 0, j)),
        out_shape=jax.ShapeDtypeStruct((depth, 8, n), F32),
        compiler_params=_cparams("arbitrary", "arbitrary"),
        name="ada_modulation",
    )(cond8, w_ada, b_ada.reshape(depth, 1, n))


def _nmm_kernel(x_ref, g_ref, sh_ref, sc_ref, w_ref, *o_refs, widths):
    h = _rms(x_ref[...]) * g_ref[...]
    h = h * (1.0 + sc_ref[0]) + sh_ref[0]
    hb = h.astype(BF16)
    off = 0
    for o_ref, w in zip(o_refs, widths):
        o_ref[...] = jnp.dot(hb, w_ref[:, off:off + w], preferred_element_type=F32)
        off += w


def _norm_mod_matmul(x, gain, shift, scale, w_bf16, widths, rows_per_mod, tm):
    n, d = x.shape
    tiles_per_mod = rows_per_mod // tm
    mod_spec = pl.BlockSpec((1, 1, d), lambda i: (i // tiles_per_mod, 0, 0))
    return pl.pallas_call(
        functools.partial(_nmm_kernel, widths=widths),
        grid=(n // tm,),
        in_specs=[pl.BlockSpec((tm, d), lambda i: (i, 0)),
                  pl.BlockSpec((1, d), lambda i: (0, 0)),
                  mod_spec, mod_spec,
                  pl.BlockSpec(w_bf16.shape, lambda i: (0, 0))],
        out_specs=[pl.BlockSpec((tm, w), lambda i: (i, 0)) for w in widths],
        out_shape=[jax.ShapeDtypeStruct((n, w), F32) for w in widths],
        compiler_params=_cparams("arbitrary"),
        name="norm_mod_matmul",
    )(x, gain.reshape(1, d), shift, scale, w_bf16)


def _exp_rows(s):
    e = jnp.exp(s - jnp.max(s, axis=-1, keepdims=True))
    return e, 1.0 / jnp.sum(e, axis=-1, keepdims=True)


def _ctx_attn_kernel(lp_ref, qn_ref, kn_ref, vn_ref, qd_ref, kd_ref, vd_ref, on_ref, od_ref, *, lam_init):
    heads = range(N_HEADS)
    hsl = [slice(h * HEAD_DIM, (h + 1) * HEAD_DIM) for h in heads]
    qn = (qn_ref[...] * ATTN_SCALE).astype(BF16)
    qd = (qd_ref[...] * ATTN_SCALE).astype(BF16)
    kn = kn_ref[...].astype(BF16)
    kd = kd_ref[...].astype(BF16)
    vn = vn_ref[...].astype(BF16)
    vd = vd_ref[...].astype(BF16)
    lam = _diff_lambda(lp_ref[...], lam_init)
    sn = [_bdot_nt(qn[:, sl], kn[:, sl]) for sl in hsl]
    sd = [_bdot_nt(qd[:, sl], kd[:, sl]) for sl in hsl]
    en = [_exp_rows(s) for s in sn]
    ed = [_exp_rows(s) for s in sd]
    for h in heads:
        e, inv = en[h]
        on_ref[:, hsl[h]] = jnp.dot((e * inv).astype(BF16), vn[:, hsl[h]], preferred_element_type=F32)
    for h in range(N_HEADS // 2):
        (e1, inv1), (e2, inv2) = ed[2 * h], ed[2 * h + 1]
        sv = slice(2 * h * HEAD_DIM, (2 * h + 2) * HEAD_DIM)
        p = e1 * inv1 - e2 * (lam * inv2)
        o = jnp.dot(p.astype(BF16), vd[:, sv], preferred_element_type=F32)
        od_ref[:, sv] = _rms(o) * (1.0 - lam_init)


def _ctx_attention(lam_params, qn, kn, vn, qd, kd, vd, seq, lam_init):
    n = qn.shape[0]
    blk = pl.BlockSpec((seq, D_GROUP), lambda b: (b, 0))
    return pl.pallas_call(
        functools.partial(_ctx_attn_kernel, lam_init=lam_init),
        grid=(n // seq,),
        in_specs=[pl.BlockSpec(lam_params.shape, lambda b: (0, 0))] + [blk] * 6,
        out_specs=[blk, blk],
        out_shape=[jax.ShapeDtypeStruct((n, D_GROUP), F32)] * 2,
        compiler_params=_cparams("arbitrary"),
        name="ctx_attention",
    )(lam_params, qn, kn, vn, qd, kd, vd)


def _rope_tables(n):
    quarter = HEAD_DIM // 4
    pos = jnp.arange(n)
    inv_freq = ROPE_BASE ** (-jnp.arange(quarter, dtype=F32) / quarter)
    lane = jnp.arange(HEAD_DIM)
    p = jnp.where(lane[None, :] < HEAD_DIM // 2, (pos // GRID_W)[:, None], (pos % GRID_W)[:, None]).astype(F32)
    ang = p * inv_freq[lane % quarter][None, :]
    sign = jnp.where((lane % (2 * quarter)) < quarter, -1.0, 1.0).astype(F32)
    cos = jnp.tile(jnp.cos(ang), (1, N_HEADS))
    sin = jnp.tile(jnp.sin(ang) * sign[None, :], (1, N_HEADS))
    return cos, sin


def _rope(x, cos, sin_signed):
    quarter = HEAD_DIM // 4
    width = x.shape[-1]
    lane = lax.broadcasted_iota(jnp.int32, x.shape, 1)
    first = (lane % (2 * quarter)) < quarter
    partner = jnp.where(first, pltpu.roll(x, width - quarter, 1), pltpu.roll(x, quarter, 1))
    return x * cos + partner * sin_signed


def _lat_diff_kernel(lp_ref, q_ref, k_ref, v_ref, ck_ref, cv_ref, cosq_ref, sinq_ref, cosk_ref, sink_ref,
                     o_ref, kall, vall, *, lam_init, n_lat):
    @pl.when(pl.program_id(1) == 0)
    def _():
        kall[0:n_lat, :] = _rope(k_ref[...], cosk_ref[...], sink_ref[...]).astype(BF16)
        kall[n_lat:, :] = ck_ref[...].astype(BF16)
        vall[0:n_lat, :] = v_ref[...].astype(BF16)
        vall[n_lat:, :] = cv_ref[...].astype(BF16)

    lam = _diff_lambda(lp_ref[...], lam_init)
    q = (_rope(q_ref[...], cosq_ref[...], sinq_ref[...]) * ATTN_SCALE).astype(BF16)
    for h in range(N_HEADS // 2):
        s1 = slice(2 * h * HEAD_DIM, (2 * h + 1) * HEAD_DIM)
        s2 = slice((2 * h + 1) * HEAD_DIM, (2 * h + 2) * HEAD_DIM)
        sv = slice(2 * h * HEAD_DIM, (2 * h + 2) * HEAD_DIM)
        e1, inv1 = _exp_rows(_bdot_nt(q[:, s1], kall[:, s1]))
        e2, inv2 = _exp_rows(_bdot_nt(q[:, s2], kall[:, s2]))
        p = e1 * inv1 - e2 * (lam * inv2)
        o = jnp.dot(p.astype(BF16), vall[:, sv], preferred_element_type=F32)
        o_ref[:, sv] = _rms(o) * (1.0 - lam_init)


def _lat_diff_attention(lam_params, qd, kd, vd, cache_k, cache_v, n_lat, lam_init, tq=256):
    n = qd.shape[0]
    nb = n // n_lat
    nq = n_lat // tq
    past = cache_k.shape[1]
    cos, sin = _rope_tables(n_lat)
    qblk = pl.BlockSpec((tq, D_GROUP), lambda b, i: (b * nq + i, 0))
    kvblk = pl.BlockSpec((n_lat, D_GROUP), lambda b, i: (b, 0))
    cblk = pl.BlockSpec((None, past, D_GROUP), lambda b, i: (b, 0, 0))
    return pl.pallas_call(
        functools.partial(_lat_diff_kernel, lam_init=lam_init, n_lat=n_lat),
        grid=(nb, nq),
        in_specs=[pl.BlockSpec(lam_params.shape, lambda b, i: (0, 0)),
                  qblk, kvblk, kvblk, cblk, cblk,
                  pl.BlockSpec((tq, D_GROUP), lambda b, i: (i, 0)),
                  pl.BlockSpec((tq, D_GROUP), lambda b, i: (i, 0)),
                  pl.BlockSpec((n_lat, D_GROUP), lambda b, i: (0, 0)),
                  pl.BlockSpec((n_lat, D_GROUP), lambda b, i: (0, 0))],
        out_specs=qblk,
        out_shape=jax.ShapeDtypeStruct((n, D_GROUP), F32),
        scratch_shapes=[pltpu.VMEM((n_lat + past, D_GROUP), BF16),
                        pltpu.VMEM((n_lat + past, D_GROUP), BF16)],
        compiler_params=_cparams("arbitrary", "arbitrary"),
        name="lat_diff_attention",
    )(lam_params, qd, kd, vd, cache_k, cache_v, cos, sin, cos, sin)


def _na_bias_kernel(rpb_ref, tt_ref):
    h = pl.program_id(0)
    lane = lax.broadcasted_iota(jnp.int32, (GRID_W, 2 * GRID_W), 1)
    qc = lax.broadcasted_iota(jnp.int32, (GRID_W, 2 * GRID_W), 0)
    dc = jnp.clip(lane % GRID_W - qc + (WIN_W - 1), 0, RPB_C - 1)
    first = lane < GRID_W
    for dr in range(RPB_R - 1):
        acc = jnp.zeros((GRID_W, 2 * GRID_W), F32)
        for d in range(RPB_C):
            val = jnp.where(first, rpb_ref[h * RPB_R + dr, d], rpb_ref[h * RPB_R + dr + 1, d])
            acc = jnp.where(dc == d, val, acc)
        tt_ref[0, dr] = acc


def _na_bias_table(rpb):
    return pl.pallas_call(
        _na_bias_kernel,
        grid=(N_HEADS,),
        in_specs=[pl.BlockSpec(memory_space=pltpu.SMEM)],
        out_specs=pl.BlockSpec((1, RPB_R - 1, GRID_W, 2 * GRID_W), lambda h: (h, 0, 0, 0)),
        out_shape=jax.ShapeDtypeStruct((N_HEADS, RPB_R - 1, GRID_W, 2 * GRID_W), F32),
        compiler_params=_cparams("arbitrary"),
        name="na_bias_table",
    )(rpb.reshape(N_HEADS * RPB_R, RPB_C))


def _lat_na_kernel(q_ref, k_ref, v_ref, ck_ref, cv_ref, tt_ref, o_ref):
    r = pl.program_id(1)
    r_start = jnp.clip(r - WIN_R // 2, 0, GRID_ROWS - WIN_R)
    dr0 = r_start - r + (WIN_R - 1)
    row0 = pl.multiple_of(r_start * GRID_W, GRID_W)
    nkeys = WIN_R * GRID_W
    past = ck_ref.shape[0]
    lane = lax.broadcasted_iota(jnp.int32, (GRID_W, nkeys + past), 1)
    kc = lane % GRID_W
    qc = lax.broadcasted_iota(jnp.int32, (GRID_W, nkeys + past), 0)
    c_start = jnp.clip(qc - WIN_W // 2, 0, GRID_W - WIN_W)
    visible = ((kc >= c_start) & (kc < c_start + WIN_W)) | (lane >= nkeys)
    no_bias = jnp.zeros((GRID_W, past), F32)
    heads = range(N_HEADS)
    hsl = [slice(h * HEAD_DIM, (h + 1) * HEAD_DIM) for h in heads]
    q = (q_ref[...] * ATTN_SCALE).astype(BF16)
    kcat = jnp.concatenate([k_ref[pl.ds(row0, nkeys), :], ck_ref[...]], axis=0).astype(BF16)
    vcat = jnp.concatenate([v_ref[pl.ds(row0, nkeys), :], cv_ref[...]], axis=0).astype(BF16)
    bias = [jnp.concatenate([tt_ref[h, dr0 + 2 * j] for j in range(WIN_R // 2)] + [no_bias], axis=1)
            for h in heads]
    s = [jnp.where(visible, _bdot_nt(q[:, hsl[h]], kcat[:, hsl[h]]) + bias[h], NEG_INF) for h in heads]
    e = [_exp_rows(x) for x in s]
    for h in heads:
        o_ref[:, hsl[h]] = jnp.dot((e[h][0] * e[h][1]).astype(BF16), vcat[:, hsl[h]],
                                   preferred_element_type=F32)


def _lat_na_attention(qn, kn, vn, cache_k, cache_v, tt, n_lat):
    n = qn.shape[0]
    nb = n // n_lat
    past = cache_k.shape[1]
    qblk = pl.BlockSpec((GRID_W, D_GROUP), lambda b, r: (b * GRID_ROWS + r, 0))
    kvblk = pl.BlockSpec((n_lat, D_GROUP), lambda b, r: (b, 0))
    cblk = pl.BlockSpec((None, past, D_GROUP), lambda b, r: (b, 0, 0))
    return pl.pallas_call(
        _lat_na_kernel,
        grid=(nb, GRID_ROWS),
        in_specs=[qblk, kvblk, kvblk, cblk, cblk,
                  pl.BlockSpec(tt.shape, lambda b, r: (0, 0, 0, 0))],
        out_specs=qblk,
        out_shape=jax.ShapeDtypeStruct((n, D_GROUP), F32),
        compiler_params=_cparams("arbitrary", "arbitrary"),
        name="lat_na_attention",
    )(qn, kn, vn, cache_k, cache_v, tt)


def _tail_kernel(x_ref, ma_ref, mb_ref, wo_ref, gm_ref, g_ref, sh_ref, sc_ref, gf_ref, wi_ref, wf_ref, fg_ref,
                 o_ref, *, final, ff_chunk):
    mix = _bdot(ma_ref[...], wo_ref[0:D_GROUP, :]) + _bdot(mb_ref[...], wo_ref[D_GROUP:, :])
    x1 = x_ref[...] + gm_ref[0] * mix
    h = _rms(x1) * g_ref[...]
    hb = (h * (1.0 + sc_ref[0]) + sh_ref[0]).astype(BF16)
    acc = jnp.zeros_like(x1)
    for c0 in range(0, D_FF, ff_chunk):
        gate = jnp.dot(hb, wi_ref[:, c0:c0 + ff_chunk], preferred_element_type=F32)
        up = jnp.dot(hb, wi_ref[:, D_FF + c0:D_FF + c0 + ff_chunk], preferred_element_type=F32)
        acc = acc + jnp.dot((_silu(gate) * up).astype(BF16), wf_ref[c0:c0 + ff_chunk, :],
                            preferred_element_type=F32)
    x2 = x1 + gf_ref[0] * acc
    if final:
        x2 = _rms(x2) * fg_ref[...]
    o_ref[...] = x2


def _layer_tail(x, mix_a, mix_b, w_out, gate_mix, gain, shift, scale, gate_ffn, w_in, w_ffn_out, final_gain,
                rows_per_mod, final, tm=256, ff_chunk=1408):
    n, d = x.shape
    tiles_per_mod = rows_per_mod // tm
    mod_spec = pl.BlockSpec((1, 1, d), lambda i: (i // tiles_per_mod, 0, 0))
    vec_spec = pl.BlockSpec((1, d), lambda i: (0, 0))
    resident = lambda a: pl.BlockSpec(a.shape, lambda i: (0, 0), pipeline_mode=pl.Buffered(1))
    return pl.pallas_call(
        functools.partial(_tail_kernel, final=final, ff_chunk=ff_chunk),
        grid=(n // tm,),
        in_specs=[pl.BlockSpec((tm, d), lambda i: (i, 0)),
                  pl.BlockSpec((tm, D_GROUP), lambda i: (i, 0)),
                  pl.BlockSpec((tm, D_GROUP), lambda i: (i, 0)),
                  resident(w_out), mod_spec, vec_spec, mod_spec, mod_spec, mod_spec,
                  resident(w_in), resident(w_ffn_out), vec_spec],
        out_specs=pl.BlockSpec((tm, d), lambda i: (i, 0)),
        out_shape=jax.ShapeDtypeStruct((n, d), F32),
        compiler_params=_cparams("arbitrary"),
        name="layer_tail",
    )(x, mix_a, mix_b, w_out, gate_mix, gain.reshape(1, d), shift, scale, gate_ffn, w_in, w_ffn_out,
      final_gain.reshape(1, d))


def _log_sigmoid(x):
    return jnp.minimum(x, 0.0) - jnp.log(1.0 + jnp.exp(-jnp.abs(x)))


def _pair_block_diag(xp):
    left = lax.broadcasted_iota(jnp.int32, xp.shape, 1) < HEAD_DIM
    zero = jnp.zeros_like(xp)
    return jnp.concatenate([jnp.where(left, xp, zero), jnp.where(left, zero, xp)], axis=0)


def _states_to_pairs(state):
    nb = state.shape[0]
    s = state.reshape(nb, 2, N_HEADS // 2, 2, HEAD_DIM, HEAD_DIM)
    zero = jnp.zeros_like(s[:, :, :, 0])
    return jnp.concatenate([jnp.concatenate([s[:, :, :, 0], zero], axis=-1),
                            jnp.concatenate([zero, s[:, :, :, 1]], axis=-1)], axis=-2)


def _pairs_to_states(sbd):
    nb = sbd.shape[0]
    s = jnp.stack([sbd[..., :HEAD_DIM, :HEAD_DIM], sbd[..., HEAD_DIM:, HEAD_DIM:]], axis=3)
    return s.reshape(nb, 2, N_HEADS, HEAD_DIM, HEAD_DIM)


def _ret_kernel(dlm_ref, dlq_ref, qf_ref, kf_ref, vf_ref, qb_ref, kb_ref, vb_ref, s0_ref,
                of_ref, ob_ref, sf_ref, st):
    c = pl.program_id(1)
    cs = RET_CHUNK
    pw = 2 * HEAD_DIM
    npair = N_HEADS // 2

    @pl.when(c == 0)
    def _():
        st[...] = s0_ref[0]

    row = lax.broadcasted_iota(jnp.int32, (cs, 2 * cs), 0)
    col = lax.broadcasted_iota(jnp.int32, (cs, 2 * cs), 1) % cs
    tok = lax.broadcasted_iota(jnp.int32, (cs, pw), 0)
    brow = lax.broadcasted_iota(jnp.int32, (pw, pw), 0) // HEAD_DIM
    bcol = lax.broadcasted_iota(jnp.int32, (pw, pw), 1) // HEAD_DIM
    same_head = brow == bcol
    refs = [(qf_ref, kf_ref, vf_ref, of_ref), (qb_ref, kb_ref, vb_ref, ob_ref)]
    chains = [(d, p) for d in range(2) for p in range(npair)]
    dist = [row - col, col - row]
    tq = [tok.astype(F32), (cs - 1 - tok).astype(F32)]
    sls = [slice(p * pw, (p + 1) * pw) for _, p in chains]
    lgm = [_log_sigmoid(dlm_ref[d, p])[0:1, :] for d, p in chains]
    lgq = [_log_sigmoid(dlq_ref[d, p])[0:1, :] for d, p in chains]
    mask = [jnp.where(dist[d] >= 0, jnp.exp(jnp.maximum(dist[d], 0).astype(F32) * lgm[i]), 0.0)
            for i, (d, _) in enumerate(chains)]
    q = [refs[d][0][:, sls[i]] for i, (d, _) in enumerate(chains)]
    k = [refs[d][1][:, sls[i]] * ATTN_SCALE for i, (d, _) in enumerate(chains)]
    v = [refs[d][2][:, sls[i]] for i, (d, _) in enumerate(chains)]
    idx = range(len(chains))
    s = [_bdot_nt(q[i], _pair_block_diag(k[i])) * mask[i] for i in idx]
    inner = [_bdot(s[i], _pair_block_diag(v[i])) for i in idx]
    state = [st[d, p] for d, p in chains]
    cross = [_bdot(q[i], state[i]) * jnp.exp((tq[d] + 1.0) * lgq[i]) for i, (d, _) in enumerate(chains)]
    kv = [_bdot_tn(k[i] * jnp.exp((cs - 1.0 - tq[d]) * lgq[i]), v[i]) for i, (d, _) in enumerate(chains)]
    for i, (d, p) in enumerate(chains):
        refs[d][3][:, sls[i]] = inner[i] + cross[i]
        st[d, p] = state[i] * jnp.exp(cs * lgq[i]) + jnp.where(same_head, kv[i], 0.0)

    @pl.when(c == pl.num_programs(1) - 1)
    def _():
        sf_ref[0] = st[...]


def _retention_scan(decay_logit, q, k, v, state0, seq):
    n = q.shape[0]
    nb = n // seq
    nc = seq // RET_CHUNK
    npair, pw = N_HEADS // 2, 2 * HEAD_DIM
    dl = decay_logit.astype(F32)
    dl_m = jnp.broadcast_to(jnp.repeat(dl, RET_CHUNK, axis=1).reshape(2, npair, 1, 2 * RET_CHUNK),
                            (2, npair, 8, 2 * RET_CHUNK))
    dl_q = jnp.broadcast_to(jnp.repeat(dl, HEAD_DIM, axis=1).reshape(2, npair, 1, pw), (2, npair, 8, pw))
    blk_f = pl.BlockSpec((RET_CHUNK, D_GROUP), lambda b, c: (b * nc + c, 0))
    blk_b = pl.BlockSpec((RET_CHUNK, D_GROUP), lambda b, c: (b * nc + nc - 1 - c, 0))
    st_blk = pl.BlockSpec((1, 2, npair, pw, pw), lambda b, c: (b, 0, 0, 0, 0))
    o_f, o_b, sf_bd = pl.pallas_call(
        _ret_kernel,
        grid=(nb, nc),
        in_specs=[pl.BlockSpec(dl_m.shape, lambda b, c: (0, 0, 0, 0)),
                  pl.BlockSpec(dl_q.shape, lambda b, c: (0, 0, 0, 0)),
                  blk_f, blk_f, blk_f, blk_b, blk_b, blk_b, st_blk],
        out_specs=[blk_f, blk_b, st_blk],
        out_shape=[jax.ShapeDtypeStruct((n, D_GROUP), F32), jax.ShapeDtypeStruct((n, D_GROUP), F32),
                   jax.ShapeDtypeStruct((nb, 2, npair, pw, pw), F32)],
        scratch_shapes=[pltpu.VMEM((2, npair, pw, pw), F32)],
        compiler_params=_cparams("arbitrary", "arbitrary"),
        name="retention_scan",
    )(dl_m, dl_q, q, k, v, q, k, v, _states_to_pairs(state0))
    return o_f, o_b, _pairs_to_states(sf_bd)


def _rwkv_kernel(xrf_ref, xkf_ref, xvf_ref, lof_ref, xrb_ref, xkb_ref, xvb_ref, lob_ref,
                 w0_ref, wup_ref, a0_ref, aup_ref, kk_ref, ka_ref, s0_ref, yf_ref, yb_ref, sf_ref, st):
    c = pl.program_id(1)
    cs = RWKV_CHUNK
    npair = N_HEADS // 2

    @pl.when(c == 0)
    def _():
        st[...] = s0_ref[0]

    row = lax.broadcasted_iota(jnp.int32, (cs, cs), 0)
    col = lax.broadcasted_iota(jnp.int32, (cs, cs), 1)
    pw = 2 * HEAD_DIM
    prow = lax.broadcasted_iota(jnp.int32, (cs, pw), 0)
    pcol = lax.broadcasted_iota(jnp.int32, (cs, pw), 1) % HEAD_DIM
    eye = jnp.where(pcol == prow, 1.0, 0.0).astype(F32)
    brow = lax.broadcasted_iota(jnp.int32, (pw, pw), 0) // HEAD_DIM
    bcol = lax.broadcasted_iota(jnp.int32, (pw, pw), 1) // HEAD_DIM
    same_head = brow == bcol
    ones_bd = jnp.where(same_head, 1.0, 0.0).astype(BF16)

    bd = _pair_block_diag

    def direction(d, xr_ref, xk_ref, xv_ref, lo_ref):
        tri = jnp.where((row >= col) if d == 0 else (col >= row), 1.0, 0.0).astype(BF16)
        dist = (prow - pcol) if d == 0 else (pcol - prow)
        xk = xk_ref[...]
        lo = lo_ref[...]
        w_log = -math.exp(-0.5) * _sigmoid(w0_ref[d] + _bdot(jnp.tanh(lo[:, 0:D_LORA]), wup_ref[d]))
        a_all = _sigmoid(a0_ref[d] + _bdot(lo[:, D_LORA:2 * D_LORA], aup_ref[d]))
        w1 = w_log.astype(BF16)
        r1 = w_log - w1.astype(F32)
        w2 = r1.astype(BF16)
        w3 = (r1 - w2.astype(F32)).astype(BF16)
        cum = (jnp.dot(tri, w1, preferred_element_type=F32) + jnp.dot(tri, w2, preferred_element_type=F32)
               + jnp.dot(tri, w3, preferred_element_type=F32))
        cum_end = cum[cs - 1:cs, :] if d == 0 else cum[0:1, :]
        return dict(earlier=dist > 0, upto=dist >= 0, xr=xr_ref[...], xv=xv_ref[...], a_all=a_all,
                    e_incl=jnp.exp(cum), e_excl=jnp.exp(cum - w_log), e_neg=jnp.exp(-cum),
                    e_end=jnp.exp(cum_end - cum), w_end=jnp.exp(cum_end), kk_all=xk * kk_ref[...],
                    keff_all=xk * (1.0 + (a_all - 1.0) * ka_ref[...]))

    dirs = [direction(0, xrf_ref, xkf_ref, xvf_ref, lof_ref), direction(1, xrb_ref, xkb_ref, xvb_ref, lob_ref)]
    y_refs = [yf_ref, yb_ref]

    chains = [(d, p) for d in range(2) for p in range(npair)]
    pairs = range(len(chains))
    sls = [slice(p * pw, (p + 1) * pw) for _, p in chains]
    pick = lambda name: [dirs[d][name][:, sls[i]] for i, (d, _) in enumerate(chains)]
    earlier = [dirs[d]["earlier"] for d, _ in chains]
    upto = [dirs[d]["upto"] for d, _ in chains]
    cat = lambda x, y: jnp.concatenate([x, y], axis=0)

    def head_sum(x):
        x1 = x.astype(BF16)
        r1 = x - x1.astype(F32)
        x2 = r1.astype(BF16)
        x3 = (r1 - x2.astype(F32)).astype(BF16)
        f = functools.partial(jnp.dot, preferred_element_type=F32)
        return f(x1, ones_bd) + (f(x2, ones_bd) + f(x3, ones_bd))

    e_incl, e_excl, e_neg, e_end, w_end = (pick(k) for k in ("e_incl", "e_excl", "e_neg", "e_end", "w_end"))
    kk = [x * lax.rsqrt(head_sum(x * x) + 1e-12) for x in pick("kk_all")]
    b = [kk[p] * a for p, a in zip(pairs, pick("a_all"))]
    keff = pick("keff_all")
    v = pick("xv")
    a_t = [-kk[p] * e_excl[p] for p in pairs]
    r_t = [xr * e_incl[p] for p, xr in zip(pairs, pick("xr"))]
    ar = [cat(a_t[p], r_t[p]).astype(BF16) for p in pairs]
    g_b = [_bdot_nt(ar[p], bd(b[p] * e_neg[p])) for p in pairs]
    g_k = [_bdot_nt(ar[p], bd(keff[p] * e_neg[p])) for p in pairs]
    n_mat = [jnp.where(earlier[p], g_b[p][:cs], 0.0) for p in pairs]
    m_rb = [jnp.where(upto[p], g_b[p][cs:], 0.0) for p in pairs]
    a_ak = [jnp.where(earlier[p], g_k[p][:cs], 0.0) for p in pairs]
    m_rk = [jnp.where(upto[p], g_k[p][cs:], 0.0) for p in pairs]
    bd_s = lambda xs: (bd(xs[0]), bd(xs[1]))
    n_s = [_split2(n) for n in n_mat]
    x = [eye + n for n in n_mat]
    pk = [_mm3s(ns, bd_s(ns)) for ns in n_s]
    for _ in range(RWKV_DOUBLINGS - 1):
        xp = [_mm3s(_split2(cat(x[p], pk[p])), bd_s(_split2(pk[p]))) for p in pairs]
        x = [x[p] + xp[p][:cs] for p in pairs]
        pk = [r[cs:] for r in xp]
    t_inv = [(x[p] + _mm3s(_split2(x[p]), bd_s(_split2(pk[p])))).astype(BF16) for p in pairs]
    av = [_bdot(cat(a_ak[p], m_rk[p]), bd(v[p])) for p in pairs]
    a_hat = [_bdot(t_inv[p], bd(a_t[p])) for p in pairs]
    u0 = [_bdot(t_inv[p], bd(av[p][:cs])) for p in pairs]
    s_prev = [st[d, p] for d, p in chains]
    hs = [_bdot_nt(cat(a_hat[p], r_t[p]), s_prev[p]) for p in pairs]
    u = [hs[p][:cs] + u0[p] for p in pairs]
    y = [hs[p][cs:] + _bdot(m_rb[p], bd(u[p])) + av[p][cs:] for p in pairs]
    for i, (d, p) in enumerate(chains):
        y_refs[d][:, sls[i]] = y[i]
        bk_end = cat(b[i] * e_end[i], keff[i] * e_end[i])
        st[d, p] = s_prev[i] * w_end[i] + jnp.where(same_head, _bdot_tn(cat(u[i], v[i]), bk_end), 0.0)

    @pl.when(c == pl.num_programs(1) - 1)
    def _():
        sf_ref[0] = st[...]


def _rwkv_scan(xr, xk, xv, lora, w0, w_up, a0, a_up, k_k, k_a, state0, seq):
    n = xr.shape[0]
    nb = n // seq
    nc = seq // RWKV_CHUNK

    fwd_idx = lambda b, c: (b * nc + c, 0)
    bwd_idx = lambda b, c: (b * nc + nc - 1 - c, 0)
    blk_f = pl.BlockSpec((RWKV_CHUNK, D_GROUP), fwd_idx)
    blk_b = pl.BlockSpec((RWKV_CHUNK, D_GROUP), bwd_idx)
    lblk_f = pl.BlockSpec((RWKV_CHUNK, lora.shape[1]), fwd_idx)
    lblk_b = pl.BlockSpec((RWKV_CHUNK, lora.shape[1]), bwd_idx)
    dvec = pl.BlockSpec((2, 1, D_GROUP), lambda b, c: (0, 0, 0))
    dmat = pl.BlockSpec((2, D_LORA, D_GROUP), lambda b, c: (0, 0, 0))
    vec = pl.BlockSpec((1, D_GROUP), lambda b, c: (0, 0))
    npair, pw = N_HEADS // 2, 2 * HEAD_DIM
    st_blk = pl.BlockSpec((1, 2, npair, pw, pw), lambda b, c: (b, 0, 0, 0, 0))
    y_f, y_b, sf_bd = pl.pallas_call(
        _rwkv_kernel,
        grid=(nb, nc),
        in_specs=[blk_f, blk_f, blk_f, lblk_f, blk_b, blk_b, blk_b, lblk_b,
                  dvec, dmat, dvec, dmat, vec, vec, st_blk],
        out_specs=[blk_f, blk_b, st_blk],
        out_shape=[jax.ShapeDtypeStruct((n, D_GROUP), F32), jax.ShapeDtypeStruct((n, D_GROUP), F32),
                   jax.ShapeDtypeStruct((nb, 2, npair, pw, pw), F32)],
        scratch_shapes=[pltpu.VMEM((2, npair, pw, pw), F32)],
        compiler_params=_cparams("arbitrary", "arbitrary"),
        name="rwkv7_scan",
    )(xr, xk, xv, lora, xr, xk, xv, lora, w0.reshape(2, 1, D_GROUP), w_up, a0.reshape(2, 1, D_GROUP), a_up,
      k_k.reshape(1, D_GROUP), k_a.reshape(1, D_GROUP), _states_to_pairs(state0))
    return y_f, y_b, _pairs_to_states(sf_bd)


def _rec_combine_kernel(of_ref, ob_ref, rg_ref, yf_ref, yb_ref, xr_ref, xk_ref, xv_ref, lo_ref, rk_ref, lng_ref,
                        lnb_ref, gup_ref, oret_out, orw_out):
    pw = 2 * HEAD_DIM
    brow = lax.broadcasted_iota(jnp.int32, (pw, pw), 0) // HEAD_DIM
    bcol = lax.broadcasted_iota(jnp.int32, (pw, pw), 1) // HEAD_DIM
    ones_bd = jnp.where(brow == bcol, 1.0, 0.0).astype(BF16)

    def head_sums(x):
        outs = []
        for t in range(D_GROUP // pw):
            xt = x[:, t * pw:(t + 1) * pw]
            hi = xt.astype(BF16)
            lo = (xt - hi.astype(F32)).astype(BF16)
            outs.append(jnp.dot(hi, ones_bd, preferred_element_type=F32)
                        + jnp.dot(lo, ones_bd, preferred_element_type=F32))
        return jnp.concatenate(outs, axis=1)

    inv_d = 1.0 / HEAD_DIM
    o = of_ref[...] + ob_ref[...]
    oret_out[...] = _silu(rg_ref[...]) * (o * lax.rsqrt(head_sums(o * o) * inv_d + RMS_EPS))
    y = yf_ref[...] + yb_ref[...]
    yc = y - head_sums(y) * inv_d
    var = head_sums(yc * yc) * inv_d
    yn = yc * lax.rsqrt(var + RWKV_GN_EPS) * lng_ref[...] + lnb_ref[...]
    bonus = head_sums(xr_ref[...] * rk_ref[...] * xk_ref[...]) * xv_ref[...]
    g_rw = _bdot(_sigmoid(lo_ref[:, 2 * D_LORA:]), gup_ref[...])
    orw_out[...] = (yn + bonus) * g_rw


def _rec_combine(o_f, o_b, rg, y_f, y_b, xr, xk, xv, lora, r_k, ln_g, ln_b, g_up, tm=256):
    n = rg.shape[0]
    blk = pl.BlockSpec((tm, D_GROUP), lambda i: (i, 0))
    vec = pl.BlockSpec((1, D_GROUP), lambda i: (0, 0))
    return pl.pallas_call(
        _rec_combine_kernel,
        grid=(n // tm,),
        in_specs=[blk] * 8 + [pl.BlockSpec((tm, lora.shape[1]), lambda i: (i, 0)),
                              vec, vec, vec, pl.BlockSpec(g_up.shape, lambda i: (0, 0))],
        out_specs=[blk, blk],
        out_shape=[jax.ShapeDtypeStruct((n, D_GROUP), F32)] * 2,
        compiler_params=_cparams("arbitrary"),
        name="rec_combine",
    )(o_f, o_b, rg, y_f, y_b, xr, xk, xv, lora, r_k.reshape(1, D_GROUP), ln_g.reshape(1, D_GROUP),
      ln_b.reshape(1, D_GROUP), g_up)


ATTN_WIDTHS = (D_GROUP,) * 6
REC_WIDTHS = (D_GROUP,) * 7 + (2 * D_LORA + D_LORA_G,)


def kernel(x_prompt, x_sample, cache_na_k, cache_na_v, cache_diff_k, cache_diff_v, state_ret, state_rwkv, c, c_ctx, norm_mix_g, norm_ffn_g, norm_final_g, w_ada, b_ada, w_in_attn, w_out_attn, na_rpb, diff_lq1, diff_lk1, diff_lq2, diff_lk2, w_in_rec, w_out_rec, ret_decay_logit, rw_w0, rw_w_up, rw_a0, rw_a_up, rw_g_up, rw_k_k, rw_k_a, rw_r_k, rw_ln_g, rw_ln_b, w_ffn_in, w_ffn_out):
    bp, seq, d = x_prompt.shape
    bl, n_lat, _ = x_sample.shape
    depth = w_ada.shape[0]
    ctx = x_prompt.reshape(bp * seq, d)
    lat = x_sample.reshape(bl * n_lat, d)

    cond8 = jnp.zeros((8, d), F32).at[0].set(c_ctx).at[1:1 + bl].set(c)
    mods = _ada_modulation(cond8, w_ada, b_ada).reshape(depth, 8, 6, d)

    outs = {}
    for layer in range(depth):
        m_ctx = [mods[layer, 0:1, j].reshape(1, 1, d) for j in range(6)]
        m_lat = [mods[layer, 1:1 + bl, j].reshape(bl, 1, d) for j in range(6)]
        if layer % 2 == 0:
            i = layer // 2
            lam_init = 0.8 - 0.6 * math.exp(-0.3 * layer)
            lam_params = jnp.stack([diff_lq1[i], diff_lk1[i], diff_lq2[i], diff_lk2[i]]).astype(F32)
            w_in = w_in_attn[i].astype(BF16)
            w_out = w_out_attn[i].astype(BF16)
            pc = _norm_mod_matmul(ctx, norm_mix_g[layer], m_ctx[0], m_ctx[1], w_in, ATTN_WIDTHS, bp * seq, 256)
            plat = _norm_mod_matmul(lat, norm_mix_g[layer], m_lat[0], m_lat[1], w_in, ATTN_WIDTHS, n_lat, 256)
            mix_ctx = _ctx_attention(lam_params, *pc, seq, lam_init)
            outs["na_k"], outs["na_v"], outs["df_k"], outs["df_v"] = pc[1], pc[2], pc[4], pc[5]
            past = cache_na_k.shape[2]
            tt = _na_bias_table(na_rpb[i])
            o_na = _lat_na_attention(plat[0], plat[1], plat[2],
                                     cache_na_k[:, i].reshape(bl, past, D_GROUP),
                                     cache_na_v[:, i].reshape(bl, past, D_GROUP), tt, n_lat)
            o_df = _lat_diff_attention(lam_params, plat[3], plat[4], plat[5],
                                       cache_diff_k[:, i].reshape(bl, past, D_GROUP),
                                       cache_diff_v[:, i].reshape(bl, past, D_GROUP), n_lat, lam_init)
            mix_lat = (o_na, o_df)
        else:
            j = layer // 2
            w_in = w_in_rec[j].astype(BF16)
            w_out = w_out_rec[j].astype(BF16)
            pc = _norm_mod_matmul(ctx, norm_mix_g[layer], m_ctx[0], m_ctx[1], w_in, REC_WIDTHS, bp * seq, 256)
            plat = _norm_mod_matmul(lat, norm_mix_g[layer], m_lat[0], m_lat[1], w_in, REC_WIDTHS, n_lat, 256)
            zero_state = jnp.zeros((bp, 2, N_HEADS, HEAD_DIM, HEAD_DIM), F32)
            mixes = []
            for is_ctx, p, s_ret0, s_rw0, sq in ((True, pc, zero_state, zero_state, seq),
                                                 (False, plat, state_ret[:, j], state_rwkv[:, j], n_lat)):
                rq, rk, rv, rg, wr, wk, wv, lora = p
                o_f, o_b, s_ret = _retention_scan(ret_decay_logit[j], rq, rk, rv, s_ret0, sq)
                y_f, y_b, s_rw = _rwkv_scan(wr, wk, wv, lora, rw_w0[j], rw_w_up[j].astype(BF16), rw_a0[j],
                                            rw_a_up[j].astype(BF16), rw_k_k[j], rw_k_a[j], s_rw0, sq)
                mixes.append(_rec_combine(o_f, o_b, rg, y_f, y_b, wr, wk, wv, lora, rw_r_k[j], rw_ln_g[j],
                                          rw_ln_b[j], rw_g_up[j].astype(BF16)))
                if is_ctx:
                    outs["ret"], outs["rwkv"] = s_ret, s_rw
            mix_ctx, mix_lat = mixes
        final = layer == depth - 1
        w_fi = w_ffn_in[layer].astype(BF16)
        w_fo = w_ffn_out[layer].astype(BF16)
        ctx = _layer_tail(ctx, mix_ctx[0], mix_ctx[1], w_out, m_ctx[2], norm_ffn_g[layer], m_ctx[3], m_ctx[4],
                          m_ctx[5], w_fi, w_fo, norm_final_g, bp * seq, final)
        lat = _layer_tail(lat, mix_lat[0], mix_lat[1], w_out, m_lat[2], norm_ffn_g[layer], m_lat[3], m_lat[4],
                          m_lat[5], w_fi, w_fo, norm_final_g, n_lat, final)

    y_prompt = ctx.reshape(bp, seq, d)
    y_sample = lat.reshape(bl, n_lat, d)
    new_cache_na_k = outs["na_k"].reshape(bp, 1, seq, N_HEADS, HEAD_DIM)
    new_cache_na_v = outs["na_v"].reshape(bp, 1, seq, N_HEADS, HEAD_DIM)
    new_cache_diff_k = outs["df_k"].reshape(bp, 1, seq, N_HEADS // 2, 2, HEAD_DIM)
    new_cache_diff_v = outs["df_v"].reshape(bp, 1, seq, N_HEADS // 2, 2 * HEAD_DIM)
    new_state_ret = outs["ret"].reshape(bp, 1, 2, N_HEADS, HEAD_DIM, HEAD_DIM)
    new_state_rwkv = outs["rwkv"].reshape(bp, 1, 2, N_HEADS, HEAD_DIM, HEAD_DIM)
    return (y_prompt, y_sample, new_cache_na_k, new_cache_na_v, new_cache_diff_k, new_cache_diff_v,
            new_state_ret, new_state_rwkv)
```

```python
import functools
import math

import jax
import jax.numpy as jnp
from jax import lax
from jax.experimental import pallas as pl
from jax.experimental.pallas import tpu as pltpu

F32 = jnp.float32
BF16 = jnp.bfloat16

D_MODEL = 1024
HEAD_DIM = 64
N_HEADS = 8
D_GROUP = 512
GRID_W = 64
GRID_ROWS = 32
WIN_R = 8
WIN_W = 16
RPB_R = 15
RPB_C = 31
PAST_LEN = 256
D_FF = 2816
D_LORA = 64
D_LORA_G = 128
ROPE_BASE = 10000.0
RMS_EPS = 1e-6
RWKV_GN_EPS = 64e-5
NEG_INF = -1e30
ATTN_SCALE = HEAD_DIM ** -0.5
RET_CHUNK = 128
RWKV_CHUNK = 64
RWKV_SUB = 16
RWKV_SUB_DOUBLINGS = 3
VMEM_LIMIT = 56 * 1024 * 1024
TAIL_TM = 512
PROJ_TM = 512
PROJ_TM_CACHE = 256


def _cparams(*sem):
    return pltpu.CompilerParams(dimension_semantics=sem, vmem_limit_bytes=VMEM_LIMIT)


def _sigmoid(x):
    return 1.0 / (1.0 + jnp.exp(-x))


def _silu(x):
    return x * _sigmoid(x)


def _rms(x, eps=RMS_EPS):
    return x * lax.rsqrt(jnp.mean(x * x, axis=-1, keepdims=True) + eps)


def _bdot(a, b):
    return jnp.dot(a.astype(BF16), b.astype(BF16), preferred_element_type=F32)


def _bdot_nt(a, b):
    return lax.dot_general(a.astype(BF16), b.astype(BF16), (((1,), (1,)), ((), ())),
                           preferred_element_type=F32)


def _bdot_tn(a, b):
    return lax.dot_general(a.astype(BF16), b.astype(BF16), (((0,), (0,)), ((), ())),
                           preferred_element_type=F32)


_DN = {"nn": (((1,), (0,)), ((), ())), "nt": (((1,), (1,)), ((), ())), "tn": (((0,), (0,)), ((), ()))}


def _split2(x):
    hi = x.astype(BF16)
    lo = (x - hi.astype(F32)).astype(BF16)
    return hi, lo


def _mm3s(a_split, b_split, mode="nn"):
    ah, al = a_split
    bh, bl = b_split
    d = functools.partial(lax.dot_general, dimension_numbers=_DN[mode], preferred_element_type=F32)
    return d(ah, bh) + (d(ah, bl) + d(al, bh))


def _mm3(a, b, mode="nn"):
    return _mm3s(_split2(a), _split2(b), mode)


def _softmax_rows(s):
    m = jnp.max(s, axis=-1, keepdims=True)
    e = jnp.exp(s - m)
    return e / jnp.sum(e, axis=-1, keepdims=True)


def _diff_lambda(lp, lam_init):
    s1 = jnp.sum(lp[0:1, :] * lp[1:2, :], axis=-1, keepdims=True)
    s2 = jnp.sum(lp[2:3, :] * lp[3:4, :], axis=-1, keepdims=True)
    return jnp.exp(s1) - jnp.exp(s2) + lam_init


def _ada_kernel(c_ref, w_ref, b_ref, o_ref):
    s = _silu(c_ref[...])
    o_ref[0] = _bdot(s, w_ref[0]) + b_ref[0]


def _ada_modulation(cond8, w_ada, b_ada):
    depth, d, n = w_ada.shape
    tn = 1536
    return pl.pallas_call(
        _ada_kernel,
        grid=(depth, n // tn),
        in_specs=[pl.BlockSpec((8, d), lambda l, j: (0, 0)),
                  pl.BlockSpec((1, d, tn), lambda l, j: (l, 0, j)),
                  pl.BlockSpec((1, 1, tn), lambda l, j: (l, 0, j))],
        out_specs=pl.BlockSpec((1, 8, tn), lambda l, j: (l, 0, j)),
        out_shape=jax.ShapeDtypeStruct((depth, 8, n), F32),
        compiler_params=_cparams("arbitrary", "arbitrary"),
        name="ada_modulation",
    )(cond8, w_ada, b_ada.reshape(depth, 1, n))


def _nmm_kernel(x_ref, g_ref, sh_ref, sc_ref, w_ref, *o_refs, widths, cache_shapes):
    h = _rms(x_ref[...]) * g_ref[...]
    h = h * (1.0 + sc_ref[0]) + sh_ref[0]
    hb = h.astype(BF16)
    c_refs = dict(zip(cache_shapes, o_refs[len(widths):]))
    off = 0
    for gi, (o_ref, w) in enumerate(zip(o_refs, widths)):
        res = jnp.dot(hb, w_ref[:, off:off + w], preferred_element_type=F32)
        o_ref[...] = res
        off += w
        if gi in c_refs:
            c_refs[gi][...] = res.reshape((res.shape[0],) + cache_shapes[gi])


def _norm_mod_matmul(x, gain, shift, scale, w_bf16, widths, rows_per_mod, tm, cache_shapes=None):
    n, d = x.shape
    cache_shapes = cache_shapes or {}
    tiles_per_mod = rows_per_mod // tm
    mod_spec = pl.BlockSpec((1, 1, d), lambda i: (i // tiles_per_mod, 0, 0))
    cache_specs = [pl.BlockSpec((tm,) + dims, lambda i, nd=len(dims): (i,) + (0,) * nd)
                   for dims in cache_shapes.values()]
    return pl.pallas_call(
        functools.partial(_nmm_kernel, widths=widths, cache_shapes=cache_shapes),
        grid=(n // tm,),
        in_specs=[pl.BlockSpec((tm, d), lambda i: (i, 0)),
                  pl.BlockSpec((1, d), lambda i: (0, 0)),
                  mod_spec, mod_spec,
                  pl.BlockSpec(w_bf16.shape, lambda i: (0, 0), pipeline_mode=pl.Buffered(1))],
        out_specs=[pl.BlockSpec((tm, w), lambda i: (i, 0)) for w in widths] + cache_specs,
        out_shape=([jax.ShapeDtypeStruct((n, w), F32) for w in widths]
                   + [jax.ShapeDtypeStruct((n,) + dims, F32) for dims in cache_shapes.values()]),
        compiler_params=_cparams("arbitrary"),
        name="norm_mod_matmul",
    )(x, gain.reshape(1, d), shift, scale, w_bf16)


PAIR_W = 2 * HEAD_DIM


def _pair_tiles(x):
    return [x[:, j * PAIR_W:(j + 1) * PAIR_W] for j in range(x.shape[1] // PAIR_W)]


def _split_pair(tile):
    left = lax.broadcasted_iota(jnp.int32, tile.shape, 1) < HEAD_DIM
    zero = jnp.zeros_like(tile)
    return jnp.where(left, tile, zero), jnp.where(left, zero, tile)


def _with_ones(v):
    return jnp.concatenate([v, jnp.ones_like(v)], axis=1)


def _exp_scores(s):
    return jnp.exp(s - jnp.max(s, axis=-1, keepdims=True)).astype(BF16)


def _merge_pair(out_even, out_odd):
    left = lax.broadcasted_iota(jnp.int32, (out_even.shape[0], PAIR_W), 1) < HEAD_DIM
    return jnp.where(left, out_even[:, :PAIR_W] * (1.0 / out_even[:, PAIR_W:]),
                     out_odd[:, :PAIR_W] * (1.0 / out_odd[:, PAIR_W:]))


def _diff_combine(out1, out2, lam, lam_init):
    o = out1[:, :PAIR_W] * (1.0 / out1[:, PAIR_W:]) - out2[:, :PAIR_W] * (lam / out2[:, PAIR_W:])
    return _rms(o) * (1.0 - lam_init)


def _ctx_attn_kernel(lp_ref, qn_ref, kn_ref, vn_ref, qd_ref, kd_ref, vd_ref, on_ref, od_ref, *, lam_init):
    tiles = range(D_GROUP // PAIR_W)
    lam = _diff_lambda(lp_ref[...], lam_init)
    q = [_split_pair(t) for t in _pair_tiles((qn_ref[...] * ATTN_SCALE).astype(BF16))
         + _pair_tiles((qd_ref[...] * ATTN_SCALE).astype(BF16))]
    k = _pair_tiles(kn_ref[...].astype(BF16)) + _pair_tiles(kd_ref[...].astype(BF16))
    v = [_with_ones(t) for t in _pair_tiles(vn_ref[...].astype(BF16)) + _pair_tiles(vd_ref[...].astype(BF16))]
    e = [[_exp_scores(_bdot_nt(q[j][c], k[j])) for c in range(2)] for j in range(2 * len(tiles))]
    out = [[jnp.dot(e[j][c], v[j], preferred_element_type=F32) for c in range(2)] for j in range(2 * len(tiles))]
    for j in tiles:
        sl = slice(j * PAIR_W, (j + 1) * PAIR_W)
        on_ref[:, sl] = _merge_pair(out[j][0], out[j][1])
        od_ref[:, sl] = _diff_combine(out[len(tiles) + j][0], out[len(tiles) + j][1], lam, lam_init)


def _ctx_attention(lam_params, qn, kn, vn, qd, kd, vd, seq, lam_init):
    n = qn.shape[0]
    blk = pl.BlockSpec((seq, D_GROUP), lambda b: (b, 0))
    return pl.pallas_call(
        functools.partial(_ctx_attn_kernel, lam_init=lam_init),
        grid=(n // seq,),
        in_specs=[pl.BlockSpec(lam_params.shape, lambda b: (0, 0))] + [blk] * 6,
        out_specs=[blk, blk],
        out_shape=[jax.ShapeDtypeStruct((n, D_GROUP), F32)] * 2,
        compiler_params=_cparams("arbitrary"),
        name="ctx_attention",
    )(lam_params, qn, kn, vn, qd, kd, vd)


def _rope_tables(n):
    quarter = HEAD_DIM // 4
    pos = jnp.arange(n)
    inv_freq = ROPE_BASE ** (-jnp.arange(quarter, dtype=F32) / quarter)
    lane = jnp.arange(HEAD_DIM)
    p = jnp.where(lane[None, :] < HEAD_DIM // 2, (pos // GRID_W)[:, None], (pos % GRID_W)[:, None]).astype(F32)
    ang = p * inv_freq[lane % quarter][None, :]
    sign = jnp.where((lane % (2 * quarter)) < quarter, -1.0, 1.0).astype(F32)
    cos = jnp.tile(jnp.cos(ang), (1, N_HEADS))
    sin = jnp.tile(jnp.sin(ang) * sign[None, :], (1, N_HEADS))
    return cos, sin


def _rope(x, cos, sin_signed):
    quarter = HEAD_DIM // 4
    width = x.shape[-1]
    lane = lax.broadcasted_iota(jnp.int32, x.shape, 1)
    first = (lane % (2 * quarter)) < quarter
    partner = jnp.where(first, pltpu.roll(x, width - quarter, 1), pltpu.roll(x, quarter, 1))
    return x * cos + partner * sin_signed


def _lat_diff_kernel(lp_ref, q_ref, k_ref, v_ref, ck_ref, cv_ref, cosq_ref, sinq_ref, cosk_ref, sink_ref,
                     o_ref, kall, vall, *, lam_init, n_lat):
    @pl.when(pl.program_id(1) == 0)
    def _():
        kall[0:n_lat, :] = _rope(k_ref[...], cosk_ref[...], sink_ref[...]).astype(BF16)
        kall[n_lat:, :] = ck_ref[...].astype(BF16)
        for j, (vt, ct) in enumerate(zip(_pair_tiles(v_ref[...].astype(BF16)),
                                         _pair_tiles(cv_ref[...].astype(BF16)))):
            vall[j, 0:n_lat, :] = _with_ones(vt)
            vall[j, n_lat:, :] = _with_ones(ct)

    lam = _diff_lambda(lp_ref[...], lam_init)
    q = (_rope(q_ref[...], cosq_ref[...], sinq_ref[...]) * ATTN_SCALE).astype(BF16)
    for j, qt in enumerate(_pair_tiles(q)):
        sl = slice(j * PAIR_W, (j + 1) * PAIR_W)
        out = [jnp.dot(_exp_scores(_bdot_nt(qc, kall[:, sl])), vall[j], preferred_element_type=F32)
               for qc in _split_pair(qt)]
        o_ref[:, sl] = _diff_combine(out[0], out[1], lam, lam_init)


def _lat_diff_attention(lam_params, qd, kd, vd, cache_k, cache_v, n_lat, lam_init, tq=256):
    n = qd.shape[0]
    nb = n // n_lat
    nq = n_lat // tq
    past = cache_k.shape[1]
    cos, sin = _rope_tables(n_lat)
    qblk = pl.BlockSpec((tq, D_GROUP), lambda b, i: (b * nq + i, 0))
    kvblk = pl.BlockSpec((n_lat, D_GROUP), lambda b, i: (b, 0))
    cblk = pl.BlockSpec((None, past, D_GROUP), lambda b, i: (b, 0, 0))
    return pl.pallas_call(
        functools.partial(_lat_diff_kernel, lam_init=lam_init, n_lat=n_lat),
        grid=(nb, nq),
        in_specs=[pl.BlockSpec(lam_params.shape, lambda b, i: (0, 0)),
                  qblk, kvblk, kvblk, cblk, cblk,
                  pl.BlockSpec((tq, D_GROUP), lambda b, i: (i, 0)),
                  pl.BlockSpec((tq, D_GROUP), lambda b, i: (i, 0)),
                  pl.BlockSpec((n_lat, D_GROUP), lambda b, i: (0, 0)),
                  pl.BlockSpec((n_lat, D_GROUP), lambda b, i: (0, 0))],
        out_specs=qblk,
        out_shape=jax.ShapeDtypeStruct((n, D_GROUP), F32),
        scratch_shapes=[pltpu.VMEM((n_lat + past, D_GROUP), BF16),
                        pltpu.VMEM((D_GROUP // PAIR_W, n_lat + past, 2 * PAIR_W), BF16)],
        compiler_params=_cparams("arbitrary", "arbitrary"),
        name="lat_diff_attention",
    )(lam_params, qd, kd, vd, cache_k, cache_v, cos, sin, cos, sin)


def _na_bias_kernel(rpb_ref, tt_ref):
    h = pl.program_id(0)
    lane = lax.broadcasted_iota(jnp.int32, (GRID_W, 2 * GRID_W), 1)
    qc = lax.broadcasted_iota(jnp.int32, (GRID_W, 2 * GRID_W), 0)
    dc = jnp.clip(lane % GRID_W - qc + (WIN_W - 1), 0, RPB_C - 1)
    first = lane < GRID_W
    for dr in range(RPB_R - 1):
        acc = jnp.zeros((GRID_W, 2 * GRID_W), F32)
        for d in range(RPB_C):
            val = jnp.where(first, rpb_ref[h * RPB_R + dr, d], rpb_ref[h * RPB_R + dr + 1, d])
            acc = jnp.where(dc == d, val, acc)
        tt_ref[0, dr] = acc


def _na_bias_table(rpb):
    return pl.pallas_call(
        _na_bias_kernel,
        grid=(N_HEADS,),
        in_specs=[pl.BlockSpec(memory_space=pltpu.SMEM)],
        out_specs=pl.BlockSpec((1, RPB_R - 1, GRID_W, 2 * GRID_W), lambda h: (h, 0, 0, 0)),
        out_shape=jax.ShapeDtypeStruct((N_HEADS, RPB_R - 1, GRID_W, 2 * GRID_W), F32),
        compiler_params=_cparams("arbitrary"),
        name="na_bias_table",
    )(rpb.reshape(N_HEADS * RPB_R, RPB_C))


def _lat_na_kernel(q_ref, k_ref, v_ref, ck_ref, cv_ref, tt_ref, o_ref):
    r = pl.program_id(1)
    r_start = jnp.clip(r - WIN_R // 2, 0, GRID_ROWS - WIN_R)
    dr0 = r_start - r + (WIN_R - 1)
    row0 = pl.multiple_of(r_start * GRID_W, GRID_W)
    nkeys = WIN_R * GRID_W
    past = ck_ref.shape[0]
    lane = lax.broadcasted_iota(jnp.int32, (GRID_W, nkeys + past), 1)
    kc = lane % GRID_W
    qc = lax.broadcasted_iota(jnp.int32, (GRID_W, nkeys + past), 0)
    c_start = jnp.clip(qc - WIN_W // 2, 0, GRID_W - WIN_W)
    visible = ((kc >= c_start) & (kc < c_start + WIN_W)) | (lane >= nkeys)
    no_bias = jnp.zeros((GRID_W, past), F32)
    q = [_split_pair(t) for t in _pair_tiles((q_ref[...] * ATTN_SCALE).astype(BF16))]
    kcat = _pair_tiles(jnp.concatenate([k_ref[pl.ds(row0, nkeys), :], ck_ref[...]], axis=0).astype(BF16))
    vcat = [_with_ones(t) for t in
            _pair_tiles(jnp.concatenate([v_ref[pl.ds(row0, nkeys), :], cv_ref[...]], axis=0).astype(BF16))]
    bias = [jnp.concatenate([tt_ref[h, dr0 + 2 * j] for j in range(WIN_R // 2)] + [no_bias], axis=1)
            for h in range(N_HEADS)]
    e = [[_exp_scores(jnp.where(visible, _bdot_nt(q[j][c], kcat[j]) + bias[2 * j + c], NEG_INF))
          for c in range(2)] for j in range(len(q))]
    out = [[jnp.dot(e[j][c], vcat[j], preferred_element_type=F32) for c in range(2)] for j in range(len(q))]
    for j in range(len(q)):
        o_ref[:, j * PAIR_W:(j + 1) * PAIR_W] = _merge_pair(out[j][0], out[j][1])


def _lat_na_attention(qn, kn, vn, cache_k, cache_v, tt, n_lat):
    n = qn.shape[0]
    nb = n // n_lat
    past = cache_k.shape[1]
    qblk = pl.BlockSpec((GRID_W, D_GROUP), lambda b, r: (b * GRID_ROWS + r, 0))
    kvblk = pl.BlockSpec((n_lat, D_GROUP), lambda b, r: (b, 0))
    cblk = pl.BlockSpec((None, past, D_GROUP), lambda b, r: (b, 0, 0))
    return pl.pallas_call(
        _lat_na_kernel,
        grid=(nb, GRID_ROWS),
        in_specs=[qblk, kvblk, kvblk, cblk, cblk,
                  pl.BlockSpec(tt.shape, lambda b, r: (0, 0, 0, 0))],
        out_specs=qblk,
        out_shape=jax.ShapeDtypeStruct((n, D_GROUP), F32),
        compiler_params=_cparams("arbitrary", "arbitrary"),
        name="lat_na_attention",
    )(qn, kn, vn, cache_k, cache_v, tt)


def _tail_kernel(x_ref, ma_ref, mb_ref, wo_ref, gm_ref, g_ref, sh_ref, sc_ref, gf_ref, wi_ref, wf_ref, fg_ref,
                 o_ref, *, final, ff_chunk):
    mix = _bdot(ma_ref[...], wo_ref[0:D_GROUP, :]) + _bdot(mb_ref[...], wo_ref[D_GROUP:, :])
    x1 = x_ref[...] + gm_ref[0] * mix
    h = _rms(x1) * g_ref[...]
    hb = (h * (1.0 + sc_ref[0]) + sh_ref[0]).astype(BF16)
    acc = jnp.zeros_like(x1)
    for c0 in range(0, D_FF, ff_chunk):
        gate = jnp.dot(hb, wi_ref[:, c0:c0 + ff_chunk], preferred_element_type=F32)
        up = jnp.dot(hb, wi_ref[:, D_FF + c0:D_FF + c0 + ff_chunk], preferred_element_type=F32)
        acc = acc + jnp.dot((_silu(gate) * up).astype(BF16), wf_ref[c0:c0 + ff_chunk, :],
                            preferred_element_type=F32)
    x2 = x1 + gf_ref[0] * acc
    if final:
        x2 = _rms(x2) * fg_ref[...]
    o_ref[...] = x2


def _layer_tail(x, mix_a, mix_b, w_out, gate_mix, gain, shift, scale, gate_ffn, w_in, w_ffn_out, final_gain,
                rows_per_mod, final, tm=TAIL_TM, ff_chunk=1408):
    n, d = x.shape
    tiles_per_mod = rows_per_mod // tm
    mod_spec = pl.BlockSpec((1, 1, d), lambda i: (i // tiles_per_mod, 0, 0))
    vec_spec = pl.BlockSpec((1, d), lambda i: (0, 0))
    resident = lambda a: pl.BlockSpec(a.shape, lambda i: (0, 0), pipeline_mode=pl.Buffered(1))
    return pl.pallas_call(
        functools.partial(_tail_kernel, final=final, ff_chunk=ff_chunk),
        grid=(n // tm,),
        in_specs=[pl.BlockSpec((tm, d), lambda i: (i, 0)),
                  pl.BlockSpec((tm, D_GROUP), lambda i: (i, 0)),
                  pl.BlockSpec((tm, D_GROUP), lambda i: (i, 0)),
                  resident(w_out), mod_spec, vec_spec, mod_spec, mod_spec, mod_spec,
                  resident(w_in), resident(w_ffn_out), vec_spec],
        out_specs=pl.BlockSpec((tm, d), lambda i: (i, 0)),
        out_shape=jax.ShapeDtypeStruct((n, d), F32),
        compiler_params=_cparams("arbitrary"),
        name="layer_tail",
    )(x, mix_a, mix_b, w_out, gate_mix, gain.reshape(1, d), shift, scale, gate_ffn, w_in, w_ffn_out,
      final_gain.reshape(1, d))


def _log_sigmoid(x):
    return jnp.minimum(x, 0.0) - jnp.log(1.0 + jnp.exp(-jnp.abs(x)))


def _pair_block_diag(xp):
    left = lax.broadcasted_iota(jnp.int32, xp.shape, 1) < HEAD_DIM
    zero = jnp.zeros_like(xp)
    return jnp.concatenate([jnp.where(left, xp, zero), jnp.where(left, zero, xp)], axis=0)


def _states_to_pairs(state):
    nb = state.shape[0]
    s = state.reshape(nb, 2, N_HEADS // 2, 2, HEAD_DIM, HEAD_DIM)
    zero = jnp.zeros_like(s[:, :, :, 0])
    return jnp.concatenate([jnp.concatenate([s[:, :, :, 0], zero], axis=-1),
                            jnp.concatenate([zero, s[:, :, :, 1]], axis=-1)], axis=-2)


def _pairs_to_states(sbd):
    nb = sbd.shape[0]
    s = jnp.stack([sbd[..., :HEAD_DIM, :HEAD_DIM], sbd[..., HEAD_DIM:, HEAD_DIM:]], axis=3)
    return s.reshape(nb, 2, N_HEADS, HEAD_DIM, HEAD_DIM)


def _ret_kernel(dlm_ref, dlq_ref, qf_ref, kf_ref, vf_ref, qb_ref, kb_ref, vb_ref, s0_ref,
                of_ref, ob_ref, sf_ref, st):
    c = pl.program_id(1)
    cs = RET_CHUNK
    pw = 2 * HEAD_DIM
    npair = N_HEADS // 2

    @pl.when(c == 0)
    def _():
        st[...] = s0_ref[0]

    row = lax.broadcasted_iota(jnp.int32, (cs, 2 * cs), 0)
    col = lax.broadcasted_iota(jnp.int32, (cs, 2 * cs), 1) % cs
    tok = lax.broadcasted_iota(jnp.int32, (cs, pw), 0)
    brow = lax.broadcasted_iota(jnp.int32, (pw, pw), 0) // HEAD_DIM
    bcol = lax.broadcasted_iota(jnp.int32, (pw, pw), 1) // HEAD_DIM
    same_head = brow == bcol
    refs = [(qf_ref, kf_ref, vf_ref, of_ref), (qb_ref, kb_ref, vb_ref, ob_ref)]
    chains = [(d, p) for d in range(2) for p in range(npair)]
    dist = [row - col, col - row]
    tq = [tok.astype(F32), (cs - 1 - tok).astype(F32)]
    sls = [slice(p * pw, (p + 1) * pw) for _, p in chains]
    lgm = [_log_sigmoid(dlm_ref[d, p])[0:1, :] for d, p in chains]
    lgq = [_log_sigmoid(dlq_ref[d, p])[0:1, :] for d, p in chains]
    mask = [jnp.where(dist[d] >= 0, jnp.exp(jnp.maximum(dist[d], 0).astype(F32) * lgm[i]), 0.0)
            for i, (d, _) in enumerate(chains)]
    q = [refs[d][0][:, sls[i]] for i, (d, _) in enumerate(chains)]
    k = [refs[d][1][:, sls[i]] * ATTN_SCALE for i, (d, _) in enumerate(chains)]
    v = [refs[d][2][:, sls[i]] for i, (d, _) in enumerate(chains)]
    idx = range(len(chains))
    s = [_bdot_nt(q[i], _pair_block_diag(k[i])) * mask[i] for i in idx]
    inner = [_bdot(s[i], _pair_block_diag(v[i])) for i in idx]
    state = [st[d, p] for d, p in chains]
    cross = [_bdot(q[i], state[i]) * jnp.exp((tq[d] + 1.0) * lgq[i]) for i, (d, _) in enumerate(chains)]
    kv = [_bdot_tn(k[i] * jnp.exp((cs - 1.0 - tq[d]) * lgq[i]), v[i]) for i, (d, _) in enumerate(chains)]
    for i, (d, p) in enumerate(chains):
        refs[d][3][:, sls[i]] = inner[i] + cross[i]
        st[d, p] = state[i] * jnp.exp(cs * lgq[i]) + jnp.where(same_head, kv[i], 0.0)

    @pl.when(c == pl.num_programs(1) - 1)
    def _():
        sf_ref[0] = st[...]


def _retention_scan(decay_logit, q, k, v, state0, seq):
    n = q.shape[0]
    nb = n // seq
    nc = seq // RET_CHUNK
    npair, pw = N_HEADS // 2, 2 * HEAD_DIM
    dl = decay_logit.astype(F32)
    dl_m = jnp.broadcast_to(jnp.repeat(dl, RET_CHUNK, axis=1).reshape(2, npair, 1, 2 * RET_CHUNK),
                            (2, npair, 8, 2 * RET_CHUNK))
    dl_q = jnp.broadcast_to(jnp.repeat(dl, HEAD_DIM, axis=1).reshape(2, npair, 1, pw), (2, npair, 8, pw))
    blk_f = pl.BlockSpec((RET_CHUNK, D_GROUP), lambda b, c: (b * nc + c, 0))
    blk_b = pl.BlockSpec((RET_CHUNK, D_GROUP), lambda b, c: (b * nc + nc - 1 - c, 0))
    st_blk = pl.BlockSpec((1, 2, npair, pw, pw), lambda b, c: (b, 0, 0, 0, 0))
    o_f, o_b, sf_bd = pl.pallas_call(
        _ret_kernel,
        grid=(nb, nc),
        in_specs=[pl.BlockSpec(dl_m.shape, lambda b, c: (0, 0, 0, 0)),
                  pl.BlockSpec(dl_q.shape, lambda b, c: (0, 0, 0, 0)),
                  blk_f, blk_f, blk_f, blk_b, blk_b, blk_b, st_blk],
        out_specs=[blk_f, blk_b, st_blk],
        out_shape=[jax.ShapeDtypeStruct((n, D_GROUP), F32), jax.ShapeDtypeStruct((n, D_GROUP), F32),
                   jax.ShapeDtypeStruct((nb, 2, npair, pw, pw), F32)],
        scratch_shapes=[pltpu.VMEM((2, npair, pw, pw), F32)],
        compiler_params=_cparams("arbitrary", "arbitrary"),
        name="retention_scan",
    )(dl_m, dl_q, q, k, v, q, k, v, _states_to_pairs(state0))
    return o_f, o_b, _pairs_to_states(sf_bd)


def _rwkv_kernel(xrf_ref, xkf_ref, xvf_ref, lof_ref, xrb_ref, xkb_ref, xvb_ref, lob_ref,
                 w0_ref, wup_ref, a0_ref, aup_ref, kk_ref, ka_ref, s0_ref, yf_ref, yb_ref, sf_ref, st):
    c = pl.program_id(1)
    cs = RWKV_CHUNK
    npair = N_HEADS // 2

    @pl.when(c == 0)
    def _():
        st[...] = s0_ref[0]

    row = lax.broadcasted_iota(jnp.int32, (cs, cs), 0)
    col = lax.broadcasted_iota(jnp.int32, (cs, cs), 1)
    pw = 2 * HEAD_DIM
    prow = lax.broadcasted_iota(jnp.int32, (cs, pw), 0)
    pcol = lax.broadcasted_iota(jnp.int32, (cs, pw), 1) % HEAD_DIM
    eye = jnp.where(pcol == prow, 1.0, 0.0).astype(F32)
    sub_diag = (prow // RWKV_SUB) == (pcol // RWKV_SUB)
    brow = lax.broadcasted_iota(jnp.int32, (pw, pw), 0) // HEAD_DIM
    bcol = lax.broadcasted_iota(jnp.int32, (pw, pw), 1) // HEAD_DIM
    same_head = brow == bcol
    ones_bd = jnp.where(same_head, 1.0, 0.0).astype(BF16)

    bd = _pair_block_diag

    def direction(d, xr_ref, xk_ref, xv_ref, lo_ref):
        tri = jnp.where((row >= col) if d == 0 else (col >= row), 1.0, 0.0).astype(BF16)
        dist = (prow - pcol) if d == 0 else (pcol - prow)
        xk = xk_ref[...]
        lo = lo_ref[...]
        w_log = -math.exp(-0.5) * _sigmoid(w0_ref[d] + _bdot(jnp.tanh(lo[:, 0:D_LORA]), wup_ref[d]))
        a_all = _sigmoid(a0_ref[d] + _bdot(lo[:, D_LORA:2 * D_LORA], aup_ref[d]))
        w1 = w_log.astype(BF16)
        r1 = w_log - w1.astype(F32)
        w2 = r1.astype(BF16)
        w3 = (r1 - w2.astype(F32)).astype(BF16)
        cum = (jnp.dot(tri, w1, preferred_element_type=F32) + jnp.dot(tri, w2, preferred_element_type=F32)
               + jnp.dot(tri, w3, preferred_element_type=F32))
        cum_end = cum[cs - 1:cs, :] if d == 0 else cum[0:1, :]
        return dict(earlier=dist > 0, upto=dist >= 0, xr=xr_ref[...], xv=xv_ref[...], a_all=a_all,
                    e_incl=jnp.exp(cum), e_excl=jnp.exp(cum - w_log), e_neg=jnp.exp(-cum),
                    e_end=jnp.exp(cum_end - cum), w_end=jnp.exp(cum_end), kk_all=xk * kk_ref[...],
                    keff_all=xk * (1.0 + (a_all - 1.0) * ka_ref[...]))

    dirs = [direction(0, xrf_ref, xkf_ref, xvf_ref, lof_ref), direction(1, xrb_ref, xkb_ref, xvb_ref, lob_ref)]
    y_refs = [yf_ref, yb_ref]

    chains = [(d, p) for d in range(2) for p in range(npair)]
    pairs = range(len(chains))
    sls = [slice(p * pw, (p + 1) * pw) for _, p in chains]
    pick = lambda name: [dirs[d][name][:, sls[i]] for i, (d, _) in enumerate(chains)]
    earlier = [dirs[d]["earlier"] for d, _ in chains]
    upto = [dirs[d]["upto"] for d, _ in chains]
    cat = lambda x, y: jnp.concatenate([x, y], axis=0)

    def head_sum(x):
        x1 = x.astype(BF16)
        r1 = x - x1.astype(F32)
        x2 = r1.astype(BF16)
        x3 = (r1 - x2.astype(F32)).astype(BF16)
        f = functools.partial(jnp.dot, preferred_element_type=F32)
        return f(x1, ones_bd) + (f(x2, ones_bd) + f(x3, ones_bd))

    e_incl, e_excl, e_neg, e_end, w_end = (pick(k) for k in ("e_incl", "e_excl", "e_neg", "e_end", "w_end"))
    kk = [x * lax.rsqrt(head_sum(x * x) + 1e-12) for x in pick("kk_all")]
    b = [kk[p] * a for p, a in zip(pairs, pick("a_all"))]
    keff = pick("keff_all")
    v = pick("xv")
    a_t = [-kk[p] * e_excl[p] for p in pairs]
    r_t = [xr * e_incl[p] for p, xr in zip(pairs, pick("xr"))]
    ar = [cat(a_t[p], r_t[p]).astype(BF16) for p in pairs]
    g_b = [_bdot_nt(ar[p], bd(b[p] * e_neg[p])) for p in pairs]
    g_k = [_bdot_nt(ar[p], bd(keff[p] * e_neg[p])) for p in pairs]
    n_mat = [jnp.where(earlier[p], g_b[p][:cs], 0.0) for p in pairs]
    m_rb = [jnp.where(upto[p], g_b[p][cs:], 0.0) for p in pairs]
    a_ak = [jnp.where(earlier[p], g_k[p][:cs], 0.0) for p in pairs]
    m_rk = [jnp.where(upto[p], g_k[p][cs:], 0.0) for p in pairs]
    bd_s = lambda xs: (bd(xs[0]), bd(xs[1]))
    n_d = [jnp.where(sub_diag, n, 0.0) for n in n_mat]
    n_o = [jnp.where(sub_diag, 0.0, n) for n in n_mat]
    n_s = [_split2(n) for n in n_d]
    x = [eye + n for n in n_d]
    pk = [_mm3s(ns, bd_s(ns)) for ns in n_s]
    for _ in range(RWKV_SUB_DOUBLINGS - 1):
        xp = [_mm3s(_split2(cat(x[p], pk[p])), bd_s(_split2(pk[p]))) for p in pairs]
        x = [x[p] + xp[p][:cs] for p in pairs]
        pk = [r[cs:] for r in xp]
    t_d = [x[p] + _mm3s(_split2(x[p]), bd_s(_split2(pk[p]))) for p in pairs]
    m_o = [_bdot(t_d[p], bd(n_o[p])) for p in pairs]
    z = [t_d[p] + _bdot(m_o[p], bd(t_d[p])) for p in pairs]
    m_2 = [_bdot(m_o[p], bd(m_o[p])) for p in pairs]
    t_inv = [(z[p] + _bdot(m_2[p], bd(z[p]))).astype(BF16) for p in pairs]
    av = [_bdot(cat(a_ak[p], m_rk[p]), bd(v[p])) for p in pairs]
    a_hat = [_bdot(t_inv[p], bd(a_t[p])) for p in pairs]
    u0 = [_bdot(t_inv[p], bd(av[p][:cs])) for p in pairs]
    s_prev = [st[d, p] for d, p in chains]
    hs = [_bdot_nt(cat(a_hat[p], r_t[p]), s_prev[p]) for p in pairs]
    u = [hs[p][:cs] + u0[p] for p in pairs]
    y = [hs[p][cs:] + _bdot(m_rb[p], bd(u[p])) + av[p][cs:] for p in pairs]
    for i, (d, p) in enumerate(chains):
        y_refs[d][:, sls[i]] = y[i]
        bk_end = cat(b[i] * e_end[i], keff[i] * e_end[i])
        st[d, p] = s_prev[i] * w_end[i] + jnp.where(same_head, _bdot_tn(cat(u[i], v[i]), bk_end), 0.0)

    @pl.when(c == pl.num_programs(1) - 1)
    def _():
        sf_ref[0] = st[...]


def _rwkv_scan(xr, xk, xv, lora, w0, w_up, a0, a_up, k_k, k_a, state0, seq):
    n = xr.shape[0]
    nb = n // seq
    nc = seq // RWKV_CHUNK

    fwd_idx = lambda b, c: (b * nc + c, 0)
    bwd_idx = lambda b, c: (b * nc + nc - 1 - c, 0)
    blk_f = pl.BlockSpec((RWKV_CHUNK, D_GROUP), fwd_idx)
    blk_b = pl.BlockSpec((RWKV_CHUNK, D_GROUP), bwd_idx)
    lblk_f = pl.BlockSpec((RWKV_CHUNK, lora.shape[1]), fwd_idx)
    lblk_b = pl.BlockSpec((RWKV_CHUNK, lora.shape[1]), bwd_idx)
    dvec = pl.BlockSpec((2, 1, D_GROUP), lambda b, c: (0, 0, 0))
    dmat = pl.BlockSpec((2, D_LORA, D_GROUP), lambda b, c: (0, 0, 0))
    vec = pl.BlockSpec((1, D_GROUP), lambda b, c: (0, 0))
    npair, pw = N_HEADS // 2, 2 * HEAD_DIM
    st_blk = pl.BlockSpec((1, 2, npair, pw, pw), lambda b, c: (b, 0, 0, 0, 0))
    y_f, y_b, sf_bd = pl.pallas_call(
        _rwkv_kernel,
        grid=(nb, nc),
        in_specs=[blk_f, blk_f, blk_f, lblk_f, blk_b, blk_b, blk_b, lblk_b,
                  dvec, dmat, dvec, dmat, vec, vec, st_blk],
        out_specs=[blk_f, blk_b, st_blk],
        out_shape=[jax.ShapeDtypeStruct((n, D_GROUP), F32), jax.ShapeDtypeStruct((n, D_GROUP), F32),
                   jax.ShapeDtypeStruct((nb, 2, npair, pw, pw), F32)],
        scratch_shapes=[pltpu.VMEM((2, npair, pw, pw), F32)],
        compiler_params=_cparams("arbitrary", "arbitrary"),
        name="rwkv7_scan",
    )(xr, xk, xv, lora, xr, xk, xv, lora, w0.reshape(2, 1, D_GROUP), w_up, a0.reshape(2, 1, D_GROUP), a_up,
      k_k.reshape(1, D_GROUP), k_a.reshape(1, D_GROUP), _states_to_pairs(state0))
    return y_f, y_b, _pairs_to_states(sf_bd)


def _rec_combine_kernel(of_ref, ob_ref, rg_ref, yf_ref, yb_ref, xr_ref, xk_ref, xv_ref, lo_ref, rk_ref, lng_ref,
                        lnb_ref, gup_ref, oret_out, orw_out):
    pw = 2 * HEAD_DIM
    brow = lax.broadcasted_iota(jnp.int32, (pw, pw), 0) // HEAD_DIM
    bcol = lax.broadcasted_iota(jnp.int32, (pw, pw), 1) // HEAD_DIM
    ones_bd = jnp.where(brow == bcol, 1.0, 0.0).astype(BF16)

    def head_sums(x):
        outs = []
        for t in range(D_GROUP // pw):
            xt = x[:, t * pw:(t + 1) * pw]
            hi = xt.astype(BF16)
            lo = (xt - hi.astype(F32)).astype(BF16)
            outs.append(jnp.dot(hi, ones_bd, preferred_element_type=F32)
                        + jnp.dot(lo, ones_bd, preferred_element_type=F32))
        return jnp.concatenate(outs, axis=1)

    inv_d = 1.0 / HEAD_DIM
    o = of_ref[...] + ob_ref[...]
    oret_out[...] = _silu(rg_ref[...]) * (o * lax.rsqrt(head_sums(o * o) * inv_d + RMS_EPS))
    y = yf_ref[...] + yb_ref[...]
    yc = y - head_sums(y) * inv_d
    var = head_sums(yc * yc) * inv_d
    yn = yc * lax.rsqrt(var + RWKV_GN_EPS) * lng_ref[...] + lnb_ref[...]
    bonus = head_sums(xr_ref[...] * rk_ref[...] * xk_ref[...]) * xv_ref[...]
    g_rw = _bdot(_sigmoid(lo_ref[:, 2 * D_LORA:]), gup_ref[...])
    orw_out[...] = (yn + bonus) * g_rw


def _rec_combine(o_f, o_b, rg, y_f, y_b, xr, xk, xv, lora, r_k, ln_g, ln_b, g_up, tm=256):
    n = rg.shape[0]
    blk = pl.BlockSpec((tm, D_GROUP), lambda i: (i, 0))
    vec = pl.BlockSpec((1, D_GROUP), lambda i: (0, 0))
    return pl.pallas_call(
        _rec_combine_kernel,
        grid=(n // tm,),
        in_specs=[blk] * 8 + [pl.BlockSpec((tm, lora.shape[1]), lambda i: (i, 0)),
                              vec, vec, vec, pl.BlockSpec(g_up.shape, lambda i: (0, 0))],
        out_specs=[blk, blk],
        out_shape=[jax.ShapeDtypeStruct((n, D_GROUP), F32)] * 2,
        compiler_params=_cparams("arbitrary"),
        name="rec_combine",
    )(o_f, o_b, rg, y_f, y_b, xr, xk, xv, lora, r_k.reshape(1, D_GROUP), ln_g.reshape(1, D_GROUP),
      ln_b.reshape(1, D_GROUP), g_up)


ATTN_WIDTHS = (D_GROUP,) * 6
REC_WIDTHS = (D_GROUP,) * 7 + (2 * D_LORA + D_LORA_G,)


def kernel(x_prompt, x_sample, cache_na_k, cache_na_v, cache_diff_k, cache_diff_v, state_ret, state_rwkv, c, c_ctx, norm_mix_g, norm_ffn_g, norm_final_g, w_ada, b_ada, w_in_attn, w_out_attn, na_rpb, diff_lq1, diff_lk1, diff_lq2, diff_lk2, w_in_rec, w_out_rec, ret_decay_logit, rw_w0, rw_w_up, rw_a0, rw_a_up, rw_g_up, rw_k_k, rw_k_a, rw_r_k, rw_ln_g, rw_ln_b, w_ffn_in, w_ffn_out):
    bp, seq, d = x_prompt.shape
    bl, n_lat, _ = x_sample.shape
    depth = w_ada.shape[0]
    ctx = x_prompt.reshape(bp * seq, d)
    lat = x_sample.reshape(bl * n_lat, d)

    cond8 = jnp.zeros((8, d), F32).at[0].set(c_ctx).at[1:1 + bl].set(c)
    mods = _ada_modulation(cond8, w_ada, b_ada).reshape(depth, 8, 6, d)

    outs = {}
    for layer in range(depth):
        m_ctx = [mods[layer, 0:1, j].reshape(1, 1, d) for j in range(6)]
        m_lat = [mods[layer, 1:1 + bl, j].reshape(bl, 1, d) for j in range(6)]
        if layer % 2 == 0:
            i = layer // 2
            lam_init = 0.8 - 0.6 * math.exp(-0.3 * layer)
            lam_params = jnp.stack([diff_lq1[i], diff_lk1[i], diff_lq2[i], diff_lk2[i]]).astype(F32)
            w_in = w_in_attn[i].astype(BF16)
            w_out = w_out_attn[i].astype(BF16)
            cache_shapes = {1: (N_HEADS, HEAD_DIM), 2: (N_HEADS, HEAD_DIM),
                            4: (N_HEADS // 2, 2, HEAD_DIM), 5: (N_HEADS // 2, 2 * HEAD_DIM)}
            pc = _norm_mod_matmul(ctx, norm_mix_g[layer], m_ctx[0], m_ctx[1], w_in, ATTN_WIDTHS, bp * seq,
                                  PROJ_TM_CACHE, cache_shapes)
            outs["na_k"], outs["na_v"], outs["df_k"], outs["df_v"] = pc[6:]
            pc = pc[:6]
            plat = _norm_mod_matmul(lat, norm_mix_g[layer], m_lat[0], m_lat[1], w_in, ATTN_WIDTHS, n_lat, PROJ_TM)
            mix_ctx = _ctx_attention(lam_params, *pc, seq, lam_init)
            past = cache_na_k.shape[2]
            tt = _na_bias_table(na_rpb[i])
            o_na = _lat_na_attention(plat[0], plat[1], plat[2],
                                     cache_na_k[:, i].reshape(bl, past, D_GROUP),
                                     cache_na_v[:, i].reshape(bl, past, D_GROUP), tt, n_lat)
            o_df = _lat_diff_attention(lam_params, plat[3], plat[4], plat[5],
                                       cache_diff_k[:, i].reshape(bl, past, D_GROUP),
                                       cache_diff_v[:, i].reshape(bl, past, D_GROUP), n_lat, lam_init)
            mix_lat = (o_na, o_df)
        else:
            j = layer // 2
            w_in = w_in_rec[j].astype(BF16)
            w_out = w_out_rec[j].astype(BF16)
            pc = _norm_mod_matmul(ctx, norm_mix_g[layer], m_ctx[0], m_ctx[1], w_in, REC_WIDTHS, bp * seq, PROJ_TM)
            plat = _norm_mod_matmul(lat, norm_mix_g[layer], m_lat[0], m_lat[1], w_in, REC_WIDTHS, n_lat, PROJ_TM)
            zero_state = jnp.zeros((bp, 2, N_HEADS, HEAD_DIM, HEAD_DIM), F32)
            mixes = []
            for is_ctx, p, s_ret0, s_rw0, sq in ((True, pc, zero_state, zero_state, seq),
                                                 (False, plat, state_ret[:, j], state_rwkv[:, j], n_lat)):
                rq, rk, rv, rg, wr, wk, wv, lora = p
                o_f, o_b, s_ret = _retention_scan(ret_decay_logit[j], rq, rk, rv, s_ret0, sq)
                y_f, y_b, s_rw = _rwkv_scan(wr, wk, wv, lora, rw_w0[j], rw_w_up[j].astype(BF16), rw_a0[j],
                                            rw_a_up[j].astype(BF16), rw_k_k[j], rw_k_a[j], s_rw0, sq)
                mixes.append(_rec_combine(o_f, o_b, rg, y_f, y_b, wr, wk, wv, lora, rw_r_k[j], rw_ln_g[j],
                                          rw_ln_b[j], rw_g_up[j].astype(BF16)))
                if is_ctx:
                    outs["ret"], outs["rwkv"] = s_ret, s_rw
            mix_ctx, mix_lat = mixes
        final = layer == depth - 1
        w_fi = w_ffn_in[layer].astype(BF16)
        w_fo = w_ffn_out[layer].astype(BF16)
        ctx = _layer_tail(ctx, mix_ctx[0], mix_ctx[1], w_out, m_ctx[2], norm_ffn_g[layer], m_ctx[3], m_ctx[4],
                          m_ctx[5], w_fi, w_fo, norm_final_g, bp * seq, final)
        lat = _layer_tail(lat, mix_lat[0], mix_lat[1], w_out, m_lat[2], norm_ffn_g[layer], m_lat[3], m_lat[4],
                          m_lat[5], w_fi, w_fo, norm_final_g, n_lat, final)

    y_prompt = ctx.reshape(bp, seq, d)
    y_sample = lat.reshape(bl, n_lat, d)
    new_cache_na_k = outs["na_k"].reshape(bp, 1, seq, N_HEADS, HEAD_DIM)
    new_cache_na_v = outs["na_v"].reshape(bp, 1, seq, N_HEADS, HEAD_DIM)
    new_cache_diff_k = outs["df_k"].reshape(bp, 1, seq, N_HEADS // 2, 2, HEAD_DIM)
    new_cache_diff_v = outs["df_v"].reshape(bp, 1, seq, N_HEADS // 2, 2 * HEAD_DIM)
    new_state_ret = outs["ret"].reshape(bp, 1, 2, N_HEADS, HEAD_DIM, HEAD_DIM)
    new_state_rwkv = outs["rwkv"].reshape(bp, 1, 2, N_HEADS, HEAD_DIM, HEAD_DIM)
    return (y_prompt, y_sample, new_cache_na_k, new_cache_na_v, new_cache_diff_k, new_cache_diff_v,
            new_state_ret, new_state_rwkv)
```

```python
import functools
import math

import jax
import jax.numpy as jnp
from jax import lax
from jax.experimental import pallas as pl
from jax.experimental.pallas import tpu as pltpu

F32 = jnp.float32
BF16 = jnp.bfloat16

D_MODEL = 1024
HEAD_DIM = 64
N_HEADS = 8
D_GROUP = 512
GRID_W = 64
GRID_ROWS = 32
WIN_R = 8
WIN_W = 16
RPB_R = 15
RPB_C = 31
PAST_LEN = 256
D_FF = 2816
D_LORA = 64
D_LORA_G = 128
ROPE_BASE = 10000.0
RMS_EPS = 1e-6
RWKV_GN_EPS = 64e-5
NEG_INF = -1e30
ATTN_SCALE = HEAD_DIM ** -0.5
RET_CHUNK = 128
RWKV_CHUNK = 64
RWKV_SUB = 16
RWKV_SUB_DOUBLINGS = 3
VMEM_LIMIT = 56 * 1024 * 1024
TAIL_TM = 512
TAIL_TM_REC = 256
PROJ_TM = 512
PROJ_TM_CACHE = 256


def _cparams(*sem):
    return pltpu.CompilerParams(dimension_semantics=sem, vmem_limit_bytes=VMEM_LIMIT)


def _sigmoid(x):
    return 1.0 / (1.0 + jnp.exp(-x))


def _silu(x):
    return x * _sigmoid(x)


def _rms(x, eps=RMS_EPS):
    return x * lax.rsqrt(jnp.mean(x * x, axis=-1, keepdims=True) + eps)


def _bdot(a, b):
    return jnp.dot(a.astype(BF16), b.astype(BF16), preferred_element_type=F32)


def _bdot_nt(a, b):
    return lax.dot_general(a.astype(BF16), b.astype(BF16), (((1,), (1,)), ((), ())),
                           preferred_element_type=F32)


def _bdot_tn(a, b):
    return lax.dot_general(a.astype(BF16), b.astype(BF16), (((0,), (0,)), ((), ())),
                           preferred_element_type=F32)


_DN = {"nn": (((1,), (0,)), ((), ())), "nt": (((1,), (1,)), ((), ())), "tn": (((0,), (0,)), ((), ()))}


def _split2(x):
    hi = x.astype(BF16)
    lo = (x - hi.astype(F32)).astype(BF16)
    return hi, lo


def _mm3s(a_split, b_split, mode="nn"):
    ah, al = a_split
    bh, bl = b_split
    d = functools.partial(lax.dot_general, dimension_numbers=_DN[mode], preferred_element_type=F32)
    return d(ah, bh) + (d(ah, bl) + d(al, bh))


def _mm3(a, b, mode="nn"):
    return _mm3s(_split2(a), _split2(b), mode)


def _diff_lambda(lp, lam_init):
    s1 = jnp.sum(lp[0:1, :] * lp[1:2, :], axis=-1, keepdims=True)
    s2 = jnp.sum(lp[2:3, :] * lp[3:4, :], axis=-1, keepdims=True)
    return jnp.exp(s1) - jnp.exp(s2) + lam_init


def _ada_kernel(c_ref, w_ref, b_ref, o_ref):
    s = _silu(c_ref[...])
    o_ref[0] = _bdot(s, w_ref[0]) + b_ref[0]


def _ada_modulation(cond8, w_ada, b_ada):
    depth, d, n = w_ada.shape
    tn = 1536
    return pl.pallas_call(
        _ada_kernel,
        grid=(depth, n // tn),
        in_specs=[pl.BlockSpec((8, d), lambda l, j: (0, 0)),
                  pl.BlockSpec((1, d, tn), lambda l, j: (l, 0, j)),
                  pl.BlockSpec((1, 1, tn), lambda l, j: (l, 0, j))],
        out_specs=pl.BlockSpec((1, 8, tn), lambda l, j: (l, 0, j)),
        out_shape=jax.ShapeDtypeStruct((depth, 8, n), F32),
        compiler_params=_cparams("arbitrary", "arbitrary"),
        name="ada_modulation",
    )(cond8, w_ada, b_ada.reshape(depth, 1, n))


def _nmm_kernel(x_ref, g_ref, sh_ref, sc_ref, w_ref, *o_refs, widths, cache_shapes):
    h = _rms(x_ref[...]) * g_ref[...]
    h = h * (1.0 + sc_ref[0]) + sh_ref[0]
    hb = h.astype(BF16)
    c_refs = dict(zip(cache_shapes, o_refs[len(widths):]))
    off = 0
    for gi, (o_ref, w) in enumerate(zip(o_refs, widths)):
        res = jnp.dot(hb, w_ref[:, off:off + w], preferred_element_type=F32)
        o_ref[...] = res
        off += w
        if gi in c_refs:
            c_refs[gi][...] = res.reshape((res.shape[0],) + cache_shapes[gi])


def _norm_mod_matmul(x, gain, shift, scale, w_bf16, widths, rows_per_mod, tm, cache_shapes=None):
    n, d = x.shape
    cache_shapes = cache_shapes or {}
    tiles_per_mod = rows_per_mod // tm
    mod_spec = pl.BlockSpec((1, 1, d), lambda i: (i // tiles_per_mod, 0, 0))
    cache_specs = [pl.BlockSpec((tm,) + dims, lambda i, nd=len(dims): (i,) + (0,) * nd)
                   for dims in cache_shapes.values()]
    return pl.pallas_call(
        functools.partial(_nmm_kernel, widths=widths, cache_shapes=cache_shapes),
        grid=(n // tm,),
        in_specs=[pl.BlockSpec((tm, d), lambda i: (i, 0)),
                  pl.BlockSpec((1, d), lambda i: (0, 0)),
                  mod_spec, mod_spec,
                  pl.BlockSpec(w_bf16.shape, lambda i: (0, 0), pipeline_mode=pl.Buffered(1))],
        out_specs=[pl.BlockSpec((tm, w), lambda i: (i, 0)) for w in widths] + cache_specs,
        out_shape=([jax.ShapeDtypeStruct((n, w), F32) for w in widths]
                   + [jax.ShapeDtypeStruct((n,) + dims, F32) for dims in cache_shapes.values()]),
        compiler_params=_cparams("arbitrary"),
        name="norm_mod_matmul",
    )(x, gain.reshape(1, d), shift, scale, w_bf16)


PAIR_W = 2 * HEAD_DIM


def _pair_tiles(x):
    return [x[:, j * PAIR_W:(j + 1) * PAIR_W] for j in range(x.shape[1] // PAIR_W)]


def _split_pair(tile):
    left = lax.broadcasted_iota(jnp.int32, tile.shape, 1) < HEAD_DIM
    zero = jnp.zeros_like(tile)
    return jnp.where(left, tile, zero), jnp.where(left, zero, tile)


def _with_ones(v):
    return jnp.concatenate([v, jnp.ones_like(v)], axis=1)


def _exp_scores(s):
    return jnp.exp(s - jnp.max(s, axis=-1, keepdims=True)).astype(BF16)


def _merge_pair(out_even, out_odd):
    left = lax.broadcasted_iota(jnp.int32, (out_even.shape[0], PAIR_W), 1) < HEAD_DIM
    return jnp.where(left, out_even[:, :PAIR_W] * (1.0 / out_even[:, PAIR_W:]),
                     out_odd[:, :PAIR_W] * (1.0 / out_odd[:, PAIR_W:]))


def _diff_combine(out1, out2, lam, lam_init):
    o = out1[:, :PAIR_W] * (1.0 / out1[:, PAIR_W:]) - out2[:, :PAIR_W] * (lam / out2[:, PAIR_W:])
    return _rms(o) * (1.0 - lam_init)


def _ctx_attn_kernel(lp_ref, qn_ref, kn_ref, vn_ref, qd_ref, kd_ref, vd_ref, on_ref, od_ref, *, lam_init):
    tiles = range(D_GROUP // PAIR_W)
    lam = _diff_lambda(lp_ref[...], lam_init)
    q = [_split_pair(t) for t in _pair_tiles((qn_ref[...] * ATTN_SCALE).astype(BF16))
         + _pair_tiles((qd_ref[...] * ATTN_SCALE).astype(BF16))]
    k = _pair_tiles(kn_ref[...].astype(BF16)) + _pair_tiles(kd_ref[...].astype(BF16))
    v = [_with_ones(t) for t in _pair_tiles(vn_ref[...].astype(BF16)) + _pair_tiles(vd_ref[...].astype(BF16))]
    e = [[_exp_scores(_bdot_nt(q[j][c], k[j])) for c in range(2)] for j in range(2 * len(tiles))]
    out = [[jnp.dot(e[j][c], v[j], preferred_element_type=F32) for c in range(2)] for j in range(2 * len(tiles))]
    for j in tiles:
        sl = slice(j * PAIR_W, (j + 1) * PAIR_W)
        on_ref[:, sl] = _merge_pair(out[j][0], out[j][1])
        od_ref[:, sl] = _diff_combine(out[len(tiles) + j][0], out[len(tiles) + j][1], lam, lam_init)


def _ctx_attention(lam_params, qn, kn, vn, qd, kd, vd, seq, lam_init):
    n = qn.shape[0]
    blk = pl.BlockSpec((seq, D_GROUP), lambda b: (b, 0))
    return pl.pallas_call(
        functools.partial(_ctx_attn_kernel, lam_init=lam_init),
        grid=(n // seq,),
        in_specs=[pl.BlockSpec(lam_params.shape, lambda b: (0, 0))] + [blk] * 6,
        out_specs=[blk, blk],
        out_shape=[jax.ShapeDtypeStruct((n, D_GROUP), F32)] * 2,
        compiler_params=_cparams("arbitrary"),
        name="ctx_attention",
    )(lam_params, qn, kn, vn, qd, kd, vd)


def _rope_tables(n):
    quarter = HEAD_DIM // 4
    pos = jnp.arange(n)
    inv_freq = ROPE_BASE ** (-jnp.arange(quarter, dtype=F32) / quarter)
    lane = jnp.arange(HEAD_DIM)
    p = jnp.where(lane[None, :] < HEAD_DIM // 2, (pos // GRID_W)[:, None], (pos % GRID_W)[:, None]).astype(F32)
    ang = p * inv_freq[lane % quarter][None, :]
    sign = jnp.where((lane % (2 * quarter)) < quarter, -1.0, 1.0).astype(F32)
    cos = jnp.tile(jnp.cos(ang), (1, N_HEADS))
    sin = jnp.tile(jnp.sin(ang) * sign[None, :], (1, N_HEADS))
    return cos, sin


def _rope(x, cos, sin_signed):
    quarter = HEAD_DIM // 4
    width = x.shape[-1]
    lane = lax.broadcasted_iota(jnp.int32, x.shape, 1)
    first = (lane % (2 * quarter)) < quarter
    partner = jnp.where(first, pltpu.roll(x, width - quarter, 1), pltpu.roll(x, quarter, 1))
    return x * cos + partner * sin_signed


def _lat_diff_kernel(lp_ref, q_ref, k_ref, v_ref, ck_ref, cv_ref, cosq_ref, sinq_ref, cosk_ref, sink_ref,
                     o_ref, kall, vall, *, lam_init, n_lat):
    @pl.when(pl.program_id(1) == 0)
    def _():
        kall[0:n_lat, :] = _rope(k_ref[...], cosk_ref[...], sink_ref[...]).astype(BF16)
        kall[n_lat:, :] = ck_ref[...].astype(BF16)
        for j, (vt, ct) in enumerate(zip(_pair_tiles(v_ref[...].astype(BF16)),
                                         _pair_tiles(cv_ref[...].astype(BF16)))):
            vall[j, 0:n_lat, :] = _with_ones(vt)
            vall[j, n_lat:, :] = _with_ones(ct)

    lam = _diff_lambda(lp_ref[...], lam_init)
    q = (_rope(q_ref[...], cosq_ref[...], sinq_ref[...]) * ATTN_SCALE).astype(BF16)
    for j, qt in enumerate(_pair_tiles(q)):
        sl = slice(j * PAIR_W, (j + 1) * PAIR_W)
        out = [jnp.dot(_exp_scores(_bdot_nt(qc, kall[:, sl])), vall[j], preferred_element_type=F32)
               for qc in _split_pair(qt)]
        o_ref[:, sl] = _diff_combine(out[0], out[1], lam, lam_init)


def _lat_diff_attention(lam_params, qd, kd, vd, cache_k, cache_v, n_lat, lam_init, tq=256):
    n = qd.shape[0]
    nb = n // n_lat
    nq = n_lat // tq
    past = cache_k.shape[1]
    cos, sin = _rope_tables(n_lat)
    qblk = pl.BlockSpec((tq, D_GROUP), lambda b, i: (b * nq + i, 0))
    once = pl.Buffered(1)
    kvblk = pl.BlockSpec((n_lat, D_GROUP), lambda b, i: (b, 0), pipeline_mode=once)
    cblk = pl.BlockSpec((None, past, D_GROUP), lambda b, i: (b, 0, 0))
    return pl.pallas_call(
        functools.partial(_lat_diff_kernel, lam_init=lam_init, n_lat=n_lat),
        grid=(nb, nq),
        in_specs=[pl.BlockSpec(lam_params.shape, lambda b, i: (0, 0)),
                  qblk, kvblk, kvblk, cblk, cblk,
                  pl.BlockSpec((tq, D_GROUP), lambda b, i: (i, 0)),
                  pl.BlockSpec((tq, D_GROUP), lambda b, i: (i, 0)),
                  pl.BlockSpec((n_lat, D_GROUP), lambda b, i: (0, 0), pipeline_mode=once),
                  pl.BlockSpec((n_lat, D_GROUP), lambda b, i: (0, 0), pipeline_mode=once)],
        out_specs=qblk,
        out_shape=jax.ShapeDtypeStruct((n, D_GROUP), F32),
        scratch_shapes=[pltpu.VMEM((n_lat + past, D_GROUP), BF16),
                        pltpu.VMEM((D_GROUP // PAIR_W, n_lat + past, 2 * PAIR_W), BF16)],
        compiler_params=_cparams("arbitrary", "arbitrary"),
        name="lat_diff_attention",
    )(lam_params, qd, kd, vd, cache_k, cache_v, cos, sin, cos, sin)


def _na_bias_kernel(rpb_ref, tt_ref):
    h = pl.program_id(0)
    lane = lax.broadcasted_iota(jnp.int32, (GRID_W, 2 * GRID_W), 1)
    qc = lax.broadcasted_iota(jnp.int32, (GRID_W, 2 * GRID_W), 0)
    dc = jnp.clip(lane % GRID_W - qc + (WIN_W - 1), 0, RPB_C - 1)
    first = lane < GRID_W
    for dr in range(RPB_R - 1):
        acc = jnp.zeros((GRID_W, 2 * GRID_W), F32)
        for d in range(RPB_C):
            val = jnp.where(first, rpb_ref[h * RPB_R + dr, d], rpb_ref[h * RPB_R + dr + 1, d])
            acc = jnp.where(dc == d, val, acc)
        tt_ref[0, dr] = acc


def _na_bias_table(rpb):
    return pl.pallas_call(
        _na_bias_kernel,
        grid=(N_HEADS,),
        in_specs=[pl.BlockSpec(memory_space=pltpu.SMEM)],
        out_specs=pl.BlockSpec((1, RPB_R - 1, GRID_W, 2 * GRID_W), lambda h: (h, 0, 0, 0)),
        out_shape=jax.ShapeDtypeStruct((N_HEADS, RPB_R - 1, GRID_W, 2 * GRID_W), F32),
        compiler_params=_cparams("arbitrary"),
        name="na_bias_table",
    )(rpb.reshape(N_HEADS * RPB_R, RPB_C))


def _lat_na_kernel(q_ref, k_ref, v_ref, ck_ref, cv_ref, tt_ref, o_ref):
    r = pl.program_id(1)
    r_start = jnp.clip(r - WIN_R // 2, 0, GRID_ROWS - WIN_R)
    dr0 = r_start - r + (WIN_R - 1)
    row0 = pl.multiple_of(r_start * GRID_W, GRID_W)
    nkeys = WIN_R * GRID_W
    past = ck_ref.shape[0]
    lane = lax.broadcasted_iota(jnp.int32, (GRID_W, nkeys + past), 1)
    kc = lane % GRID_W
    qc = lax.broadcasted_iota(jnp.int32, (GRID_W, nkeys + past), 0)
    c_start = jnp.clip(qc - WIN_W // 2, 0, GRID_W - WIN_W)
    visible = ((kc >= c_start) & (kc < c_start + WIN_W)) | (lane >= nkeys)
    no_bias = jnp.zeros((GRID_W, past), F32)
    q = [_split_pair(t) for t in _pair_tiles((q_ref[...] * ATTN_SCALE).astype(BF16))]
    kcat = _pair_tiles(jnp.concatenate([k_ref[pl.ds(row0, nkeys), :], ck_ref[...]], axis=0).astype(BF16))
    vcat = [_with_ones(t) for t in
            _pair_tiles(jnp.concatenate([v_ref[pl.ds(row0, nkeys), :], cv_ref[...]], axis=0).astype(BF16))]
    bias = [jnp.concatenate([tt_ref[h, dr0 + 2 * j] for j in range(WIN_R // 2)] + [no_bias], axis=1)
            for h in range(N_HEADS)]
    e = [[_exp_scores(jnp.where(visible, _bdot_nt(q[j][c], kcat[j]) + bias[2 * j + c], NEG_INF))
          for c in range(2)] for j in range(len(q))]
    out = [[jnp.dot(e[j][c], vcat[j], preferred_element_type=F32) for c in range(2)] for j in range(len(q))]
    for j in range(len(q)):
        o_ref[:, j * PAIR_W:(j + 1) * PAIR_W] = _merge_pair(out[j][0], out[j][1])


def _lat_na_attention(qn, kn, vn, cache_k, cache_v, tt, n_lat):
    n = qn.shape[0]
    nb = n // n_lat
    past = cache_k.shape[1]
    qblk = pl.BlockSpec((GRID_W, D_GROUP), lambda b, r: (b * GRID_ROWS + r, 0))
    kvblk = pl.BlockSpec((n_lat, D_GROUP), lambda b, r: (b, 0))
    cblk = pl.BlockSpec((None, past, D_GROUP), lambda b, r: (b, 0, 0))
    return pl.pallas_call(
        _lat_na_kernel,
        grid=(nb, GRID_ROWS),
        in_specs=[qblk, kvblk, kvblk, cblk, cblk,
                  pl.BlockSpec(tt.shape, lambda b, r: (0, 0, 0, 0))],
        out_specs=qblk,
        out_shape=jax.ShapeDtypeStruct((n, D_GROUP), F32),
        compiler_params=_cparams("arbitrary", "arbitrary"),
        name="lat_na_attention",
    )(qn, kn, vn, cache_k, cache_v, tt)


def _rec_mix(of_ref, ob_ref, rg_ref, yf_ref, yb_ref, xr_ref, xk_ref, xv_ref, lo_ref, rk_ref, lng_ref, lnb_ref,
             gup_ref):
    pw = 2 * HEAD_DIM
    brow = lax.broadcasted_iota(jnp.int32, (pw, pw), 0) // HEAD_DIM
    bcol = lax.broadcasted_iota(jnp.int32, (pw, pw), 1) // HEAD_DIM
    ones_bd = jnp.where(brow == bcol, 1.0, 0.0).astype(BF16)

    def head_sums(x):
        outs = []
        for t in range(D_GROUP // pw):
            xt = x[:, t * pw:(t + 1) * pw]
            hi = xt.astype(BF16)
            lo = (xt - hi.astype(F32)).astype(BF16)
            outs.append(jnp.dot(hi, ones_bd, preferred_element_type=F32)
                        + jnp.dot(lo, ones_bd, preferred_element_type=F32))
        return jnp.concatenate(outs, axis=1)

    inv_d = 1.0 / HEAD_DIM
    o = of_ref[...] + ob_ref[...]
    o_ret = _silu(rg_ref[...]) * (o * lax.rsqrt(head_sums(o * o) * inv_d + RMS_EPS))
    y = yf_ref[...] + yb_ref[...]
    yc = y - head_sums(y) * inv_d
    var = head_sums(yc * yc) * inv_d
    yn = yc * lax.rsqrt(var + RWKV_GN_EPS) * lng_ref[...] + lnb_ref[...]
    bonus = head_sums(xr_ref[...] * rk_ref[...] * xk_ref[...]) * xv_ref[...]
    g_rw = _bdot(_sigmoid(lo_ref[:, 2 * D_LORA:]), gup_ref[...])
    return o_ret, (yn + bonus) * g_rw


def _tail_kernel(*refs, final, ff_chunk, n_mix):
    x_ref, mix_refs = refs[0], refs[1:1 + n_mix]
    wo_ref, gm_ref, g_ref, sh_ref, sc_ref, gf_ref, wi_ref, wf_ref, fg_ref, o_ref = refs[1 + n_mix:]
    if n_mix == 2:
        mix_a, mix_b = mix_refs[0][...], mix_refs[1][...]
    else:
        mix_a, mix_b = _rec_mix(*mix_refs)
    mix = _bdot(mix_a, wo_ref[0:D_GROUP, :]) + _bdot(mix_b, wo_ref[D_GROUP:, :])
    x1 = x_ref[...] + gm_ref[0] * mix
    h = _rms(x1) * g_ref[...]
    hb = (h * (1.0 + sc_ref[0]) + sh_ref[0]).astype(BF16)
    acc = jnp.zeros_like(x1)
    for c0 in range(0, D_FF, ff_chunk):
        gate = jnp.dot(hb, wi_ref[:, c0:c0 + ff_chunk], preferred_element_type=F32)
        up = jnp.dot(hb, wi_ref[:, D_FF + c0:D_FF + c0 + ff_chunk], preferred_element_type=F32)
        acc = acc + jnp.dot((_silu(gate) * up).astype(BF16), wf_ref[c0:c0 + ff_chunk, :],
                            preferred_element_type=F32)
    x2 = x1 + gf_ref[0] * acc
    if final:
        x2 = _rms(x2) * fg_ref[...]
    o_ref[...] = x2


def _layer_tail(x, mix_inputs, w_out, gate_mix, gain, shift, scale, gate_ffn, w_ffn_in_all, w_ffn_out_all, layer,
                final_gain, rows_per_mod, final, ff_chunk=1408):
    n, d = x.shape
    tm = TAIL_TM if len(mix_inputs) == 2 else TAIL_TM_REC
    tiles_per_mod = rows_per_mod // tm
    mod_spec = pl.BlockSpec((1, 1, d), lambda i: (i // tiles_per_mod, 0, 0))
    vec_spec = pl.BlockSpec((1, d), lambda i: (0, 0))
    resident = lambda a: pl.BlockSpec(a.shape, lambda i: (0,) * a.ndim, pipeline_mode=pl.Buffered(1))
    layer_slab = lambda a: pl.BlockSpec((None,) + a.shape[1:], lambda i: (layer, 0, 0),
                                        pipeline_mode=pl.Buffered(1))
    mix_specs = [pl.BlockSpec((tm, a.shape[1]), lambda i: (i, 0)) if a.shape[0] == n else resident(a)
                 for a in mix_inputs]
    return pl.pallas_call(
        functools.partial(_tail_kernel, final=final, ff_chunk=ff_chunk, n_mix=len(mix_inputs)),
        grid=(n // tm,),
        in_specs=[pl.BlockSpec((tm, d), lambda i: (i, 0))] + mix_specs
                 + [resident(w_out), mod_spec, vec_spec, mod_spec, mod_spec, mod_spec,
                    layer_slab(w_ffn_in_all), layer_slab(w_ffn_out_all), vec_spec],
        out_specs=pl.BlockSpec((tm, d), lambda i: (i, 0)),
        out_shape=jax.ShapeDtypeStruct((n, d), F32),
        compiler_params=_cparams("arbitrary"),
        name="layer_tail",
    )(x, *mix_inputs, w_out, gate_mix, gain.reshape(1, d), shift, scale, gate_ffn, w_ffn_in_all, w_ffn_out_all,
      final_gain.reshape(1, d))


def _log_sigmoid(x):
    return jnp.minimum(x, 0.0) - jnp.log(1.0 + jnp.exp(-jnp.abs(x)))


def _pair_block_diag(xp):
    left = lax.broadcasted_iota(jnp.int32, xp.shape, 1) < HEAD_DIM
    zero = jnp.zeros_like(xp)
    return jnp.concatenate([jnp.where(left, xp, zero), jnp.where(left, zero, xp)], axis=0)


def _states_to_pairs(state):
    nb = state.shape[0]
    s = state.reshape(nb, 2, N_HEADS // 2, 2, HEAD_DIM, HEAD_DIM)
    zero = jnp.zeros_like(s[:, :, :, 0])
    return jnp.concatenate([jnp.concatenate([s[:, :, :, 0], zero], axis=-1),
                            jnp.concatenate([zero, s[:, :, :, 1]], axis=-1)], axis=-2)


def _pairs_to_states(sbd):
    nb = sbd.shape[0]
    s = jnp.stack([sbd[..., :HEAD_DIM, :HEAD_DIM], sbd[..., HEAD_DIM:, HEAD_DIM:]], axis=3)
    return s.reshape(nb, 2, N_HEADS, HEAD_DIM, HEAD_DIM)


def _ret_kernel(dlm_ref, dlq_ref, qf_ref, kf_ref, vf_ref, qb_ref, kb_ref, vb_ref, s0_ref,
                of_ref, ob_ref, sf_ref, st):
    c = pl.program_id(1)
    cs = RET_CHUNK
    pw = 2 * HEAD_DIM
    npair = N_HEADS // 2

    @pl.when(c == 0)
    def _():
        st[...] = s0_ref[0]

    row = lax.broadcasted_iota(jnp.int32, (cs, 2 * cs), 0)
    col = lax.broadcasted_iota(jnp.int32, (cs, 2 * cs), 1) % cs
    tok = lax.broadcasted_iota(jnp.int32, (cs, pw), 0)
    brow = lax.broadcasted_iota(jnp.int32, (pw, pw), 0) // HEAD_DIM
    bcol = lax.broadcasted_iota(jnp.int32, (pw, pw), 1) // HEAD_DIM
    same_head = brow == bcol
    refs = [(qf_ref, kf_ref, vf_ref, of_ref), (qb_ref, kb_ref, vb_ref, ob_ref)]
    chains = [(d, p) for d in range(2) for p in range(npair)]
    dist = [row - col, col - row]
    tq = [tok.astype(F32), (cs - 1 - tok).astype(F32)]
    sls = [slice(p * pw, (p + 1) * pw) for _, p in chains]
    lgm = [_log_sigmoid(dlm_ref[d, p])[0:1, :] for d, p in chains]
    lgq = [_log_sigmoid(dlq_ref[d, p])[0:1, :] for d, p in chains]
    mask = [jnp.where(dist[d] >= 0, jnp.exp(jnp.maximum(dist[d], 0).astype(F32) * lgm[i]), 0.0)
            for i, (d, _) in enumerate(chains)]
    q = [refs[d][0][:, sls[i]] for i, (d, _) in enumerate(chains)]
    k = [refs[d][1][:, sls[i]] * ATTN_SCALE for i, (d, _) in enumerate(chains)]
    v = [refs[d][2][:, sls[i]] for i, (d, _) in enumerate(chains)]
    idx = range(len(chains))
    s = [_bdot_nt(q[i], _pair_block_diag(k[i])) * mask[i] for i in idx]
    inner = [_bdot(s[i], _pair_block_diag(v[i])) for i in idx]
    state = [st[d, p] for d, p in chains]
    cross = [_bdot(q[i], state[i]) * jnp.exp((tq[d] + 1.0) * lgq[i]) for i, (d, _) in enumerate(chains)]
    kv = [_bdot_tn(k[i] * jnp.exp((cs - 1.0 - tq[d]) * lgq[i]), v[i]) for i, (d, _) in enumerate(chains)]
    for i, (d, p) in enumerate(chains):
        refs[d][3][:, sls[i]] = inner[i] + cross[i]
        st[d, p] = state[i] * jnp.exp(cs * lgq[i]) + jnp.where(same_head, kv[i], 0.0)

    @pl.when(c == pl.num_programs(1) - 1)
    def _():
        sf_ref[0] = st[...]


def _retention_scan(decay_logit, q, k, v, state0, seq):
    n = q.shape[0]
    nb = n // seq
    nc = seq // RET_CHUNK
    npair, pw = N_HEADS // 2, 2 * HEAD_DIM
    dl = decay_logit.astype(F32)
    dl_m = jnp.broadcast_to(jnp.repeat(dl, RET_CHUNK, axis=1).reshape(2, npair, 1, 2 * RET_CHUNK),
                            (2, npair, 8, 2 * RET_CHUNK))
    dl_q = jnp.broadcast_to(jnp.repeat(dl, HEAD_DIM, axis=1).reshape(2, npair, 1, pw), (2, npair, 8, pw))
    blk_f = pl.BlockSpec((RET_CHUNK, D_GROUP), lambda b, c: (b * nc + c, 0))
    blk_b = pl.BlockSpec((RET_CHUNK, D_GROUP), lambda b, c: (b * nc + nc - 1 - c, 0))
    st_blk = pl.BlockSpec((1, 2, npair, pw, pw), lambda b, c: (b, 0, 0, 0, 0))
    o_f, o_b, sf_bd = pl.pallas_call(
        _ret_kernel,
        grid=(nb, nc),
        in_specs=[pl.BlockSpec(dl_m.shape, lambda b, c: (0, 0, 0, 0)),
                  pl.BlockSpec(dl_q.shape, lambda b, c: (0, 0, 0, 0)),
                  blk_f, blk_f, blk_f, blk_b, blk_b, blk_b, st_blk],
        out_specs=[blk_f, blk_b, st_blk],
        out_shape=[jax.ShapeDtypeStruct((n, D_GROUP), F32), jax.ShapeDtypeStruct((n, D_GROUP), F32),
                   jax.ShapeDtypeStruct((nb, 2, npair, pw, pw), F32)],
        scratch_shapes=[pltpu.VMEM((2, npair, pw, pw), F32)],
        compiler_params=_cparams("arbitrary", "arbitrary"),
        name="retention_scan",
    )(dl_m, dl_q, q, k, v, q, k, v, _states_to_pairs(state0))
    return o_f, o_b, _pairs_to_states(sf_bd)


def _rwkv_kernel(xrf_ref, xkf_ref, xvf_ref, lof_ref, xrb_ref, xkb_ref, xvb_ref, lob_ref,
                 w0_ref, wup_ref, a0_ref, aup_ref, kk_ref, ka_ref, s0_ref, yf_ref, yb_ref, sf_ref, st):
    c = pl.program_id(1)
    cs = RWKV_CHUNK
    npair = N_HEADS // 2

    @pl.when(c == 0)
    def _():
        st[...] = s0_ref[0]

    row = lax.broadcasted_iota(jnp.int32, (cs, cs), 0)
    col = lax.broadcasted_iota(jnp.int32, (cs, cs), 1)
    pw = 2 * HEAD_DIM
    prow = lax.broadcasted_iota(jnp.int32, (cs, pw), 0)
    pcol = lax.broadcasted_iota(jnp.int32, (cs, pw), 1) % HEAD_DIM
    eye = jnp.where(pcol == prow, 1.0, 0.0).astype(F32)
    sub_diag = (prow // RWKV_SUB) == (pcol // RWKV_SUB)
    brow = lax.broadcasted_iota(jnp.int32, (pw, pw), 0) // HEAD_DIM
    bcol = lax.broadcasted_iota(jnp.int32, (pw, pw), 1) // HEAD_DIM
    same_head = brow == bcol
    ones_bd = jnp.where(same_head, 1.0, 0.0).astype(BF16)

    bd = _pair_block_diag

    def direction(d, xr_ref, xk_ref, xv_ref, lo_ref):
        tri = jnp.where((row >= col) if d == 0 else (col >= row), 1.0, 0.0).astype(BF16)
        dist = (prow - pcol) if d == 0 else (pcol - prow)
        xk = xk_ref[...]
        lo = lo_ref[...]
        w_log = -math.exp(-0.5) * _sigmoid(w0_ref[d] + _bdot(jnp.tanh(lo[:, 0:D_LORA]), wup_ref[d]))
        a_all = _sigmoid(a0_ref[d] + _bdot(lo[:, D_LORA:2 * D_LORA], aup_ref[d]))
        w1 = w_log.astype(BF16)
        r1 = w_log - w1.astype(F32)
        w2 = r1.astype(BF16)
        w3 = (r1 - w2.astype(F32)).astype(BF16)
        cum = (jnp.dot(tri, w1, preferred_element_type=F32) + jnp.dot(tri, w2, preferred_element_type=F32)
               + jnp.dot(tri, w3, preferred_element_type=F32))
        cum_end = cum[cs - 1:cs, :] if d == 0 else cum[0:1, :]
        return dict(earlier=dist > 0, upto=dist >= 0, xr=xr_ref[...], xv=xv_ref[...], a_all=a_all,
                    e_incl=jnp.exp(cum), e_excl=jnp.exp(cum - w_log), e_neg=jnp.exp(-cum),
                    e_end=jnp.exp(cum_end - cum), w_end=jnp.exp(cum_end), kk_all=xk * kk_ref[...],
                    keff_all=xk * (1.0 + (a_all - 1.0) * ka_ref[...]))

    dirs = [direction(0, xrf_ref, xkf_ref, xvf_ref, lof_ref), direction(1, xrb_ref, xkb_ref, xvb_ref, lob_ref)]
    y_refs = [yf_ref, yb_ref]

    chains = [(d, p) for d in range(2) for p in range(npair)]
    pairs = range(len(chains))
    sls = [slice(p * pw, (p + 1) * pw) for _, p in chains]
    pick = lambda name: [dirs[d][name][:, sls[i]] for i, (d, _) in enumerate(chains)]
    earlier = [dirs[d]["earlier"] for d, _ in chains]
    upto = [dirs[d]["upto"] for d, _ in chains]
    cat = lambda x, y: jnp.concatenate([x, y], axis=0)

    def head_sum(x):
        x1 = x.astype(BF16)
        r1 = x - x1.astype(F32)
        x2 = r1.astype(BF16)
        x3 = (r1 - x2.astype(F32)).astype(BF16)
        f = functools.partial(jnp.dot, preferred_element_type=F32)
        return f(x1, ones_bd) + (f(x2, ones_bd) + f(x3, ones_bd))

    e_incl, e_excl, e_neg, e_end, w_end = (pick(k) for k in ("e_incl", "e_excl", "e_neg", "e_end", "w_end"))
    kk = [x * lax.rsqrt(head_sum(x * x) + 1e-12) for x in pick("kk_all")]
    b = [kk[p] * a for p, a in zip(pairs, pick("a_all"))]
    keff = pick("keff_all")
    v = pick("xv")
    a_t = [-kk[p] * e_excl[p] for p in pairs]
    r_t = [xr * e_incl[p] for p, xr in zip(pairs, pick("xr"))]
    ar = [cat(a_t[p], r_t[p]).astype(BF16) for p in pairs]
    g_b = [_bdot_nt(ar[p], bd(b[p] * e_neg[p])) for p in pairs]
    g_k = [_bdot_nt(ar[p], bd(keff[p] * e_neg[p])) for p in pairs]
    n_mat = [jnp.where(earlier[p], g_b[p][:cs], 0.0) for p in pairs]
    m_rb = [jnp.where(upto[p], g_b[p][cs:], 0.0) for p in pairs]
    a_ak = [jnp.where(earlier[p], g_k[p][:cs], 0.0) for p in pairs]
    m_rk = [jnp.where(upto[p], g_k[p][cs:], 0.0) for p in pairs]
    bd_s = lambda xs: (bd(xs[0]), bd(xs[1]))
    n_d = [jnp.where(sub_diag, n, 0.0) for n in n_mat]
    n_o = [jnp.where(sub_diag, 0.0, n) for n in n_mat]
    n_s = [_split2(n) for n in n_d]
    x = [eye + n for n in n_d]
    pk = [_mm3s(ns, bd_s(ns)) for ns in n_s]
    for _ in range(RWKV_SUB_DOUBLINGS - 1):
        xp = [_mm3s(_split2(cat(x[p], pk[p])), bd_s(_split2(pk[p]))) for p in pairs]
        x = [x[p] + xp[p][:cs] for p in pairs]
        pk = [r[cs:] for r in xp]
    t_d = [x[p] + _mm3s(_split2(x[p]), bd_s(_split2(pk[p]))) for p in pairs]
    m_o = [_bdot(t_d[p], bd(n_o[p])) for p in pairs]
    z = [t_d[p] + _bdot(m_o[p], bd(t_d[p])) for p in pairs]
    m_2 = [_bdot(m_o[p], bd(m_o[p])) for p in pairs]
    t_inv = [(z[p] + _bdot(m_2[p], bd(z[p]))).astype(BF16) for p in pairs]
    av = [_bdot(cat(a_ak[p], m_rk[p]), bd(v[p])) for p in pairs]
    a_hat = [_bdot(t_inv[p], bd(a_t[p])) for p in pairs]
    u0 = [_bdot(t_inv[p], bd(av[p][:cs])) for p in pairs]
    s_prev = [st[d, p] for d, p in chains]
    hs = [_bdot_nt(cat(a_hat[p], r_t[p]), s_prev[p]) for p in pairs]
    u = [hs[p][:cs] + u0[p] for p in pairs]
    y = [hs[p][cs:] + _bdot(m_rb[p], bd(u[p])) + av[p][cs:] for p in pairs]
    for i, (d, p) in enumerate(chains):
        y_refs[d][:, sls[i]] = y[i]
        bk_end = cat(b[i] * e_end[i], keff[i] * e_end[i])
        st[d, p] = s_prev[i] * w_end[i] + jnp.where(same_head, _bdot_tn(cat(u[i], v[i]), bk_end), 0.0)

    @pl.when(c == pl.num_programs(1) - 1)
    def _():
        sf_ref[0] = st[...]


def _rwkv_scan(xr, xk, xv, lora, w0, w_up, a0, a_up, k_k, k_a, state0, seq):
    n = xr.shape[0]
    nb = n // seq
    nc = seq // RWKV_CHUNK

    fwd_idx = lambda b, c: (b * nc + c, 0)
    bwd_idx = lambda b, c: (b * nc + nc - 1 - c, 0)
    blk_f = pl.BlockSpec((RWKV_CHUNK, D_GROUP), fwd_idx)
    blk_b = pl.BlockSpec((RWKV_CHUNK, D_GROUP), bwd_idx)
    lblk_f = pl.BlockSpec((RWKV_CHUNK, lora.shape[1]), fwd_idx)
    lblk_b = pl.BlockSpec((RWKV_CHUNK, lora.shape[1]), bwd_idx)
    dvec = pl.BlockSpec((2, 1, D_GROUP), lambda b, c: (0, 0, 0))
    dmat = pl.BlockSpec((2, D_LORA, D_GROUP), lambda b, c: (0, 0, 0))
    vec = pl.BlockSpec((1, D_GROUP), lambda b, c: (0, 0))
    npair, pw = N_HEADS // 2, 2 * HEAD_DIM
    st_blk = pl.BlockSpec((1, 2, npair, pw, pw), lambda b, c: (b, 0, 0, 0, 0))
    y_f, y_b, sf_bd = pl.pallas_call(
        _rwkv_kernel,
        grid=(nb, nc),
        in_specs=[blk_f, blk_f, blk_f, lblk_f, blk_b, blk_b, blk_b, lblk_b,
                  dvec, dmat, dvec, dmat, vec, vec, st_blk],
        out_specs=[blk_f, blk_b, st_blk],
        out_shape=[jax.ShapeDtypeStruct((n, D_GROUP), F32), jax.ShapeDtypeStruct((n, D_GROUP), F32),
                   jax.ShapeDtypeStruct((nb, 2, npair, pw, pw), F32)],
        scratch_shapes=[pltpu.VMEM((2, npair, pw, pw), F32)],
        compiler_params=_cparams("arbitrary", "arbitrary"),
        name="rwkv7_scan",
    )(xr, xk, xv, lora, xr, xk, xv, lora, w0.reshape(2, 1, D_GROUP), w_up, a0.reshape(2, 1, D_GROUP), a_up,
      k_k.reshape(1, D_GROUP), k_a.reshape(1, D_GROUP), _states_to_pairs(state0))
    return y_f, y_b, _pairs_to_states(sf_bd)


ATTN_WIDTHS = (D_GROUP,) * 6
REC_WIDTHS = (D_GROUP,) * 7 + (2 * D_LORA + D_LORA_G,)


def kernel(x_prompt, x_sample, cache_na_k, cache_na_v, cache_diff_k, cache_diff_v, state_ret, state_rwkv, c, c_ctx, norm_mix_g, norm_ffn_g, norm_final_g, w_ada, b_ada, w_in_attn, w_out_attn, na_rpb, diff_lq1, diff_lk1, diff_lq2, diff_lk2, w_in_rec, w_out_rec, ret_decay_logit, rw_w0, rw_w_up, rw_a0, rw_a_up, rw_g_up, rw_k_k, rw_k_a, rw_r_k, rw_ln_g, rw_ln_b, w_ffn_in, w_ffn_out):
    bp, seq, d = x_prompt.shape
    bl, n_lat, _ = x_sample.shape
    depth = w_ada.shape[0]
    ctx = x_prompt.reshape(bp * seq, d)
    lat = x_sample.reshape(bl * n_lat, d)

    cond8 = jnp.zeros((8, d), F32).at[0].set(c_ctx).at[1:1 + bl].set(c)
    mods = _ada_modulation(cond8, w_ada, b_ada).reshape(depth, 8, 6, d)
    w_ffn_in_bf16 = w_ffn_in.astype(BF16)
    w_ffn_out_bf16 = w_ffn_out.astype(BF16)

    outs = {}
    for layer in range(depth):
        m_ctx = [mods[layer, 0:1, j].reshape(1, 1, d) for j in range(6)]
        m_lat = [mods[layer, 1:1 + bl, j].reshape(bl, 1, d) for j in range(6)]
        if layer % 2 == 0:
            i = layer // 2
            lam_init = 0.8 - 0.6 * math.exp(-0.3 * layer)
            lam_params = jnp.stack([diff_lq1[i], diff_lk1[i], diff_lq2[i], diff_lk2[i]]).astype(F32)
            w_in = w_in_attn[i].astype(BF16)
            w_out = w_out_attn[i].astype(BF16)
            cache_shapes = {1: (N_HEADS, HEAD_DIM), 2: (N_HEADS, HEAD_DIM),
                            4: (N_HEADS // 2, 2, HEAD_DIM), 5: (N_HEADS // 2, 2 * HEAD_DIM)}
            pc = _norm_mod_matmul(ctx, norm_mix_g[layer], m_ctx[0], m_ctx[1], w_in, ATTN_WIDTHS, bp * seq,
                                  PROJ_TM_CACHE, cache_shapes)
            outs["na_k"], outs["na_v"], outs["df_k"], outs["df_v"] = pc[6:]
            pc = pc[:6]
            plat = _norm_mod_matmul(lat, norm_mix_g[layer], m_lat[0], m_lat[1], w_in, ATTN_WIDTHS, n_lat, PROJ_TM)
            mix_ctx = _ctx_attention(lam_params, *pc, seq, lam_init)
            past = cache_na_k.shape[2]
            tt = _na_bias_table(na_rpb[i])
            o_na = _lat_na_attention(plat[0], plat[1], plat[2],
                                     cache_na_k[:, i].reshape(bl, past, D_GROUP),
                                     cache_na_v[:, i].reshape(bl, past, D_GROUP), tt, n_lat)
            o_df = _lat_diff_attention(lam_params, plat[3], plat[4], plat[5],
                                       cache_diff_k[:, i].reshape(bl, past, D_GROUP),
                                       cache_diff_v[:, i].reshape(bl, past, D_GROUP), n_lat, lam_init)
            mix_lat = (o_na, o_df)
        else:
            j = layer // 2
            w_in = w_in_rec[j].astype(BF16)
            w_out = w_out_rec[j].astype(BF16)
            pc = _norm_mod_matmul(ctx, norm_mix_g[layer], m_ctx[0], m_ctx[1], w_in, REC_WIDTHS, bp * seq, PROJ_TM)
            plat = _norm_mod_matmul(lat, norm_mix_g[layer], m_lat[0], m_lat[1], w_in, REC_WIDTHS, n_lat, PROJ_TM)
            zero_state = jnp.zeros((bp, 2, N_HEADS, HEAD_DIM, HEAD_DIM), F32)
            mixes = []
            for is_ctx, p, s_ret0, s_rw0, sq in ((True, pc, zero_state, zero_state, seq),
                                                 (False, plat, state_ret[:, j], state_rwkv[:, j], n_lat)):
                rq, rk, rv, rg, wr, wk, wv, lora = p
                o_f, o_b, s_ret = _retention_scan(ret_decay_logit[j], rq, rk, rv, s_ret0, sq)
                y_f, y_b, s_rw = _rwkv_scan(wr, wk, wv, lora, rw_w0[j], rw_w_up[j].astype(BF16), rw_a0[j],
                                            rw_a_up[j].astype(BF16), rw_k_k[j], rw_k_a[j], s_rw0, sq)
                mixes.append((o_f, o_b, rg, y_f, y_b, wr, wk, wv, lora, rw_r_k[j].reshape(1, D_GROUP),
                              rw_ln_g[j].reshape(1, D_GROUP), rw_ln_b[j].reshape(1, D_GROUP),
                              rw_g_up[j].astype(BF16)))
                if is_ctx:
                    outs["ret"], outs["rwkv"] = s_ret, s_rw
            mix_ctx, mix_lat = mixes
        final = layer == depth - 1
        ctx = _layer_tail(ctx, mix_ctx, w_out, m_ctx[2], norm_ffn_g[layer], m_ctx[3], m_ctx[4], m_ctx[5],
                          w_ffn_in_bf16, w_ffn_out_bf16, layer, norm_final_g, bp * seq, final)
        lat = _layer_tail(lat, mix_lat, w_out, m_lat[2], norm_ffn_g[layer], m_lat[3], m_lat[4], m_lat[5],
                          w_ffn_in_bf16, w_ffn_out_bf16, layer, norm_final_g, n_lat, final)

    y_prompt = ctx.reshape(bp, seq, d)
    y_sample = lat.reshape(bl, n_lat, d)
    new_cache_na_k = outs["na_k"].reshape(bp, 1, seq, N_HEADS, HEAD_DIM)
    new_cache_na_v = outs["na_v"].reshape(bp, 1, seq, N_HEADS, HEAD_DIM)
    new_cache_diff_k = outs["df_k"].reshape(bp, 1, seq, N_HEADS // 2, 2, HEAD_DIM)
    new_cache_diff_v = outs["df_v"].reshape(bp, 1, seq, N_HEADS // 2, 2 * HEAD_DIM)
    new_state_ret = outs["ret"].reshape(bp, 1, 2, N_HEADS, HEAD_DIM, HEAD_DIM)
    new_state_rwkv = outs["rwkv"].reshape(bp, 1, 2, N_HEADS, HEAD_DIM, HEAD_DIM)
    return (y_prompt, y_sample, new_cache_na_k, new_cache_na_v, new_cache_diff_k, new_cache_diff_v,
            new_state_ret, new_state_rwkv)
```

```python
import functools
import math

import jax
import jax.numpy as jnp
from jax import lax
from jax.experimental import pallas as pl
from jax.experimental.pallas import tpu as pltpu

F32 = jnp.float32
BF16 = jnp.bfloat16

D_MODEL = 1024
HEAD_DIM = 64
N_HEADS = 8
D_GROUP = 512
GRID_W = 64
GRID_ROWS = 32
WIN_R = 8
WIN_W = 16
RPB_R = 15
RPB_C = 31
PAST_LEN = 256
D_FF = 2816
D_LORA = 64
D_LORA_G = 128
ROPE_BASE = 10000.0
RMS_EPS = 1e-6
RWKV_GN_EPS = 64e-5
NEG_INF = -1e30
ATTN_SCALE = HEAD_DIM ** -0.5
RET_CHUNK = 128
RWKV_CHUNK = 64
RWKV_SUB = 16
RWKV_SUB_DOUBLINGS = 3
VMEM_LIMIT = 56 * 1024 * 1024
TAIL_TM = 512
TAIL_TM_REC = 256
PROJ_TM = 512
PROJ_TM_CACHE = 256


def _cparams(*sem):
    return pltpu.CompilerParams(dimension_semantics=sem, vmem_limit_bytes=VMEM_LIMIT)


def _sigmoid(x):
    return 1.0 / (1.0 + jnp.exp(-x))


def _silu(x):
    return x * _sigmoid(x)


def _rms(x, eps=RMS_EPS):
    return x * lax.rsqrt(jnp.mean(x * x, axis=-1, keepdims=True) + eps)


def _bdot(a, b):
    return jnp.dot(a.astype(BF16), b.astype(BF16), preferred_element_type=F32)


def _bdot_nt(a, b):
    return lax.dot_general(a.astype(BF16), b.astype(BF16), (((1,), (1,)), ((), ())),
                           preferred_element_type=F32)


def _bdot_tn(a, b):
    return lax.dot_general(a.astype(BF16), b.astype(BF16), (((0,), (0,)), ((), ())),
                           preferred_element_type=F32)


_DN = {"nn": (((1,), (0,)), ((), ())), "nt": (((1,), (1,)), ((), ())), "tn": (((0,), (0,)), ((), ()))}


def _split2(x):
    hi = x.astype(BF16)
    lo = (x - hi.astype(F32)).astype(BF16)
    return hi, lo


def _mm3s(a_split, b_split):
    ah, al = a_split
    bh, bl = b_split
    d = functools.partial(jnp.dot, preferred_element_type=F32)
    return d(jnp.concatenate([ah, al], axis=1), jnp.concatenate([bh, bh], axis=0)) + d(ah, bl)


def _diff_lambda(lp, lam_init):
    s1 = jnp.sum(lp[0:1, :] * lp[1:2, :], axis=-1, keepdims=True)
    s2 = jnp.sum(lp[2:3, :] * lp[3:4, :], axis=-1, keepdims=True)
    return jnp.exp(s1) - jnp.exp(s2) + lam_init


def _ada_kernel(c_ref, w_ref, b_ref, o_ref):
    s = _silu(c_ref[...])
    o_ref[0] = _bdot(s, w_ref[0]) + b_ref[0]


def _ada_modulation(cond8, w_ada, b_ada):
    depth, d, n = w_ada.shape
    tn = 1536
    return pl.pallas_call(
        _ada_kernel,
        grid=(depth, n // tn),
        in_specs=[pl.BlockSpec((8, d), lambda l, j: (0, 0)),
                  pl.BlockSpec((1, d, tn), lambda l, j: (l, 0, j)),
                  pl.BlockSpec((1, 1, tn), lambda l, j: (l, 0, j))],
        out_specs=pl.BlockSpec((1, 8, tn), lambda l, j: (l, 0, j)),
        out_shape=jax.ShapeDtypeStruct((depth, 8, n), F32),
        compiler_params=_cparams("arbitrary", "arbitrary"),
        name="ada_modulation",
    )(cond8, w_ada, b_ada.reshape(depth, 1, n))


def _nmm_kernel(x_ref, g_ref, sh_ref, sc_ref, w_ref, *o_refs, widths, cache_shapes):
    h = _rms(x_ref[...]) * g_ref[...]
    h = h * (1.0 + sc_ref[0]) + sh_ref[0]
    hb = h.astype(BF16)
    c_refs = dict(zip(cache_shapes, o_refs[len(widths):]))
    off = 0
    for gi, (o_ref, w) in enumerate(zip(o_refs, widths)):
        res = jnp.dot(hb, w_ref[:, off:off + w], preferred_element_type=F32)
        o_ref[...] = res
        off += w
        if gi in c_refs:
            c_refs[gi][...] = res.reshape((res.shape[0],) + cache_shapes[gi])


def _norm_mod_matmul(x, gain, shift, scale, w_bf16, widths, rows_per_mod, tm, cache_shapes=None):
    n, d = x.shape
    cache_shapes = cache_shapes or {}
    tiles_per_mod = rows_per_mod // tm
    mod_spec = pl.BlockSpec((1, 1, d), lambda i: (i // tiles_per_mod, 0, 0))
    cache_specs = [pl.BlockSpec((tm,) + dims, lambda i, nd=len(dims): (i,) + (0,) * nd)
                   for dims in cache_shapes.values()]
    return pl.pallas_call(
        functools.partial(_nmm_kernel, widths=widths, cache_shapes=cache_shapes),
        grid=(n // tm,),
        in_specs=[pl.BlockSpec((tm, d), lambda i: (i, 0)),
                  pl.BlockSpec((1, d), lambda i: (0, 0)),
                  mod_spec, mod_spec,
                  pl.BlockSpec(w_bf16.shape, lambda i: (0, 0), pipeline_mode=pl.Buffered(1))],
        out_specs=[pl.BlockSpec((tm, w), lambda i: (i, 0)) for w in widths] + cache_specs,
        out_shape=([jax.ShapeDtypeStruct((n, w), F32) for w in widths]
                   + [jax.ShapeDtypeStruct((n,) + dims, F32) for dims in cache_shapes.values()]),
        compiler_params=_cparams("arbitrary"),
        name="norm_mod_matmul",
    )(x, gain.reshape(1, d), shift, scale, w_bf16)


PAIR_W = 2 * HEAD_DIM


def _pair_tiles(x):
    return [x[:, j * PAIR_W:(j + 1) * PAIR_W] for j in range(x.shape[1] // PAIR_W)]


def _split_pair(tile):
    left = lax.broadcasted_iota(jnp.int32, tile.shape, 1) < HEAD_DIM
    zero = jnp.zeros_like(tile)
    return jnp.where(left, tile, zero), jnp.where(left, zero, tile)


def _with_ones(v):
    return jnp.concatenate([v, jnp.ones_like(v)], axis=1)


def _exp_scores(s):
    return jnp.exp(s - jnp.max(s, axis=-1, keepdims=True)).astype(BF16)


def _merge_pair(out_even, out_odd):
    left = lax.broadcasted_iota(jnp.int32, (out_even.shape[0], PAIR_W), 1) < HEAD_DIM
    return jnp.where(left, out_even[:, :PAIR_W] * (1.0 / out_even[:, PAIR_W:]),
                     out_odd[:, :PAIR_W] * (1.0 / out_odd[:, PAIR_W:]))


def _diff_combine(out1, out2, lam, lam_init):
    o = out1[:, :PAIR_W] * (1.0 / out1[:, PAIR_W:]) - out2[:, :PAIR_W] * (lam / out2[:, PAIR_W:])
    return _rms(o) * (1.0 - lam_init)


def _ctx_attn_kernel(lp_ref, qn_ref, kn_ref, vn_ref, qd_ref, kd_ref, vd_ref, on_ref, od_ref, *, lam_init):
    tiles = range(D_GROUP // PAIR_W)
    lam = _diff_lambda(lp_ref[...], lam_init)
    q = [_split_pair(t) for t in _pair_tiles((qn_ref[...] * ATTN_SCALE).astype(BF16))
         + _pair_tiles((qd_ref[...] * ATTN_SCALE).astype(BF16))]
    k = _pair_tiles(kn_ref[...].astype(BF16)) + _pair_tiles(kd_ref[...].astype(BF16))
    v = [_with_ones(t) for t in _pair_tiles(vn_ref[...].astype(BF16)) + _pair_tiles(vd_ref[...].astype(BF16))]
    e = [[_exp_scores(_bdot_nt(q[j][c], k[j])) for c in range(2)] for j in range(2 * len(tiles))]
    out = [[jnp.dot(e[j][c], v[j], preferred_element_type=F32) for c in range(2)] for j in range(2 * len(tiles))]
    for j in tiles:
        sl = slice(j * PAIR_W, (j + 1) * PAIR_W)
        on_ref[:, sl] = _merge_pair(out[j][0], out[j][1])
        od_ref[:, sl] = _diff_combine(out[len(tiles) + j][0], out[len(tiles) + j][1], lam, lam_init)


def _ctx_attention(lam_params, qn, kn, vn, qd, kd, vd, seq, lam_init):
    n = qn.shape[0]
    blk = pl.BlockSpec((seq, D_GROUP), lambda b: (b, 0))
    return pl.pallas_call(
        functools.partial(_ctx_attn_kernel, lam_init=lam_init),
        grid=(n // seq,),
        in_specs=[pl.BlockSpec(lam_params.shape, lambda b: (0, 0))] + [blk] * 6,
        out_specs=[blk, blk],
        out_shape=[jax.ShapeDtypeStruct((n, D_GROUP), F32)] * 2,
        compiler_params=_cparams("arbitrary"),
        name="ctx_attention",
    )(lam_params, qn, kn, vn, qd, kd, vd)


def _rope_tables(n):
    quarter = HEAD_DIM // 4
    pos = jnp.arange(n)
    inv_freq = ROPE_BASE ** (-jnp.arange(quarter, dtype=F32) / quarter)
    lane = jnp.arange(HEAD_DIM)
    p = jnp.where(lane[None, :] < HEAD_DIM // 2, (pos // GRID_W)[:, None], (pos % GRID_W)[:, None]).astype(F32)
    ang = p * inv_freq[lane % quarter][None, :]
    sign = jnp.where((lane % (2 * quarter)) < quarter, -1.0, 1.0).astype(F32)
    cos = jnp.tile(jnp.cos(ang), (1, N_HEADS))
    sin = jnp.tile(jnp.sin(ang) * sign[None, :], (1, N_HEADS))
    return cos, sin


def _rope(x, cos, sin_signed):
    quarter = HEAD_DIM // 4
    width = x.shape[-1]
    lane = lax.broadcasted_iota(jnp.int32, x.shape, 1)
    first = (lane % (2 * quarter)) < quarter
    partner = jnp.where(first, pltpu.roll(x, width - quarter, 1), pltpu.roll(x, quarter, 1))
    return x * cos + partner * sin_signed


def _lat_diff_kernel(lp_ref, q_ref, k_ref, v_ref, ck_ref, cv_ref, cosq_ref, sinq_ref, cosk_ref, sink_ref,
                     o_ref, kall, vall, *, lam_init, n_lat):
    @pl.when(pl.program_id(1) == 0)
    def _():
        kall[0:n_lat, :] = _rope(k_ref[...], cosk_ref[...], sink_ref[...]).astype(BF16)
        kall[n_lat:, :] = ck_ref[...].astype(BF16)
        for j, (vt, ct) in enumerate(zip(_pair_tiles(v_ref[...].astype(BF16)),
                                         _pair_tiles(cv_ref[...].astype(BF16)))):
            vall[j, 0:n_lat, :] = _with_ones(vt)
            vall[j, n_lat:, :] = _with_ones(ct)

    lam = _diff_lambda(lp_ref[...], lam_init)
    q = (_rope(q_ref[...], cosq_ref[...], sinq_ref[...]) * ATTN_SCALE).astype(BF16)
    for j, qt in enumerate(_pair_tiles(q)):
        sl = slice(j * PAIR_W, (j + 1) * PAIR_W)
        out = [jnp.dot(_exp_scores(_bdot_nt(qc, kall[:, sl])), vall[j], preferred_element_type=F32)
               for qc in _split_pair(qt)]
        o_ref[:, sl] = _diff_combine(out[0], out[1], lam, lam_init)


def _lat_diff_attention(lam_params, qd, kd, vd, cache_k, cache_v, n_lat, lam_init, tq=256):
    n = qd.shape[0]
    nb = n // n_lat
    nq = n_lat // tq
    past = cache_k.shape[1]
    cos, sin = _rope_tables(n_lat)
    qblk = pl.BlockSpec((tq, D_GROUP), lambda b, i: (b * nq + i, 0))
    once = pl.Buffered(1)
    kvblk = pl.BlockSpec((n_lat, D_GROUP), lambda b, i: (b, 0), pipeline_mode=once)
    cblk = pl.BlockSpec((None, past, D_GROUP), lambda b, i: (b, 0, 0))
    return pl.pallas_call(
        functools.partial(_lat_diff_kernel, lam_init=lam_init, n_lat=n_lat),
        grid=(nb, nq),
        in_specs=[pl.BlockSpec(lam_params.shape, lambda b, i: (0, 0)),
                  qblk, kvblk, kvblk, cblk, cblk,
                  pl.BlockSpec((tq, D_GROUP), lambda b, i: (i, 0)),
                  pl.BlockSpec((tq, D_GROUP), lambda b, i: (i, 0)),
                  pl.BlockSpec((n_lat, D_GROUP), lambda b, i: (0, 0), pipeline_mode=once),
                  pl.BlockSpec((n_lat, D_GROUP), lambda b, i: (0, 0), pipeline_mode=once)],
        out_specs=qblk,
        out_shape=jax.ShapeDtypeStruct((n, D_GROUP), F32),
        scratch_shapes=[pltpu.VMEM((n_lat + past, D_GROUP), BF16),
                        pltpu.VMEM((D_GROUP // PAIR_W, n_lat + past, 2 * PAIR_W), BF16)],
        compiler_params=_cparams("arbitrary", "arbitrary"),
        name="lat_diff_attention",
    )(lam_params, qd, kd, vd, cache_k, cache_v, cos, sin, cos, sin)


def _na_bias_kernel(rpb_ref, tt_ref):
    h = pl.program_id(0)
    lane = lax.broadcasted_iota(jnp.int32, (GRID_W, 2 * GRID_W), 1)
    qc = lax.broadcasted_iota(jnp.int32, (GRID_W, 2 * GRID_W), 0)
    dc = jnp.clip(lane % GRID_W - qc + (WIN_W - 1), 0, RPB_C - 1)
    first = lane < GRID_W
    for dr in range(RPB_R - 1):
        acc = jnp.zeros((GRID_W, 2 * GRID_W), F32)
        for d in range(RPB_C):
            val = jnp.where(first, rpb_ref[h * RPB_R + dr, d], rpb_ref[h * RPB_R + dr + 1, d])
            acc = jnp.where(dc == d, val, acc)
        tt_ref[0, dr] = acc


def _na_bias_table(rpb):
    return pl.pallas_call(
        _na_bias_kernel,
        grid=(N_HEADS,),
        in_specs=[pl.BlockSpec(memory_space=pltpu.SMEM)],
        out_specs=pl.BlockSpec((1, RPB_R - 1, GRID_W, 2 * GRID_W), lambda h: (h, 0, 0, 0)),
        out_shape=jax.ShapeDtypeStruct((N_HEADS, RPB_R - 1, GRID_W, 2 * GRID_W), F32),
        compiler_params=_cparams("arbitrary"),
        name="na_bias_table",
    )(rpb.reshape(N_HEADS * RPB_R, RPB_C))


def _lat_na_kernel(q_ref, k_ref, v_ref, ck_ref, cv_ref, tt_ref, o_ref):
    r = pl.program_id(1)
    r_start = jnp.clip(r - WIN_R // 2, 0, GRID_ROWS - WIN_R)
    dr0 = r_start - r + (WIN_R - 1)
    row0 = pl.multiple_of(r_start * GRID_W, GRID_W)
    nkeys = WIN_R * GRID_W
    past = ck_ref.shape[0]
    lane = lax.broadcasted_iota(jnp.int32, (GRID_W, nkeys + past), 1)
    kc = lane % GRID_W
    qc = lax.broadcasted_iota(jnp.int32, (GRID_W, nkeys + past), 0)
    c_start = jnp.clip(qc - WIN_W // 2, 0, GRID_W - WIN_W)
    visible = ((kc >= c_start) & (kc < c_start + WIN_W)) | (lane >= nkeys)
    no_bias = jnp.zeros((GRID_W, past), F32)
    q = [_split_pair(t) for t in _pair_tiles((q_ref[...] * ATTN_SCALE).astype(BF16))]
    kcat = _pair_tiles(jnp.concatenate([k_ref[pl.ds(row0, nkeys), :], ck_ref[...]], axis=0).astype(BF16))
    vcat = [_with_ones(t) for t in
            _pair_tiles(jnp.concatenate([v_ref[pl.ds(row0, nkeys), :], cv_ref[...]], axis=0).astype(BF16))]
    bias = [jnp.concatenate([tt_ref[h, dr0 + 2 * j] for j in range(WIN_R // 2)] + [no_bias], axis=1)
            for h in range(N_HEADS)]
    e = [[_exp_scores(jnp.where(visible, _bdot_nt(q[j][c], kcat[j]) + bias[2 * j + c], NEG_INF))
          for c in range(2)] for j in range(len(q))]
    out = [[jnp.dot(e[j][c], vcat[j], preferred_element_type=F32) for c in range(2)] for j in range(len(q))]
    for j in range(len(q)):
        o_ref[:, j * PAIR_W:(j + 1) * PAIR_W] = _merge_pair(out[j][0], out[j][1])


def _lat_na_attention(qn, kn, vn, cache_k, cache_v, tt, n_lat):
    n = qn.shape[0]
    nb = n // n_lat
    past = cache_k.shape[1]
    qblk = pl.BlockSpec((GRID_W, D_GROUP), lambda b, r: (b * GRID_ROWS + r, 0))
    kvblk = pl.BlockSpec((n_lat, D_GROUP), lambda b, r: (b, 0))
    cblk = pl.BlockSpec((None, past, D_GROUP), lambda b, r: (b, 0, 0))
    return pl.pallas_call(
        _lat_na_kernel,
        grid=(nb, GRID_ROWS),
        in_specs=[qblk, kvblk, kvblk, cblk, cblk,
                  pl.BlockSpec(tt.shape, lambda b, r: (0, 0, 0, 0))],
        out_specs=qblk,
        out_shape=jax.ShapeDtypeStruct((n, D_GROUP), F32),
        compiler_params=_cparams("arbitrary", "arbitrary"),
        name="lat_na_attention",
    )(qn, kn, vn, cache_k, cache_v, tt)


def _rec_mix(of_ref, ob_ref, rg_ref, yf_ref, yb_ref, xr_ref, xk_ref, xv_ref, lo_ref, rk_ref, lng_ref, lnb_ref,
             gup_ref):
    pw = 2 * HEAD_DIM
    brow = lax.broadcasted_iota(jnp.int32, (pw, pw), 0) // HEAD_DIM
    bcol = lax.broadcasted_iota(jnp.int32, (pw, pw), 1) // HEAD_DIM
    ones_bd = jnp.where(brow == bcol, 1.0, 0.0).astype(BF16)

    def head_sums(x):
        outs = []
        for t in range(D_GROUP // pw):
            xt = x[:, t * pw:(t + 1) * pw]
            hi = xt.astype(BF16)
            lo = (xt - hi.astype(F32)).astype(BF16)
            outs.append(jnp.dot(hi, ones_bd, preferred_element_type=F32)
                        + jnp.dot(lo, ones_bd, preferred_element_type=F32))
        return jnp.concatenate(outs, axis=1)

    inv_d = 1.0 / HEAD_DIM
    o = of_ref[...] + ob_ref[...]
    o_ret = _silu(rg_ref[...]) * (o * lax.rsqrt(head_sums(o * o) * inv_d + RMS_EPS))
    y = yf_ref[...] + yb_ref[...]
    yc = y - head_sums(y) * inv_d
    var = head_sums(yc * yc) * inv_d
    yn = yc * lax.rsqrt(var + RWKV_GN_EPS) * lng_ref[...] + lnb_ref[...]
    bonus = head_sums(xr_ref[...] * rk_ref[...] * xk_ref[...]) * xv_ref[...]
    g_rw = _bdot(_sigmoid(lo_ref[:, 2 * D_LORA:]), gup_ref[...])
    return o_ret, (yn + bonus) * g_rw


def _tail_kernel(*refs, final, ff_chunk, n_mix):
    x_ref, mix_refs = refs[0], refs[1:1 + n_mix]
    wo_ref, gm_ref, g_ref, sh_ref, sc_ref, gf_ref, wi_ref, wf_ref, fg_ref, o_ref = refs[1 + n_mix:]
    if n_mix == 2:
        mix_a, mix_b = mix_refs[0][...], mix_refs[1][...]
    else:
        mix_a, mix_b = _rec_mix(*mix_refs)
    mix = _bdot(mix_a, wo_ref[0:D_GROUP, :]) + _bdot(mix_b, wo_ref[D_GROUP:, :])
    x1 = x_ref[...] + gm_ref[0] * mix
    h = _rms(x1) * g_ref[...]
    hb = (h * (1.0 + sc_ref[0]) + sh_ref[0]).astype(BF16)
    acc = jnp.zeros_like(x1)
    for c0 in range(0, D_FF, ff_chunk):
        gate = jnp.dot(hb, wi_ref[:, c0:c0 + ff_chunk], preferred_element_type=F32)
        up = jnp.dot(hb, wi_ref[:, D_FF + c0:D_FF + c0 + ff_chunk], preferred_element_type=F32)
        acc = acc + jnp.dot((_silu(gate) * up).astype(BF16), wf_ref[c0:c0 + ff_chunk, :],
                            preferred_element_type=F32)
    x2 = x1 + gf_ref[0] * acc
    if final:
        x2 = _rms(x2) * fg_ref[...]
    o_ref[...] = x2


def _layer_tail(x, mix_inputs, w_out, gate_mix, gain, shift, scale, gate_ffn, w_ffn_in_all, w_ffn_out_all, layer,
                final_gain, rows_per_mod, final, ff_chunk=1408):
    n, d = x.shape
    tm = TAIL_TM if len(mix_inputs) == 2 else TAIL_TM_REC
    tiles_per_mod = rows_per_mod // tm
    mod_spec = pl.BlockSpec((1, 1, d), lambda i: (i // tiles_per_mod, 0, 0))
    vec_spec = pl.BlockSpec((1, d), lambda i: (0, 0))
    resident = lambda a: pl.BlockSpec(a.shape, lambda i: (0,) * a.ndim, pipeline_mode=pl.Buffered(1))
    layer_slab = lambda a: pl.BlockSpec((None,) + a.shape[1:], lambda i: (layer, 0, 0),
                                        pipeline_mode=pl.Buffered(1))
    mix_specs = [pl.BlockSpec((tm, a.shape[1]), lambda i: (i, 0)) if a.shape[0] == n else resident(a)
                 for a in mix_inputs]
    return pl.pallas_call(
        functools.partial(_tail_kernel, final=final, ff_chunk=ff_chunk, n_mix=len(mix_inputs)),
        grid=(n // tm,),
        in_specs=[pl.BlockSpec((tm, d), lambda i: (i, 0))] + mix_specs
                 + [resident(w_out), mod_spec, vec_spec, mod_spec, mod_spec, mod_spec,
                    layer_slab(w_ffn_in_all), layer_slab(w_ffn_out_all), vec_spec],
        out_specs=pl.BlockSpec((tm, d), lambda i: (i, 0)),
        out_shape=jax.ShapeDtypeStruct((n, d), F32),
        compiler_params=_cparams("arbitrary"),
        name="layer_tail",
    )(x, *mix_inputs, w_out, gate_mix, gain.reshape(1, d), shift, scale, gate_ffn, w_ffn_in_all, w_ffn_out_all,
      final_gain.reshape(1, d))


def _log_sigmoid(x):
    return jnp.minimum(x, 0.0) - jnp.log(1.0 + jnp.exp(-jnp.abs(x)))


def _pair_block_diag(xp):
    left = lax.broadcasted_iota(jnp.int32, xp.shape, 1) < HEAD_DIM
    zero = jnp.zeros_like(xp)
    return jnp.concatenate([jnp.where(left, xp, zero), jnp.where(left, zero, xp)], axis=0)


def _states_to_pairs(state):
    nb = state.shape[0]
    s = state.reshape(nb, 2, N_HEADS // 2, 2, HEAD_DIM, HEAD_DIM)
    zero = jnp.zeros_like(s[:, :, :, 0])
    return jnp.concatenate([jnp.concatenate([s[:, :, :, 0], zero], axis=-1),
                            jnp.concatenate([zero, s[:, :, :, 1]], axis=-1)], axis=-2)


def _pairs_to_states(sbd):
    nb = sbd.shape[0]
    s = jnp.stack([sbd[..., :HEAD_DIM, :HEAD_DIM], sbd[..., HEAD_DIM:, HEAD_DIM:]], axis=3)
    return s.reshape(nb, 2, N_HEADS, HEAD_DIM, HEAD_DIM)


def _ret_kernel(dlm_ref, dlq_ref, qf_ref, kf_ref, vf_ref, qb_ref, kb_ref, vb_ref, s0_ref,
                of_ref, ob_ref, sf_ref, st, mask_scr, qdec_scr, kdec_scr, cdec_scr):
    c = pl.program_id(1)
    cs = RET_CHUNK
    pw = 2 * HEAD_DIM
    npair = N_HEADS // 2
    chains = [(d, p) for d in range(2) for p in range(npair)]

    @pl.when(c == 0)
    def _():
        st[...] = s0_ref[0]

    @pl.when((pl.program_id(0) == 0) & (c == 0))
    def _():
        row = lax.broadcasted_iota(jnp.int32, (cs, 2 * cs), 0)
        col = lax.broadcasted_iota(jnp.int32, (cs, 2 * cs), 1) % cs
        tok = lax.broadcasted_iota(jnp.int32, (cs, pw), 0)
        dist = [row - col, col - row]
        tq = [tok.astype(F32), (cs - 1 - tok).astype(F32)]
        for i, (d, p) in enumerate(chains):
            lgm = _log_sigmoid(dlm_ref[d, p])[0:1, :]
            lgq = _log_sigmoid(dlq_ref[d, p])
            mask_scr[i] = jnp.where(dist[d] >= 0, jnp.exp(jnp.maximum(dist[d], 0).astype(F32) * lgm), 0.0)
            qdec_scr[i] = jnp.exp((tq[d] + 1.0) * lgq[0:1, :])
            kdec_scr[i] = jnp.exp((cs - 1.0 - tq[d]) * lgq[0:1, :])
            cdec_scr[i] = jnp.exp(cs * lgq)

    brow = lax.broadcasted_iota(jnp.int32, (pw, pw), 0) // HEAD_DIM
    bcol = lax.broadcasted_iota(jnp.int32, (pw, pw), 1) // HEAD_DIM
    same_head = brow == bcol
    refs = [(qf_ref, kf_ref, vf_ref, of_ref), (qb_ref, kb_ref, vb_ref, ob_ref)]
    sls = [slice(p * pw, (p + 1) * pw) for _, p in chains]
    q = [refs[d][0][:, sls[i]] for i, (d, _) in enumerate(chains)]
    k = [refs[d][1][:, sls[i]] * ATTN_SCALE for i, (d, _) in enumerate(chains)]
    v = [refs[d][2][:, sls[i]] for i, (d, _) in enumerate(chains)]
    idx = range(len(chains))
    s = [_bdot_nt(q[i], _pair_block_diag(k[i])) * mask_scr[i] for i in idx]
    inner = [_bdot(s[i], _pair_block_diag(v[i])) for i in idx]
    state = [st[d, p] for d, p in chains]
    cross = [_bdot(q[i], state[i]) * qdec_scr[i] for i in idx]
    kv = [_bdot_tn(k[i] * kdec_scr[i], v[i]) for i in idx]
    for i, (d, p) in enumerate(chains):
        refs[d][3][:, sls[i]] = inner[i] + cross[i]
        st[d, p] = state[i] * cdec_scr[i][0:1, :] + jnp.where(same_head, kv[i], 0.0)

    @pl.when(c == pl.num_programs(1) - 1)
    def _():
        sf_ref[0] = st[...]


def _retention_scan(decay_logit, q, k, v, state0, seq):
    n = q.shape[0]
    nb = n // seq
    nc = seq // RET_CHUNK
    npair, pw = N_HEADS // 2, 2 * HEAD_DIM
    dl = decay_logit.astype(F32)
    dl_m = jnp.broadcast_to(jnp.repeat(dl, RET_CHUNK, axis=1).reshape(2, npair, 1, 2 * RET_CHUNK),
                            (2, npair, 8, 2 * RET_CHUNK))
    dl_q = jnp.broadcast_to(jnp.repeat(dl, HEAD_DIM, axis=1).reshape(2, npair, 1, pw), (2, npair, 8, pw))
    blk_f = pl.BlockSpec((RET_CHUNK, D_GROUP), lambda b, c: (b * nc + c, 0))
    blk_b = pl.BlockSpec((RET_CHUNK, D_GROUP), lambda b, c: (b * nc + nc - 1 - c, 0))
    st_blk = pl.BlockSpec((1, 2, npair, pw, pw), lambda b, c: (b, 0, 0, 0, 0))
    o_f, o_b, sf_bd = pl.pallas_call(
        _ret_kernel,
        grid=(nb, nc),
        in_specs=[pl.BlockSpec(dl_m.shape, lambda b, c: (0, 0, 0, 0)),
                  pl.BlockSpec(dl_q.shape, lambda b, c: (0, 0, 0, 0)),
                  blk_f, blk_f, blk_f, blk_b, blk_b, blk_b, st_blk],
        out_specs=[blk_f, blk_b, st_blk],
        out_shape=[jax.ShapeDtypeStruct((n, D_GROUP), F32), jax.ShapeDtypeStruct((n, D_GROUP), F32),
                   jax.ShapeDtypeStruct((nb, 2, npair, pw, pw), F32)],
        scratch_shapes=[pltpu.VMEM((2, npair, pw, pw), F32),
                        pltpu.VMEM((2 * npair, RET_CHUNK, 2 * RET_CHUNK), F32),
                        pltpu.VMEM((2 * npair, RET_CHUNK, pw), F32),
                        pltpu.VMEM((2 * npair, RET_CHUNK, pw), F32),
                        pltpu.VMEM((2 * npair, 8, pw), F32)],
        compiler_params=_cparams("arbitrary", "arbitrary"),
        name="retention_scan",
    )(dl_m, dl_q, q, k, v, q, k, v, _states_to_pairs(state0))
    return o_f, o_b, _pairs_to_states(sf_bd)


def _rwkv_kernel(xrf_ref, xkf_ref, xvf_ref, lof_ref, xrb_ref, xkb_ref, xvb_ref, lob_ref,
                 w0_ref, wup_ref, a0_ref, aup_ref, kk_ref, ka_ref, s0_ref, yf_ref, yb_ref, sf_ref, st):
    c = pl.program_id(1)
    cs = RWKV_CHUNK
    npair = N_HEADS // 2

    @pl.when(c == 0)
    def _():
        st[...] = s0_ref[0]

    row = lax.broadcasted_iota(jnp.int32, (cs, cs), 0)
    col = lax.broadcasted_iota(jnp.int32, (cs, cs), 1)
    pw = 2 * HEAD_DIM
    prow = lax.broadcasted_iota(jnp.int32, (cs, pw), 0)
    pcol = lax.broadcasted_iota(jnp.int32, (cs, pw), 1) % HEAD_DIM
    eye = jnp.where(pcol == prow, 1.0, 0.0).astype(F32)
    sub_diag = (prow // RWKV_SUB) == (pcol // RWKV_SUB)
    brow = lax.broadcasted_iota(jnp.int32, (pw, pw), 0) // HEAD_DIM
    bcol = lax.broadcasted_iota(jnp.int32, (pw, pw), 1) // HEAD_DIM
    same_head = brow == bcol
    ones_bd = jnp.where(same_head, 1.0, 0.0).astype(BF16)

    bd = _pair_block_diag

    def direction(d, xr_ref, xk_ref, xv_ref, lo_ref):
        tri = jnp.where((row >= col) if d == 0 else (col >= row), 1.0, 0.0).astype(BF16)
        dist = (prow - pcol) if d == 0 else (pcol - prow)
        xk = xk_ref[...]
        lo = lo_ref[...]
        w_log = -math.exp(-0.5) * _sigmoid(w0_ref[d] + _bdot(jnp.tanh(lo[:, 0:D_LORA]), wup_ref[d]))
        a_all = _sigmoid(a0_ref[d] + _bdot(lo[:, D_LORA:2 * D_LORA], aup_ref[d]))
        w1 = w_log.astype(BF16)
        r1 = w_log - w1.astype(F32)
        w2 = r1.astype(BF16)
        w3 = (r1 - w2.astype(F32)).astype(BF16)
        cum = (jnp.dot(tri, w1, preferred_element_type=F32) + jnp.dot(tri, w2, preferred_element_type=F32)
               + jnp.dot(tri, w3, preferred_element_type=F32))
        cum_end = cum[cs - 1:cs, :] if d == 0 else cum[0:1, :]
        return dict(earlier=dist > 0, upto=dist >= 0, xr=xr_ref[...], xv=xv_ref[...], a_all=a_all,
                    e_incl=jnp.exp(cum), e_excl=jnp.exp(cum - w_log), e_neg=jnp.exp(-cum),
                    e_end=jnp.exp(cum_end - cum), w_end=jnp.exp(cum_end), kk_all=xk * kk_ref[...],
                    keff_all=xk * (1.0 + (a_all - 1.0) * ka_ref[...]))

    dirs = [direction(0, xrf_ref, xkf_ref, xvf_ref, lof_ref), direction(1, xrb_ref, xkb_ref, xvb_ref, lob_ref)]
    y_refs = [yf_ref, yb_ref]

    chains = [(d, p) for d in range(2) for p in range(npair)]
    pairs = range(len(chains))
    sls = [slice(p * pw, (p + 1) * pw) for _, p in chains]
    pick = lambda name: [dirs[d][name][:, sls[i]] for i, (d, _) in enumerate(chains)]
    earlier = [dirs[d]["earlier"] for d, _ in chains]
    upto = [dirs[d]["upto"] for d, _ in chains]
    cat = lambda x, y: jnp.concatenate([x, y], axis=0)

    def head_sum(x):
        x1 = x.astype(BF16)
        r1 = x - x1.astype(F32)
        x2 = r1.astype(BF16)
        x3 = (r1 - x2.astype(F32)).astype(BF16)
        f = functools.partial(jnp.dot, preferred_element_type=F32)
        return f(jnp.concatenate([x1, x2], axis=1), jnp.concatenate([ones_bd, ones_bd], axis=0)) + f(x3, ones_bd)

    e_incl, e_excl, e_neg, e_end, w_end = (pick(k) for k in ("e_incl", "e_excl", "e_neg", "e_end", "w_end"))
    kk = [x * lax.rsqrt(head_sum(x * x) + 1e-12) for x in pick("kk_all")]
    b = [kk[p] * a for p, a in zip(pairs, pick("a_all"))]
    keff = pick("keff_all")
    v = pick("xv")
    a_t = [-kk[p] * e_excl[p] for p in pairs]
    r_t = [xr * e_incl[p] for p, xr in zip(pairs, pick("xr"))]
    ar = [cat(a_t[p], r_t[p]).astype(BF16) for p in pairs]
    g = [_bdot_nt(ar[p], cat(bd(b[p] * e_neg[p]), bd(keff[p] * e_neg[p]))) for p in pairs]
    n_mat = [jnp.where(earlier[p], g[p][:cs, :pw], 0.0) for p in pairs]
    m_rb = [jnp.where(upto[p], g[p][cs:, :pw], 0.0) for p in pairs]
    a_ak = [jnp.where(earlier[p], g[p][:cs, pw:], 0.0) for p in pairs]
    m_rk = [jnp.where(upto[p], g[p][cs:, pw:], 0.0) for p in pairs]
    bd_s = lambda xs: (bd(xs[0]), bd(xs[1]))
    n_d = [jnp.where(sub_diag, n, 0.0) for n in n_mat]
    n_o = [jnp.where(sub_diag, 0.0, n) for n in n_mat]
    n_s = [_split2(n) for n in n_d]
    x = [eye + n for n in n_d]
    pk = [_mm3s(ns, bd_s(ns)) for ns in n_s]
    for _ in range(RWKV_SUB_DOUBLINGS - 1):
        xp = [_mm3s(_split2(cat(x[p], pk[p])), bd_s(_split2(pk[p]))) for p in pairs]
        x = [x[p] + xp[p][:cs] for p in pairs]
        pk = [r[cs:] for r in xp]
    t_d = [x[p] + _mm3s(_split2(x[p]), bd_s(_split2(pk[p]))) for p in pairs]
    lcat = lambda x, y: jnp.concatenate([x, y], axis=1)
    m_o = [_bdot(t_d[p], bd(n_o[p])) for p in pairs]
    mz = [_bdot(m_o[p], lcat(bd(t_d[p]), bd(m_o[p]))) for p in pairs]
    z = [t_d[p] + mz[p][:, :pw] for p in pairs]
    t_inv = [(z[p] + _bdot(mz[p][:, pw:], bd(z[p]))).astype(BF16) for p in pairs]
    av = [_bdot(cat(a_ak[p], m_rk[p]), bd(v[p])) for p in pairs]
    tau = [_bdot(t_inv[p], lcat(bd(a_t[p]), bd(av[p][:cs]))) for p in pairs]
    s_prev = [st[d, p] for d, p in chains]
    hs = [_bdot_nt(cat(tau[p][:, :pw], r_t[p]), s_prev[p]) for p in pairs]
    u = [hs[p][:cs] + tau[p][:, pw:] for p in pairs]
    y = [hs[p][cs:] + _bdot(m_rb[p], bd(u[p])) + av[p][cs:] for p in pairs]
    for i, (d, p) in enumerate(chains):
        y_refs[d][:, sls[i]] = y[i]
        bk_end = cat(b[i] * e_end[i], keff[i] * e_end[i])
        st[d, p] = s_prev[i] * w_end[i] + jnp.where(same_head, _bdot_tn(cat(u[i], v[i]), bk_end), 0.0)

    @pl.when(c == pl.num_programs(1) - 1)
    def _():
        sf_ref[0] = st[...]


def _rwkv_scan(xr, xk, xv, lora, w0, w_up, a0, a_up, k_k, k_a, state0, seq):
    n = xr.shape[0]
    nb = n // seq
    nc = seq // RWKV_CHUNK

    fwd_idx = lambda b, c: (b * nc + c, 0)
    bwd_idx = lambda b, c: (b * nc + nc - 1 - c, 0)
    blk_f = pl.BlockSpec((RWKV_CHUNK, D_GROUP), fwd_idx)
    blk_b = pl.BlockSpec((RWKV_CHUNK, D_GROUP), bwd_idx)
    lblk_f = pl.BlockSpec((RWKV_CHUNK, lora.shape[1]), fwd_idx)
    lblk_b = pl.BlockSpec((RWKV_CHUNK, lora.shape[1]), bwd_idx)
    dvec = pl.BlockSpec((2, 1, D_GROUP), lambda b, c: (0, 0, 0))
    dmat = pl.BlockSpec((2, D_LORA, D_GROUP), lambda b, c: (0, 0, 0))
    vec = pl.BlockSpec((1, D_GROUP), lambda b, c: (0, 0))
    npair, pw = N_HEADS // 2, 2 * HEAD_DIM
    st_blk = pl.BlockSpec((1, 2, npair, pw, pw), lambda b, c: (b, 0, 0, 0, 0))
    y_f, y_b, sf_bd = pl.pallas_call(
        _rwkv_kernel,
        grid=(nb, nc),
        in_specs=[blk_f, blk_f, blk_f, lblk_f, blk_b, blk_b, blk_b, lblk_b,
                  dvec, dmat, dvec, dmat, vec, vec, st_blk],
        out_specs=[blk_f, blk_b, st_blk],
        out_shape=[jax.ShapeDtypeStruct((n, D_GROUP), F32), jax.ShapeDtypeStruct((n, D_GROUP), F32),
                   jax.ShapeDtypeStruct((nb, 2, npair, pw, pw), F32)],
        scratch_shapes=[pltpu.VMEM((2, npair, pw, pw), F32)],
        compiler_params=_cparams("arbitrary", "arbitrary"),
        name="rwkv7_scan",
    )(xr, xk, xv, lora, xr, xk, xv, lora, w0.reshape(2, 1, D_GROUP), w_up, a0.reshape(2, 1, D_GROUP), a_up,
      k_k.reshape(1, D_GROUP), k_a.reshape(1, D_GROUP), _states_to_pairs(state0))
    return y_f, y_b, _pairs_to_states(sf_bd)


ATTN_WIDTHS = (D_GROUP,) * 6
REC_WIDTHS = (D_GROUP,) * 7 + (2 * D_LORA + D_LORA_G,)


def kernel(x_prompt, x_sample, cache_na_k, cache_na_v, cache_diff_k, cache_diff_v, state_ret, state_rwkv, c, c_ctx, norm_mix_g, norm_ffn_g, norm_final_g, w_ada, b_ada, w_in_attn, w_out_attn, na_rpb, diff_lq1, diff_lk1, diff_lq2, diff_lk2, w_in_rec, w_out_rec, ret_decay_logit, rw_w0, rw_w_up, rw_a0, rw_a_up, rw_g_up, rw_k_k, rw_k_a, rw_r_k, rw_ln_g, rw_ln_b, w_ffn_in, w_ffn_out):
    bp, seq, d = x_prompt.shape
    bl, n_lat, _ = x_sample.shape
    depth = w_ada.shape[0]
    ctx = x_prompt.reshape(bp * seq, d)
    lat = x_sample.reshape(bl * n_lat, d)

    cond8 = jnp.zeros((8, d), F32).at[0].set(c_ctx).at[1:1 + bl].set(c)
    mods = _ada_modulation(cond8, w_ada, b_ada).reshape(depth, 8, 6, d)
    w_ffn_in_bf16 = w_ffn_in.astype(BF16)
    w_ffn_out_bf16 = w_ffn_out.astype(BF16)

    outs = {}
    for layer in range(depth):
        m_ctx = [mods[layer, 0:1, j].reshape(1, 1, d) for j in range(6)]
        m_lat = [mods[layer, 1:1 + bl, j].reshape(bl, 1, d) for j in range(6)]
        if layer % 2 == 0:
            i = layer // 2
            lam_init = 0.8 - 0.6 * math.exp(-0.3 * layer)
            lam_params = jnp.stack([diff_lq1[i], diff_lk1[i], diff_lq2[i], diff_lk2[i]]).astype(F32)
            w_in = w_in_attn[i].astype(BF16)
            w_out = w_out_attn[i].astype(BF16)
            cache_shapes = {1: (N_HEADS, HEAD_DIM), 2: (N_HEADS, HEAD_DIM),
                            4: (N_HEADS // 2, 2, HEAD_DIM), 5: (N_HEADS // 2, 2 * HEAD_DIM)}
            pc = _norm_mod_matmul(ctx, norm_mix_g[layer], m_ctx[0], m_ctx[1], w_in, ATTN_WIDTHS, bp * seq,
                                  PROJ_TM_CACHE, cache_shapes)
            outs["na_k"], outs["na_v"], outs["df_k"], outs["df_v"] = pc[6:]
            pc = pc[:6]
            plat = _norm_mod_matmul(lat, norm_mix_g[layer], m_lat[0], m_lat[1], w_in, ATTN_WIDTHS, n_lat, PROJ_TM)
            mix_ctx = _ctx_attention(lam_params, *pc, seq, lam_init)
            past = cache_na_k.shape[2]
            tt = _na_bias_table(na_rpb[i])
            o_na = _lat_na_attention(plat[0], plat[1], plat[2],
                                     cache_na_k[:, i].reshape(bl, past, D_GROUP),
                                     cache_na_v[:, i].reshape(bl, past, D_GROUP), tt, n_lat)
            o_df = _lat_diff_attention(lam_params, plat[3], plat[4], plat[5],
                                       cache_diff_k[:, i].reshape(bl, past, D_GROUP),
                                       cache_diff_v[:, i].reshape(bl, past, D_GROUP), n_lat, lam_init)
            mix_lat = (o_na, o_df)
        else:
            j = layer // 2
            w_in = w_in_rec[j].astype(BF16)
            w_out = w_out_rec[j].astype(BF16)
            pc = _norm_mod_matmul(ctx, norm_mix_g[layer], m_ctx[0], m_ctx[1], w_in, REC_WIDTHS, bp * seq, PROJ_TM)
            plat = _norm_mod_matmul(lat, norm_mix_g[layer], m_lat[0], m_lat[1], w_in, REC_WIDTHS, n_lat, PROJ_TM)
            zero_state = jnp.zeros((bp, 2, N_HEADS, HEAD_DIM, HEAD_DIM), F32)
            mixes = []
            for is_ctx, p, s_ret0, s_rw0, sq in ((True, pc, zero_state, zero_state, seq),
                                                 (False, plat, state_ret[:, j], state_rwkv[:, j], n_lat)):
                rq, rk, rv, rg, wr, wk, wv, lora = p
                o_f, o_b, s_ret = _retention_scan(ret_decay_logit[j], rq, rk, rv, s_ret0, sq)
                y_f, y_b, s_rw = _rwkv_scan(wr, wk, wv, lora, rw_w0[j], rw_w_up[j].astype(BF16), rw_a0[j],
                                            rw_a_up[j].astype(BF16), rw_k_k[j], rw_k_a[j], s_rw0, sq)
                mixes.append((o_f, o_b, rg, y_f, y_b, wr, wk, wv, lora, rw_r_k[j].reshape(1, D_GROUP),
                              rw_ln_g[j].reshape(1, D_GROUP), rw_ln_b[j].reshape(1, D_GROUP),
                              rw_g_up[j].astype(BF16)))
                if is_ctx:
                    outs["ret"], outs["rwkv"] = s_ret, s_rw
            mix_ctx, mix_lat = mixes
        final = layer == depth - 1
        ctx = _layer_tail(ctx, mix_ctx, w_out, m_ctx[2], norm_ffn_g[layer], m_ctx[3], m_ctx[4], m_ctx[5],
                          w_ffn_in_bf16, w_ffn_out_bf16, layer, norm_final_g, bp * seq, final)
        lat = _layer_tail(lat, mix_lat, w_out, m_lat[2], norm_ffn_g[layer], m_lat[3], m_lat[4], m_lat[5],
                          w_ffn_in_bf16, w_ffn_out_bf16, layer, norm_final_g, n_lat, final)

    y_prompt = ctx.reshape(bp, seq, d)
    y_sample = lat.reshape(bl, n_lat, d)
    new_cache_na_k = outs["na_k"].reshape(bp, 1, seq, N_HEADS, HEAD_DIM)
    new_cache_na_v = outs["na_v"].reshape(bp, 1, seq, N_HEADS, HEAD_DIM)
    new_cache_diff_k = outs["df_k"].reshape(bp, 1, seq, N_HEADS // 2, 2, HEAD_DIM)
    new_cache_diff_v = outs["df_v"].reshape(bp, 1, seq, N_HEADS // 2, 2 * HEAD_DIM)
    new_state_ret = outs["ret"].reshape(bp, 1, 2, N_HEADS, HEAD_DIM, HEAD_DIM)
    new_state_rwkv = outs["rwkv"].reshape(bp, 1, 2, N_HEADS, HEAD_DIM, HEAD_DIM)
    return (y_prompt, y_sample, new_cache_na_k, new_cache_na_v, new_cache_diff_k, new_cache_diff_v,
            new_state_ret, new_state_rwkv)
```

```python
import functools
import math

import jax
import jax.numpy as jnp
from jax import lax
from jax.experimental import pallas as pl
from jax.experimental.pallas import tpu as pltpu

F32 = jnp.float32
BF16 = jnp.bfloat16

D_MODEL = 1024
HEAD_DIM = 64
N_HEADS = 8
D_GROUP = 512
GRID_W = 64
GRID_ROWS = 32
WIN_R = 8
WIN_W = 16
RPB_R = 15
RPB_C = 31
PAST_LEN = 256
D_FF = 2816
D_LORA = 64
D_LORA_G = 128
ROPE_BASE = 10000.0
RMS_EPS = 1e-6
RWKV_GN_EPS = 64e-5
NEG_INF = -1e30
ATTN_SCALE = HEAD_DIM ** -0.5
RET_CHUNK = 128
RWKV_CHUNK = 64
RWKV_CHUNKS_PER_STEP = 4
RWKV_SUB = 16
RWKV_SUB_DOUBLINGS = 3
VMEM_LIMIT = 56 * 1024 * 1024
TAIL_TM = 512
TAIL_TM_REC = 256
PROJ_TM = 512
PROJ_TM_CACHE = 256


def _cparams(*sem):
    return pltpu.CompilerParams(dimension_semantics=sem, vmem_limit_bytes=VMEM_LIMIT)


def _sigmoid(x):
    return 1.0 / (1.0 + jnp.exp(-x))


def _silu(x):
    return x * _sigmoid(x)


def _rms(x, eps=RMS_EPS):
    return x * lax.rsqrt(jnp.mean(x * x, axis=-1, keepdims=True) + eps)


def _bdot(a, b):
    return jnp.dot(a.astype(BF16), b.astype(BF16), preferred_element_type=F32)


def _bdot_nt(a, b):
    return lax.dot_general(a.astype(BF16), b.astype(BF16), (((1,), (1,)), ((), ())),
                           preferred_element_type=F32)


def _bdot_tn(a, b):
    return lax.dot_general(a.astype(BF16), b.astype(BF16), (((0,), (0,)), ((), ())),
                           preferred_element_type=F32)


_DN = {"nn": (((1,), (0,)), ((), ())), "nt": (((1,), (1,)), ((), ())), "tn": (((0,), (0,)), ((), ()))}


def _split2(x):
    hi = x.astype(BF16)
    lo = (x - hi.astype(F32)).astype(BF16)
    return hi, lo


def _mm3s(a_split, b_split):
    ah, al = a_split
    bh, bl = b_split
    d = functools.partial(jnp.dot, preferred_element_type=F32)
    return d(jnp.concatenate([ah, al], axis=1), jnp.concatenate([bh, bh], axis=0)) + d(ah, bl)


def _diff_lambda(lp, lam_init):
    s1 = jnp.sum(lp[0:1, :] * lp[1:2, :], axis=-1, keepdims=True)
    s2 = jnp.sum(lp[2:3, :] * lp[3:4, :], axis=-1, keepdims=True)
    return jnp.exp(s1) - jnp.exp(s2) + lam_init


def _ada_kernel(c_ref, w_ref, b_ref, o_ref):
    s = _silu(c_ref[...])
    o_ref[0] = _bdot(s, w_ref[0]) + b_ref[0]


def _ada_modulation(cond8, w_ada, b_ada):
    depth, d, n = w_ada.shape
    tn = 1536
    return pl.pallas_call(
        _ada_kernel,
        grid=(depth, n // tn),
        in_specs=[pl.BlockSpec((8, d), lambda l, j: (0, 0)),
                  pl.BlockSpec((1, d, tn), lambda l, j: (l, 0, j)),
                  pl.BlockSpec((1, 1, tn), lambda l, j: (l, 0, j))],
        out_specs=pl.BlockSpec((1, 8, tn), lambda l, j: (l, 0, j)),
        out_shape=jax.ShapeDtypeStruct((depth, 8, n), F32),
        compiler_params=_cparams("arbitrary", "arbitrary"),
        name="ada_modulation",
    )(cond8, w_ada, b_ada.reshape(depth, 1, n))


def _nmm_kernel(x_ref, g_ref, sh_ref, sc_ref, w_ref, *o_refs, widths, cache_shapes):
    h = _rms(x_ref[...]) * g_ref[...]
    h = h * (1.0 + sc_ref[0]) + sh_ref[0]
    hb = h.astype(BF16)
    c_refs = dict(zip(cache_shapes, o_refs[len(widths):]))
    off = 0
    for gi, (o_ref, w) in enumerate(zip(o_refs, widths)):
        res = jnp.dot(hb, w_ref[:, off:off + w], preferred_element_type=F32)
        o_ref[...] = res
        off += w
        if gi in c_refs:
            c_refs[gi][...] = res.reshape((res.shape[0],) + cache_shapes[gi])


def _norm_mod_matmul(x, gain, shift, scale, w_bf16, widths, rows_per_mod, tm, cache_shapes=None):
    n, d = x.shape
    cache_shapes = cache_shapes or {}
    tiles_per_mod = rows_per_mod // tm
    mod_spec = pl.BlockSpec((1, 1, d), lambda i: (i // tiles_per_mod, 0, 0))
    cache_specs = [pl.BlockSpec((tm,) + dims, lambda i, nd=len(dims): (i,) + (0,) * nd)
                   for dims in cache_shapes.values()]
    return pl.pallas_call(
        functools.partial(_nmm_kernel, widths=widths, cache_shapes=cache_shapes),
        grid=(n // tm,),
        in_specs=[pl.BlockSpec((tm, d), lambda i: (i, 0)),
                  pl.BlockSpec((1, d), lambda i: (0, 0)),
                  mod_spec, mod_spec,
                  pl.BlockSpec(w_bf16.shape, lambda i: (0, 0), pipeline_mode=pl.Buffered(1))],
        out_specs=[pl.BlockSpec((tm, w), lambda i: (i, 0)) for w in widths] + cache_specs,
        out_shape=([jax.ShapeDtypeStruct((n, w), F32) for w in widths]
                   + [jax.ShapeDtypeStruct((n,) + dims, F32) for dims in cache_shapes.values()]),
        compiler_params=_cparams("arbitrary"),
        name="norm_mod_matmul",
    )(x, gain.reshape(1, d), shift, scale, w_bf16)


PAIR_W = 2 * HEAD_DIM


def _pair_tiles(x):
    return [x[:, j * PAIR_W:(j + 1) * PAIR_W] for j in range(x.shape[1] // PAIR_W)]


def _split_pair(tile):
    left = lax.broadcasted_iota(jnp.int32, tile.shape, 1) < HEAD_DIM
    zero = jnp.zeros_like(tile)
    return jnp.where(left, tile, zero), jnp.where(left, zero, tile)


def _with_ones(v):
    return jnp.concatenate([v, jnp.ones_like(v)], axis=1)


def _exp_scores(s):
    return jnp.exp(s - jnp.max(s, axis=-1, keepdims=True)).astype(BF16)


def _merge_pair(out_even, out_odd):
    left = lax.broadcasted_iota(jnp.int32, (out_even.shape[0], PAIR_W), 1) < HEAD_DIM
    return jnp.where(left, out_even[:, :PAIR_W] * (1.0 / out_even[:, PAIR_W:]),
                     out_odd[:, :PAIR_W] * (1.0 / out_odd[:, PAIR_W:]))


def _diff_combine(out1, out2, lam, lam_init):
    o = out1[:, :PAIR_W] * (1.0 / out1[:, PAIR_W:]) - out2[:, :PAIR_W] * (lam / out2[:, PAIR_W:])
    return _rms(o) * (1.0 - lam_init)


def _ctx_attn_kernel(lp_ref, qn_ref, kn_ref, vn_ref, qd_ref, kd_ref, vd_ref, on_ref, od_ref, *, lam_init):
    tiles = range(D_GROUP // PAIR_W)
    lam = _diff_lambda(lp_ref[...], lam_init)
    q = [_split_pair(t) for t in _pair_tiles((qn_ref[...] * ATTN_SCALE).astype(BF16))
         + _pair_tiles((qd_ref[...] * ATTN_SCALE).astype(BF16))]
    k = _pair_tiles(kn_ref[...].astype(BF16)) + _pair_tiles(kd_ref[...].astype(BF16))
    v = [_with_ones(t) for t in _pair_tiles(vn_ref[...].astype(BF16)) + _pair_tiles(vd_ref[...].astype(BF16))]
    e = [[_exp_scores(_bdot_nt(q[j][c], k[j])) for c in range(2)] for j in range(2 * len(tiles))]
    out = [[jnp.dot(e[j][c], v[j], preferred_element_type=F32) for c in range(2)] for j in range(2 * len(tiles))]
    for j in tiles:
        sl = slice(j * PAIR_W, (j + 1) * PAIR_W)
        on_ref[:, sl] = _merge_pair(out[j][0], out[j][1])
        od_ref[:, sl] = _diff_combine(out[len(tiles) + j][0], out[len(tiles) + j][1], lam, lam_init)


def _ctx_attention(lam_params, qn, kn, vn, qd, kd, vd, seq, lam_init):
    n = qn.shape[0]
    blk = pl.BlockSpec((seq, D_GROUP), lambda b: (b, 0))
    return pl.pallas_call(
        functools.partial(_ctx_attn_kernel, lam_init=lam_init),
        grid=(n // seq,),
        in_specs=[pl.BlockSpec(lam_params.shape, lambda b: (0, 0))] + [blk] * 6,
        out_specs=[blk, blk],
        out_shape=[jax.ShapeDtypeStruct((n, D_GROUP), F32)] * 2,
        compiler_params=_cparams("arbitrary"),
        name="ctx_attention",
    )(lam_params, qn, kn, vn, qd, kd, vd)


def _rope_tables(n):
    quarter = HEAD_DIM // 4
    pos = jnp.arange(n)
    inv_freq = ROPE_BASE ** (-jnp.arange(quarter, dtype=F32) / quarter)
    lane = jnp.arange(HEAD_DIM)
    p = jnp.where(lane[None, :] < HEAD_DIM // 2, (pos // GRID_W)[:, None], (pos % GRID_W)[:, None]).astype(F32)
    ang = p * inv_freq[lane % quarter][None, :]
    sign = jnp.where((lane % (2 * quarter)) < quarter, -1.0, 1.0).astype(F32)
    cos = jnp.tile(jnp.cos(ang), (1, N_HEADS))
    sin = jnp.tile(jnp.sin(ang) * sign[None, :], (1, N_HEADS))
    return cos, sin


def _rope(x, cos, sin_signed):
    quarter = HEAD_DIM // 4
    width = x.shape[-1]
    lane = lax.broadcasted_iota(jnp.int32, x.shape, 1)
    first = (lane % (2 * quarter)) < quarter
    partner = jnp.where(first, pltpu.roll(x, width - quarter, 1), pltpu.roll(x, quarter, 1))
    return x * cos + partner * sin_signed


def _lat_diff_kernel(lp_ref, q_ref, k_ref, v_ref, ck_ref, cv_ref, cosq_ref, sinq_ref, cosk_ref, sink_ref,
                     o_ref, kall, vall, *, lam_init, n_lat):
    @pl.when(pl.program_id(1) == 0)
    def _():
        kall[0:n_lat, :] = _rope(k_ref[...], cosk_ref[...], sink_ref[...]).astype(BF16)
        kall[n_lat:, :] = ck_ref[...].astype(BF16)
        for j, (vt, ct) in enumerate(zip(_pair_tiles(v_ref[...].astype(BF16)),
                                         _pair_tiles(cv_ref[...].astype(BF16)))):
            vall[j, 0:n_lat, :] = _with_ones(vt)
            vall[j, n_lat:, :] = _with_ones(ct)

    lam = _diff_lambda(lp_ref[...], lam_init)
    q = (_rope(q_ref[...], cosq_ref[...], sinq_ref[...]) * ATTN_SCALE).astype(BF16)
    for j, qt in enumerate(_pair_tiles(q)):
        sl = slice(j * PAIR_W, (j + 1) * PAIR_W)
        out = [jnp.dot(_exp_scores(_bdot_nt(qc, kall[:, sl])), vall[j], preferred_element_type=F32)
               for qc in _split_pair(qt)]
        o_ref[:, sl] = _diff_combine(out[0], out[1], lam, lam_init)


def _lat_diff_attention(lam_params, qd, kd, vd, cache_k, cache_v, n_lat, lam_init, tq=256):
    n = qd.shape[0]
    nb = n // n_lat
    nq = n_lat // tq
    past = cache_k.shape[1]
    cos, sin = _rope_tables(n_lat)
    qblk = pl.BlockSpec((tq, D_GROUP), lambda b, i: (b * nq + i, 0))
    once = pl.Buffered(1)
    kvblk = pl.BlockSpec((n_lat, D_GROUP), lambda b, i: (b, 0), pipeline_mode=once)
    cblk = pl.BlockSpec((None, past, D_GROUP), lambda b, i: (b, 0, 0))
    return pl.pallas_call(
        functools.partial(_lat_diff_kernel, lam_init=lam_init, n_lat=n_lat),
        grid=(nb, nq),
        in_specs=[pl.BlockSpec(lam_params.shape, lambda b, i: (0, 0)),
                  qblk, kvblk, kvblk, cblk, cblk,
                  pl.BlockSpec((tq, D_GROUP), lambda b, i: (i, 0)),
                  pl.BlockSpec((tq, D_GROUP), lambda b, i: (i, 0)),
                  pl.BlockSpec((n_lat, D_GROUP), lambda b, i: (0, 0), pipeline_mode=once),
                  pl.BlockSpec((n_lat, D_GROUP), lambda b, i: (0, 0), pipeline_mode=once)],
        out_specs=qblk,
        out_shape=jax.ShapeDtypeStruct((n, D_GROUP), F32),
        scratch_shapes=[pltpu.VMEM((n_lat + past, D_GROUP), BF16),
                        pltpu.VMEM((D_GROUP // PAIR_W, n_lat + past, 2 * PAIR_W), BF16)],
        compiler_params=_cparams("arbitrary", "arbitrary"),
        name="lat_diff_attention",
    )(lam_params, qd, kd, vd, cache_k, cache_v, cos, sin, cos, sin)


def _na_bias_kernel(rpb_ref, tt_ref):
    h = pl.program_id(0)
    lane = lax.broadcasted_iota(jnp.int32, (GRID_W, 2 * GRID_W), 1)
    qc = lax.broadcasted_iota(jnp.int32, (GRID_W, 2 * GRID_W), 0)
    dc = jnp.clip(lane % GRID_W - qc + (WIN_W - 1), 0, RPB_C - 1)
    first = lane < GRID_W
    for dr in range(RPB_R - 1):
        acc = jnp.zeros((GRID_W, 2 * GRID_W), F32)
        for d in range(RPB_C):
            val = jnp.where(first, rpb_ref[h * RPB_R + dr, d], rpb_ref[h * RPB_R + dr + 1, d])
            acc = jnp.where(dc == d, val, acc)
        tt_ref[0, dr] = acc


def _na_bias_table(rpb):
    return pl.pallas_call(
        _na_bias_kernel,
        grid=(N_HEADS,),
        in_specs=[pl.BlockSpec(memory_space=pltpu.SMEM)],
        out_specs=pl.BlockSpec((1, RPB_R - 1, GRID_W, 2 * GRID_W), lambda h: (h, 0, 0, 0)),
        out_shape=jax.ShapeDtypeStruct((N_HEADS, RPB_R - 1, GRID_W, 2 * GRID_W), F32),
        compiler_params=_cparams("arbitrary"),
        name="na_bias_table",
    )(rpb.reshape(N_HEADS * RPB_R, RPB_C))


def _lat_na_kernel(q_ref, k_ref, v_ref, ck_ref, cv_ref, tt_ref, o_ref):
    r = pl.program_id(1)
    r_start = jnp.clip(r - WIN_R // 2, 0, GRID_ROWS - WIN_R)
    dr0 = r_start - r + (WIN_R - 1)
    row0 = pl.multiple_of(r_start * GRID_W, GRID_W)
    nkeys = WIN_R * GRID_W
    past = ck_ref.shape[0]
    lane = lax.broadcasted_iota(jnp.int32, (GRID_W, nkeys + past), 1)
    kc = lane % GRID_W
    qc = lax.broadcasted_iota(jnp.int32, (GRID_W, nkeys + past), 0)
    c_start = jnp.clip(qc - WIN_W // 2, 0, GRID_W - WIN_W)
    visible = ((kc >= c_start) & (kc < c_start + WIN_W)) | (lane >= nkeys)
    no_bias = jnp.zeros((GRID_W, past), F32)
    q = [_split_pair(t) for t in _pair_tiles((q_ref[...] * ATTN_SCALE).astype(BF16))]
    kcat = _pair_tiles(jnp.concatenate([k_ref[pl.ds(row0, nkeys), :], ck_ref[...]], axis=0).astype(BF16))
    vcat = [_with_ones(t) for t in
            _pair_tiles(jnp.concatenate([v_ref[pl.ds(row0, nkeys), :], cv_ref[...]], axis=0).astype(BF16))]
    bias = [jnp.concatenate([tt_ref[h, dr0 + 2 * j] for j in range(WIN_R // 2)] + [no_bias], axis=1)
            for h in range(N_HEADS)]
    e = [[_exp_scores(jnp.where(visible, _bdot_nt(q[j][c], kcat[j]) + bias[2 * j + c], NEG_INF))
          for c in range(2)] for j in range(len(q))]
    out = [[jnp.dot(e[j][c], vcat[j], preferred_element_type=F32) for c in range(2)] for j in range(len(q))]
    for j in range(len(q)):
        o_ref[:, j * PAIR_W:(j + 1) * PAIR_W] = _merge_pair(out[j][0], out[j][1])


def _lat_na_attention(qn, kn, vn, cache_k, cache_v, tt, n_lat):
    n = qn.shape[0]
    nb = n // n_lat
    past = cache_k.shape[1]
    qblk = pl.BlockSpec((GRID_W, D_GROUP), lambda b, r: (b * GRID_ROWS + r, 0))
    kvblk = pl.BlockSpec((n_lat, D_GROUP), lambda b, r: (b, 0))
    cblk = pl.BlockSpec((None, past, D_GROUP), lambda b, r: (b, 0, 0))
    return pl.pallas_call(
        _lat_na_kernel,
        grid=(nb, GRID_ROWS),
        in_specs=[qblk, kvblk, kvblk, cblk, cblk,
                  pl.BlockSpec(tt.shape, lambda b, r: (0, 0, 0, 0))],
        out_specs=qblk,
        out_shape=jax.ShapeDtypeStruct((n, D_GROUP), F32),
        compiler_params=_cparams("arbitrary", "arbitrary"),
        name="lat_na_attention",
    )(qn, kn, vn, cache_k, cache_v, tt)


def _rec_mix(of_ref, ob_ref, rg_ref, yf_ref, yb_ref, xr_ref, xk_ref, xv_ref, lo_ref, rk_ref, lng_ref, lnb_ref,
             gup_ref):
    pw = 2 * HEAD_DIM
    brow = lax.broadcasted_iota(jnp.int32, (pw, pw), 0) // HEAD_DIM
    bcol = lax.broadcasted_iota(jnp.int32, (pw, pw), 1) // HEAD_DIM
    ones_bd = jnp.where(brow == bcol, 1.0, 0.0).astype(BF16)

    def head_sums(x):
        outs = []
        for t in range(D_GROUP // pw):
            xt = x[:, t * pw:(t + 1) * pw]
            hi = xt.astype(BF16)
            lo = (xt - hi.astype(F32)).astype(BF16)
            outs.append(jnp.dot(hi, ones_bd, preferred_element_type=F32)
                        + jnp.dot(lo, ones_bd, preferred_element_type=F32))
        return jnp.concatenate(outs, axis=1)

    inv_d = 1.0 / HEAD_DIM
    o = of_ref[...] + ob_ref[...]
    o_ret = _silu(rg_ref[...]) * (o * lax.rsqrt(head_sums(o * o) * inv_d + RMS_EPS))
    y = yf_ref[...] + yb_ref[...]
    yc = y - head_sums(y) * inv_d
    var = head_sums(yc * yc) * inv_d
    yn = yc * lax.rsqrt(var + RWKV_GN_EPS) * lng_ref[...] + lnb_ref[...]
    bonus = head_sums(xr_ref[...] * rk_ref[...] * xk_ref[...]) * xv_ref[...]
    g_rw = _bdot(_sigmoid(lo_ref[:, 2 * D_LORA:]), gup_ref[...])
    return o_ret, (yn + bonus) * g_rw


def _tail_kernel(*refs, final, ff_chunk, n_mix):
    x_ref, mix_refs = refs[0], refs[1:1 + n_mix]
    wo_ref, gm_ref, g_ref, sh_ref, sc_ref, gf_ref, wi_ref, wf_ref, fg_ref, o_ref = refs[1 + n_mix:]
    if n_mix == 2:
        mix_a, mix_b = mix_refs[0][...], mix_refs[1][...]
    else:
        mix_a, mix_b = _rec_mix(*mix_refs)
    mix = _bdot(mix_a, wo_ref[0:D_GROUP, :]) + _bdot(mix_b, wo_ref[D_GROUP:, :])
    x1 = x_ref[...] + gm_ref[0] * mix
    h = _rms(x1) * g_ref[...]
    hb = (h * (1.0 + sc_ref[0]) + sh_ref[0]).astype(BF16)
    acc = jnp.zeros_like(x1)
    for c0 in range(0, D_FF, ff_chunk):
        gate = jnp.dot(hb, wi_ref[:, c0:c0 + ff_chunk], preferred_element_type=F32)
        up = jnp.dot(hb, wi_ref[:, D_FF + c0:D_FF + c0 + ff_chunk], preferred_element_type=F32)
        acc = acc + jnp.dot((_silu(gate) * up).astype(BF16), wf_ref[c0:c0 + ff_chunk, :],
                            preferred_element_type=F32)
    x2 = x1 + gf_ref[0] * acc
    if final:
        x2 = _rms(x2) * fg_ref[...]
    o_ref[...] = x2


def _layer_tail(x, mix_inputs, w_out, gate_mix, gain, shift, scale, gate_ffn, w_ffn_in_all, w_ffn_out_all, layer,
                final_gain, rows_per_mod, final, ff_chunk=1408):
    n, d = x.shape
    tm = TAIL_TM if len(mix_inputs) == 2 else TAIL_TM_REC
    tiles_per_mod = rows_per_mod // tm
    mod_spec = pl.BlockSpec((1, 1, d), lambda i: (i // tiles_per_mod, 0, 0))
    vec_spec = pl.BlockSpec((1, d), lambda i: (0, 0))
    resident = lambda a: pl.BlockSpec(a.shape, lambda i: (0,) * a.ndim, pipeline_mode=pl.Buffered(1))
    layer_slab = lambda a: pl.BlockSpec((None,) + a.shape[1:], lambda i: (layer, 0, 0),
                                        pipeline_mode=pl.Buffered(1))
    mix_specs = [pl.BlockSpec((tm, a.shape[1]), lambda i: (i, 0)) if a.shape[0] == n else resident(a)
                 for a in mix_inputs]
    return pl.pallas_call(
        functools.partial(_tail_kernel, final=final, ff_chunk=ff_chunk, n_mix=len(mix_inputs)),
        grid=(n // tm,),
        in_specs=[pl.BlockSpec((tm, d), lambda i: (i, 0))] + mix_specs
                 + [resident(w_out), mod_spec, vec_spec, mod_spec, mod_spec, mod_spec,
                    layer_slab(w_ffn_in_all), layer_slab(w_ffn_out_all), vec_spec],
        out_specs=pl.BlockSpec((tm, d), lambda i: (i, 0)),
        out_shape=jax.ShapeDtypeStruct((n, d), F32),
        compiler_params=_cparams("arbitrary"),
        name="layer_tail",
    )(x, *mix_inputs, w_out, gate_mix, gain.reshape(1, d), shift, scale, gate_ffn, w_ffn_in_all, w_ffn_out_all,
      final_gain.reshape(1, d))


def _log_sigmoid(x):
    return jnp.minimum(x, 0.0) - jnp.log(1.0 + jnp.exp(-jnp.abs(x)))


def _pair_block_diag(xp):
    left = lax.broadcasted_iota(jnp.int32, xp.shape, 1) < HEAD_DIM
    zero = jnp.zeros_like(xp)
    return jnp.concatenate([jnp.where(left, xp, zero), jnp.where(left, zero, xp)], axis=0)


def _states_to_pairs(state):
    nb = state.shape[0]
    s = state.reshape(nb, 2, N_HEADS // 2, 2, HEAD_DIM, HEAD_DIM)
    zero = jnp.zeros_like(s[:, :, :, 0])
    return jnp.concatenate([jnp.concatenate([s[:, :, :, 0], zero], axis=-1),
                            jnp.concatenate([zero, s[:, :, :, 1]], axis=-1)], axis=-2)


def _pairs_to_states(sbd):
    nb = sbd.shape[0]
    s = jnp.stack([sbd[..., :HEAD_DIM, :HEAD_DIM], sbd[..., HEAD_DIM:, HEAD_DIM:]], axis=3)
    return s.reshape(nb, 2, N_HEADS, HEAD_DIM, HEAD_DIM)


def _ret_kernel(dlm_ref, dlq_ref, qf_ref, kf_ref, vf_ref, qb_ref, kb_ref, vb_ref, s0_ref,
                of_ref, ob_ref, sf_ref, st, mask_scr, qdec_scr, kdec_scr, cdec_scr):
    c = pl.program_id(1)
    cs = RET_CHUNK
    pw = 2 * HEAD_DIM
    npair = N_HEADS // 2
    chains = [(d, p) for d in range(2) for p in range(npair)]

    @pl.when(c == 0)
    def _():
        st[...] = s0_ref[0]

    @pl.when((pl.program_id(0) == 0) & (c == 0))
    def _():
        row = lax.broadcasted_iota(jnp.int32, (cs, 2 * cs), 0)
        col = lax.broadcasted_iota(jnp.int32, (cs, 2 * cs), 1) % cs
        tok = lax.broadcasted_iota(jnp.int32, (cs, pw), 0)
        dist = [row - col, col - row]
        tq = [tok.astype(F32), (cs - 1 - tok).astype(F32)]
        for i, (d, p) in enumerate(chains):
            lgm = _log_sigmoid(dlm_ref[d, p])[0:1, :]
            lgq = _log_sigmoid(dlq_ref[d, p])
            mask_scr[i] = jnp.where(dist[d] >= 0, jnp.exp(jnp.maximum(dist[d], 0).astype(F32) * lgm), 0.0)
            qdec_scr[i] = jnp.exp((tq[d] + 1.0) * lgq[0:1, :])
            kdec_scr[i] = jnp.exp((cs - 1.0 - tq[d]) * lgq[0:1, :])
            cdec_scr[i] = jnp.exp(cs * lgq)

    brow = lax.broadcasted_iota(jnp.int32, (pw, pw), 0) // HEAD_DIM
    bcol = lax.broadcasted_iota(jnp.int32, (pw, pw), 1) // HEAD_DIM
    same_head = brow == bcol
    refs = [(qf_ref, kf_ref, vf_ref, of_ref), (qb_ref, kb_ref, vb_ref, ob_ref)]
    sls = [slice(p * pw, (p + 1) * pw) for _, p in chains]
    q = [refs[d][0][:, sls[i]] for i, (d, _) in enumerate(chains)]
    k = [refs[d][1][:, sls[i]] * ATTN_SCALE for i, (d, _) in enumerate(chains)]
    v = [refs[d][2][:, sls[i]] for i, (d, _) in enumerate(chains)]
    idx = range(len(chains))
    s = [_bdot_nt(q[i], _pair_block_diag(k[i])) * mask_scr[i] for i in idx]
    inner = [_bdot(s[i], _pair_block_diag(v[i])) for i in idx]
    state = [st[d, p] for d, p in chains]
    cross = [_bdot(q[i], state[i]) * qdec_scr[i] for i in idx]
    kv = [_bdot_tn(k[i] * kdec_scr[i], v[i]) for i in idx]
    for i, (d, p) in enumerate(chains):
        refs[d][3][:, sls[i]] = inner[i] + cross[i]
        st[d, p] = state[i] * cdec_scr[i][0:1, :] + jnp.where(same_head, kv[i], 0.0)

    @pl.when(c == pl.num_programs(1) - 1)
    def _():
        sf_ref[0] = st[...]


def _retention_scan(decay_logit, q, k, v, state0, seq):
    n = q.shape[0]
    nb = n // seq
    nc = seq // RET_CHUNK
    npair, pw = N_HEADS // 2, 2 * HEAD_DIM
    dl = decay_logit.astype(F32)
    dl_m = jnp.broadcast_to(jnp.repeat(dl, RET_CHUNK, axis=1).reshape(2, npair, 1, 2 * RET_CHUNK),
                            (2, npair, 8, 2 * RET_CHUNK))
    dl_q = jnp.broadcast_to(jnp.repeat(dl, HEAD_DIM, axis=1).reshape(2, npair, 1, pw), (2, npair, 8, pw))
    blk_f = pl.BlockSpec((RET_CHUNK, D_GROUP), lambda b, c: (b * nc + c, 0))
    blk_b = pl.BlockSpec((RET_CHUNK, D_GROUP), lambda b, c: (b * nc + nc - 1 - c, 0))
    st_blk = pl.BlockSpec((1, 2, npair, pw, pw), lambda b, c: (b, 0, 0, 0, 0))
    o_f, o_b, sf_bd = pl.pallas_call(
        _ret_kernel,
        grid=(nb, nc),
        in_specs=[pl.BlockSpec(dl_m.shape, lambda b, c: (0, 0, 0, 0)),
                  pl.BlockSpec(dl_q.shape, lambda b, c: (0, 0, 0, 0)),
                  blk_f, blk_f, blk_f, blk_b, blk_b, blk_b, st_blk],
        out_specs=[blk_f, blk_b, st_blk],
        out_shape=[jax.ShapeDtypeStruct((n, D_GROUP), F32), jax.ShapeDtypeStruct((n, D_GROUP), F32),
                   jax.ShapeDtypeStruct((nb, 2, npair, pw, pw), F32)],
        scratch_shapes=[pltpu.VMEM((2, npair, pw, pw), F32),
                        pltpu.VMEM((2 * npair, RET_CHUNK, 2 * RET_CHUNK), F32),
                        pltpu.VMEM((2 * npair, RET_CHUNK, pw), F32),
                        pltpu.VMEM((2 * npair, RET_CHUNK, pw), F32),
                        pltpu.VMEM((2 * npair, 8, pw), F32)],
        compiler_params=_cparams("arbitrary", "arbitrary"),
        name="retention_scan",
    )(dl_m, dl_q, q, k, v, q, k, v, _states_to_pairs(state0))
    return o_f, o_b, _pairs_to_states(sf_bd)


def _rwkv_kernel(xrf_ref, xkf_ref, xvf_ref, lof_ref, xrb_ref, xkb_ref, xvb_ref, lob_ref,
                 w0_ref, wup_ref, a0_ref, aup_ref, kk_ref, ka_ref, s0_ref, yf_ref, yb_ref, sf_ref, st):
    c = pl.program_id(1)
    cs = RWKV_CHUNK
    nsub = RWKV_CHUNKS_PER_STEP
    npair = N_HEADS // 2

    @pl.when(c == 0)
    def _():
        st[...] = s0_ref[0]

    row = lax.broadcasted_iota(jnp.int32, (cs, cs), 0)
    col = lax.broadcasted_iota(jnp.int32, (cs, cs), 1)
    pw = 2 * HEAD_DIM
    prow = lax.broadcasted_iota(jnp.int32, (cs, pw), 0)
    pcol = lax.broadcasted_iota(jnp.int32, (cs, pw), 1) % HEAD_DIM
    eye = jnp.where(pcol == prow, 1.0, 0.0).astype(F32)
    sub_diag = (prow // RWKV_SUB) == (pcol // RWKV_SUB)
    brow = lax.broadcasted_iota(jnp.int32, (pw, pw), 0) // HEAD_DIM
    bcol = lax.broadcasted_iota(jnp.int32, (pw, pw), 1) // HEAD_DIM
    same_head = brow == bcol
    ones_bd = jnp.where(same_head, 1.0, 0.0).astype(BF16)

    bd = _pair_block_diag

    def direction(d, sub, xr_ref, xk_ref, xv_ref, lo_ref):
        rows = pl.ds(sub * cs, cs)
        tri = jnp.where((row >= col) if d == 0 else (col >= row), 1.0, 0.0).astype(BF16)
        dist = (prow - pcol) if d == 0 else (pcol - prow)
        xk = xk_ref[rows, :]
        lo = lo_ref[rows, :]
        w_log = -math.exp(-0.5) * _sigmoid(w0_ref[d] + _bdot(jnp.tanh(lo[:, 0:D_LORA]), wup_ref[d]))
        a_all = _sigmoid(a0_ref[d] + _bdot(lo[:, D_LORA:2 * D_LORA], aup_ref[d]))
        w1 = w_log.astype(BF16)
        r1 = w_log - w1.astype(F32)
        w2 = r1.astype(BF16)
        w3 = (r1 - w2.astype(F32)).astype(BF16)
        cum = (jnp.dot(tri, w1, preferred_element_type=F32) + jnp.dot(tri, w2, preferred_element_type=F32)
               + jnp.dot(tri, w3, preferred_element_type=F32))
        cum_end = cum[cs - 1:cs, :] if d == 0 else cum[0:1, :]
        return dict(earlier=dist > 0, upto=dist >= 0, xr=xr_ref[rows, :], xv=xv_ref[rows, :], a_all=a_all,
                    e_incl=jnp.exp(cum), e_excl=jnp.exp(cum - w_log), e_neg=jnp.exp(-cum),
                    e_end=jnp.exp(cum_end - cum), w_end=jnp.exp(cum_end), kk_all=xk * kk_ref[...],
                    keff_all=xk * (1.0 + (a_all - 1.0) * ka_ref[...]))

    in_refs = [(xrf_ref, xkf_ref, xvf_ref, lof_ref), (xrb_ref, xkb_ref, xvb_ref, lob_ref)]
    dirs = {(d, sub): direction(d, sub, *in_refs[d]) for d in range(2) for sub in range(nsub)}
    y_refs = [yf_ref, yb_ref]

    chains = [(d, sub, p) for d in range(2) for sub in range(nsub) for p in range(npair)]
    pairs = range(len(chains))
    sls = [slice(p * pw, (p + 1) * pw) for _, _, p in chains]
    pick = lambda name: [dirs[d, sub][name][:, sls[i]] for i, (d, sub, _) in enumerate(chains)]
    earlier = [dirs[d, sub]["earlier"] for d, sub, _ in chains]
    upto = [dirs[d, sub]["upto"] for d, sub, _ in chains]
    cat = lambda x, y: jnp.concatenate([x, y], axis=0)

    def head_sum(x):
        x1 = x.astype(BF16)
        r1 = x - x1.astype(F32)
        x2 = r1.astype(BF16)
        x3 = (r1 - x2.astype(F32)).astype(BF16)
        f = functools.partial(jnp.dot, preferred_element_type=F32)
        return f(jnp.concatenate([x1, x2], axis=1), jnp.concatenate([ones_bd, ones_bd], axis=0)) + f(x3, ones_bd)

    e_incl, e_excl, e_neg, e_end, w_end = (pick(k) for k in ("e_incl", "e_excl", "e_neg", "e_end", "w_end"))
    kk = [x * lax.rsqrt(head_sum(x * x) + 1e-12) for x in pick("kk_all")]
    b = [kk[p] * a for p, a in zip(pairs, pick("a_all"))]
    keff = pick("keff_all")
    v = pick("xv")
    a_t = [-kk[p] * e_excl[p] for p in pairs]
    r_t = [xr * e_incl[p] for p, xr in zip(pairs, pick("xr"))]
    ar = [cat(a_t[p], r_t[p]).astype(BF16) for p in pairs]
    g = [_bdot_nt(ar[p], cat(bd(b[p] * e_neg[p]), bd(keff[p] * e_neg[p]))) for p in pairs]
    n_mat = [jnp.where(earlier[p], g[p][:cs, :pw], 0.0) for p in pairs]
    m_rb = [jnp.where(upto[p], g[p][cs:, :pw], 0.0) for p in pairs]
    a_ak = [jnp.where(earlier[p], g[p][:cs, pw:], 0.0) for p in pairs]
    m_rk = [jnp.where(upto[p], g[p][cs:, pw:], 0.0) for p in pairs]
    bd_s = lambda xs: (bd(xs[0]), bd(xs[1]))
    n_d = [jnp.where(sub_diag, n, 0.0) for n in n_mat]
    n_o = [jnp.where(sub_diag, 0.0, n) for n in n_mat]
    n_s = [_split2(n) for n in n_d]
    x = [eye + n for n in n_d]
    pk = [_mm3s(ns, bd_s(ns)) for ns in n_s]
    for _ in range(RWKV_SUB_DOUBLINGS - 1):
        xp = [_mm3s(_split2(cat(x[p], pk[p])), bd_s(_split2(pk[p]))) for p in pairs]
        x = [x[p] + xp[p][:cs] for p in pairs]
        pk = [r[cs:] for r in xp]
    t_d = [x[p] + _mm3s(_split2(x[p]), bd_s(_split2(pk[p]))) for p in pairs]
    lcat = lambda x, y: jnp.concatenate([x, y], axis=1)
    m_o = [_bdot(t_d[p], bd(n_o[p])) for p in pairs]
    mz = [_bdot(m_o[p], lcat(bd(t_d[p]), bd(m_o[p]))) for p in pairs]
    z = [t_d[p] + mz[p][:, :pw] for p in pairs]
    t_inv = [(z[p] + _bdot(mz[p][:, pw:], bd(z[p]))).astype(BF16) for p in pairs]
    av = [_bdot(cat(a_ak[p], m_rk[p]), bd(v[p])) for p in pairs]
    tau = [_bdot(t_inv[p], lcat(bd(a_t[p]), bd(av[p][:cs]))) for p in pairs]
    bk_end = [cat(b[p] * e_end[p], keff[p] * e_end[p]) for p in pairs]
    for step in range(nsub):
        now = [i for i, (d, sub, _) in enumerate(chains) if sub == (step if d == 0 else nsub - 1 - step)]
        s_prev = {i: st[chains[i][0], chains[i][2]] for i in now}
        hs = {i: _bdot_nt(cat(tau[i][:, :pw], r_t[i]), s_prev[i]) for i in now}
        u = {i: hs[i][:cs] + tau[i][:, pw:] for i in now}
        y = {i: hs[i][cs:] + _bdot(m_rb[i], bd(u[i])) + av[i][cs:] for i in now}
        for i in now:
            d, sub, p = chains[i]
            y_refs[d][pl.ds(sub * cs, cs), sls[i]] = y[i]
            st[d, p] = s_prev[i] * w_end[i] + jnp.where(same_head, _bdot_tn(cat(u[i], v[i]), bk_end[i]), 0.0)

    @pl.when(c == pl.num_programs(1) - 1)
    def _():
        sf_ref[0] = st[...]


def _rwkv_scan(xr, xk, xv, lora, w0, w_up, a0, a_up, k_k, k_a, state0, seq):
    n = xr.shape[0]
    nb = n // seq
    blk = RWKV_CHUNK * RWKV_CHUNKS_PER_STEP
    nc = seq // blk

    fwd_idx = lambda b, c: (b * nc + c, 0)
    bwd_idx = lambda b, c: (b * nc + nc - 1 - c, 0)
    blk_f = pl.BlockSpec((blk, D_GROUP), fwd_idx)
    blk_b = pl.BlockSpec((blk, D_GROUP), bwd_idx)
    lblk_f = pl.BlockSpec((blk, lora.shape[1]), fwd_idx)
    lblk_b = pl.BlockSpec((blk, lora.shape[1]), bwd_idx)
    dvec = pl.BlockSpec((2, 1, D_GROUP), lambda b, c: (0, 0, 0))
    dmat = pl.BlockSpec((2, D_LORA, D_GROUP), lambda b, c: (0, 0, 0))
    vec = pl.BlockSpec((1, D_GROUP), lambda b, c: (0, 0))
    npair, pw = N_HEADS // 2, 2 * HEAD_DIM
    st_blk = pl.BlockSpec((1, 2, npair, pw, pw), lambda b, c: (b, 0, 0, 0, 0))
    y_f, y_b, sf_bd = pl.pallas_call(
        _rwkv_kernel,
        grid=(nb, nc),
        in_specs=[blk_f, blk_f, blk_f, lblk_f, blk_b, blk_b, blk_b, lblk_b,
                  dvec, dmat, dvec, dmat, vec, vec, st_blk],
        out_specs=[blk_f, blk_b, st_blk],
        out_shape=[jax.ShapeDtypeStruct((n, D_GROUP), F32), jax.ShapeDtypeStruct((n, D_GROUP), F32),
                   jax.ShapeDtypeStruct((nb, 2, npair, pw, pw), F32)],
        scratch_shapes=[pltpu.VMEM((2, npair, pw, pw), F32)],
        compiler_params=_cparams("arbitrary", "arbitrary"),
        name="rwkv7_scan",
    )(xr, xk, xv, lora, xr, xk, xv, lora, w0.reshape(2, 1, D_GROUP), w_up, a0.reshape(2, 1, D_GROUP), a_up,
      k_k.reshape(1, D_GROUP), k_a.reshape(1, D_GROUP), _states_to_pairs(state0))
    return y_f, y_b, _pairs_to_states(sf_bd)


ATTN_WIDTHS = (D_GROUP,) * 6
REC_WIDTHS = (D_GROUP,) * 7 + (2 * D_LORA + D_LORA_G,)


def kernel(x_prompt, x_sample, cache_na_k, cache_na_v, cache_diff_k, cache_diff_v, state_ret, state_rwkv, c, c_ctx, norm_mix_g, norm_ffn_g, norm_final_g, w_ada, b_ada, w_in_attn, w_out_attn, na_rpb, diff_lq1, diff_lk1, diff_lq2, diff_lk2, w_in_rec, w_out_rec, ret_decay_logit, rw_w0, rw_w_up, rw_a0, rw_a_up, rw_g_up, rw_k_k, rw_k_a, rw_r_k, rw_ln_g, rw_ln_b, w_ffn_in, w_ffn_out):
    bp, seq, d = x_prompt.shape
    bl, n_lat, _ = x_sample.shape
    depth = w_ada.shape[0]
    ctx = x_prompt.reshape(bp * seq, d)
    lat = x_sample.reshape(bl * n_lat, d)

    cond8 = jnp.zeros((8, d), F32).at[0].set(c_ctx).at[1:1 + bl].set(c)
    mods = _ada_modulation(cond8, w_ada, b_ada).reshape(depth, 8, 6, d)
    w_ffn_in_bf16 = w_ffn_in.astype(BF16)
    w_ffn_out_bf16 = w_ffn_out.astype(BF16)

    outs = {}
    for layer in range(depth):
        m_ctx = [mods[layer, 0:1, j].reshape(1, 1, d) for j in range(6)]
        m_lat = [mods[layer, 1:1 + bl, j].reshape(bl, 1, d) for j in range(6)]
        if layer % 2 == 0:
            i = layer // 2
            lam_init = 0.8 - 0.6 * math.exp(-0.3 * layer)
            lam_params = jnp.stack([diff_lq1[i], diff_lk1[i], diff_lq2[i], diff_lk2[i]]).astype(F32)
            w_in = w_in_attn[i].astype(BF16)
            w_out = w_out_attn[i].astype(BF16)
            cache_shapes = {1: (N_HEADS, HEAD_DIM), 2: (N_HEADS, HEAD_DIM),
                            4: (N_HEADS // 2, 2, HEAD_DIM), 5: (N_HEADS // 2, 2 * HEAD_DIM)}
            pc = _norm_mod_matmul(ctx, norm_mix_g[layer], m_ctx[0], m_ctx[1], w_in, ATTN_WIDTHS, bp * seq,
                                  PROJ_TM_CACHE, cache_shapes)
            outs["na_k"], outs["na_v"], outs["df_k"], outs["df_v"] = pc[6:]
            pc = pc[:6]
            plat = _norm_mod_matmul(lat, norm_mix_g[layer], m_lat[0], m_lat[1], w_in, ATTN_WIDTHS, n_lat, PROJ_TM)
            mix_ctx = _ctx_attention(lam_params, *pc, seq, lam_init)
            past = cache_na_k.shape[2]
            tt = _na_bias_table(na_rpb[i])
            o_na = _lat_na_attention(plat[0], plat[1], plat[2],
                                     cache_na_k[:, i].reshape(bl, past, D_GROUP),
                                     cache_na_v[:, i].reshape(bl, past, D_GROUP), tt, n_lat)
            o_df = _lat_diff_attention(lam_params, plat[3], plat[4], plat[5],
                                       cache_diff_k[:, i].reshape(bl, past, D_GROUP),
                                       cache_diff_v[:, i].reshape(bl, past, D_GROUP), n_lat, lam_init)
            mix_lat = (o_na, o_df)
        else:
            j = layer // 2
            w_in = w_in_rec[j].astype(BF16)
            w_out = w_out_rec[j].astype(BF16)
            pc = _norm_mod_matmul(ctx, norm_mix_g[layer], m_ctx[0], m_ctx[1], w_in, REC_WIDTHS, bp * seq, PROJ_TM)
            plat = _norm_mod_matmul(lat, norm_mix_g[layer], m_lat[0], m_lat[1], w_in, REC_WIDTHS, n_lat, PROJ_TM)
            zero_state = jnp.zeros((bp, 2, N_HEADS, HEAD_DIM, HEAD_DIM), F32)
            mixes = []
            for is_ctx, p, s_ret0, s_rw0, sq in ((True, pc, zero_state, zero_state, seq),
                                                 (False, plat, state_ret[:, j], state_rwkv[:, j], n_lat)):
                rq, rk, rv, rg, wr, wk, wv, lora = p
                o_f, o_b, s_ret = _retention_scan(ret_decay_logit[j], rq, rk, rv, s_ret0, sq)
                y_f, y_b, s_rw = _rwkv_scan(wr, wk, wv, lora, rw_w0[j], rw_w_up[j].astype(BF16), rw_a0[j],
                                            rw_a_up[j].astype(BF16), rw_k_k[j], rw_k_a[j], s_rw0, sq)
                mixes.append((o_f, o_b, rg, y_f, y_b, wr, wk, wv, lora, rw_r_k[j].reshape(1, D_GROUP),
                              rw_ln_g[j].reshape(1, D_GROUP), rw_ln_b[j].reshape(1, D_GROUP),
                              rw_g_up[j].astype(BF16)))
                if is_ctx:
                    outs["ret"], outs["rwkv"] = s_ret, s_rw
            mix_ctx, mix_lat = mixes
        final = layer == depth - 1
        ctx = _layer_tail(ctx, mix_ctx, w_out, m_ctx[2], norm_ffn_g[layer], m_ctx[3], m_ctx[4], m_ctx[5],
                          w_ffn_in_bf16, w_ffn_out_bf16, layer, norm_final_g, bp * seq, final)
        lat = _layer_tail(lat, mix_lat, w_out, m_lat[2], norm_ffn_g[layer], m_lat[3], m_lat[4], m_lat[5],
                          w_ffn_in_bf16, w_ffn_out_bf16, layer, norm_final_g, n_lat, final)

    y_prompt = ctx.reshape(bp, seq, d)
    y_sample = lat.reshape(bl, n_lat, d)
    new_cache_na_k = outs["na_k"].reshape(bp, 1, seq, N_HEADS, HEAD_DIM)
    new_cache_na_v = outs["na_v"].reshape(bp, 1, seq, N_HEADS, HEAD_DIM)
    new_cache_diff_k = outs["df_k"].reshape(bp, 1, seq, N_HEADS // 2, 2, HEAD_DIM)
    new_cache_diff_v = outs["df_v"].reshape(bp, 1, seq, N_HEADS // 2, 2 * HEAD_DIM)
    new_state_ret = outs["ret"].reshape(bp, 1, 2, N_HEADS, HEAD_DIM, HEAD_DIM)
    new_state_rwkv = outs["rwkv"].reshape(bp, 1, 2, N_HEADS, HEAD_DIM, HEAD_DIM)
    return (y_prompt, y_sample, new_cache_na_k, new_cache_na_v, new_cache_diff_k, new_cache_diff_v,
            new_state_ret, new_state_rwkv)
```

```python
import functools
import math

import jax
import jax.numpy as jnp
from jax import lax
from jax.experimental import pallas as pl
from jax.experimental.pallas import tpu as pltpu

F32 = jnp.float32
BF16 = jnp.bfloat16

D_MODEL = 1024
HEAD_DIM = 64
N_HEADS = 8
D_GROUP = 512
GRID_W = 64
GRID_ROWS = 32
WIN_R = 8
WIN_W = 16
RPB_R = 15
RPB_C = 31
PAST_LEN = 256
D_FF = 2816
D_LORA = 64
D_LORA_G = 128
ROPE_BASE = 10000.0
RMS_EPS = 1e-6
RWKV_GN_EPS = 64e-5
NEG_INF = -1e30
ATTN_SCALE = HEAD_DIM ** -0.5
RET_CHUNK = 128
RET_CHUNKS_PER_STEP = 2
RWKV_CHUNK = 64
RWKV_CHUNKS_PER_STEP = 4
RWKV_SUB = 16
RWKV_SUB_DOUBLINGS = 3
VMEM_LIMIT = 56 * 1024 * 1024
TAIL_TM = 512
TAIL_TM_REC = 256
PROJ_TM = 512
PROJ_TM_CACHE = 256


def _cparams(*sem):
    return pltpu.CompilerParams(dimension_semantics=sem, vmem_limit_bytes=VMEM_LIMIT)


def _sigmoid(x):
    return 1.0 / (1.0 + jnp.exp(-x))


def _silu(x):
    return x * _sigmoid(x)


def _rms(x, eps=RMS_EPS):
    return x * lax.rsqrt(jnp.mean(x * x, axis=-1, keepdims=True) + eps)


def _bdot(a, b):
    return jnp.dot(a.astype(BF16), b.astype(BF16), preferred_element_type=F32)


def _bdot_nt(a, b):
    return lax.dot_general(a.astype(BF16), b.astype(BF16), (((1,), (1,)), ((), ())),
                           preferred_element_type=F32)


def _bdot_tn(a, b):
    return lax.dot_general(a.astype(BF16), b.astype(BF16), (((0,), (0,)), ((), ())),
                           preferred_element_type=F32)


_DN = {"nn": (((1,), (0,)), ((), ())), "nt": (((1,), (1,)), ((), ())), "tn": (((0,), (0,)), ((), ()))}


def _split2(x):
    hi = x.astype(BF16)
    lo = (x - hi.astype(F32)).astype(BF16)
    return hi, lo


def _mm3s(a_split, b_split):
    ah, al = a_split
    bh, bl = b_split
    d = functools.partial(jnp.dot, preferred_element_type=F32)
    return d(jnp.concatenate([ah, al], axis=1), jnp.concatenate([bh, bh], axis=0)) + d(ah, bl)


def _diff_lambda(lp, lam_init):
    s1 = jnp.sum(lp[0:1, :] * lp[1:2, :], axis=-1, keepdims=True)
    s2 = jnp.sum(lp[2:3, :] * lp[3:4, :], axis=-1, keepdims=True)
    return jnp.exp(s1) - jnp.exp(s2) + lam_init


def _ada_kernel(c_ref, w_ref, b_ref, o_ref):
    s = _silu(c_ref[...])
    o_ref[0] = _bdot(s, w_ref[0]) + b_ref[0]


def _ada_modulation(cond8, w_ada, b_ada):
    depth, d, n = w_ada.shape
    tn = 1536
    return pl.pallas_call(
        _ada_kernel,
        grid=(depth, n // tn),
        in_specs=[pl.BlockSpec((8, d), lambda l, j: (0, 0)),
                  pl.BlockSpec((1, d, tn), lambda l, j: (l, 0, j)),
                  pl.BlockSpec((1, 1, tn), lambda l, j: (l, 0, j))],
        out_specs=pl.BlockSpec((1, 8, tn), lambda l, j: (l, 0, j)),
        out_shape=jax.ShapeDtypeStruct((depth, 8, n), F32),
        compiler_params=_cparams("arbitrary", "arbitrary"),
        name="ada_modulation",
    )(cond8, w_ada, b_ada.reshape(depth, 1, n))


def _nmm_kernel(x_ref, g_ref, sh_ref, sc_ref, w_ref, *o_refs, widths, cache_shapes):
    h = _rms(x_ref[...]) * g_ref[...]
    h = h * (1.0 + sc_ref[0]) + sh_ref[0]
    hb = h.astype(BF16)
    c_refs = dict(zip(cache_shapes, o_refs[len(widths):]))
    off = 0
    for gi, (o_ref, w) in enumerate(zip(o_refs, widths)):
        res = jnp.dot(hb, w_ref[:, off:off + w], preferred_element_type=F32)
        o_ref[...] = res
        off += w
        if gi in c_refs:
            c_refs[gi][...] = res.reshape((res.shape[0],) + cache_shapes[gi])


def _norm_mod_matmul(x, gain, shift, scale, w_bf16, widths, rows_per_mod, tm, cache_shapes=None):
    n, d = x.shape
    cache_shapes = cache_shapes or {}
    tiles_per_mod = rows_per_mod // tm
    mod_spec = pl.BlockSpec((1, 1, d), lambda i: (i // tiles_per_mod, 0, 0))
    cache_specs = [pl.BlockSpec((tm,) + dims, lambda i, nd=len(dims): (i,) + (0,) * nd)
                   for dims in cache_shapes.values()]
    return pl.pallas_call(
        functools.partial(_nmm_kernel, widths=widths, cache_shapes=cache_shapes),
        grid=(n // tm,),
        in_specs=[pl.BlockSpec((tm, d), lambda i: (i, 0)),
                  pl.BlockSpec((1, d), lambda i: (0, 0)),
                  mod_spec, mod_spec,
                  pl.BlockSpec(w_bf16.shape, lambda i: (0, 0), pipeline_mode=pl.Buffered(1))],
        out_specs=[pl.BlockSpec((tm, w), lambda i: (i, 0)) for w in widths] + cache_specs,
        out_shape=([jax.ShapeDtypeStruct((n, w), F32) for w in widths]
                   + [jax.ShapeDtypeStruct((n,) + dims, F32) for dims in cache_shapes.values()]),
        compiler_params=_cparams("arbitrary"),
        name="norm_mod_matmul",
    )(x, gain.reshape(1, d), shift, scale, w_bf16)


PAIR_W = 2 * HEAD_DIM


def _pair_tiles(x):
    return [x[:, j * PAIR_W:(j + 1) * PAIR_W] for j in range(x.shape[1] // PAIR_W)]


def _split_pair(tile):
    left = lax.broadcasted_iota(jnp.int32, tile.shape, 1) < HEAD_DIM
    zero = jnp.zeros_like(tile)
    return jnp.where(left, tile, zero), jnp.where(left, zero, tile)


def _with_ones(v):
    return jnp.concatenate([v, jnp.ones_like(v)], axis=1)


def _exp_scores(s):
    return jnp.exp(s - jnp.max(s, axis=-1, keepdims=True)).astype(BF16)


def _merge_pair(out_even, out_odd):
    left = lax.broadcasted_iota(jnp.int32, (out_even.shape[0], PAIR_W), 1) < HEAD_DIM
    return jnp.where(left, out_even[:, :PAIR_W] * (1.0 / out_even[:, PAIR_W:]),
                     out_odd[:, :PAIR_W] * (1.0 / out_odd[:, PAIR_W:]))


def _diff_combine(out1, out2, lam, lam_init):
    o = out1[:, :PAIR_W] * (1.0 / out1[:, PAIR_W:]) - out2[:, :PAIR_W] * (lam / out2[:, PAIR_W:])
    return _rms(o) * (1.0 - lam_init)


def _ctx_attn_kernel(lp_ref, qn_ref, kn_ref, vn_ref, qd_ref, kd_ref, vd_ref, on_ref, od_ref, *, lam_init):
    tiles = range(D_GROUP // PAIR_W)
    lam = _diff_lambda(lp_ref[...], lam_init)
    q = [_split_pair(t) for t in _pair_tiles((qn_ref[...] * ATTN_SCALE).astype(BF16))
         + _pair_tiles((qd_ref[...] * ATTN_SCALE).astype(BF16))]
    k = _pair_tiles(kn_ref[...].astype(BF16)) + _pair_tiles(kd_ref[...].astype(BF16))
    v = [_with_ones(t) for t in _pair_tiles(vn_ref[...].astype(BF16)) + _pair_tiles(vd_ref[...].astype(BF16))]
    e = [[_exp_scores(_bdot_nt(q[j][c], k[j])) for c in range(2)] for j in range(2 * len(tiles))]
    out = [[jnp.dot(e[j][c], v[j], preferred_element_type=F32) for c in range(2)] for j in range(2 * len(tiles))]
    for j in tiles:
        sl = slice(j * PAIR_W, (j + 1) * PAIR_W)
        on_ref[:, sl] = _merge_pair(out[j][0], out[j][1])
        od_ref[:, sl] = _diff_combine(out[len(tiles) + j][0], out[len(tiles) + j][1], lam, lam_init)


def _ctx_attention(lam_params, qn, kn, vn, qd, kd, vd, seq, lam_init):
    n = qn.shape[0]
    blk = pl.BlockSpec((seq, D_GROUP), lambda b: (b, 0))
    return pl.pallas_call(
        functools.partial(_ctx_attn_kernel, lam_init=lam_init),
        grid=(n // seq,),
        in_specs=[pl.BlockSpec(lam_params.shape, lambda b: (0, 0))] + [blk] * 6,
        out_specs=[blk, blk],
        out_shape=[jax.ShapeDtypeStruct((n, D_GROUP), F32)] * 2,
        compiler_params=_cparams("arbitrary"),
        name="ctx_attention",
    )(lam_params, qn, kn, vn, qd, kd, vd)


def _rope_tables(n):
    quarter = HEAD_DIM // 4
    pos = jnp.arange(n)
    inv_freq = ROPE_BASE ** (-jnp.arange(quarter, dtype=F32) / quarter)
    lane = jnp.arange(HEAD_DIM)
    p = jnp.where(lane[None, :] < HEAD_DIM // 2, (pos // GRID_W)[:, None], (pos % GRID_W)[:, None]).astype(F32)
    ang = p * inv_freq[lane % quarter][None, :]
    sign = jnp.where((lane % (2 * quarter)) < quarter, -1.0, 1.0).astype(F32)
    cos = jnp.tile(jnp.cos(ang), (1, N_HEADS))
    sin = jnp.tile(jnp.sin(ang) * sign[None, :], (1, N_HEADS))
    return cos, sin


def _rope(x, cos, sin_signed):
    quarter = HEAD_DIM // 4
    width = x.shape[-1]
    lane = lax.broadcasted_iota(jnp.int32, x.shape, 1)
    first = (lane % (2 * quarter)) < quarter
    partner = jnp.where(first, pltpu.roll(x, width - quarter, 1), pltpu.roll(x, quarter, 1))
    return x * cos + partner * sin_signed


def _lat_diff_kernel(lp_ref, q_ref, k_ref, v_ref, ck_ref, cv_ref, cosq_ref, sinq_ref, cosk_ref, sink_ref,
                     o_ref, kall, vall, *, lam_init, n_lat):
    @pl.when(pl.program_id(1) == 0)
    def _():
        kall[0:n_lat, :] = _rope(k_ref[...], cosk_ref[...], sink_ref[...]).astype(BF16)
        kall[n_lat:, :] = ck_ref[...].astype(BF16)
        for j, (vt, ct) in enumerate(zip(_pair_tiles(v_ref[...].astype(BF16)),
                                         _pair_tiles(cv_ref[...].astype(BF16)))):
            vall[j, 0:n_lat, :] = _with_ones(vt)
            vall[j, n_lat:, :] = _with_ones(ct)

    lam = _diff_lambda(lp_ref[...], lam_init)
    q = (_rope(q_ref[...], cosq_ref[...], sinq_ref[...]) * ATTN_SCALE).astype(BF16)
    for j, qt in enumerate(_pair_tiles(q)):
        sl = slice(j * PAIR_W, (j + 1) * PAIR_W)
        out = [jnp.dot(_exp_scores(_bdot_nt(qc, kall[:, sl])), vall[j], preferred_element_type=F32)
               for qc in _split_pair(qt)]
        o_ref[:, sl] = _diff_combine(out[0], out[1], lam, lam_init)


def _lat_diff_attention(lam_params, qd, kd, vd, cache_k, cache_v, n_lat, lam_init, tq=256):
    n = qd.shape[0]
    nb = n // n_lat
    nq = n_lat // tq
    past = cache_k.shape[1]
    cos, sin = _rope_tables(n_lat)
    qblk = pl.BlockSpec((tq, D_GROUP), lambda b, i: (b * nq + i, 0))
    once = pl.Buffered(1)
    kvblk = pl.BlockSpec((n_lat, D_GROUP), lambda b, i: (b, 0), pipeline_mode=once)
    cblk = pl.BlockSpec((None, past, D_GROUP), lambda b, i: (b, 0, 0))
    return pl.pallas_call(
        functools.partial(_lat_diff_kernel, lam_init=lam_init, n_lat=n_lat),
        grid=(nb, nq),
        in_specs=[pl.BlockSpec(lam_params.shape, lambda b, i: (0, 0)),
                  qblk, kvblk, kvblk, cblk, cblk,
                  pl.BlockSpec((tq, D_GROUP), lambda b, i: (i, 0)),
                  pl.BlockSpec((tq, D_GROUP), lambda b, i: (i, 0)),
                  pl.BlockSpec((n_lat, D_GROUP), lambda b, i: (0, 0), pipeline_mode=once),
                  pl.BlockSpec((n_lat, D_GROUP), lambda b, i: (0, 0), pipeline_mode=once)],
        out_specs=qblk,
        out_shape=jax.ShapeDtypeStruct((n, D_GROUP), F32),
        scratch_shapes=[pltpu.VMEM((n_lat + past, D_GROUP), BF16),
                        pltpu.VMEM((D_GROUP // PAIR_W, n_lat + past, 2 * PAIR_W), BF16)],
        compiler_params=_cparams("arbitrary", "arbitrary"),
        name="lat_diff_attention",
    )(lam_params, qd, kd, vd, cache_k, cache_v, cos, sin, cos, sin)


def _na_bias_kernel(rpb_ref, tt_ref):
    h = pl.program_id(0)
    lane = lax.broadcasted_iota(jnp.int32, (GRID_W, 2 * GRID_W), 1)
    qc = lax.broadcasted_iota(jnp.int32, (GRID_W, 2 * GRID_W), 0)
    dc = jnp.clip(lane % GRID_W - qc + (WIN_W - 1), 0, RPB_C - 1)
    first = lane < GRID_W
    for dr in range(RPB_R - 1):
        acc = jnp.zeros((GRID_W, 2 * GRID_W), F32)
        for d in range(RPB_C):
            val = jnp.where(first, rpb_ref[h * RPB_R + dr, d], rpb_ref[h * RPB_R + dr + 1, d])
            acc = jnp.where(dc == d, val, acc)
        tt_ref[0, dr] = acc


def _na_bias_table(rpb):
    return pl.pallas_call(
        _na_bias_kernel,
        grid=(N_HEADS,),
        in_specs=[pl.BlockSpec(memory_space=pltpu.SMEM)],
        out_specs=pl.BlockSpec((1, RPB_R - 1, GRID_W, 2 * GRID_W), lambda h: (h, 0, 0, 0)),
        out_shape=jax.ShapeDtypeStruct((N_HEADS, RPB_R - 1, GRID_W, 2 * GRID_W), F32),
        compiler_params=_cparams("arbitrary"),
        name="na_bias_table",
    )(rpb.reshape(N_HEADS * RPB_R, RPB_C))


def _lat_na_kernel(q_ref, k_ref, v_ref, ck_ref, cv_ref, tt_ref, o_ref):
    r = pl.program_id(1)
    r_start = jnp.clip(r - WIN_R // 2, 0, GRID_ROWS - WIN_R)
    dr0 = r_start - r + (WIN_R - 1)
    row0 = pl.multiple_of(r_start * GRID_W, GRID_W)
    nkeys = WIN_R * GRID_W
    past = ck_ref.shape[0]
    lane = lax.broadcasted_iota(jnp.int32, (GRID_W, nkeys + past), 1)
    kc = lane % GRID_W
    qc = lax.broadcasted_iota(jnp.int32, (GRID_W, nkeys + past), 0)
    c_start = jnp.clip(qc - WIN_W // 2, 0, GRID_W - WIN_W)
    visible = ((kc >= c_start) & (kc < c_start + WIN_W)) | (lane >= nkeys)
    no_bias = jnp.zeros((GRID_W, past), F32)
    q = [_split_pair(t) for t in _pair_tiles((q_ref[...] * ATTN_SCALE).astype(BF16))]
    kcat = _pair_tiles(jnp.concatenate([k_ref[pl.ds(row0, nkeys), :], ck_ref[...]], axis=0).astype(BF16))
    vcat = [_with_ones(t) for t in
            _pair_tiles(jnp.concatenate([v_ref[pl.ds(row0, nkeys), :], cv_ref[...]], axis=0).astype(BF16))]
    bias = [jnp.concatenate([tt_ref[h, dr0 + 2 * j] for j in range(WIN_R // 2)] + [no_bias], axis=1)
            for h in range(N_HEADS)]
    e = [[_exp_scores(jnp.where(visible, _bdot_nt(q[j][c], kcat[j]) + bias[2 * j + c], NEG_INF))
          for c in range(2)] for j in range(len(q))]
    out = [[jnp.dot(e[j][c], vcat[j], preferred_element_type=F32) for c in range(2)] for j in range(len(q))]
    for j in range(len(q)):
        o_ref[:, j * PAIR_W:(j + 1) * PAIR_W] = _merge_pair(out[j][0], out[j][1])


def _lat_na_attention(qn, kn, vn, cache_k, cache_v, tt, n_lat):
    n = qn.shape[0]
    nb = n // n_lat
    past = cache_k.shape[1]
    qblk = pl.BlockSpec((GRID_W, D_GROUP), lambda b, r: (b * GRID_ROWS + r, 0))
    kvblk = pl.BlockSpec((n_lat, D_GROUP), lambda b, r: (b, 0))
    cblk = pl.BlockSpec((None, past, D_GROUP), lambda b, r: (b, 0, 0))
    return pl.pallas_call(
        _lat_na_kernel,
        grid=(nb, GRID_ROWS),
        in_specs=[qblk, kvblk, kvblk, cblk, cblk,
                  pl.BlockSpec(tt.shape, lambda b, r: (0, 0, 0, 0))],
        out_specs=qblk,
        out_shape=jax.ShapeDtypeStruct((n, D_GROUP), F32),
        compiler_params=_cparams("arbitrary", "arbitrary"),
        name="lat_na_attention",
    )(qn, kn, vn, cache_k, cache_v, tt)


def _rec_mix(of_ref, ob_ref, rg_ref, yf_ref, yb_ref, xr_ref, xk_ref, xv_ref, lo_ref, rk_ref, lng_ref, lnb_ref,
             gup_ref):
    pw = 2 * HEAD_DIM
    brow = lax.broadcasted_iota(jnp.int32, (pw, pw), 0) // HEAD_DIM
    bcol = lax.broadcasted_iota(jnp.int32, (pw, pw), 1) // HEAD_DIM
    ones_bd = jnp.where(brow == bcol, 1.0, 0.0).astype(BF16)

    def head_sums(x):
        outs = []
        for t in range(D_GROUP // pw):
            xt = x[:, t * pw:(t + 1) * pw]
            hi = xt.astype(BF16)
            lo = (xt - hi.astype(F32)).astype(BF16)
            outs.append(jnp.dot(hi, ones_bd, preferred_element_type=F32)
                        + jnp.dot(lo, ones_bd, preferred_element_type=F32))
        return jnp.concatenate(outs, axis=1)

    inv_d = 1.0 / HEAD_DIM
    o = of_ref[...] + ob_ref[...]
    o_ret = _silu(rg_ref[...]) * (o * lax.rsqrt(head_sums(o * o) * inv_d + RMS_EPS))
    y = yf_ref[...] + yb_ref[...]
    yc = y - head_sums(y) * inv_d
    var = head_sums(yc * yc) * inv_d
    yn = yc * lax.rsqrt(var + RWKV_GN_EPS) * lng_ref[...] + lnb_ref[...]
    bonus = head_sums(xr_ref[...] * rk_ref[...] * xk_ref[...]) * xv_ref[...]
    g_rw = _bdot(_sigmoid(lo_ref[:, 2 * D_LORA:]), gup_ref[...])
    return o_ret, (yn + bonus) * g_rw


def _tail_kernel(*refs, final, ff_chunk, n_mix):
    x_ref, mix_refs = refs[0], refs[1:1 + n_mix]
    wo_ref, gm_ref, g_ref, sh_ref, sc_ref, gf_ref, wi_ref, wf_ref, fg_ref, o_ref = refs[1 + n_mix:]
    if n_mix == 2:
        mix_a, mix_b = mix_refs[0][...], mix_refs[1][...]
    else:
        mix_a, mix_b = _rec_mix(*mix_refs)
    mix = _bdot(mix_a, wo_ref[0:D_GROUP, :]) + _bdot(mix_b, wo_ref[D_GROUP:, :])
    x1 = x_ref[...] + gm_ref[0] * mix
    h = _rms(x1) * g_ref[...]
    hb = (h * (1.0 + sc_ref[0]) + sh_ref[0]).astype(BF16)
    acc = jnp.zeros_like(x1)
    for c0 in range(0, D_FF, ff_chunk):
        gate = jnp.dot(hb, wi_ref[:, c0:c0 + ff_chunk], preferred_element_type=F32)
        up = jnp.dot(hb, wi_ref[:, D_FF + c0:D_FF + c0 + ff_chunk], preferred_element_type=F32)
        acc = acc + jnp.dot((_silu(gate) * up).astype(BF16), wf_ref[c0:c0 + ff_chunk, :],
                            preferred_element_type=F32)
    x2 = x1 + gf_ref[0] * acc
    if final:
        x2 = _rms(x2) * fg_ref[...]
    o_ref[...] = x2


def _layer_tail(x, mix_inputs, w_out, gate_mix, gain, shift, scale, gate_ffn, w_ffn_in_all, w_ffn_out_all, layer,
                final_gain, rows_per_mod, final, ff_chunk=1408):
    n, d = x.shape
    tm = TAIL_TM if len(mix_inputs) == 2 else TAIL_TM_REC
    tiles_per_mod = rows_per_mod // tm
    mod_spec = pl.BlockSpec((1, 1, d), lambda i: (i // tiles_per_mod, 0, 0))
    vec_spec = pl.BlockSpec((1, d), lambda i: (0, 0))
    resident = lambda a: pl.BlockSpec(a.shape, lambda i: (0,) * a.ndim, pipeline_mode=pl.Buffered(1))
    layer_slab = lambda a: pl.BlockSpec((None,) + a.shape[1:], lambda i: (layer, 0, 0),
                                        pipeline_mode=pl.Buffered(1))
    mix_specs = [pl.BlockSpec((tm, a.shape[1]), lambda i: (i, 0)) if a.shape[0] == n else resident(a)
                 for a in mix_inputs]
    return pl.pallas_call(
        functools.partial(_tail_kernel, final=final, ff_chunk=ff_chunk, n_mix=len(mix_inputs)),
        grid=(n // tm,),
        in_specs=[pl.BlockSpec((tm, d), lambda i: (i, 0))] + mix_specs
                 + [resident(w_out), mod_spec, vec_spec, mod_spec, mod_spec, mod_spec,
                    layer_slab(w_ffn_in_all), layer_slab(w_ffn_out_all), vec_spec],
        out_specs=pl.BlockSpec((tm, d), lambda i: (i, 0)),
        out_shape=jax.ShapeDtypeStruct((n, d), F32),
        compiler_params=_cparams("arbitrary"),
        name="layer_tail",
    )(x, *mix_inputs, w_out, gate_mix, gain.reshape(1, d), shift, scale, gate_ffn, w_ffn_in_all, w_ffn_out_all,
      final_gain.reshape(1, d))


def _log_sigmoid(x):
    return jnp.minimum(x, 0.0) - jnp.log(1.0 + jnp.exp(-jnp.abs(x)))


def _pair_block_diag(xp):
    left = lax.broadcasted_iota(jnp.int32, xp.shape, 1) < HEAD_DIM
    zero = jnp.zeros_like(xp)
    return jnp.concatenate([jnp.where(left, xp, zero), jnp.where(left, zero, xp)], axis=0)


def _states_to_pairs(state):
    nb = state.shape[0]
    s = state.reshape(nb, 2, N_HEADS // 2, 2, HEAD_DIM, HEAD_DIM)
    zero = jnp.zeros_like(s[:, :, :, 0])
    return jnp.concatenate([jnp.concatenate([s[:, :, :, 0], zero], axis=-1),
                            jnp.concatenate([zero, s[:, :, :, 1]], axis=-1)], axis=-2)


def _load_states(st, s0_ref):
    st[...] = jnp.zeros(st.shape, st.dtype) if s0_ref is None else s0_ref[0]


def _store_states(sf_ref, st):
    for d in range(st.shape[0]):
        for p in range(st.shape[1]):
            s = st[d, p]
            sf_ref[0, d, 2 * p] = s[:HEAD_DIM, :HEAD_DIM]
            sf_ref[0, d, 2 * p + 1] = s[HEAD_DIM:, HEAD_DIM:]


def _ret_kernel(*refs, zero_init):
    dlm_ref, dlq_ref, qf_ref, kf_ref, vf_ref, qb_ref, kb_ref, vb_ref = refs[:8]
    s0_ref = None if zero_init else refs[8]
    of_ref, ob_ref, sf_ref, st, mask_scr, qdec_scr, kdec_scr, cdec_scr = refs[8 if zero_init else 9:]
    c = pl.program_id(1)
    cs = RET_CHUNK
    pw = 2 * HEAD_DIM
    npair = N_HEADS // 2
    chains = [(d, p) for d in range(2) for p in range(npair)]

    @pl.when(c == 0)
    def _():
        _load_states(st, s0_ref)

    @pl.when((pl.program_id(0) == 0) & (c == 0))
    def _():
        row = lax.broadcasted_iota(jnp.int32, (cs, 2 * cs), 0)
        col = lax.broadcasted_iota(jnp.int32, (cs, 2 * cs), 1) % cs
        tok = lax.broadcasted_iota(jnp.int32, (cs, pw), 0)
        dist = [row - col, col - row]
        tq = [tok.astype(F32), (cs - 1 - tok).astype(F32)]
        for i, (d, p) in enumerate(chains):
            lgm = _log_sigmoid(dlm_ref[d, p])[0:1, :]
            lgq = _log_sigmoid(dlq_ref[d, p])
            mask_scr[i] = jnp.where(dist[d] >= 0, jnp.exp(jnp.maximum(dist[d], 0).astype(F32) * lgm), 0.0)
            qdec_scr[i] = jnp.exp((tq[d] + 1.0) * lgq[0:1, :])
            kdec_scr[i] = jnp.exp((cs - 1.0 - tq[d]) * lgq[0:1, :])
            cdec_scr[i] = jnp.exp(cs * lgq)

    brow = lax.broadcasted_iota(jnp.int32, (pw, pw), 0) // HEAD_DIM
    bcol = lax.broadcasted_iota(jnp.int32, (pw, pw), 1) // HEAD_DIM
    same_head = brow == bcol
    refs = [(qf_ref, kf_ref, vf_ref, of_ref), (qb_ref, kb_ref, vb_ref, ob_ref)]
    nsub = RET_CHUNKS_PER_STEP
    work = [(t, sub) for sub in range(nsub) for t in range(len(chains))]
    sl_of = lambda t: slice(chains[t][1] * pw, (chains[t][1] + 1) * pw)
    rows_of = lambda sub: pl.ds(sub * cs, cs)
    q = {(t, sub): refs[chains[t][0]][0][rows_of(sub), sl_of(t)] for t, sub in work}
    k = {(t, sub): refs[chains[t][0]][1][rows_of(sub), sl_of(t)] * ATTN_SCALE for t, sub in work}
    v = {(t, sub): refs[chains[t][0]][2][rows_of(sub), sl_of(t)] for t, sub in work}
    s = {w: _bdot_nt(q[w], _pair_block_diag(k[w])) * mask_scr[w[0]] for w in work}
    inner = {w: _bdot(s[w], _pair_block_diag(v[w])) for w in work}
    kv = {w: _bdot_tn(k[w] * kdec_scr[w[0]], v[w]) for w in work}
    for step in range(nsub):
        now = [(t, (step if chains[t][0] == 0 else nsub - 1 - step)) for t in range(len(chains))]
        state = {w: st[chains[w[0]]] for w in now}
        cross = {w: _bdot(q[w], state[w]) * qdec_scr[w[0]] for w in now}
        for w in now:
            t, sub = w
            d, p = chains[t]
            refs[d][3][rows_of(sub), sl_of(t)] = inner[w] + cross[w]
            st[d, p] = state[w] * cdec_scr[t][0:1, :] + jnp.where(same_head, kv[w], 0.0)

    @pl.when(c == pl.num_programs(1) - 1)
    def _():
        _store_states(sf_ref, st)


def _retention_scan(decay_logit, q, k, v, state0, seq):
    n = q.shape[0]
    nb = n // seq
    rows = RET_CHUNK * RET_CHUNKS_PER_STEP
    nc = seq // rows
    npair, pw = N_HEADS // 2, 2 * HEAD_DIM
    dl = decay_logit.astype(F32)
    dl_m = jnp.broadcast_to(jnp.repeat(dl, RET_CHUNK, axis=1).reshape(2, npair, 1, 2 * RET_CHUNK),
                            (2, npair, 8, 2 * RET_CHUNK))
    dl_q = jnp.broadcast_to(jnp.repeat(dl, HEAD_DIM, axis=1).reshape(2, npair, 1, pw), (2, npair, 8, pw))
    blk_f = pl.BlockSpec((rows, D_GROUP), lambda b, c: (b * nc + c, 0))
    blk_b = pl.BlockSpec((rows, D_GROUP), lambda b, c: (b * nc + nc - 1 - c, 0))
    st_in = ([] if state0 is None else
             [pl.BlockSpec((1, 2, npair, pw, pw), lambda b, c: (b, 0, 0, 0, 0))])
    st_out = pl.BlockSpec((1, 2, N_HEADS, HEAD_DIM, HEAD_DIM), lambda b, c: (b, 0, 0, 0, 0))
    return pl.pallas_call(
        functools.partial(_ret_kernel, zero_init=state0 is None),
        grid=(nb, nc),
        in_specs=[pl.BlockSpec(dl_m.shape, lambda b, c: (0, 0, 0, 0)),
                  pl.BlockSpec(dl_q.shape, lambda b, c: (0, 0, 0, 0)),
                  blk_f, blk_f, blk_f, blk_b, blk_b, blk_b] + st_in,
        out_specs=[blk_f, blk_b, st_out],
        out_shape=[jax.ShapeDtypeStruct((n, D_GROUP), F32), jax.ShapeDtypeStruct((n, D_GROUP), F32),
                   jax.ShapeDtypeStruct((nb, 2, N_HEADS, HEAD_DIM, HEAD_DIM), F32)],
        scratch_shapes=[pltpu.VMEM((2, npair, pw, pw), F32),
                        pltpu.VMEM((2 * npair, RET_CHUNK, 2 * RET_CHUNK), F32),
                        pltpu.VMEM((2 * npair, RET_CHUNK, pw), F32),
                        pltpu.VMEM((2 * npair, RET_CHUNK, pw), F32),
                        pltpu.VMEM((2 * npair, 8, pw), F32)],
        compiler_params=_cparams("arbitrary", "arbitrary"),
        name="retention_scan",
    )(dl_m, dl_q, q, k, v, q, k, v, *([] if state0 is None else [_states_to_pairs(state0)]))


def _rwkv_kernel(*refs, zero_init):
    (xrf_ref, xkf_ref, xvf_ref, lof_ref, xrb_ref, xkb_ref, xvb_ref, lob_ref,
     w0_ref, wup_ref, a0_ref, aup_ref, kk_ref, ka_ref) = refs[:14]
    s0_ref = None if zero_init else refs[14]
    yf_ref, yb_ref, sf_ref, st = refs[14 if zero_init else 15:]
    c = pl.program_id(1)
    cs = RWKV_CHUNK
    nsub = RWKV_CHUNKS_PER_STEP
    npair = N_HEADS // 2

    @pl.when(c == 0)
    def _():
        _load_states(st, s0_ref)

    row = lax.broadcasted_iota(jnp.int32, (cs, cs), 0)
    col = lax.broadcasted_iota(jnp.int32, (cs, cs), 1)
    pw = 2 * HEAD_DIM
    prow = lax.broadcasted_iota(jnp.int32, (cs, pw), 0)
    pcol = lax.broadcasted_iota(jnp.int32, (cs, pw), 1) % HEAD_DIM
    eye = jnp.where(pcol == prow, 1.0, 0.0).astype(F32)
    sub_diag = (prow // RWKV_SUB) == (pcol // RWKV_SUB)
    brow = lax.broadcasted_iota(jnp.int32, (pw, pw), 0) // HEAD_DIM
    bcol = lax.broadcasted_iota(jnp.int32, (pw, pw), 1) // HEAD_DIM
    same_head = brow == bcol
    ones_bd = jnp.where(same_head, 1.0, 0.0).astype(BF16)

    bd = _pair_block_diag

    def direction(d, sub, xr_ref, xk_ref, xv_ref, lo_ref):
        rows = pl.ds(sub * cs, cs)
        tri = jnp.where((row >= col) if d == 0 else (col >= row), 1.0, 0.0).astype(BF16)
        dist = (prow - pcol) if d == 0 else (pcol - prow)
        xk = xk_ref[rows, :]
        lo = lo_ref[rows, :]
        w_log = -math.exp(-0.5) * _sigmoid(w0_ref[d] + _bdot(jnp.tanh(lo[:, 0:D_LORA]), wup_ref[d]))
        a_all = _sigmoid(a0_ref[d] + _bdot(lo[:, D_LORA:2 * D_LORA], aup_ref[d]))
        w1 = w_log.astype(BF16)
        r1 = w_log - w1.astype(F32)
        w2 = r1.astype(BF16)
        w3 = (r1 - w2.astype(F32)).astype(BF16)
        cum = (jnp.dot(tri, w1, preferred_element_type=F32) + jnp.dot(tri, w2, preferred_element_type=F32)
               + jnp.dot(tri, w3, preferred_element_type=F32))
        cum_end = cum[cs - 1:cs, :] if d == 0 else cum[0:1, :]
        return dict(earlier=dist > 0, upto=dist >= 0, xr=xr_ref[rows, :], xv=xv_ref[rows, :], a_all=a_all,
                    e_incl=jnp.exp(cum), e_excl=jnp.exp(cum - w_log), e_neg=jnp.exp(-cum),
                    e_end=jnp.exp(cum_end - cum), w_end=jnp.exp(cum_end), kk_all=xk * kk_ref[...],
                    keff_all=xk * (1.0 + (a_all - 1.0) * ka_ref[...]))

    in_refs = [(xrf_ref, xkf_ref, xvf_ref, lof_ref), (xrb_ref, xkb_ref, xvb_ref, lob_ref)]
    dirs = {(d, sub): direction(d, sub, *in_refs[d]) for d in range(2) for sub in range(nsub)}
    y_refs = [yf_ref, yb_ref]

    chains = [(d, sub, p) for d in range(2) for sub in range(nsub) for p in range(npair)]
    pairs = range(len(chains))
    sls = [slice(p * pw, (p + 1) * pw) for _, _, p in chains]
    pick = lambda name: [dirs[d, sub][name][:, sls[i]] for i, (d, sub, _) in enumerate(chains)]
    earlier = [dirs[d, sub]["earlier"] for d, sub, _ in chains]
    upto = [dirs[d, sub]["upto"] for d, sub, _ in chains]
    cat = lambda x, y: jnp.concatenate([x, y], axis=0)

    def head_sum(x):
        x1 = x.astype(BF16)
        r1 = x - x1.astype(F32)
        x2 = r1.astype(BF16)
        x3 = (r1 - x2.astype(F32)).astype(BF16)
        f = functools.partial(jnp.dot, preferred_element_type=F32)
        return f(jnp.concatenate([x1, x2], axis=1), jnp.concatenate([ones_bd, ones_bd], axis=0)) + f(x3, ones_bd)

    e_incl, e_excl, e_neg, e_end, w_end = (pick(k) for k in ("e_incl", "e_excl", "e_neg", "e_end", "w_end"))
    kk = [x * lax.rsqrt(head_sum(x * x) + 1e-12) for x in pick("kk_all")]
    b = [kk[p] * a for p, a in zip(pairs, pick("a_all"))]
    keff = pick("keff_all")
    v = pick("xv")
    a_t = [-kk[p] * e_excl[p] for p in pairs]
    r_t = [xr * e_incl[p] for p, xr in zip(pairs, pick("xr"))]
    ar = [cat(a_t[p], r_t[p]).astype(BF16) for p in pairs]
    g = [_bdot_nt(ar[p], cat(bd(b[p] * e_neg[p]), bd(keff[p] * e_neg[p]))) for p in pairs]
    n_mat = [jnp.where(earlier[p], g[p][:cs, :pw], 0.0) for p in pairs]
    m_rb = [jnp.where(upto[p], g[p][cs:, :pw], 0.0) for p in pairs]
    a_ak = [jnp.where(earlier[p], g[p][:cs, pw:], 0.0) for p in pairs]
    m_rk = [jnp.where(upto[p], g[p][cs:, pw:], 0.0) for p in pairs]
    bd_s = lambda xs: (bd(xs[0]), bd(xs[1]))
    n_d = [jnp.where(sub_diag, n, 0.0) for n in n_mat]
    n_o = [jnp.where(sub_diag, 0.0, n) for n in n_mat]
    n_s = [_split2(n) for n in n_d]
    x = [eye + n for n in n_d]
    pk = [_mm3s(ns, bd_s(ns)) for ns in n_s]
    for _ in range(RWKV_SUB_DOUBLINGS - 1):
        xp = [_mm3s(_split2(cat(x[p], pk[p])), bd_s(_split2(pk[p]))) for p in pairs]
        x = [x[p] + xp[p][:cs] for p in pairs]
        pk = [r[cs:] for r in xp]
    t_d = [x[p] + _mm3s(_split2(x[p]), bd_s(_split2(pk[p]))) for p in pairs]
    lcat = lambda x, y: jnp.concatenate([x, y], axis=1)
    m_o = [_bdot(t_d[p], bd(n_o[p])) for p in pairs]
    mz = [_bdot(m_o[p], lcat(bd(t_d[p]), bd(m_o[p]))) for p in pairs]
    z = [t_d[p] + mz[p][:, :pw] for p in pairs]
    t_inv = [(z[p] + _bdot(mz[p][:, pw:], bd(z[p]))).astype(BF16) for p in pairs]
    av = [_bdot(cat(a_ak[p], m_rk[p]), bd(v[p])) for p in pairs]
    tau = [_bdot(t_inv[p], lcat(bd(a_t[p]), bd(av[p][:cs]))) for p in pairs]
    bk_end = [cat(b[p] * e_end[p], keff[p] * e_end[p]) for p in pairs]
    for step in range(nsub):
        now = [i for i, (d, sub, _) in enumerate(chains) if sub == (step if d == 0 else nsub - 1 - step)]
        s_prev = {i: st[chains[i][0], chains[i][2]] for i in now}
        hs = {i: _bdot_nt(cat(tau[i][:, :pw], r_t[i]), s_prev[i]) for i in now}
        u = {i: hs[i][:cs] + tau[i][:, pw:] for i in now}
        y = {i: hs[i][cs:] + _bdot(m_rb[i], bd(u[i])) + av[i][cs:] for i in now}
        for i in now:
            d, sub, p = chains[i]
            y_refs[d][pl.ds(sub * cs, cs), sls[i]] = y[i]
            st[d, p] = s_prev[i] * w_end[i] + jnp.where(same_head, _bdot_tn(cat(u[i], v[i]), bk_end[i]), 0.0)

    @pl.when(c == pl.num_programs(1) - 1)
    def _():
        _store_states(sf_ref, st)


def _rwkv_scan(xr, xk, xv, lora, w0, w_up, a0, a_up, k_k, k_a, state0, seq):
    n = xr.shape[0]
    nb = n // seq
    blk = RWKV_CHUNK * RWKV_CHUNKS_PER_STEP
    nc = seq // blk

    fwd_idx = lambda b, c: (b * nc + c, 0)
    bwd_idx = lambda b, c: (b * nc + nc - 1 - c, 0)
    blk_f = pl.BlockSpec((blk, D_GROUP), fwd_idx)
    blk_b = pl.BlockSpec((blk, D_GROUP), bwd_idx)
    lblk_f = pl.BlockSpec((blk, lora.shape[1]), fwd_idx)
    lblk_b = pl.BlockSpec((blk, lora.shape[1]), bwd_idx)
    dvec = pl.BlockSpec((2, 1, D_GROUP), lambda b, c: (0, 0, 0))
    dmat = pl.BlockSpec((2, D_LORA, D_GROUP), lambda b, c: (0, 0, 0))
    vec = pl.BlockSpec((1, D_GROUP), lambda b, c: (0, 0))
    npair, pw = N_HEADS // 2, 2 * HEAD_DIM
    st_in = ([] if state0 is None else
             [pl.BlockSpec((1, 2, npair, pw, pw), lambda b, c: (b, 0, 0, 0, 0))])
    st_out = pl.BlockSpec((1, 2, N_HEADS, HEAD_DIM, HEAD_DIM), lambda b, c: (b, 0, 0, 0, 0))
    return pl.pallas_call(
        functools.partial(_rwkv_kernel, zero_init=state0 is None),
        grid=(nb, nc),
        in_specs=[blk_f, blk_f, blk_f, lblk_f, blk_b, blk_b, blk_b, lblk_b,
                  dvec, dmat, dvec, dmat, vec, vec] + st_in,
        out_specs=[blk_f, blk_b, st_out],
        out_shape=[jax.ShapeDtypeStruct((n, D_GROUP), F32), jax.ShapeDtypeStruct((n, D_GROUP), F32),
                   jax.ShapeDtypeStruct((nb, 2, N_HEADS, HEAD_DIM, HEAD_DIM), F32)],
        scratch_shapes=[pltpu.VMEM((2, npair, pw, pw), F32)],
        compiler_params=_cparams("arbitrary", "arbitrary"),
        name="rwkv7_scan",
    )(xr, xk, xv, lora, xr, xk, xv, lora, w0.reshape(2, 1, D_GROUP), w_up, a0.reshape(2, 1, D_GROUP), a_up,
      k_k.reshape(1, D_GROUP), k_a.reshape(1, D_GROUP), *([] if state0 is None else [_states_to_pairs(state0)]))


ATTN_WIDTHS = (D_GROUP,) * 6
REC_WIDTHS = (D_GROUP,) * 7 + (2 * D_LORA + D_LORA_G,)


def kernel(x_prompt, x_sample, cache_na_k, cache_na_v, cache_diff_k, cache_diff_v, state_ret, state_rwkv, c, c_ctx, norm_mix_g, norm_ffn_g, norm_final_g, w_ada, b_ada, w_in_attn, w_out_attn, na_rpb, diff_lq1, diff_lk1, diff_lq2, diff_lk2, w_in_rec, w_out_rec, ret_decay_logit, rw_w0, rw_w_up, rw_a0, rw_a_up, rw_g_up, rw_k_k, rw_k_a, rw_r_k, rw_ln_g, rw_ln_b, w_ffn_in, w_ffn_out):
    bp, seq, d = x_prompt.shape
    bl, n_lat, _ = x_sample.shape
    depth = w_ada.shape[0]
    ctx = x_prompt.reshape(bp * seq, d)
    lat = x_sample.reshape(bl * n_lat, d)

    cond8 = jnp.zeros((8, d), F32).at[0].set(c_ctx).at[1:1 + bl].set(c)
    mods = _ada_modulation(cond8, w_ada, b_ada).reshape(depth, 8, 6, d)
    w_ffn_in_bf16 = w_ffn_in.astype(BF16)
    w_ffn_out_bf16 = w_ffn_out.astype(BF16)

    outs = {}
    for layer in range(depth):
        m_ctx = [mods[layer, 0:1, j].reshape(1, 1, d) for j in range(6)]
        m_lat = [mods[layer, 1:1 + bl, j].reshape(bl, 1, d) for j in range(6)]
        if layer % 2 == 0:
            i = layer // 2
            lam_init = 0.8 - 0.6 * math.exp(-0.3 * layer)
            lam_params = jnp.stack([diff_lq1[i], diff_lk1[i], diff_lq2[i], diff_lk2[i]]).astype(F32)
            w_in = w_in_attn[i].astype(BF16)
            w_out = w_out_attn[i].astype(BF16)
            cache_shapes = {1: (N_HEADS, HEAD_DIM), 2: (N_HEADS, HEAD_DIM),
                            4: (N_HEADS // 2, 2, HEAD_DIM), 5: (N_HEADS // 2, 2 * HEAD_DIM)}
            pc = _norm_mod_matmul(ctx, norm_mix_g[layer], m_ctx[0], m_ctx[1], w_in, ATTN_WIDTHS, bp * seq,
                                  PROJ_TM_CACHE, cache_shapes)
            outs["na_k"], outs["na_v"], outs["df_k"], outs["df_v"] = pc[6:]
            pc = pc[:6]
            plat = _norm_mod_matmul(lat, norm_mix_g[layer], m_lat[0], m_lat[1], w_in, ATTN_WIDTHS, n_lat, PROJ_TM)
            mix_ctx = _ctx_attention(lam_params, *pc, seq, lam_init)
            past = cache_na_k.shape[2]
            tt = _na_bias_table(na_rpb[i])
            o_na = _lat_na_attention(plat[0], plat[1], plat[2],
                                     cache_na_k[:, i].reshape(bl, past, D_GROUP),
                                     cache_na_v[:, i].reshape(bl, past, D_GROUP), tt, n_lat)
            o_df = _lat_diff_attention(lam_params, plat[3], plat[4], plat[5],
                                       cache_diff_k[:, i].reshape(bl, past, D_GROUP),
                                       cache_diff_v[:, i].reshape(bl, past, D_GROUP), n_lat, lam_init)
            mix_lat = (o_na, o_df)
        else:
            j = layer // 2
            w_in = w_in_rec[j].astype(BF16)
            w_out = w_out_rec[j].astype(BF16)
            pc = _norm_mod_matmul(ctx, norm_mix_g[layer], m_ctx[0], m_ctx[1], w_in, REC_WIDTHS, bp * seq, PROJ_TM)
            plat = _norm_mod_matmul(lat, norm_mix_g[layer], m_lat[0], m_lat[1], w_in, REC_WIDTHS, n_lat, PROJ_TM)
            mixes = []
            for is_ctx, p, s_ret0, s_rw0, sq in ((True, pc, None, None, seq),
                                                 (False, plat, state_ret[:, j], state_rwkv[:, j], n_lat)):
                rq, rk, rv, rg, wr, wk, wv, lora = p
                o_f, o_b, s_ret = _retention_scan(ret_decay_logit[j], rq, rk, rv, s_ret0, sq)
                y_f, y_b, s_rw = _rwkv_scan(wr, wk, wv, lora, rw_w0[j], rw_w_up[j].astype(BF16), rw_a0[j],
                                            rw_a_up[j].astype(BF16), rw_k_k[j], rw_k_a[j], s_rw0, sq)
                mixes.append((o_f, o_b, rg, y_f, y_b, wr, wk, wv, lora, rw_r_k[j].reshape(1, D_GROUP),
                              rw_ln_g[j].reshape(1, D_GROUP), rw_ln_b[j].reshape(1, D_GROUP),
                              rw_g_up[j].astype(BF16)))
                if is_ctx:
                    outs["ret"], outs["rwkv"] = s_ret, s_rw
            mix_ctx, mix_lat = mixes
        final = layer == depth - 1
        ctx = _layer_tail(ctx, mix_ctx, w_out, m_ctx[2], norm_ffn_g[layer], m_ctx[3], m_ctx[4], m_ctx[5],
                          w_ffn_in_bf16, w_ffn_out_bf16, layer, norm_final_g, bp * seq, final)
        lat = _layer_tail(lat, mix_lat, w_out, m_lat[2], norm_ffn_g[layer], m_lat[3], m_lat[4], m_lat[5],
                          w_ffn_in_bf16, w_ffn_out_bf16, layer, norm_final_g, n_lat, final)

    y_prompt = ctx.reshape(bp, seq, d)
    y_sample = lat.reshape(bl, n_lat, d)
    new_cache_na_k = outs["na_k"].reshape(bp, 1, seq, N_HEADS, HEAD_DIM)
    new_cache_na_v = outs["na_v"].reshape(bp, 1, seq, N_HEADS, HEAD_DIM)
    new_cache_diff_k = outs["df_k"].reshape(bp, 1, seq, N_HEADS // 2, 2, HEAD_DIM)
    new_cache_diff_v = outs["df_v"].reshape(bp, 1, seq, N_HEADS // 2, 2 * HEAD_DIM)
    new_state_ret = outs["ret"].reshape(bp, 1, 2, N_HEADS, HEAD_DIM, HEAD_DIM)
    new_state_rwkv = outs["rwkv"].reshape(bp, 1, 2, N_HEADS, HEAD_DIM, HEAD_DIM)
    return (y_prompt, y_sample, new_cache_na_k, new_cache_na_v, new_cache_diff_k, new_cache_diff_v,
            new_state_ret, new_state_rwkv)
```

```python
import functools
import math

import jax
import jax.numpy as jnp
from jax import lax
from jax.experimental import pallas as pl
from jax.experimental.pallas import tpu as pltpu

F32 = jnp.float32
BF16 = jnp.bfloat16

D_MODEL = 1024
HEAD_DIM = 64
N_HEADS = 8
D_GROUP = 512
GRID_W = 64
GRID_ROWS = 32
WIN_R = 8
WIN_W = 16
RPB_R = 15
RPB_C = 31
PAST_LEN = 256
D_FF = 2816
D_LORA = 64
D_LORA_G = 128
ROPE_BASE = 10000.0
RMS_EPS = 1e-6
RWKV_GN_EPS = 64e-5
NEG_INF = -1e30
ATTN_SCALE = HEAD_DIM ** -0.5
RET_CHUNK = 128
RET_CHUNKS_PER_STEP = 2
RWKV_CHUNK = 64
RWKV_CHUNKS_PER_STEP = 4
RWKV_SUB = 16
RWKV_SUB_DOUBLINGS = 3
VMEM_LIMIT = 56 * 1024 * 1024
TAIL_TM = 512
TAIL_TM_REC = 256
PROJ_TM = 512


def _cparams(*sem):
    return pltpu.CompilerParams(dimension_semantics=sem, vmem_limit_bytes=VMEM_LIMIT)


def _sigmoid(x):
    return 1.0 / (1.0 + jnp.exp(-x))


def _silu(x):
    return x * _sigmoid(x)


def _rms(x, eps=RMS_EPS):
    return x * lax.rsqrt(jnp.mean(x * x, axis=-1, keepdims=True) + eps)


def _bdot(a, b):
    return jnp.dot(a.astype(BF16), b.astype(BF16), preferred_element_type=F32)


def _bdot_nt(a, b):
    return lax.dot_general(a.astype(BF16), b.astype(BF16), (((1,), (1,)), ((), ())),
                           preferred_element_type=F32)


def _bdot_tn(a, b):
    return lax.dot_general(a.astype(BF16), b.astype(BF16), (((0,), (0,)), ((), ())),
                           preferred_element_type=F32)


_DN = {"nn": (((1,), (0,)), ((), ())), "nt": (((1,), (1,)), ((), ())), "tn": (((0,), (0,)), ((), ()))}


def _split2(x):
    hi = x.astype(BF16)
    lo = (x - hi.astype(F32)).astype(BF16)
    return hi, lo


def _mm3s(a_split, b_split):
    ah, al = a_split
    bh, bl = b_split
    d = functools.partial(jnp.dot, preferred_element_type=F32)
    return d(jnp.concatenate([ah, al], axis=1), jnp.concatenate([bh, bh], axis=0)) + d(ah, bl)


def _diff_lambda(lp, lam_init):
    s1 = jnp.sum(lp[0:1, :] * lp[1:2, :], axis=-1, keepdims=True)
    s2 = jnp.sum(lp[2:3, :] * lp[3:4, :], axis=-1, keepdims=True)
    return jnp.exp(s1) - jnp.exp(s2) + lam_init


def _ada_kernel(c_ref, w_ref, b_ref, o_ref):
    s = _silu(c_ref[...])
    o_ref[0] = _bdot(s, w_ref[0]) + b_ref[0]


def _ada_modulation(cond8, w_ada, b_ada):
    depth, d, n = w_ada.shape
    tn = 1536
    return pl.pallas_call(
        _ada_kernel,
        grid=(depth, n // tn),
        in_specs=[pl.BlockSpec((8, d), lambda l, j: (0, 0)),
                  pl.BlockSpec((1, d, tn), lambda l, j: (l, 0, j)),
                  pl.BlockSpec((1, 1, tn), lambda l, j: (l, 0, j))],
        out_specs=pl.BlockSpec((1, 8, tn), lambda l, j: (l, 0, j)),
        out_shape=jax.ShapeDtypeStruct((depth, 8, n), F32),
        compiler_params=_cparams("arbitrary", "arbitrary"),
        name="ada_modulation",
    )(cond8, w_ada, b_ada.reshape(depth, 1, n))


def _norm_mod(x_ref, g_ref, sh_ref, sc_ref):
    h = _rms(x_ref[...]) * g_ref[...]
    return (h * (1.0 + sc_ref[0]) + sh_ref[0]).astype(BF16)


def _nmm_kernel(x_ref, g_ref, sh_ref, sc_ref, w_ref, *o_refs, widths):
    hb = _norm_mod(x_ref, g_ref, sh_ref, sc_ref)
    off = 0
    for o_ref, w in zip(o_refs, widths):
        o_ref[...] = jnp.dot(hb, w_ref[:, off:off + w], preferred_element_type=F32)
        off += w


def _norm_mod_matmul(x, gain, shift, scale, w_bf16, widths, rows_per_mod, tm):
    n, d = x.shape
    tiles_per_mod = rows_per_mod // tm
    mod_spec = pl.BlockSpec((1, 1, d), lambda i: (i // tiles_per_mod, 0, 0))
    return pl.pallas_call(
        functools.partial(_nmm_kernel, widths=widths),
        grid=(n // tm,),
        in_specs=[pl.BlockSpec((tm, d), lambda i: (i, 0)),
                  pl.BlockSpec((1, d), lambda i: (0, 0)),
                  mod_spec, mod_spec,
                  pl.BlockSpec(w_bf16.shape, lambda i: (0, 0), pipeline_mode=pl.Buffered(1))],
        out_specs=[pl.BlockSpec((tm, w), lambda i: (i, 0)) for w in widths],
        out_shape=[jax.ShapeDtypeStruct((n, w), F32) for w in widths],
        compiler_params=_cparams("arbitrary"),
        name="norm_mod_matmul",
    )(x, gain.reshape(1, d), shift, scale, w_bf16)


PAIR_W = 2 * HEAD_DIM


def _pair_tiles(x):
    return [x[:, j * PAIR_W:(j + 1) * PAIR_W] for j in range(x.shape[1] // PAIR_W)]


def _split_pair(tile):
    left = lax.broadcasted_iota(jnp.int32, tile.shape, 1) < HEAD_DIM
    zero = jnp.zeros_like(tile)
    return jnp.where(left, tile, zero), jnp.where(left, zero, tile)


def _with_ones(v):
    return jnp.concatenate([v, jnp.ones_like(v)], axis=1)


def _exp_scores(s):
    return jnp.exp(s - jnp.max(s, axis=-1, keepdims=True)).astype(BF16)


def _merge_pair(out_even, out_odd):
    left = lax.broadcasted_iota(jnp.int32, (out_even.shape[0], PAIR_W), 1) < HEAD_DIM
    return jnp.where(left, out_even[:, :PAIR_W] * (1.0 / out_even[:, PAIR_W:]),
                     out_odd[:, :PAIR_W] * (1.0 / out_odd[:, PAIR_W:]))


def _diff_combine(out1, out2, lam, lam_init):
    o = out1[:, :PAIR_W] * (1.0 / out1[:, PAIR_W:]) - out2[:, :PAIR_W] * (lam / out2[:, PAIR_W:])
    return _rms(o) * (1.0 - lam_init)


CTX_CACHE_SHAPES = ((N_HEADS, HEAD_DIM), (N_HEADS, HEAD_DIM),
                    (N_HEADS // 2, 2, HEAD_DIM), (N_HEADS // 2, 2 * HEAD_DIM))


def _ctx_attn_kernel(x_ref, g_ref, sh_ref, sc_ref, w_ref, lp_ref, on_ref, od_ref, *cache_refs, lam_init):
    hb = _norm_mod(x_ref, g_ref, sh_ref, sc_ref)
    qn, kn, vn, qd, kd, vd = (jnp.dot(hb, w_ref[:, i * D_GROUP:(i + 1) * D_GROUP], preferred_element_type=F32)
                              for i in range(6))
    for c_ref, val in zip(cache_refs, (kn, vn, kd, vd)):
        c_ref[...] = val.reshape(c_ref.shape)
    tiles = range(D_GROUP // PAIR_W)
    lam = _diff_lambda(lp_ref[...], lam_init)
    q = [_split_pair(t) for t in _pair_tiles((qn * ATTN_SCALE).astype(BF16))
         + _pair_tiles((qd * ATTN_SCALE).astype(BF16))]
    k = _pair_tiles(kn.astype(BF16)) + _pair_tiles(kd.astype(BF16))
    v = [_with_ones(t) for t in _pair_tiles(vn.astype(BF16)) + _pair_tiles(vd.astype(BF16))]
    e = [[_exp_scores(_bdot_nt(q[j][c], k[j])) for c in range(2)] for j in range(2 * len(tiles))]
    out = [[jnp.dot(e[j][c], v[j], preferred_element_type=F32) for c in range(2)] for j in range(2 * len(tiles))]
    for j in tiles:
        sl = slice(j * PAIR_W, (j + 1) * PAIR_W)
        on_ref[:, sl] = _merge_pair(out[j][0], out[j][1])
        od_ref[:, sl] = _diff_combine(out[len(tiles) + j][0], out[len(tiles) + j][1], lam, lam_init)


def _ctx_attention(x, gain, shift, scale, w_bf16, lam_params, seq, lam_init):
    n, d = x.shape
    blk = pl.BlockSpec((seq, D_GROUP), lambda b: (b, 0))
    mod_spec = pl.BlockSpec((1, 1, d), lambda b: (0, 0, 0))
    cache_specs = [pl.BlockSpec((seq,) + dims, lambda b, nd=len(dims): (b,) + (0,) * nd)
                   for dims in CTX_CACHE_SHAPES]
    return pl.pallas_call(
        functools.partial(_ctx_attn_kernel, lam_init=lam_init),
        grid=(n // seq,),
        in_specs=[pl.BlockSpec((seq, d), lambda b: (b, 0)),
                  pl.BlockSpec((1, d), lambda b: (0, 0)),
                  mod_spec, mod_spec,
                  pl.BlockSpec(w_bf16.shape, lambda b: (0, 0), pipeline_mode=pl.Buffered(1)),
                  pl.BlockSpec(lam_params.shape, lambda b: (0, 0))],
        out_specs=[blk, blk] + cache_specs,
        out_shape=([jax.ShapeDtypeStruct((n, D_GROUP), F32)] * 2
                   + [jax.ShapeDtypeStruct((n,) + dims, F32) for dims in CTX_CACHE_SHAPES]),
        compiler_params=_cparams("arbitrary"),
        name="ctx_attention",
    )(x, gain.reshape(1, d), shift, scale, w_bf16, lam_params)


def _rope_tables(n):
    quarter = HEAD_DIM // 4
    pos = jnp.arange(n)
    inv_freq = ROPE_BASE ** (-jnp.arange(quarter, dtype=F32) / quarter)
    lane = jnp.arange(HEAD_DIM)
    p = jnp.where(lane[None, :] < HEAD_DIM // 2, (pos // GRID_W)[:, None], (pos % GRID_W)[:, None]).astype(F32)
    ang = p * inv_freq[lane % quarter][None, :]
    sign = jnp.where((lane % (2 * quarter)) < quarter, -1.0, 1.0).astype(F32)
    cos = jnp.tile(jnp.cos(ang), (1, N_HEADS))
    sin = jnp.tile(jnp.sin(ang) * sign[None, :], (1, N_HEADS))
    return cos, sin


def _rope(x, cos, sin_signed):
    quarter = HEAD_DIM // 4
    width = x.shape[-1]
    lane = lax.broadcasted_iota(jnp.int32, x.shape, 1)
    first = (lane % (2 * quarter)) < quarter
    partner = jnp.where(first, pltpu.roll(x, width - quarter, 1), pltpu.roll(x, quarter, 1))
    return x * cos + partner * sin_signed


def _lat_diff_kernel(lp_ref, q_ref, k_ref, v_ref, ck_ref, cv_ref, cosq_ref, sinq_ref, cosk_ref, sink_ref,
                     o_ref, kall, vall, *, lam_init, n_lat):
    @pl.when(pl.program_id(1) == 0)
    def _():
        kall[0:n_lat, :] = _rope(k_ref[...], cosk_ref[...], sink_ref[...]).astype(BF16)
        kall[n_lat:, :] = ck_ref[...].astype(BF16)
        for j, (vt, ct) in enumerate(zip(_pair_tiles(v_ref[...].astype(BF16)),
                                         _pair_tiles(cv_ref[...].astype(BF16)))):
            vall[j, 0:n_lat, :] = _with_ones(vt)
            vall[j, n_lat:, :] = _with_ones(ct)

    lam = _diff_lambda(lp_ref[...], lam_init)
    q = (_rope(q_ref[...], cosq_ref[...], sinq_ref[...]) * ATTN_SCALE).astype(BF16)
    for j, qt in enumerate(_pair_tiles(q)):
        sl = slice(j * PAIR_W, (j + 1) * PAIR_W)
        out = [jnp.dot(_exp_scores(_bdot_nt(qc, kall[:, sl])), vall[j], preferred_element_type=F32)
               for qc in _split_pair(qt)]
        o_ref[:, sl] = _diff_combine(out[0], out[1], lam, lam_init)


def _lat_diff_attention(lam_params, qd, kd, vd, cache_k, cache_v, n_lat, lam_init, tq=256):
    n = qd.shape[0]
    nb = n // n_lat
    nq = n_lat // tq
    past = cache_k.shape[1]
    cos, sin = _rope_tables(n_lat)
    qblk = pl.BlockSpec((tq, D_GROUP), lambda b, i: (b * nq + i, 0))
    once = pl.Buffered(1)
    kvblk = pl.BlockSpec((n_lat, D_GROUP), lambda b, i: (b, 0), pipeline_mode=once)
    cblk = pl.BlockSpec((None, past, D_GROUP), lambda b, i: (b, 0, 0))
    return pl.pallas_call(
        functools.partial(_lat_diff_kernel, lam_init=lam_init, n_lat=n_lat),
        grid=(nb, nq),
        in_specs=[pl.BlockSpec(lam_params.shape, lambda b, i: (0, 0)),
                  qblk, kvblk, kvblk, cblk, cblk,
                  pl.BlockSpec((tq, D_GROUP), lambda b, i: (i, 0)),
                  pl.BlockSpec((tq, D_GROUP), lambda b, i: (i, 0)),
                  pl.BlockSpec((n_lat, D_GROUP), lambda b, i: (0, 0), pipeline_mode=once),
                  pl.BlockSpec((n_lat, D_GROUP), lambda b, i: (0, 0), pipeline_mode=once)],
        out_specs=qblk,
        out_shape=jax.ShapeDtypeStruct((n, D_GROUP), F32),
        scratch_shapes=[pltpu.VMEM((n_lat + past, D_GROUP), BF16),
                        pltpu.VMEM((D_GROUP // PAIR_W, n_lat + past, 2 * PAIR_W), BF16)],
        compiler_params=_cparams("arbitrary", "arbitrary"),
        name="lat_diff_attention",
    )(lam_params, qd, kd, vd, cache_k, cache_v, cos, sin, cos, sin)


def _na_bias_kernel(rpb_ref, tt_ref):
    h = pl.program_id(0)
    lane = lax.broadcasted_iota(jnp.int32, (GRID_W, 2 * GRID_W), 1)
    qc = lax.broadcasted_iota(jnp.int32, (GRID_W, 2 * GRID_W), 0)
    dc = jnp.clip(lane % GRID_W - qc + (WIN_W - 1), 0, RPB_C - 1)
    first = lane < GRID_W
    for dr in range(RPB_R - 1):
        acc = jnp.zeros((GRID_W, 2 * GRID_W), F32)
        for d in range(RPB_C):
            val = jnp.where(first, rpb_ref[h * RPB_R + dr, d], rpb_ref[h * RPB_R + dr + 1, d])
            acc = jnp.where(dc == d, val, acc)
        tt_ref[0, dr] = acc


def _na_bias_table(rpb):
    return pl.pallas_call(
        _na_bias_kernel,
        grid=(N_HEADS,),
        in_specs=[pl.BlockSpec(memory_space=pltpu.SMEM)],
        out_specs=pl.BlockSpec((1, RPB_R - 1, GRID_W, 2 * GRID_W), lambda h: (h, 0, 0, 0)),
        out_shape=jax.ShapeDtypeStruct((N_HEADS, RPB_R - 1, GRID_W, 2 * GRID_W), F32),
        compiler_params=_cparams("arbitrary"),
        name="na_bias_table",
    )(rpb.reshape(N_HEADS * RPB_R, RPB_C))


def _lat_na_kernel(q_ref, k_ref, v_ref, ck_ref, cv_ref, tt_ref, o_ref):
    r = pl.program_id(1)
    r_start = jnp.clip(r - WIN_R // 2, 0, GRID_ROWS - WIN_R)
    dr0 = r_start - r + (WIN_R - 1)
    row0 = pl.multiple_of(r_start * GRID_W, GRID_W)
    nkeys = WIN_R * GRID_W
    past = ck_ref.shape[0]
    lane = lax.broadcasted_iota(jnp.int32, (GRID_W, nkeys + past), 1)
    kc = lane % GRID_W
    qc = lax.broadcasted_iota(jnp.int32, (GRID_W, nkeys + past), 0)
    c_start = jnp.clip(qc - WIN_W // 2, 0, GRID_W - WIN_W)
    visible = ((kc >= c_start) & (kc < c_start + WIN_W)) | (lane >= nkeys)
    no_bias = jnp.zeros((GRID_W, past), F32)
    q = [_split_pair(t) for t in _pair_tiles((q_ref[...] * ATTN_SCALE).astype(BF16))]
    kcat = _pair_tiles(jnp.concatenate([k_ref[pl.ds(row0, nkeys), :], ck_ref[...]], axis=0).astype(BF16))
    vcat = [_with_ones(t) for t in
            _pair_tiles(jnp.concatenate([v_ref[pl.ds(row0, nkeys), :], cv_ref[...]], axis=0).astype(BF16))]
    bias = [jnp.concatenate([tt_ref[h, dr0 + 2 * j] for j in range(WIN_R // 2)] + [no_bias], axis=1)
            for h in range(N_HEADS)]
    e = [[_exp_scores(jnp.where(visible, _bdot_nt(q[j][c], kcat[j]) + bias[2 * j + c], NEG_INF))
          for c in range(2)] for j in range(len(q))]
    out = [[jnp.dot(e[j][c], vcat[j], preferred_element_type=F32) for c in range(2)] for j in range(len(q))]
    for j in range(len(q)):
        o_ref[:, j * PAIR_W:(j + 1) * PAIR_W] = _merge_pair(out[j][0], out[j][1])


def _lat_na_attention(qn, kn, vn, cache_k, cache_v, tt, n_lat):
    n = qn.shape[0]
    nb = n // n_lat
    past = cache_k.shape[1]
    qblk = pl.BlockSpec((GRID_W, D_GROUP), lambda b, r: (b * GRID_ROWS + r, 0))
    kvblk = pl.BlockSpec((n_lat, D_GROUP), lambda b, r: (b, 0))
    cblk = pl.BlockSpec((None, past, D_GROUP), lambda b, r: (b, 0, 0))
    return pl.pallas_call(
        _lat_na_kernel,
        grid=(nb, GRID_ROWS),
        in_specs=[qblk, kvblk, kvblk, cblk, cblk,
                  pl.BlockSpec(tt.shape, lambda b, r: (0, 0, 0, 0))],
        out_specs=qblk,
        out_shape=jax.ShapeDtypeStruct((n, D_GROUP), F32),
        compiler_params=_cparams("arbitrary", "arbitrary"),
        name="lat_na_attention",
    )(qn, kn, vn, cache_k, cache_v, tt)


def _rec_mix(of_ref, ob_ref, rg_ref, yf_ref, yb_ref, xr_ref, xk_ref, xv_ref, lo_ref, rk_ref, lng_ref, lnb_ref,
             gup_ref):
    pw = 2 * HEAD_DIM
    brow = lax.broadcasted_iota(jnp.int32, (pw, pw), 0) // HEAD_DIM
    bcol = lax.broadcasted_iota(jnp.int32, (pw, pw), 1) // HEAD_DIM
    ones_bd = jnp.where(brow == bcol, 1.0, 0.0).astype(BF16)

    def head_sums(x):
        outs = []
        for t in range(D_GROUP // pw):
            xt = x[:, t * pw:(t + 1) * pw]
            hi = xt.astype(BF16)
            lo = (xt - hi.astype(F32)).astype(BF16)
            outs.append(jnp.dot(hi, ones_bd, preferred_element_type=F32)
                        + jnp.dot(lo, ones_bd, preferred_element_type=F32))
        return jnp.concatenate(outs, axis=1)

    inv_d = 1.0 / HEAD_DIM
    o = of_ref[...] + ob_ref[...]
    o_ret = _silu(rg_ref[...]) * (o * lax.rsqrt(head_sums(o * o) * inv_d + RMS_EPS))
    y = yf_ref[...] + yb_ref[...]
    yc = y - head_sums(y) * inv_d
    var = head_sums(yc * yc) * inv_d
    yn = yc * lax.rsqrt(var + RWKV_GN_EPS) * lng_ref[...] + lnb_ref[...]
    bonus = head_sums(xr_ref[...] * rk_ref[...] * xk_ref[...]) * xv_ref[...]
    g_rw = _bdot(_sigmoid(lo_ref[:, 2 * D_LORA:]), gup_ref[...])
    return o_ret, (yn + bonus) * g_rw


def _tail_kernel(*refs, final, ff_chunk, n_mix):
    x_ref, mix_refs = refs[0], refs[1:1 + n_mix]
    wo_ref, gm_ref, g_ref, sh_ref, sc_ref, gf_ref, wi_ref, wf_ref, fg_ref, o_ref = refs[1 + n_mix:]
    if n_mix == 2:
        mix_a, mix_b = mix_refs[0][...], mix_refs[1][...]
    else:
        mix_a, mix_b = _rec_mix(*mix_refs)
    mix = _bdot(mix_a, wo_ref[0:D_GROUP, :]) + _bdot(mix_b, wo_ref[D_GROUP:, :])
    x1 = x_ref[...] + gm_ref[0] * mix
    h = _rms(x1) * g_ref[...]
    hb = (h * (1.0 + sc_ref[0]) + sh_ref[0]).astype(BF16)
    acc = jnp.zeros_like(x1)
    for c0 in range(0, D_FF, ff_chunk):
        gate = jnp.dot(hb, wi_ref[:, c0:c0 + ff_chunk], preferred_element_type=F32)
        up = jnp.dot(hb, wi_ref[:, D_FF + c0:D_FF + c0 + ff_chunk], preferred_element_type=F32)
        acc = acc + jnp.dot((_silu(gate) * up).astype(BF16), wf_ref[c0:c0 + ff_chunk, :],
                            preferred_element_type=F32)
    x2 = x1 + gf_ref[0] * acc
    if final:
        x2 = _rms(x2) * fg_ref[...]
    o_ref[...] = x2


def _layer_tail(x, mix_inputs, w_out, gate_mix, gain, shift, scale, gate_ffn, w_ffn_in_all, w_ffn_out_all, layer,
                final_gain, rows_per_mod, final, ff_chunk=1408):
    n, d = x.shape
    tm = TAIL_TM if len(mix_inputs) == 2 else TAIL_TM_REC
    tiles_per_mod = rows_per_mod // tm
    mod_spec = pl.BlockSpec((1, 1, d), lambda i: (i // tiles_per_mod, 0, 0))
    vec_spec = pl.BlockSpec((1, d), lambda i: (0, 0))
    resident = lambda a: pl.BlockSpec(a.shape, lambda i: (0,) * a.ndim, pipeline_mode=pl.Buffered(1))
    layer_slab = lambda a: pl.BlockSpec((None,) + a.shape[1:], lambda i: (layer, 0, 0),
                                        pipeline_mode=pl.Buffered(1))
    mix_specs = [pl.BlockSpec((tm, a.shape[1]), lambda i: (i, 0)) if a.shape[0] == n else resident(a)
                 for a in mix_inputs]
    return pl.pallas_call(
        functools.partial(_tail_kernel, final=final, ff_chunk=ff_chunk, n_mix=len(mix_inputs)),
        grid=(n // tm,),
        in_specs=[pl.BlockSpec((tm, d), lambda i: (i, 0))] + mix_specs
                 + [resident(w_out), mod_spec, vec_spec, mod_spec, mod_spec, mod_spec,
                    layer_slab(w_ffn_in_all), layer_slab(w_ffn_out_all), vec_spec],
        out_specs=pl.BlockSpec((tm, d), lambda i: (i, 0)),
        out_shape=jax.ShapeDtypeStruct((n, d), F32),
        compiler_params=_cparams("arbitrary"),
        name="layer_tail",
    )(x, *mix_inputs, w_out, gate_mix, gain.reshape(1, d), shift, scale, gate_ffn, w_ffn_in_all, w_ffn_out_all,
      final_gain.reshape(1, d))


def _log_sigmoid(x):
    return jnp.minimum(x, 0.0) - jnp.log(1.0 + jnp.exp(-jnp.abs(x)))


def _pair_block_diag(xp):
    left = lax.broadcasted_iota(jnp.int32, xp.shape, 1) < HEAD_DIM
    zero = jnp.zeros_like(xp)
    return jnp.concatenate([jnp.where(left, xp, zero), jnp.where(left, zero, xp)], axis=0)


def _states_to_pairs(state):
    nb = state.shape[0]
    s = state.reshape(nb, 2, N_HEADS // 2, 2, HEAD_DIM, HEAD_DIM)
    zero = jnp.zeros_like(s[:, :, :, 0])
    return jnp.concatenate([jnp.concatenate([s[:, :, :, 0], zero], axis=-1),
                            jnp.concatenate([zero, s[:, :, :, 1]], axis=-1)], axis=-2)


def _load_states(st, s0_ref):
    st[...] = jnp.zeros(st.shape, st.dtype) if s0_ref is None else s0_ref[0]


def _store_states(sf_ref, st):
    for d in range(st.shape[0]):
        for p in range(st.shape[1]):
            s = st[d, p]
            sf_ref[0, d, 2 * p] = s[:HEAD_DIM, :HEAD_DIM]
            sf_ref[0, d, 2 * p + 1] = s[HEAD_DIM:, HEAD_DIM:]


def _ret_kernel(*refs, zero_init):
    dlm_ref, dlq_ref, qf_ref, kf_ref, vf_ref, qb_ref, kb_ref, vb_ref = refs[:8]
    s0_ref = None if zero_init else refs[8]
    of_ref, ob_ref, sf_ref, st, mask_scr, qdec_scr, kdec_scr, cdec_scr = refs[8 if zero_init else 9:]
    c = pl.program_id(1)
    cs = RET_CHUNK
    pw = 2 * HEAD_DIM
    npair = N_HEADS // 2
    chains = [(d, p) for d in range(2) for p in range(npair)]

    @pl.when(c == 0)
    def _():
        _load_states(st, s0_ref)

    @pl.when((pl.program_id(0) == 0) & (c == 0))
    def _():
        row = lax.broadcasted_iota(jnp.int32, (cs, 2 * cs), 0)
        col = lax.broadcasted_iota(jnp.int32, (cs, 2 * cs), 1) % cs
        tok = lax.broadcasted_iota(jnp.int32, (cs, pw), 0)
        dist = [row - col, col - row]
        tq = [tok.astype(F32), (cs - 1 - tok).astype(F32)]
        for i, (d, p) in enumerate(chains):
            lgm = _log_sigmoid(dlm_ref[d, p])[0:1, :]
            lgq = _log_sigmoid(dlq_ref[d, p])
            mask_scr[i] = jnp.where(dist[d] >= 0, jnp.exp(jnp.maximum(dist[d], 0).astype(F32) * lgm), 0.0)
            qdec_scr[i] = jnp.exp((tq[d] + 1.0) * lgq[0:1, :])
            kdec_scr[i] = jnp.exp((cs - 1.0 - tq[d]) * lgq[0:1, :])
            cdec_scr[i] = jnp.exp(cs * lgq)

    brow = lax.broadcasted_iota(jnp.int32, (pw, pw), 0) // HEAD_DIM
    bcol = lax.broadcasted_iota(jnp.int32, (pw, pw), 1) // HEAD_DIM
    same_head = brow == bcol
    refs = [(qf_ref, kf_ref, vf_ref, of_ref), (qb_ref, kb_ref, vb_ref, ob_ref)]
    nsub = RET_CHUNKS_PER_STEP
    work = [(t, sub) for sub in range(nsub) for t in range(len(chains))]
    sl_of = lambda t: slice(chains[t][1] * pw, (chains[t][1] + 1) * pw)
    rows_of = lambda sub: pl.ds(sub * cs, cs)
    q = {(t, sub): refs[chains[t][0]][0][rows_of(sub), sl_of(t)] for t, sub in work}
    k = {(t, sub): refs[chains[t][0]][1][rows_of(sub), sl_of(t)] * ATTN_SCALE for t, sub in work}
    v = {(t, sub): refs[chains[t][0]][2][rows_of(sub), sl_of(t)] for t, sub in work}
    s = {w: _bdot_nt(q[w], _pair_block_diag(k[w])) * mask_scr[w[0]] for w in work}
    inner = {w: _bdot(s[w], _pair_block_diag(v[w])) for w in work}
    kv = {w: _bdot_tn(k[w] * kdec_scr[w[0]], v[w]) for w in work}
    for step in range(nsub):
        now = [(t, (step if chains[t][0] == 0 else nsub - 1 - step)) for t in range(len(chains))]
        state = {w: st[chains[w[0]]] for w in now}
        cross = {w: _bdot(q[w], state[w]) * qdec_scr[w[0]] for w in now}
        for w in now:
            t, sub = w
            d, p = chains[t]
            refs[d][3][rows_of(sub), sl_of(t)] = inner[w] + cross[w]
            st[d, p] = state[w] * cdec_scr[t][0:1, :] + jnp.where(same_head, kv[w], 0.0)

    @pl.when(c == pl.num_programs(1) - 1)
    def _():
        _store_states(sf_ref, st)


def _retention_scan(decay_logit, q, k, v, state0, seq):
    n = q.shape[0]
    nb = n // seq
    rows = RET_CHUNK * RET_CHUNKS_PER_STEP
    nc = seq // rows
    npair, pw = N_HEADS // 2, 2 * HEAD_DIM
    dl = decay_logit.astype(F32)
    dl_m = jnp.broadcast_to(jnp.repeat(dl, RET_CHUNK, axis=1).reshape(2, npair, 1, 2 * RET_CHUNK),
                            (2, npair, 8, 2 * RET_CHUNK))
    dl_q = jnp.broadcast_to(jnp.repeat(dl, HEAD_DIM, axis=1).reshape(2, npair, 1, pw), (2, npair, 8, pw))
    blk_f = pl.BlockSpec((rows, D_GROUP), lambda b, c: (b * nc + c, 0))
    blk_b = pl.BlockSpec((rows, D_GROUP), lambda b, c: (b * nc + nc - 1 - c, 0))
    st_in = ([] if state0 is None else
             [pl.BlockSpec((1, 2, npair, pw, pw), lambda b, c: (b, 0, 0, 0, 0))])
    st_out = pl.BlockSpec((1, 2, N_HEADS, HEAD_DIM, HEAD_DIM), lambda b, c: (b, 0, 0, 0, 0))
    return pl.pallas_call(
        functools.partial(_ret_kernel, zero_init=state0 is None),
        grid=(nb, nc),
        in_specs=[pl.BlockSpec(dl_m.shape, lambda b, c: (0, 0, 0, 0)),
                  pl.BlockSpec(dl_q.shape, lambda b, c: (0, 0, 0, 0)),
                  blk_f, blk_f, blk_f, blk_b, blk_b, blk_b] + st_in,
        out_specs=[blk_f, blk_b, st_out],
        out_shape=[jax.ShapeDtypeStruct((n, D_GROUP), F32), jax.ShapeDtypeStruct((n, D_GROUP), F32),
                   jax.ShapeDtypeStruct((nb, 2, N_HEADS, HEAD_DIM, HEAD_DIM), F32)],
        scratch_shapes=[pltpu.VMEM((2, npair, pw, pw), F32),
                        pltpu.VMEM((2 * npair, RET_CHUNK, 2 * RET_CHUNK), F32),
                        pltpu.VMEM((2 * npair, RET_CHUNK, pw), F32),
                        pltpu.VMEM((2 * npair, RET_CHUNK, pw), F32),
                        pltpu.VMEM((2 * npair, 8, pw), F32)],
        compiler_params=_cparams("arbitrary", "arbitrary"),
        name="retention_scan",
    )(dl_m, dl_q, q, k, v, q, k, v, *([] if state0 is None else [_states_to_pairs(state0)]))


def _rwkv_kernel(*refs, zero_init):
    (xrf_ref, xkf_ref, xvf_ref, lof_ref, xrb_ref, xkb_ref, xvb_ref, lob_ref,
     w0_ref, wup_ref, a0_ref, aup_ref, kk_ref, ka_ref) = refs[:14]
    s0_ref = None if zero_init else refs[14]
    yf_ref, yb_ref, sf_ref, st = refs[14 if zero_init else 15:]
    c = pl.program_id(1)
    cs = RWKV_CHUNK
    nsub = RWKV_CHUNKS_PER_STEP
    npair = N_HEADS // 2

    @pl.when(c == 0)
    def _():
        _load_states(st, s0_ref)

    row = lax.broadcasted_iota(jnp.int32, (cs, 3 * cs), 0)
    col = lax.broadcasted_iota(jnp.int32, (cs, 3 * cs), 1) % cs
    pw = 2 * HEAD_DIM
    prow = lax.broadcasted_iota(jnp.int32, (cs, pw), 0)
    pcol = lax.broadcasted_iota(jnp.int32, (cs, pw), 1) % HEAD_DIM
    eye = jnp.where(pcol == prow, 1.0, 0.0).astype(F32)
    sub_diag = (prow // RWKV_SUB) == (pcol // RWKV_SUB)
    brow = lax.broadcasted_iota(jnp.int32, (pw, pw), 0) // HEAD_DIM
    bcol = lax.broadcasted_iota(jnp.int32, (pw, pw), 1) // HEAD_DIM
    same_head = brow == bcol

    bd = _pair_block_diag

    def direction(d, sub, xr_ref, xk_ref, xv_ref, lo_ref):
        rows = pl.ds(sub * cs, cs)
        tri3 = jnp.where((row >= col) if d == 0 else (col >= row), 1.0, 0.0).astype(BF16)
        dist = (prow - pcol) if d == 0 else (pcol - prow)
        xk = xk_ref[rows, :]
        lo = lo_ref[rows, :]
        w_log = -math.exp(-0.5) * _sigmoid(w0_ref[d] + _bdot(jnp.tanh(lo[:, 0:D_LORA]), wup_ref[d]))
        a_all = _sigmoid(a0_ref[d] + _bdot(lo[:, D_LORA:2 * D_LORA], aup_ref[d]))
        w1 = w_log.astype(BF16)
        r1 = w_log - w1.astype(F32)
        w2 = r1.astype(BF16)
        w3 = (r1 - w2.astype(F32)).astype(BF16)
        cum = jnp.dot(tri3, jnp.concatenate([w1, w2, w3], axis=0), preferred_element_type=F32)
        cum_end = cum[cs - 1:cs, :] if d == 0 else cum[0:1, :]
        return dict(earlier=dist > 0, upto=dist >= 0, xr=xr_ref[rows, :], xv=xv_ref[rows, :], a_all=a_all,
                    e_incl=jnp.exp(cum), e_excl=jnp.exp(cum - w_log), e_neg=jnp.exp(-cum),
                    e_end=jnp.exp(cum_end - cum), w_end=jnp.exp(cum_end), kk_all=xk * kk_ref[...],
                    keff_all=xk * (1.0 + (a_all - 1.0) * ka_ref[...]))

    in_refs = [(xrf_ref, xkf_ref, xvf_ref, lof_ref), (xrb_ref, xkb_ref, xvb_ref, lob_ref)]
    dirs = {(d, sub): direction(d, sub, *in_refs[d]) for d in range(2) for sub in range(nsub)}
    y_refs = [yf_ref, yb_ref]

    chains = [(d, sub, p) for d in range(2) for sub in range(nsub) for p in range(npair)]
    pairs = range(len(chains))
    sls = [slice(p * pw, (p + 1) * pw) for _, _, p in chains]
    pick = lambda name: [dirs[d, sub][name][:, sls[i]] for i, (d, sub, _) in enumerate(chains)]
    earlier = [dirs[d, sub]["earlier"] for d, sub, _ in chains]
    upto = [dirs[d, sub]["upto"] for d, sub, _ in chains]
    cat = lambda x, y: jnp.concatenate([x, y], axis=0)

    def head_sum(x):
        left = lax.broadcasted_iota(jnp.int32, x.shape, 1) < HEAD_DIM
        zero = jnp.zeros_like(x)
        s_left = jnp.sum(jnp.where(left, x, zero), axis=-1, keepdims=True)
        s_right = jnp.sum(jnp.where(left, zero, x), axis=-1, keepdims=True)
        return jnp.where(left, s_left, s_right)

    e_incl, e_excl, e_neg, e_end, w_end = (pick(k) for k in ("e_incl", "e_excl", "e_neg", "e_end", "w_end"))
    kk = [x * lax.rsqrt(head_sum(x * x) + 1e-12) for x in pick("kk_all")]
    b = [kk[p] * a for p, a in zip(pairs, pick("a_all"))]
    keff = pick("keff_all")
    v = pick("xv")
    a_t = [-kk[p] * e_excl[p] for p in pairs]
    r_t = [xr * e_incl[p] for p, xr in zip(pairs, pick("xr"))]
    ar = [cat(a_t[p], r_t[p]).astype(BF16) for p in pairs]
    g = [_bdot_nt(ar[p], cat(bd(b[p] * e_neg[p]), bd(keff[p] * e_neg[p]))) for p in pairs]
    n_mat = [jnp.where(earlier[p], g[p][:cs, :pw], 0.0) for p in pairs]
    m_rb = [jnp.where(upto[p], g[p][cs:, :pw], 0.0) for p in pairs]
    a_ak = [jnp.where(earlier[p], g[p][:cs, pw:], 0.0) for p in pairs]
    m_rk = [jnp.where(upto[p], g[p][cs:, pw:], 0.0) for p in pairs]
    bd_s = lambda xs: (bd(xs[0]), bd(xs[1]))
    n_d = [jnp.where(sub_diag, n, 0.0) for n in n_mat]
    n_o = [jnp.where(sub_diag, 0.0, n) for n in n_mat]
    n_s = [_split2(n) for n in n_d]
    x = [eye + n for n in n_d]
    pk = [_mm3s(ns, bd_s(ns)) for ns in n_s]
    for _ in range(RWKV_SUB_DOUBLINGS - 1):
        xp = [_mm3s(_split2(cat(x[p], pk[p])), bd_s(_split2(pk[p]))) for p in pairs]
        x = [x[p] + xp[p][:cs] for p in pairs]
        pk = [r[cs:] for r in xp]
    t_d = [x[p] + _mm3s(_split2(x[p]), bd_s(_split2(pk[p]))) for p in pairs]
    lcat = lambda x, y: jnp.concatenate([x, y], axis=1)
    m_o = [_bdot(t_d[p], bd(n_o[p])) for p in pairs]
    mz = [_bdot(m_o[p], lcat(bd(t_d[p]), bd(m_o[p]))) for p in pairs]
    z = [t_d[p] + mz[p][:, :pw] for p in pairs]
    t_inv = [(z[p] + _bdot(mz[p][:, pw:], bd(z[p]))).astype(BF16) for p in pairs]
    av = [_bdot(cat(a_ak[p], m_rk[p]), bd(v[p])) for p in pairs]
    tau = [_bdot(t_inv[p], lcat(bd(a_t[p]), bd(av[p][:cs]))) for p in pairs]
    bk_end = [cat(b[p] * e_end[p], keff[p] * e_end[p]) for p in pairs]
    for step in range(nsub):
        now = [i for i, (d, sub, _) in enumerate(chains) if sub == (step if d == 0 else nsub - 1 - step)]
        s_prev = {i: st[chains[i][0], chains[i][2]] for i in now}
        hs = {i: _bdot_nt(cat(tau[i][:, :pw], r_t[i]), s_prev[i]) for i in now}
        u = {i: hs[i][:cs] + tau[i][:, pw:] for i in now}
        y = {i: hs[i][cs:] + _bdot(m_rb[i], bd(u[i])) + av[i][cs:] for i in now}
        for i in now:
            d, sub, p = chains[i]
            y_refs[d][pl.ds(sub * cs, cs), sls[i]] = y[i]
            st[d, p] = s_prev[i] * w_end[i] + jnp.where(same_head, _bdot_tn(cat(u[i], v[i]), bk_end[i]), 0.0)

    @pl.when(c == pl.num_programs(1) - 1)
    def _():
        _store_states(sf_ref, st)


def _rwkv_scan(xr, xk, xv, lora, w0, w_up, a0, a_up, k_k, k_a, state0, seq):
    n = xr.shape[0]
    nb = n // seq
    blk = RWKV_CHUNK * RWKV_CHUNKS_PER_STEP
    nc = seq // blk

    fwd_idx = lambda b, c: (b * nc + c, 0)
    bwd_idx = lambda b, c: (b * nc + nc - 1 - c, 0)
    blk_f = pl.BlockSpec((blk, D_GROUP), fwd_idx)
    blk_b = pl.BlockSpec((blk, D_GROUP), bwd_idx)
    lblk_f = pl.BlockSpec((blk, lora.shape[1]), fwd_idx)
    lblk_b = pl.BlockSpec((blk, lora.shape[1]), bwd_idx)
    dvec = pl.BlockSpec((2, 1, D_GROUP), lambda b, c: (0, 0, 0))
    dmat = pl.BlockSpec((2, D_LORA, D_GROUP), lambda b, c: (0, 0, 0))
    vec = pl.BlockSpec((1, D_GROUP), lambda b, c: (0, 0))
    npair, pw = N_HEADS // 2, 2 * HEAD_DIM
    st_in = ([] if state0 is None else
             [pl.BlockSpec((1, 2, npair, pw, pw), lambda b, c: (b, 0, 0, 0, 0))])
    st_out = pl.BlockSpec((1, 2, N_HEADS, HEAD_DIM, HEAD_DIM), lambda b, c: (b, 0, 0, 0, 0))
    return pl.pallas_call(
        functools.partial(_rwkv_kernel, zero_init=state0 is None),
        grid=(nb, nc),
        in_specs=[blk_f, blk_f, blk_f, lblk_f, blk_b, blk_b, blk_b, lblk_b,
                  dvec, dmat, dvec, dmat, vec, vec] + st_in,
        out_specs=[blk_f, blk_b, st_out],
        out_shape=[jax.ShapeDtypeStruct((n, D_GROUP), F32), jax.ShapeDtypeStruct((n, D_GROUP), F32),
                   jax.ShapeDtypeStruct((nb, 2, N_HEADS, HEAD_DIM, HEAD_DIM), F32)],
        scratch_shapes=[pltpu.VMEM((2, npair, pw, pw), F32)],
        compiler_params=_cparams("arbitrary", "arbitrary"),
        name="rwkv7_scan",
    )(xr, xk, xv, lora, xr, xk, xv, lora, w0.reshape(2, 1, D_GROUP), w_up, a0.reshape(2, 1, D_GROUP), a_up,
      k_k.reshape(1, D_GROUP), k_a.reshape(1, D_GROUP), *([] if state0 is None else [_states_to_pairs(state0)]))


ATTN_WIDTHS = (D_GROUP,) * 6
REC_WIDTHS = (D_GROUP,) * 7 + (2 * D_LORA + D_LORA_G,)


def kernel(x_prompt, x_sample, cache_na_k, cache_na_v, cache_diff_k, cache_diff_v, state_ret, state_rwkv, c, c_ctx, norm_mix_g, norm_ffn_g, norm_final_g, w_ada, b_ada, w_in_attn, w_out_attn, na_rpb, diff_lq1, diff_lk1, diff_lq2, diff_lk2, w_in_rec, w_out_rec, ret_decay_logit, rw_w0, rw_w_up, rw_a0, rw_a_up, rw_g_up, rw_k_k, rw_k_a, rw_r_k, rw_ln_g, rw_ln_b, w_ffn_in, w_ffn_out):
    bp, seq, d = x_prompt.shape
    bl, n_lat, _ = x_sample.shape
    depth = w_ada.shape[0]
    ctx = x_prompt.reshape(bp * seq, d)
    lat = x_sample.reshape(bl * n_lat, d)

    cond8 = jnp.zeros((8, d), F32).at[0].set(c_ctx).at[1:1 + bl].set(c)
    mods = _ada_modulation(cond8, w_ada, b_ada).reshape(depth, 8, 6, d)
    w_ffn_in_bf16 = w_ffn_in.astype(BF16)
    w_ffn_out_bf16 = w_ffn_out.astype(BF16)

    outs = {}
    for layer in range(depth):
        m_ctx = [mods[layer, 0:1, j].reshape(1, 1, d) for j in range(6)]
        m_lat = [mods[layer, 1:1 + bl, j].reshape(bl, 1, d) for j in range(6)]
        if layer % 2 == 0:
            i = layer // 2
            lam_init = 0.8 - 0.6 * math.exp(-0.3 * layer)
            lam_params = jnp.stack([diff_lq1[i], diff_lk1[i], diff_lq2[i], diff_lk2[i]]).astype(F32)
            w_in = w_in_attn[i].astype(BF16)
            w_out = w_out_attn[i].astype(BF16)
            res = _ctx_attention(ctx, norm_mix_g[layer], m_ctx[0], m_ctx[1], w_in, lam_params, seq, lam_init)
            mix_ctx = res[:2]
            outs["na_k"], outs["na_v"], outs["df_k"], outs["df_v"] = res[2:]
            plat = _norm_mod_matmul(lat, norm_mix_g[layer], m_lat[0], m_lat[1], w_in, ATTN_WIDTHS, n_lat, PROJ_TM)
            past = cache_na_k.shape[2]
            tt = _na_bias_table(na_rpb[i])
            o_na = _lat_na_attention(plat[0], plat[1], plat[2],
                                     cache_na_k[:, i].reshape(bl, past, D_GROUP),
                                     cache_na_v[:, i].reshape(bl, past, D_GROUP), tt, n_lat)
            o_df = _lat_diff_attention(lam_params, plat[3], plat[4], plat[5],
                                       cache_diff_k[:, i].reshape(bl, past, D_GROUP),
                                       cache_diff_v[:, i].reshape(bl, past, D_GROUP), n_lat, lam_init)
            mix_lat = (o_na, o_df)
        else:
            j = layer // 2
            w_in = w_in_rec[j].astype(BF16)
            w_out = w_out_rec[j].astype(BF16)
            pc = _norm_mod_matmul(ctx, norm_mix_g[layer], m_ctx[0], m_ctx[1], w_in, REC_WIDTHS, bp * seq, PROJ_TM)
            plat = _norm_mod_matmul(lat, norm_mix_g[layer], m_lat[0], m_lat[1], w_in, REC_WIDTHS, n_lat, PROJ_TM)
            mixes = []
            for is_ctx, p, s_ret0, s_rw0, sq in ((True, pc, None, None, seq),
                                                 (False, plat, state_ret[:, j], state_rwkv[:, j], n_lat)):
                rq, rk, rv, rg, wr, wk, wv, lora = p
                o_f, o_b, s_ret = _retention_scan(ret_decay_logit[j], rq, rk, rv, s_ret0, sq)
                y_f, y_b, s_rw = _rwkv_scan(wr, wk, wv, lora, rw_w0[j], rw_w_up[j].astype(BF16), rw_a0[j],
                                            rw_a_up[j].astype(BF16), rw_k_k[j], rw_k_a[j], s_rw0, sq)
                mixes.append((o_f, o_b, rg, y_f, y_b, wr, wk, wv, lora, rw_r_k[j].reshape(1, D_GROUP),
                              rw_ln_g[j].reshape(1, D_GROUP), rw_ln_b[j].reshape(1, D_GROUP),
                              rw_g_up[j].astype(BF16)))
                if is_ctx:
                    outs["ret"], outs["rwkv"] = s_ret, s_rw
            mix_ctx, mix_lat = mixes
        final = layer == depth - 1
        ctx = _layer_tail(ctx, mix_ctx, w_out, m_ctx[2], norm_ffn_g[layer], m_ctx[3], m_ctx[4], m_ctx[5],
                          w_ffn_in_bf16, w_ffn_out_bf16, layer, norm_final_g, bp * seq, final)
        lat = _layer_tail(lat, mix_lat, w_out, m_lat[2], norm_ffn_g[layer], m_lat[3], m_lat[4], m_lat[5],
                          w_ffn_in_bf16, w_ffn_out_bf16, layer, norm_final_g, n_lat, final)

    y_prompt = ctx.reshape(bp, seq, d)
    y_sample = lat.reshape(bl, n_lat, d)
    new_cache_na_k = outs["na_k"].reshape(bp, 1, seq, N_HEADS, HEAD_DIM)
    new_cache_na_v = outs["na_v"].reshape(bp, 1, seq, N_HEADS, HEAD_DIM)
    new_cache_diff_k = outs["df_k"].reshape(bp, 1, seq, N_HEADS // 2, 2, HEAD_DIM)
    new_cache_diff_v = outs["df_v"].reshape(bp, 1, seq, N_HEADS // 2, 2 * HEAD_DIM)
    new_state_ret = outs["ret"].reshape(bp, 1, 2, N_HEADS, HEAD_DIM, HEAD_DIM)
    new_state_rwkv = outs["rwkv"].reshape(bp, 1, 2, N_HEADS, HEAD_DIM, HEAD_DIM)
    return (y_prompt, y_sample, new_cache_na_k, new_cache_na_v, new_cache_diff_k, new_cache_diff_v,
            new_state_ret, new_state_rwkv)
```

```python
import functools
import math

import jax
import jax.numpy as jnp
from jax import lax
from jax.experimental import pallas as pl
from jax.experimental.pallas import tpu as pltpu

F32 = jnp.float32
BF16 = jnp.bfloat16

D_MODEL = 1024
HEAD_DIM = 64
N_HEADS = 8
D_GROUP = 512
GRID_W = 64
GRID_ROWS = 32
WIN_R = 8
WIN_W = 16
RPB_R = 15
RPB_C = 31
PAST_LEN = 256
D_FF = 2816
D_LORA = 64
D_LORA_G = 128
ROPE_BASE = 10000.0
RMS_EPS = 1e-6
RWKV_GN_EPS = 64e-5
NEG_INF = -1e30
ATTN_SCALE = HEAD_DIM ** -0.5
RET_CHUNK = 128
RET_CHUNKS_PER_STEP = 2
RWKV_CHUNK = 64
RWKV_CHUNKS_PER_STEP = 4
RWKV_SUB = 16
RWKV_SUB_DOUBLINGS = 3
VMEM_LIMIT = 56 * 1024 * 1024
TAIL_TM = 512
TAIL_TM_REC = 256
PROJ_TM = 512
LAT_DIFF_TQ = 256


def _cparams(*sem):
    return pltpu.CompilerParams(dimension_semantics=sem, vmem_limit_bytes=VMEM_LIMIT)


def _sigmoid(x):
    return 1.0 / (1.0 + jnp.exp(-x))


def _silu(x):
    return x * _sigmoid(x)


def _rms(x, eps=RMS_EPS):
    return x * lax.rsqrt(jnp.mean(x * x, axis=-1, keepdims=True) + eps)


def _bdot(a, b):
    return jnp.dot(a.astype(BF16), b.astype(BF16), preferred_element_type=F32)


def _bdot_nt(a, b):
    return lax.dot_general(a.astype(BF16), b.astype(BF16), (((1,), (1,)), ((), ())),
                           preferred_element_type=F32)


def _bdot_tn(a, b):
    return lax.dot_general(a.astype(BF16), b.astype(BF16), (((0,), (0,)), ((), ())),
                           preferred_element_type=F32)


_DN = {"nn": (((1,), (0,)), ((), ())), "nt": (((1,), (1,)), ((), ())), "tn": (((0,), (0,)), ((), ()))}


def _split2(x):
    hi = x.astype(BF16)
    lo = (x - hi.astype(F32)).astype(BF16)
    return hi, lo


def _mm3s(a_split, b_split):
    ah, al = a_split
    bh, bl = b_split
    d = functools.partial(jnp.dot, preferred_element_type=F32)
    return d(jnp.concatenate([ah, al], axis=1), jnp.concatenate([bh, bh], axis=0)) + d(ah, bl)


def _diff_lambda(lp, lam_init):
    s1 = jnp.sum(lp[0:1, :] * lp[1:2, :], axis=-1, keepdims=True)
    s2 = jnp.sum(lp[2:3, :] * lp[3:4, :], axis=-1, keepdims=True)
    return jnp.exp(s1) - jnp.exp(s2) + lam_init


def _ada_kernel(c_ref, w_ref, b_ref, o_ref):
    s = _silu(c_ref[...])
    o_ref[0] = _bdot(s, w_ref[0]) + b_ref[0]


def _ada_modulation(cond8, w_ada, b_ada):
    depth, d, n = w_ada.shape
    tn = 1536
    return pl.pallas_call(
        _ada_kernel,
        grid=(depth, n // tn),
        in_specs=[pl.BlockSpec((8, d), lambda l, j: (0, 0)),
                  pl.BlockSpec((1, d, tn), lambda l, j: (l, 0, j)),
                  pl.BlockSpec((1, 1, tn), lambda l, j: (l, 0, j))],
        out_specs=pl.BlockSpec((1, 8, tn), lambda l, j: (l, 0, j)),
        out_shape=jax.ShapeDtypeStruct((depth, 8, n), F32),
        compiler_params=_cparams("arbitrary", "arbitrary"),
        name="ada_modulation",
    )(cond8, w_ada, b_ada.reshape(depth, 1, n))


def _norm_mod(x_ref, g_ref, sh_ref, sc_ref):
    h = _rms(x_ref[...]) * g_ref[...]
    return (h * (1.0 + sc_ref[0]) + sh_ref[0]).astype(BF16)


def _nmm_kernel(x_ref, g_ref, sh_ref, sc_ref, w_ref, *o_refs, widths):
    hb = _norm_mod(x_ref, g_ref, sh_ref, sc_ref)
    off = 0
    for o_ref, w in zip(o_refs, widths):
        o_ref[...] = jnp.dot(hb, w_ref[:, off:off + w], preferred_element_type=F32)
        off += w


def _norm_mod_matmul(x, gain, shift, scale, w_bf16, widths, rows_per_mod, tm):
    n, d = x.shape
    tiles_per_mod = rows_per_mod // tm
    mod_spec = pl.BlockSpec((1, 1, d), lambda i: (i // tiles_per_mod, 0, 0))
    return pl.pallas_call(
        functools.partial(_nmm_kernel, widths=widths),
        grid=(n // tm,),
        in_specs=[pl.BlockSpec((tm, d), lambda i: (i, 0)),
                  pl.BlockSpec((1, d), lambda i: (0, 0)),
                  mod_spec, mod_spec,
                  pl.BlockSpec(w_bf16.shape, lambda i: (0, 0), pipeline_mode=pl.Buffered(1))],
        out_specs=[pl.BlockSpec((tm, w), lambda i: (i, 0)) for w in widths],
        out_shape=[jax.ShapeDtypeStruct((n, w), F32) for w in widths],
        compiler_params=_cparams("arbitrary"),
        name="norm_mod_matmul",
    )(x, gain.reshape(1, d), shift, scale, w_bf16)


PAIR_W = 2 * HEAD_DIM


def _pair_tiles(x):
    return [x[:, j * PAIR_W:(j + 1) * PAIR_W] for j in range(x.shape[1] // PAIR_W)]


def _split_pair(tile):
    left = lax.broadcasted_iota(jnp.int32, tile.shape, 1) < HEAD_DIM
    zero = jnp.zeros_like(tile)
    return jnp.where(left, tile, zero), jnp.where(left, zero, tile)


def _with_ones(v):
    return jnp.concatenate([v, jnp.ones_like(v)], axis=1)


def _exp_scores(s):
    return jnp.exp(s - jnp.max(s, axis=-1, keepdims=True)).astype(BF16)


def _merge_pair(out_even, out_odd):
    left = lax.broadcasted_iota(jnp.int32, (out_even.shape[0], PAIR_W), 1) < HEAD_DIM
    return jnp.where(left, out_even[:, :PAIR_W] * (1.0 / out_even[:, PAIR_W:]),
                     out_odd[:, :PAIR_W] * (1.0 / out_odd[:, PAIR_W:]))


def _diff_combine(out1, out2, lam, lam_init):
    o = out1[:, :PAIR_W] * (1.0 / out1[:, PAIR_W:]) - out2[:, :PAIR_W] * (lam / out2[:, PAIR_W:])
    return _rms(o) * (1.0 - lam_init)


CTX_CACHE_SHAPES = ((N_HEADS, HEAD_DIM), (N_HEADS, HEAD_DIM),
                    (N_HEADS // 2, 2, HEAD_DIM), (N_HEADS // 2, 2 * HEAD_DIM))


def _ctx_attn_kernel(x_ref, g_ref, sh_ref, sc_ref, w_ref, lp_ref, on_ref, od_ref, *cache_refs, lam_init):
    hb = _norm_mod(x_ref, g_ref, sh_ref, sc_ref)
    qn, kn, vn, qd, kd, vd = (jnp.dot(hb, w_ref[:, i * D_GROUP:(i + 1) * D_GROUP], preferred_element_type=F32)
                              for i in range(6))
    for c_ref, val in zip(cache_refs, (kn, vn, kd, vd)):
        c_ref[...] = val.reshape(c_ref.shape)
    tiles = range(D_GROUP // PAIR_W)
    lam = _diff_lambda(lp_ref[...], lam_init)
    q = [_split_pair(t) for t in _pair_tiles((qn * ATTN_SCALE).astype(BF16))
         + _pair_tiles((qd * ATTN_SCALE).astype(BF16))]
    k = _pair_tiles(kn.astype(BF16)) + _pair_tiles(kd.astype(BF16))
    v = [_with_ones(t) for t in _pair_tiles(vn.astype(BF16)) + _pair_tiles(vd.astype(BF16))]
    e = [[_exp_scores(_bdot_nt(q[j][c], k[j])) for c in range(2)] for j in range(2 * len(tiles))]
    out = [[jnp.dot(e[j][c], v[j], preferred_element_type=F32) for c in range(2)] for j in range(2 * len(tiles))]
    for j in tiles:
        sl = slice(j * PAIR_W, (j + 1) * PAIR_W)
        on_ref[:, sl] = _merge_pair(out[j][0], out[j][1])
        od_ref[:, sl] = _diff_combine(out[len(tiles) + j][0], out[len(tiles) + j][1], lam, lam_init)


def _ctx_attention(x, gain, shift, scale, w_bf16, lam_params, seq, lam_init):
    n, d = x.shape
    blk = pl.BlockSpec((seq, D_GROUP), lambda b: (b, 0))
    mod_spec = pl.BlockSpec((1, 1, d), lambda b: (0, 0, 0))
    cache_specs = [pl.BlockSpec((seq,) + dims, lambda b, nd=len(dims): (b,) + (0,) * nd)
                   for dims in CTX_CACHE_SHAPES]
    return pl.pallas_call(
        functools.partial(_ctx_attn_kernel, lam_init=lam_init),
        grid=(n // seq,),
        in_specs=[pl.BlockSpec((seq, d), lambda b: (b, 0)),
                  pl.BlockSpec((1, d), lambda b: (0, 0)),
                  mod_spec, mod_spec,
                  pl.BlockSpec(w_bf16.shape, lambda b: (0, 0), pipeline_mode=pl.Buffered(1)),
                  pl.BlockSpec(lam_params.shape, lambda b: (0, 0))],
        out_specs=[blk, blk] + cache_specs,
        out_shape=([jax.ShapeDtypeStruct((n, D_GROUP), F32)] * 2
                   + [jax.ShapeDtypeStruct((n,) + dims, F32) for dims in CTX_CACHE_SHAPES]),
        compiler_params=_cparams("arbitrary"),
        name="ctx_attention",
    )(x, gain.reshape(1, d), shift, scale, w_bf16, lam_params)


def _rope_tables(n):
    quarter = HEAD_DIM // 4
    pos = jnp.arange(n)
    inv_freq = ROPE_BASE ** (-jnp.arange(quarter, dtype=F32) / quarter)
    lane = jnp.arange(HEAD_DIM)
    p = jnp.where(lane[None, :] < HEAD_DIM // 2, (pos // GRID_W)[:, None], (pos % GRID_W)[:, None]).astype(F32)
    ang = p * inv_freq[lane % quarter][None, :]
    sign = jnp.where((lane % (2 * quarter)) < quarter, -1.0, 1.0).astype(F32)
    cos = jnp.tile(jnp.cos(ang), (1, 2))
    sin = jnp.tile(jnp.sin(ang) * sign[None, :], (1, 2))
    return cos, sin


def _rope(x, cos, sin_signed):
    quarter = HEAD_DIM // 4
    lane = lax.broadcasted_iota(jnp.int32, cos.shape, 1)
    first = (lane % (2 * quarter)) < quarter
    tiles = []
    for t in _pair_tiles(x):
        partner = jnp.where(first, pltpu.roll(t, PAIR_W - quarter, 1), pltpu.roll(t, quarter, 1))
        tiles.append(t * cos + partner * sin_signed)
    return jnp.concatenate(tiles, axis=1)


def _lat_diff_kernel(lp_ref, q_ref, k_ref, v_ref, ck_ref, cv_ref, cosq_ref, sinq_ref, cosk_ref, sink_ref,
                     o_ref, kall, vall, *, lam_init, n_lat):
    @pl.when(pl.program_id(1) == 0)
    def _():
        kall[0:n_lat, :] = _rope(k_ref[...], cosk_ref[...], sink_ref[...]).astype(BF16)
        kall[n_lat:, :] = ck_ref[...].astype(BF16)
        for j, (vt, ct) in enumerate(zip(_pair_tiles(v_ref[...].astype(BF16)),
                                         _pair_tiles(cv_ref[...].astype(BF16)))):
            vall[j, 0:n_lat, :] = _with_ones(vt)
            vall[j, n_lat:, :] = _with_ones(ct)

    lam = _diff_lambda(lp_ref[...], lam_init)
    q = (_rope(q_ref[...], cosq_ref[...], sinq_ref[...]) * ATTN_SCALE).astype(BF16)
    qs = [_split_pair(qt) for qt in _pair_tiles(q)]
    sls = [slice(j * PAIR_W, (j + 1) * PAIR_W) for j in range(len(qs))]
    for group in ((0, 1), (2, 3)):
        e = {j: [_exp_scores(_bdot_nt(qs[j][c], kall[:, sls[j]])) for c in range(2)] for j in group}
        out = {j: [jnp.dot(e[j][c], vall[j], preferred_element_type=F32) for c in range(2)] for j in group}
        for j in group:
            o_ref[:, sls[j]] = _diff_combine(out[j][0], out[j][1], lam, lam_init)


def _lat_diff_attention(lam_params, qd, kd, vd, cache_k, cache_v, n_lat, lam_init, tq=LAT_DIFF_TQ):
    n = qd.shape[0]
    nb = n // n_lat
    nq = n_lat // tq
    past = cache_k.shape[1]
    cos, sin = _rope_tables(n_lat)
    qblk = pl.BlockSpec((tq, D_GROUP), lambda b, i: (b * nq + i, 0))
    once = pl.Buffered(1)
    kvblk = pl.BlockSpec((n_lat, D_GROUP), lambda b, i: (b, 0), pipeline_mode=once)
    cblk = pl.BlockSpec((None, past, D_GROUP), lambda b, i: (b, 0, 0))
    return pl.pallas_call(
        functools.partial(_lat_diff_kernel, lam_init=lam_init, n_lat=n_lat),
        grid=(nb, nq),
        in_specs=[pl.BlockSpec(lam_params.shape, lambda b, i: (0, 0)),
                  qblk, kvblk, kvblk, cblk, cblk,
                  pl.BlockSpec((tq, PAIR_W), lambda b, i: (i, 0)),
                  pl.BlockSpec((tq, PAIR_W), lambda b, i: (i, 0)),
                  pl.BlockSpec((n_lat, PAIR_W), lambda b, i: (0, 0), pipeline_mode=once),
                  pl.BlockSpec((n_lat, PAIR_W), lambda b, i: (0, 0), pipeline_mode=once)],
        out_specs=qblk,
        out_shape=jax.ShapeDtypeStruct((n, D_GROUP), F32),
        scratch_shapes=[pltpu.VMEM((n_lat + past, D_GROUP), BF16),
                        pltpu.VMEM((D_GROUP // PAIR_W, n_lat + past, 2 * PAIR_W), BF16)],
        compiler_params=_cparams("arbitrary", "arbitrary"),
        name="lat_diff_attention",
    )(lam_params, qd, kd, vd, cache_k, cache_v, cos, sin, cos, sin)


def _na_bias_kernel(rpb_ref, tt_ref):
    h = pl.program_id(0)
    lane = lax.broadcasted_iota(jnp.int32, (GRID_W, 2 * GRID_W), 1)
    qc = lax.broadcasted_iota(jnp.int32, (GRID_W, 2 * GRID_W), 0)
    dc = jnp.clip(lane % GRID_W - qc + (WIN_W - 1), 0, RPB_C - 1)
    first = lane < GRID_W
    for dr in range(RPB_R - 1):
        acc = jnp.zeros((GRID_W, 2 * GRID_W), F32)
        for d in range(RPB_C):
            val = jnp.where(first, rpb_ref[h * RPB_R + dr, d], rpb_ref[h * RPB_R + dr + 1, d])
            acc = jnp.where(dc == d, val, acc)
        tt_ref[0, dr] = acc


def _na_bias_table(rpb):
    return pl.pallas_call(
        _na_bias_kernel,
        grid=(N_HEADS,),
        in_specs=[pl.BlockSpec(memory_space=pltpu.SMEM)],
        out_specs=pl.BlockSpec((1, RPB_R - 1, GRID_W, 2 * GRID_W), lambda h: (h, 0, 0, 0)),
        out_shape=jax.ShapeDtypeStruct((N_HEADS, RPB_R - 1, GRID_W, 2 * GRID_W), F32),
        compiler_params=_cparams("arbitrary"),
        name="na_bias_table",
    )(rpb.reshape(N_HEADS * RPB_R, RPB_C))


def _lat_na_kernel(q_ref, k_ref, v_ref, ck_ref, cv_ref, tt_ref, o_ref):
    r = pl.program_id(1)
    r_start = jnp.clip(r - WIN_R // 2, 0, GRID_ROWS - WIN_R)
    dr0 = r_start - r + (WIN_R - 1)
    row0 = pl.multiple_of(r_start * GRID_W, GRID_W)
    nkeys = WIN_R * GRID_W
    past = ck_ref.shape[0]
    lane = lax.broadcasted_iota(jnp.int32, (GRID_W, nkeys + past), 1)
    kc = lane % GRID_W
    qc = lax.broadcasted_iota(jnp.int32, (GRID_W, nkeys + past), 0)
    c_start = jnp.clip(qc - WIN_W // 2, 0, GRID_W - WIN_W)
    visible = ((kc >= c_start) & (kc < c_start + WIN_W)) | (lane >= nkeys)
    no_bias = jnp.zeros((GRID_W, past), F32)
    q = [_split_pair(t) for t in _pair_tiles((q_ref[...] * ATTN_SCALE).astype(BF16))]
    kcat = _pair_tiles(jnp.concatenate([k_ref[pl.ds(row0, nkeys), :], ck_ref[...]], axis=0).astype(BF16))
    vcat = [_with_ones(t) for t in
            _pair_tiles(jnp.concatenate([v_ref[pl.ds(row0, nkeys), :], cv_ref[...]], axis=0).astype(BF16))]
    bias = [jnp.concatenate([tt_ref[h, dr0 + 2 * j] for j in range(WIN_R // 2)] + [no_bias], axis=1)
            for h in range(N_HEADS)]
    e = [[_exp_scores(jnp.where(visible, _bdot_nt(q[j][c], kcat[j]) + bias[2 * j + c], NEG_INF))
          for c in range(2)] for j in range(len(q))]
    out = [[jnp.dot(e[j][c], vcat[j], preferred_element_type=F32) for c in range(2)] for j in range(len(q))]
    for j in range(len(q)):
        o_ref[:, j * PAIR_W:(j + 1) * PAIR_W] = _merge_pair(out[j][0], out[j][1])


def _lat_na_attention(qn, kn, vn, cache_k, cache_v, tt, n_lat):
    n = qn.shape[0]
    nb = n // n_lat
    past = cache_k.shape[1]
    qblk = pl.BlockSpec((GRID_W, D_GROUP), lambda b, r: (b * GRID_ROWS + r, 0))
    kvblk = pl.BlockSpec((n_lat, D_GROUP), lambda b, r: (b, 0))
    cblk = pl.BlockSpec((None, past, D_GROUP), lambda b, r: (b, 0, 0))
    return pl.pallas_call(
        _lat_na_kernel,
        grid=(nb, GRID_ROWS),
        in_specs=[qblk, kvblk, kvblk, cblk, cblk,
                  pl.BlockSpec(tt.shape, lambda b, r: (0, 0, 0, 0))],
        out_specs=qblk,
        out_shape=jax.ShapeDtypeStruct((n, D_GROUP), F32),
        compiler_params=_cparams("arbitrary", "arbitrary"),
        name="lat_na_attention",
    )(qn, kn, vn, cache_k, cache_v, tt)


def _rec_mix(of_ref, ob_ref, rg_ref, yf_ref, yb_ref, xr_ref, xk_ref, xv_ref, lo_ref, rk_ref, lng_ref, lnb_ref,
             gup_ref):
    pw = 2 * HEAD_DIM
    brow = lax.broadcasted_iota(jnp.int32, (pw, pw), 0) // HEAD_DIM
    bcol = lax.broadcasted_iota(jnp.int32, (pw, pw), 1) // HEAD_DIM
    ones_bd = jnp.where(brow == bcol, 1.0, 0.0).astype(BF16)

    def head_sums(x):
        outs = []
        for t in range(D_GROUP // pw):
            xt = x[:, t * pw:(t + 1) * pw]
            hi = xt.astype(BF16)
            lo = (xt - hi.astype(F32)).astype(BF16)
            outs.append(jnp.dot(hi, ones_bd, preferred_element_type=F32)
                        + jnp.dot(lo, ones_bd, preferred_element_type=F32))
        return jnp.concatenate(outs, axis=1)

    inv_d = 1.0 / HEAD_DIM
    o = of_ref[...] + ob_ref[...]
    o_ret = _silu(rg_ref[...]) * (o * lax.rsqrt(head_sums(o * o) * inv_d + RMS_EPS))
    y = yf_ref[...] + yb_ref[...]
    yc = y - head_sums(y) * inv_d
    var = head_sums(yc * yc) * inv_d
    yn = yc * lax.rsqrt(var + RWKV_GN_EPS) * lng_ref[...] + lnb_ref[...]
    bonus = head_sums(xr_ref[...] * rk_ref[...] * xk_ref[...]) * xv_ref[...]
    g_rw = _bdot(_sigmoid(lo_ref[:, 2 * D_LORA:]), gup_ref[...])
    return o_ret, (yn + bonus) * g_rw


def _tail_kernel(*refs, final, ff_chunk, n_mix):
    x_ref, mix_refs = refs[0], refs[1:1 + n_mix]
    wo_ref, gm_ref, g_ref, sh_ref, sc_ref, gf_ref, wi_ref, wf_ref, fg_ref, o_ref = refs[1 + n_mix:]
    if n_mix == 2:
        mix_a, mix_b = mix_refs[0][...], mix_refs[1][...]
    else:
        mix_a, mix_b = _rec_mix(*mix_refs)
    mix = _bdot(mix_a, wo_ref[0:D_GROUP, :]) + _bdot(mix_b, wo_ref[D_GROUP:, :])
    x1 = x_ref[...] + gm_ref[0] * mix
    h = _rms(x1) * g_ref[...]
    hb = (h * (1.0 + sc_ref[0]) + sh_ref[0]).astype(BF16)
    acc = jnp.zeros_like(x1)
    for c0 in range(0, D_FF, ff_chunk):
        gate = jnp.dot(hb, wi_ref[:, c0:c0 + ff_chunk], preferred_element_type=F32)
        up = jnp.dot(hb, wi_ref[:, D_FF + c0:D_FF + c0 + ff_chunk], preferred_element_type=F32)
        acc = acc + jnp.dot((_silu(gate) * up).astype(BF16), wf_ref[c0:c0 + ff_chunk, :],
                            preferred_element_type=F32)
    x2 = x1 + gf_ref[0] * acc
    if final:
        x2 = _rms(x2) * fg_ref[...]
    o_ref[...] = x2


def _layer_tail(x, mix_inputs, w_out, gate_mix, gain, shift, scale, gate_ffn, w_ffn_in_all, w_ffn_out_all, layer,
                final_gain, rows_per_mod, final, ff_chunk=1408):
    n, d = x.shape
    tm = TAIL_TM if len(mix_inputs) == 2 else TAIL_TM_REC
    tiles_per_mod = rows_per_mod // tm
    mod_spec = pl.BlockSpec((1, 1, d), lambda i: (i // tiles_per_mod, 0, 0))
    vec_spec = pl.BlockSpec((1, d), lambda i: (0, 0))
    resident = lambda a: pl.BlockSpec(a.shape, lambda i: (0,) * a.ndim, pipeline_mode=pl.Buffered(1))
    layer_slab = lambda a: pl.BlockSpec((None,) + a.shape[1:], lambda i: (layer, 0, 0),
                                        pipeline_mode=pl.Buffered(1))
    mix_specs = [pl.BlockSpec((tm, a.shape[1]), lambda i: (i, 0)) if a.shape[0] == n else resident(a)
                 for a in mix_inputs]
    return pl.pallas_call(
        functools.partial(_tail_kernel, final=final, ff_chunk=ff_chunk, n_mix=len(mix_inputs)),
        grid=(n // tm,),
        in_specs=[pl.BlockSpec((tm, d), lambda i: (i, 0))] + mix_specs
                 + [resident(w_out), mod_spec, vec_spec, mod_spec, mod_spec, mod_spec,
                    layer_slab(w_ffn_in_all), layer_slab(w_ffn_out_all), vec_spec],
        out_specs=pl.BlockSpec((tm, d), lambda i: (i, 0)),
        out_shape=jax.ShapeDtypeStruct((n, d), F32),
        compiler_params=_cparams("arbitrary"),
        name="layer_tail",
    )(x, *mix_inputs, w_out, gate_mix, gain.reshape(1, d), shift, scale, gate_ffn, w_ffn_in_all, w_ffn_out_all,
      final_gain.reshape(1, d))


def _log_sigmoid(x):
    return jnp.minimum(x, 0.0) - jnp.log(1.0 + jnp.exp(-jnp.abs(x)))


def _pair_block_diag(xp):
    left = lax.broadcasted_iota(jnp.int32, xp.shape, 1) < HEAD_DIM
    zero = jnp.zeros_like(xp)
    return jnp.concatenate([jnp.where(left, xp, zero), jnp.where(left, zero, xp)], axis=0)


def _states_to_pairs(state):
    nb = state.shape[0]
    s = state.reshape(nb, 2, N_HEADS // 2, 2, HEAD_DIM, HEAD_DIM)
    zero = jnp.zeros_like(s[:, :, :, 0])
    return jnp.concatenate([jnp.concatenate([s[:, :, :, 0], zero], axis=-1),
                            jnp.concatenate([zero, s[:, :, :, 1]], axis=-1)], axis=-2)


def _load_states(st, s0_ref):
    st[...] = jnp.zeros(st.shape, st.dtype) if s0_ref is None else s0_ref[0]


def _store_states(sf_ref, st):
    for d in range(st.shape[0]):
        for p in range(st.shape[1]):
            s = st[d, p]
            sf_ref[0, d, 2 * p] = s[:HEAD_DIM, :HEAD_DIM]
            sf_ref[0, d, 2 * p + 1] = s[HEAD_DIM:, HEAD_DIM:]


def _ret_kernel(*refs, zero_init):
    dlm_ref, dlq_ref, qf_ref, kf_ref, vf_ref, qb_ref, kb_ref, vb_ref = refs[:8]
    s0_ref = None if zero_init else refs[8]
    of_ref, ob_ref, sf_ref, st, mask_scr, qdec_scr, kdec_scr, cdec_scr = refs[8 if zero_init else 9:]
    c = pl.program_id(1)
    cs = RET_CHUNK
    pw = 2 * HEAD_DIM
    npair = N_HEADS // 2
    chains = [(d, p) for d in range(2) for p in range(npair)]

    @pl.when(c == 0)
    def _():
        _load_states(st, s0_ref)

    @pl.when((pl.program_id(0) == 0) & (c == 0))
    def _():
        row = lax.broadcasted_iota(jnp.int32, (cs, 2 * cs), 0)
        col = lax.broadcasted_iota(jnp.int32, (cs, 2 * cs), 1) % cs
        tok = lax.broadcasted_iota(jnp.int32, (cs, pw), 0)
        dist = [row - col, col - row]
        tq = [tok.astype(F32), (cs - 1 - tok).astype(F32)]
        for i, (d, p) in enumerate(chains):
            lgm = _log_sigmoid(dlm_ref[d, p])[0:1, :]
            lgq = _log_sigmoid(dlq_ref[d, p])
            mask_scr[i] = jnp.where(dist[d] >= 0, jnp.exp(jnp.maximum(dist[d], 0).astype(F32) * lgm), 0.0)
            qdec_scr[i] = jnp.exp((tq[d] + 1.0) * lgq[0:1, :])
            kdec_scr[i] = jnp.exp((cs - 1.0 - tq[d]) * lgq[0:1, :])
            cdec_scr[i] = jnp.exp(cs * lgq)

    brow = lax.broadcasted_iota(jnp.int32, (pw, pw), 0) // HEAD_DIM
    bcol = lax.broadcasted_iota(jnp.int32, (pw, pw), 1) // HEAD_DIM
    same_head = brow == bcol
    refs = [(qf_ref, kf_ref, vf_ref, of_ref), (qb_ref, kb_ref, vb_ref, ob_ref)]
    nsub = RET_CHUNKS_PER_STEP
    work = [(t, sub) for sub in range(nsub) for t in range(len(chains))]
    sl_of = lambda t: slice(chains[t][1] * pw, (chains[t][1] + 1) * pw)
    rows_of = lambda sub: pl.ds(sub * cs, cs)
    q = {(t, sub): refs[chains[t][0]][0][rows_of(sub), sl_of(t)] for t, sub in work}
    k = {(t, sub): refs[chains[t][0]][1][rows_of(sub), sl_of(t)] * ATTN_SCALE for t, sub in work}
    v = {(t, sub): refs[chains[t][0]][2][rows_of(sub), sl_of(t)] for t, sub in work}
    s = {w: _bdot_nt(q[w], _pair_block_diag(k[w])) * mask_scr[w[0]] for w in work}
    inner = {w: _bdot(s[w], _pair_block_diag(v[w])) for w in work}
    kv = {w: _bdot_tn(k[w] * kdec_scr[w[0]], v[w]) for w in work}
    for step in range(nsub):
        now = [(t, (step if chains[t][0] == 0 else nsub - 1 - step)) for t in range(len(chains))]
        state = {w: st[chains[w[0]]] for w in now}
        cross = {w: _bdot(q[w], state[w]) * qdec_scr[w[0]] for w in now}
        for w in now:
            t, sub = w
            d, p = chains[t]
            refs[d][3][rows_of(sub), sl_of(t)] = inner[w] + cross[w]
            st[d, p] = state[w] * cdec_scr[t][0:1, :] + jnp.where(same_head, kv[w], 0.0)

    @pl.when(c == pl.num_programs(1) - 1)
    def _():
        _store_states(sf_ref, st)


def _retention_scan(decay_logit, q, k, v, state0, seq):
    n = q.shape[0]
    nb = n // seq
    rows = RET_CHUNK * RET_CHUNKS_PER_STEP
    nc = seq // rows
    npair, pw = N_HEADS // 2, 2 * HEAD_DIM
    dl = decay_logit.astype(F32)
    dl_m = jnp.broadcast_to(jnp.repeat(dl, RET_CHUNK, axis=1).reshape(2, npair, 1, 2 * RET_CHUNK),
                            (2, npair, 8, 2 * RET_CHUNK))
    dl_q = jnp.broadcast_to(jnp.repeat(dl, HEAD_DIM, axis=1).reshape(2, npair, 1, pw), (2, npair, 8, pw))
    blk_f = pl.BlockSpec((rows, D_GROUP), lambda b, c: (b * nc + c, 0))
    blk_b = pl.BlockSpec((rows, D_GROUP), lambda b, c: (b * nc + nc - 1 - c, 0))
    st_in = ([] if state0 is None else
             [pl.BlockSpec((1, 2, npair, pw, pw), lambda b, c: (b, 0, 0, 0, 0))])
    st_out = pl.BlockSpec((1, 2, N_HEADS, HEAD_DIM, HEAD_DIM), lambda b, c: (b, 0, 0, 0, 0))
    return pl.pallas_call(
        functools.partial(_ret_kernel, zero_init=state0 is None),
        grid=(nb, nc),
        in_specs=[pl.BlockSpec(dl_m.shape, lambda b, c: (0, 0, 0, 0)),
                  pl.BlockSpec(dl_q.shape, lambda b, c: (0, 0, 0, 0)),
                  blk_f, blk_f, blk_f, blk_b, blk_b, blk_b] + st_in,
        out_specs=[blk_f, blk_b, st_out],
        out_shape=[jax.ShapeDtypeStruct((n, D_GROUP), F32), jax.ShapeDtypeStruct((n, D_GROUP), F32),
                   jax.ShapeDtypeStruct((nb, 2, N_HEADS, HEAD_DIM, HEAD_DIM), F32)],
        scratch_shapes=[pltpu.VMEM((2, npair, pw, pw), F32),
                        pltpu.VMEM((2 * npair, RET_CHUNK, 2 * RET_CHUNK), F32),
                        pltpu.VMEM((2 * npair, RET_CHUNK, pw), F32),
                        pltpu.VMEM((2 * npair, RET_CHUNK, pw), F32),
                        pltpu.VMEM((2 * npair, 8, pw), F32)],
        compiler_params=_cparams("arbitrary", "arbitrary"),
        name="retention_scan",
    )(dl_m, dl_q, q, k, v, q, k, v, *([] if state0 is None else [_states_to_pairs(state0)]))


def _rwkv_kernel(*refs, zero_init):
    (xrf_ref, xkf_ref, xvf_ref, lof_ref, xrb_ref, xkb_ref, xvb_ref, lob_ref,
     w0_ref, wup_ref, a0_ref, aup_ref, kk_ref, ka_ref) = refs[:14]
    s0_ref = None if zero_init else refs[14]
    yf_ref, yb_ref, sf_ref, st = refs[14 if zero_init else 15:]
    c = pl.program_id(1)
    cs = RWKV_CHUNK
    nsub = RWKV_CHUNKS_PER_STEP
    npair = N_HEADS // 2

    @pl.when(c == 0)
    def _():
        _load_states(st, s0_ref)

    row = lax.broadcasted_iota(jnp.int32, (cs, 3 * cs), 0)
    col = lax.broadcasted_iota(jnp.int32, (cs, 3 * cs), 1) % cs
    pw = 2 * HEAD_DIM
    prow = lax.broadcasted_iota(jnp.int32, (cs, pw), 0)
    pcol = lax.broadcasted_iota(jnp.int32, (cs, pw), 1) % HEAD_DIM
    eye = jnp.where(pcol == prow, 1.0, 0.0).astype(F32)
    sub_diag = (prow // RWKV_SUB) == (pcol // RWKV_SUB)
    brow = lax.broadcasted_iota(jnp.int32, (pw, pw), 0) // HEAD_DIM
    bcol = lax.broadcasted_iota(jnp.int32, (pw, pw), 1) // HEAD_DIM
    same_head = brow == bcol

    bd = _pair_block_diag

    def direction(d, sub, xr_ref, xk_ref, xv_ref, lo_ref):
        rows = pl.ds(sub * cs, cs)
        tri3 = jnp.where((row >= col) if d == 0 else (col >= row), 1.0, 0.0).astype(BF16)
        dist = (prow - pcol) if d == 0 else (pcol - prow)
        xk = xk_ref[rows, :]
        lo = lo_ref[rows, :]
        w_log = -math.exp(-0.5) * _sigmoid(w0_ref[d] + _bdot(jnp.tanh(lo[:, 0:D_LORA]), wup_ref[d]))
        a_all = _sigmoid(a0_ref[d] + _bdot(lo[:, D_LORA:2 * D_LORA], aup_ref[d]))
        w1 = w_log.astype(BF16)
        r1 = w_log - w1.astype(F32)
        w2 = r1.astype(BF16)
        w3 = (r1 - w2.astype(F32)).astype(BF16)
        cum = jnp.dot(tri3, jnp.concatenate([w1, w2, w3], axis=0), preferred_element_type=F32)
        cum_end = cum[cs - 1:cs, :] if d == 0 else cum[0:1, :]
        return dict(earlier=dist > 0, upto=dist >= 0, xr=xr_ref[rows, :], xv=xv_ref[rows, :], a_all=a_all,
                    e_incl=jnp.exp(cum), e_excl=jnp.exp(cum - w_log), e_neg=jnp.exp(-cum),
                    e_end=jnp.exp(cum_end - cum), w_end=jnp.exp(cum_end), kk_all=xk * kk_ref[...],
                    keff_all=xk * (1.0 + (a_all - 1.0) * ka_ref[...]))

    in_refs = [(xrf_ref, xkf_ref, xvf_ref, lof_ref), (xrb_ref, xkb_ref, xvb_ref, lob_ref)]
    dirs = {(d, sub): direction(d, sub, *in_refs[d]) for d in range(2) for sub in range(nsub)}
    y_refs = [yf_ref, yb_ref]

    chains = [(d, sub, p) for d in range(2) for sub in range(nsub) for p in range(npair)]
    pairs = range(len(chains))
    sls = [slice(p * pw, (p + 1) * pw) for _, _, p in chains]
    pick = lambda name: [dirs[d, sub][name][:, sls[i]] for i, (d, sub, _) in enumerate(chains)]
    earlier = [dirs[d, sub]["earlier"] for d, sub, _ in chains]
    upto = [dirs[d, sub]["upto"] for d, sub, _ in chains]
    cat = lambda x, y: jnp.concatenate([x, y], axis=0)

    def head_sum(x):
        left = lax.broadcasted_iota(jnp.int32, x.shape, 1) < HEAD_DIM
        zero = jnp.zeros_like(x)
        s_left = jnp.sum(jnp.where(left, x, zero), axis=-1, keepdims=True)
        s_right = jnp.sum(jnp.where(left, zero, x), axis=-1, keepdims=True)
        return jnp.where(left, s_left, s_right)

    e_incl, e_excl, e_neg, e_end, w_end = (pick(k) for k in ("e_incl", "e_excl", "e_neg", "e_end", "w_end"))
    kk = [x * lax.rsqrt(head_sum(x * x) + 1e-12) for x in pick("kk_all")]
    b = [kk[p] * a for p, a in zip(pairs, pick("a_all"))]
    keff = pick("keff_all")
    v = pick("xv")
    a_t = [-kk[p] * e_excl[p] for p in pairs]
    r_t = [xr * e_incl[p] for p, xr in zip(pairs, pick("xr"))]
    ar = [cat(a_t[p], r_t[p]).astype(BF16) for p in pairs]
    g = [_bdot_nt(ar[p], cat(bd(b[p] * e_neg[p]), bd(keff[p] * e_neg[p]))) for p in pairs]
    n_mat = [jnp.where(earlier[p], g[p][:cs, :pw], 0.0) for p in pairs]
    m_rb = [jnp.where(upto[p], g[p][cs:, :pw], 0.0) for p in pairs]
    a_ak = [jnp.where(earlier[p], g[p][:cs, pw:], 0.0) for p in pairs]
    m_rk = [jnp.where(upto[p], g[p][cs:, pw:], 0.0) for p in pairs]
    bd_s = lambda xs: (bd(xs[0]), bd(xs[1]))
    n_d = [jnp.where(sub_diag, n, 0.0) for n in n_mat]
    n_o = [jnp.where(sub_diag, 0.0, n) for n in n_mat]
    n_s = [_split2(n) for n in n_d]
    x = [eye + n for n in n_d]
    pk = [_mm3s(ns, bd_s(ns)) for ns in n_s]
    for _ in range(RWKV_SUB_DOUBLINGS - 1):
        xp = [_mm3s(_split2(cat(x[p], pk[p])), bd_s(_split2(pk[p]))) for p in pairs]
        x = [x[p] + xp[p][:cs] for p in pairs]
        pk = [r[cs:] for r in xp]
    t_d = [x[p] + _mm3s(_split2(x[p]), bd_s(_split2(pk[p]))) for p in pairs]
    lcat = lambda x, y: jnp.concatenate([x, y], axis=1)
    m_o = [_bdot(t_d[p], bd(n_o[p])) for p in pairs]
    mz = [_bdot(m_o[p], lcat(bd(t_d[p]), bd(m_o[p]))) for p in pairs]
    z = [t_d[p] + mz[p][:, :pw] for p in pairs]
    t_inv = [(z[p] + _bdot(mz[p][:, pw:], bd(z[p]))).astype(BF16) for p in pairs]
    av = [_bdot(cat(a_ak[p], m_rk[p]), bd(v[p])) for p in pairs]
    tau = [_bdot(t_inv[p], lcat(bd(a_t[p]), bd(av[p][:cs]))) for p in pairs]
    bk_end = [cat(b[p] * e_end[p], keff[p] * e_end[p]) for p in pairs]
    for step in range(nsub):
        now = [i for i, (d, sub, _) in enumerate(chains) if sub == (step if d == 0 else nsub - 1 - step)]
        s_prev = {i: st[chains[i][0], chains[i][2]] for i in now}
        hs = {i: _bdot_nt(cat(tau[i][:, :pw], r_t[i]), s_prev[i]) for i in now}
        u = {i: hs[i][:cs] + tau[i][:, pw:] for i in now}
        y = {i: hs[i][cs:] + _bdot(m_rb[i], bd(u[i])) + av[i][cs:] for i in now}
        for i in now:
            d, sub, p = chains[i]
            y_refs[d][pl.ds(sub * cs, cs), sls[i]] = y[i]
            st[d, p] = s_prev[i] * w_end[i] + jnp.where(same_head, _bdot_tn(cat(u[i], v[i]), bk_end[i]), 0.0)

    @pl.when(c == pl.num_programs(1) - 1)
    def _():
        _store_states(sf_ref, st)


def _rwkv_scan(xr, xk, xv, lora, w0, w_up, a0, a_up, k_k, k_a, state0, seq):
    n = xr.shape[0]
    nb = n // seq
    blk = RWKV_CHUNK * RWKV_CHUNKS_PER_STEP
    nc = seq // blk

    fwd_idx = lambda b, c: (b * nc + c, 0)
    bwd_idx = lambda b, c: (b * nc + nc - 1 - c, 0)
    blk_f = pl.BlockSpec((blk, D_GROUP), fwd_idx)
    blk_b = pl.BlockSpec((blk, D_GROUP), bwd_idx)
    lblk_f = pl.BlockSpec((blk, lora.shape[1]), fwd_idx)
    lblk_b = pl.BlockSpec((blk, lora.shape[1]), bwd_idx)
    dvec = pl.BlockSpec((2, 1, D_GROUP), lambda b, c: (0, 0, 0))
    dmat = pl.BlockSpec((2, D_LORA, D_GROUP), lambda b, c: (0, 0, 0))
    vec = pl.BlockSpec((1, D_GROUP), lambda b, c: (0, 0))
    npair, pw = N_HEADS // 2, 2 * HEAD_DIM
    st_in = ([] if state0 is None else
             [pl.BlockSpec((1, 2, npair, pw, pw), lambda b, c: (b, 0, 0, 0, 0))])
    st_out = pl.BlockSpec((1, 2, N_HEADS, HEAD_DIM, HEAD_DIM), lambda b, c: (b, 0, 0, 0, 0))
    return pl.pallas_call(
        functools.partial(_rwkv_kernel, zero_init=state0 is None),
        grid=(nb, nc),
        in_specs=[blk_f, blk_f, blk_f, lblk_f, blk_b, blk_b, blk_b, lblk_b,
                  dvec, dmat, dvec, dmat, vec, vec] + st_in,
        out_specs=[blk_f, blk_b, st_out],
        out_shape=[jax.ShapeDtypeStruct((n, D_GROUP), F32), jax.ShapeDtypeStruct((n, D_GROUP), F32),
                   jax.ShapeDtypeStruct((nb, 2, N_HEADS, HEAD_DIM, HEAD_DIM), F32)],
        scratch_shapes=[pltpu.VMEM((2, npair, pw, pw), F32)],
        compiler_params=_cparams("arbitrary", "arbitrary"),
        name="rwkv7_scan",
    )(xr, xk, xv, lora, xr, xk, xv, lora, w0.reshape(2, 1, D_GROUP), w_up, a0.reshape(2, 1, D_GROUP), a_up,
      k_k.reshape(1, D_GROUP), k_a.reshape(1, D_GROUP), *([] if state0 is None else [_states_to_pairs(state0)]))


ATTN_WIDTHS = (D_GROUP,) * 6
REC_WIDTHS = (D_GROUP,) * 7 + (2 * D_LORA + D_LORA_G,)


def kernel(x_prompt, x_sample, cache_na_k, cache_na_v, cache_diff_k, cache_diff_v, state_ret, state_rwkv, c, c_ctx, norm_mix_g, norm_ffn_g, norm_final_g, w_ada, b_ada, w_in_attn, w_out_attn, na_rpb, diff_lq1, diff_lk1, diff_lq2, diff_lk2, w_in_rec, w_out_rec, ret_decay_logit, rw_w0, rw_w_up, rw_a0, rw_a_up, rw_g_up, rw_k_k, rw_k_a, rw_r_k, rw_ln_g, rw_ln_b, w_ffn_in, w_ffn_out):
    bp, seq, d = x_prompt.shape
    bl, n_lat, _ = x_sample.shape
    depth = w_ada.shape[0]
    ctx = x_prompt.reshape(bp * seq, d)
    lat = x_sample.reshape(bl * n_lat, d)

    cond8 = jnp.zeros((8, d), F32).at[0].set(c_ctx).at[1:1 + bl].set(c)
    mods = _ada_modulation(cond8, w_ada, b_ada).reshape(depth, 8, 6, d)
    w_ffn_in_bf16 = w_ffn_in.astype(BF16)
    w_ffn_out_bf16 = w_ffn_out.astype(BF16)

    outs = {}
    for layer in range(depth):
        m_ctx = [mods[layer, 0:1, j].reshape(1, 1, d) for j in range(6)]
        m_lat = [mods[layer, 1:1 + bl, j].reshape(bl, 1, d) for j in range(6)]
        if layer % 2 == 0:
            i = layer // 2
            lam_init = 0.8 - 0.6 * math.exp(-0.3 * layer)
            lam_params = jnp.stack([diff_lq1[i], diff_lk1[i], diff_lq2[i], diff_lk2[i]]).astype(F32)
            w_in = w_in_attn[i].astype(BF16)
            w_out = w_out_attn[i].astype(BF16)
            res = _ctx_attention(ctx, norm_mix_g[layer], m_ctx[0], m_ctx[1], w_in, lam_params, seq, lam_init)
            mix_ctx = res[:2]
            outs["na_k"], outs["na_v"], outs["df_k"], outs["df_v"] = res[2:]
            plat = _norm_mod_matmul(lat, norm_mix_g[layer], m_lat[0], m_lat[1], w_in, ATTN_WIDTHS, n_lat, PROJ_TM)
            past = cache_na_k.shape[2]
            tt = _na_bias_table(na_rpb[i])
            o_na = _lat_na_attention(plat[0], plat[1], plat[2],
                                     cache_na_k[:, i].reshape(bl, past, D_GROUP),
                                     cache_na_v[:, i].reshape(bl, past, D_GROUP), tt, n_lat)
            o_df = _lat_diff_attention(lam_params, plat[3], plat[4], plat[5],
                                       cache_diff_k[:, i].reshape(bl, past, D_GROUP),
                                       cache_diff_v[:, i].reshape(bl, past, D_GROUP), n_lat, lam_init)
            mix_lat = (o_na, o_df)
        else:
            j = layer // 2
            w_in = w_in_rec[j].astype(BF16)
            w_out = w_out_rec[j].astype(BF16)
            pc = _norm_mod_matmul(ctx, norm_mix_g[layer], m_ctx[0], m_ctx[1], w_in, REC_WIDTHS, bp * seq, PROJ_TM)
            plat = _norm_mod_matmul(lat, norm_mix_g[layer], m_lat[0], m_lat[1], w_in, REC_WIDTHS, n_lat, PROJ_TM)
            mixes = []
            for is_ctx, p, s_ret0, s_rw0, sq in ((True, pc, None, None, seq),
                                                 (False, plat, state_ret[:, j], state_rwkv[:, j], n_lat)):
                rq, rk, rv, rg, wr, wk, wv, lora = p
                o_f, o_b, s_ret = _retention_scan(ret_decay_logit[j], rq, rk, rv, s_ret0, sq)
                y_f, y_b, s_rw = _rwkv_scan(wr, wk, wv, lora, rw_w0[j], rw_w_up[j].astype(BF16), rw_a0[j],
                                            rw_a_up[j].astype(BF16), rw_k_k[j], rw_k_a[j], s_rw0, sq)
                mixes.append((o_f, o_b, rg, y_f, y_b, wr, wk, wv, lora, rw_r_k[j].reshape(1, D_GROUP),
                              rw_ln_g[j].reshape(1, D_GROUP), rw_ln_b[j].reshape(1, D_GROUP),
                              rw_g_up[j].astype(BF16)))
                if is_ctx:
                    outs["ret"], outs["rwkv"] = s_ret, s_rw
            mix_ctx, mix_lat = mixes
        final = layer == depth - 1
        ctx = _layer_tail(ctx, mix_ctx, w_out, m_ctx[2], norm_ffn_g[layer], m_ctx[3], m_ctx[4], m_ctx[5],
                          w_ffn_in_bf16, w_ffn_out_bf16, layer, norm_final_g, bp * seq, final)
        lat = _layer_tail(lat, mix_lat, w_out, m_lat[2], norm_ffn_g[layer], m_lat[3], m_lat[4], m_lat[5],
                          w_ffn_in_bf16, w_ffn_out_bf16, layer, norm_final_g, n_lat, final)

    y_prompt = ctx.reshape(bp, seq, d)
    y_sample = lat.reshape(bl, n_lat, d)
    new_cache_na_k = outs["na_k"].reshape(bp, 1, seq, N_HEADS, HEAD_DIM)
    new_cache_na_v = outs["na_v"].reshape(bp, 1, seq, N_HEADS, HEAD_DIM)
    new_cache_diff_k = outs["df_k"].reshape(bp, 1, seq, N_HEADS // 2, 2, HEAD_DIM)
    new_cache_diff_v = outs["df_v"].reshape(bp, 1, seq, N_HEADS // 2, 2 * HEAD_DIM)
    new_state_ret = outs["ret"].reshape(bp, 1, 2, N_HEADS, HEAD_DIM, HEAD_DIM)
    new_state_rwkv = outs["rwkv"].reshape(bp, 1, 2, N_HEADS, HEAD_DIM, HEAD_DIM)
    return (y_prompt, y_sample, new_cache_na_k, new_cache_na_v, new_cache_diff_k, new_cache_diff_v,
            new_state_ret, new_state_rwkv)
```

```python
import functools
import math

import jax
import jax.numpy as jnp
from jax import lax
from jax.experimental import pallas as pl
from jax.experimental.pallas import tpu as pltpu

F32 = jnp.float32
BF16 = jnp.bfloat16

D_MODEL = 1024
HEAD_DIM = 64
N_HEADS = 8
D_GROUP = 512
GRID_W = 64
GRID_ROWS = 32
WIN_R = 8
WIN_W = 16
RPB_R = 15
RPB_C = 31
PAST_LEN = 256
D_FF = 2816
D_LORA = 64
D_LORA_G = 128
ROPE_BASE = 10000.0
RMS_EPS = 1e-6
RWKV_GN_EPS = 64e-5
NEG_INF = -1e30
ATTN_SCALE = HEAD_DIM ** -0.5
RET_CHUNK = 128
RET_CHUNKS_PER_STEP = 2
RWKV_CHUNK = 64
RWKV_CHUNKS_PER_STEP = 4
RWKV_SUB = 16
RWKV_SUB_DOUBLINGS = 3
VMEM_LIMIT = 56 * 1024 * 1024
TAIL_TM = 512
TAIL_TM_REC = 256
PROJ_TM = 512
LAT_DIFF_TQ = 256


def _cparams(*sem):
    return pltpu.CompilerParams(dimension_semantics=sem, vmem_limit_bytes=VMEM_LIMIT)


def _sigmoid(x):
    return 1.0 / (1.0 + jnp.exp(-x))


def _silu(x):
    return x * _sigmoid(x)


def _rms(x, eps=RMS_EPS):
    return x * lax.rsqrt(jnp.mean(x * x, axis=-1, keepdims=True) + eps)


def _bdot(a, b):
    return jnp.dot(a.astype(BF16), b.astype(BF16), preferred_element_type=F32)


def _bdot_nt(a, b):
    return lax.dot_general(a.astype(BF16), b.astype(BF16), (((1,), (1,)), ((), ())),
                           preferred_element_type=F32)


def _bdot_tn(a, b):
    return lax.dot_general(a.astype(BF16), b.astype(BF16), (((0,), (0,)), ((), ())),
                           preferred_element_type=F32)


_DN = {"nn": (((1,), (0,)), ((), ())), "nt": (((1,), (1,)), ((), ())), "tn": (((0,), (0,)), ((), ()))}


def _split2(x):
    hi = x.astype(BF16)
    lo = (x - hi.astype(F32)).astype(BF16)
    return hi, lo


def _mm3s(a_split, b_split):
    ah, al = a_split
    bh, bl = b_split
    n = bh.shape[1]
    rhs = jnp.concatenate([jnp.concatenate([bh, bl], axis=1),
                           jnp.concatenate([bh, jnp.zeros_like(bl)], axis=1)], axis=0)
    out = jnp.dot(jnp.concatenate([ah, al], axis=1), rhs, preferred_element_type=F32)
    return out[:, :n] + out[:, n:]


def _diff_lambda(lp, lam_init):
    s1 = jnp.sum(lp[0:1, :] * lp[1:2, :], axis=-1, keepdims=True)
    s2 = jnp.sum(lp[2:3, :] * lp[3:4, :], axis=-1, keepdims=True)
    return jnp.exp(s1) - jnp.exp(s2) + lam_init


def _ada_kernel(c_ref, w_ref, b_ref, o_ref):
    s = _silu(c_ref[...])
    o_ref[0] = _bdot(s, w_ref[0]) + b_ref[0]


def _ada_modulation(cond8, w_ada, b_ada):
    depth, d, n = w_ada.shape
    tn = 1536
    return pl.pallas_call(
        _ada_kernel,
        grid=(depth, n // tn),
        in_specs=[pl.BlockSpec((8, d), lambda l, j: (0, 0)),
                  pl.BlockSpec((1, d, tn), lambda l, j: (l, 0, j)),
                  pl.BlockSpec((1, 1, tn), lambda l, j: (l, 0, j))],
        out_specs=pl.BlockSpec((1, 8, tn), lambda l, j: (l, 0, j)),
        out_shape=jax.ShapeDtypeStruct((depth, 8, n), F32),
        compiler_params=_cparams("arbitrary", "arbitrary"),
        name="ada_modulation",
    )(cond8, w_ada, b_ada.reshape(depth, 1, n))


def _norm_mod(x_ref, g_ref, sh_ref, sc_ref):
    h = _rms(x_ref[...]) * g_ref[...]
    return (h * (1.0 + sc_ref[0]) + sh_ref[0]).astype(BF16)


def _nmm_kernel(x_ref, g_ref, sh_ref, sc_ref, w_ref, *o_refs, widths):
    hb = _norm_mod(x_ref, g_ref, sh_ref, sc_ref)
    off = 0
    for o_ref, w in zip(o_refs, widths):
        o_ref[...] = jnp.dot(hb, w_ref[:, off:off + w], preferred_element_type=F32)
        off += w


def _norm_mod_matmul(x, gain, shift, scale, w_bf16, widths, rows_per_mod, tm):
    n, d = x.shape
    tiles_per_mod = rows_per_mod // tm
    mod_spec = pl.BlockSpec((1, 1, d), lambda i: (i // tiles_per_mod, 0, 0))
    return pl.pallas_call(
        functools.partial(_nmm_kernel, widths=widths),
        grid=(n // tm,),
        in_specs=[pl.BlockSpec((tm, d), lambda i: (i, 0)),
                  pl.BlockSpec((1, d), lambda i: (0, 0)),
                  mod_spec, mod_spec,
                  pl.BlockSpec(w_bf16.shape, lambda i: (0, 0), pipeline_mode=pl.Buffered(1))],
        out_specs=[pl.BlockSpec((tm, w), lambda i: (i, 0)) for w in widths],
        out_shape=[jax.ShapeDtypeStruct((n, w), F32) for w in widths],
        compiler_params=_cparams("arbitrary"),
        name="norm_mod_matmul",
    )(x, gain.reshape(1, d), shift, scale, w_bf16)


PAIR_W = 2 * HEAD_DIM


def _pair_tiles(x):
    return [x[:, j * PAIR_W:(j + 1) * PAIR_W] for j in range(x.shape[1] // PAIR_W)]


def _split_pair(tile):
    left = lax.broadcasted_iota(jnp.int32, tile.shape, 1) < HEAD_DIM
    zero = jnp.zeros_like(tile)
    return jnp.where(left, tile, zero), jnp.where(left, zero, tile)


def _pair_head_sum(tile):
    left = lax.broadcasted_iota(jnp.int32, tile.shape, 1) < HEAD_DIM
    zero = jnp.zeros_like(tile)
    s_left = jnp.sum(jnp.where(left, tile, zero), axis=-1, keepdims=True)
    s_right = jnp.sum(jnp.where(left, zero, tile), axis=-1, keepdims=True)
    return jnp.where(left, s_left, s_right)


def _with_ones(v):
    return jnp.concatenate([v, jnp.ones_like(v)], axis=1)


def _exp_scores(s):
    return jnp.exp(s - jnp.max(s, axis=-1, keepdims=True)).astype(BF16)


def _merge_pair(out_even, out_odd):
    left = lax.broadcasted_iota(jnp.int32, (out_even.shape[0], PAIR_W), 1) < HEAD_DIM
    return jnp.where(left, out_even[:, :PAIR_W] * (1.0 / out_even[:, PAIR_W:]),
                     out_odd[:, :PAIR_W] * (1.0 / out_odd[:, PAIR_W:]))


def _diff_combine(out1, out2, lam, lam_init):
    o = out1[:, :PAIR_W] * (1.0 / out1[:, PAIR_W:]) - out2[:, :PAIR_W] * (lam / out2[:, PAIR_W:])
    return _rms(o) * (1.0 - lam_init)


CTX_CACHE_SHAPES = ((N_HEADS, HEAD_DIM), (N_HEADS, HEAD_DIM),
                    (N_HEADS // 2, 2, HEAD_DIM), (N_HEADS // 2, 2 * HEAD_DIM))


def _ctx_attn_kernel(x_ref, g_ref, sh_ref, sc_ref, w_ref, lp_ref, on_ref, od_ref, *cache_refs, lam_init):
    hb = _norm_mod(x_ref, g_ref, sh_ref, sc_ref)
    qn, kn, vn, qd, kd, vd = (jnp.dot(hb, w_ref[:, i * D_GROUP:(i + 1) * D_GROUP], preferred_element_type=F32)
                              for i in range(6))
    for c_ref, val in zip(cache_refs, (kn, vn, kd, vd)):
        c_ref[...] = val.reshape(c_ref.shape)
    tiles = range(D_GROUP // PAIR_W)
    lam = _diff_lambda(lp_ref[...], lam_init)
    q = [_split_pair(t) for t in _pair_tiles((qn * ATTN_SCALE).astype(BF16))
         + _pair_tiles((qd * ATTN_SCALE).astype(BF16))]
    k = _pair_tiles(kn.astype(BF16)) + _pair_tiles(kd.astype(BF16))
    v = [_with_ones(t) for t in _pair_tiles(vn.astype(BF16)) + _pair_tiles(vd.astype(BF16))]
    e = [[_exp_scores(_bdot_nt(q[j][c], k[j])) for c in range(2)] for j in range(2 * len(tiles))]
    out = [[jnp.dot(e[j][c], v[j], preferred_element_type=F32) for c in range(2)] for j in range(2 * len(tiles))]
    for j in tiles:
        sl = slice(j * PAIR_W, (j + 1) * PAIR_W)
        on_ref[:, sl] = _merge_pair(out[j][0], out[j][1])
        od_ref[:, sl] = _diff_combine(out[len(tiles) + j][0], out[len(tiles) + j][1], lam, lam_init)


def _ctx_attention(x, gain, shift, scale, w_bf16, lam_params, seq, lam_init):
    n, d = x.shape
    blk = pl.BlockSpec((seq, D_GROUP), lambda b: (b, 0))
    mod_spec = pl.BlockSpec((1, 1, d), lambda b: (0, 0, 0))
    cache_specs = [pl.BlockSpec((seq,) + dims, lambda b, nd=len(dims): (b,) + (0,) * nd)
                   for dims in CTX_CACHE_SHAPES]
    return pl.pallas_call(
        functools.partial(_ctx_attn_kernel, lam_init=lam_init),
        grid=(n // seq,),
        in_specs=[pl.BlockSpec((seq, d), lambda b: (b, 0)),
                  pl.BlockSpec((1, d), lambda b: (0, 0)),
                  mod_spec, mod_spec,
                  pl.BlockSpec(w_bf16.shape, lambda b: (0, 0), pipeline_mode=pl.Buffered(1)),
                  pl.BlockSpec(lam_params.shape, lambda b: (0, 0))],
        out_specs=[blk, blk] + cache_specs,
        out_shape=([jax.ShapeDtypeStruct((n, D_GROUP), F32)] * 2
                   + [jax.ShapeDtypeStruct((n,) + dims, F32) for dims in CTX_CACHE_SHAPES]),
        compiler_params=_cparams("arbitrary"),
        name="ctx_attention",
    )(x, gain.reshape(1, d), shift, scale, w_bf16, lam_params)


def _rope_tables(n):
    quarter = HEAD_DIM // 4
    pos = jnp.arange(n)
    inv_freq = ROPE_BASE ** (-jnp.arange(quarter, dtype=F32) / quarter)
    lane = jnp.arange(HEAD_DIM)
    p = jnp.where(lane[None, :] < HEAD_DIM // 2, (pos // GRID_W)[:, None], (pos % GRID_W)[:, None]).astype(F32)
    ang = p * inv_freq[lane % quarter][None, :]
    sign = jnp.where((lane % (2 * quarter)) < quarter, -1.0, 1.0).astype(F32)
    cos = jnp.tile(jnp.cos(ang), (1, 2))
    sin = jnp.tile(jnp.sin(ang) * sign[None, :], (1, 2))
    return cos, sin


def _rope(x, cos, sin_signed):
    quarter = HEAD_DIM // 4
    lane = lax.broadcasted_iota(jnp.int32, cos.shape, 1)
    first = (lane % (2 * quarter)) < quarter
    tiles = []
    for t in _pair_tiles(x):
        partner = jnp.where(first, pltpu.roll(t, PAIR_W - quarter, 1), pltpu.roll(t, quarter, 1))
        tiles.append(t * cos + partner * sin_signed)
    return jnp.concatenate(tiles, axis=1)


def _lat_diff_kernel(lp_ref, q_ref, k_ref, v_ref, ck_ref, cv_ref, cosq_ref, sinq_ref, cosk_ref, sink_ref,
                     o_ref, kall, vall, *, lam_init, n_lat):
    @pl.when(pl.program_id(1) == 0)
    def _():
        kall[0:n_lat, :] = _rope(k_ref[...], cosk_ref[...], sink_ref[...]).astype(BF16)
        kall[n_lat:, :] = ck_ref[...].astype(BF16)
        for j, (vt, ct) in enumerate(zip(_pair_tiles(v_ref[...].astype(BF16)),
                                         _pair_tiles(cv_ref[...].astype(BF16)))):
            vall[j, 0:n_lat, :] = _with_ones(vt)
            vall[j, n_lat:, :] = _with_ones(ct)

    lam = _diff_lambda(lp_ref[...], lam_init)
    q = (_rope(q_ref[...], cosq_ref[...], sinq_ref[...]) * ATTN_SCALE).astype(BF16)
    qs = [_split_pair(qt) for qt in _pair_tiles(q)]
    sls = [slice(j * PAIR_W, (j + 1) * PAIR_W) for j in range(len(qs))]
    for group in ((0, 1), (2, 3)):
        e = {j: [_exp_scores(_bdot_nt(qs[j][c], kall[:, sls[j]])) for c in range(2)] for j in group}
        out = {j: [jnp.dot(e[j][c], vall[j], preferred_element_type=F32) for c in range(2)] for j in group}
        for j in group:
            o_ref[:, sls[j]] = _diff_combine(out[j][0], out[j][1], lam, lam_init)


def _lat_diff_attention(lam_params, qd, kd, vd, cache_k, cache_v, n_lat, lam_init, tq=LAT_DIFF_TQ):
    n = qd.shape[0]
    nb = n // n_lat
    nq = n_lat // tq
    past = cache_k.shape[1]
    cos, sin = _rope_tables(n_lat)
    qblk = pl.BlockSpec((tq, D_GROUP), lambda b, i: (b * nq + i, 0))
    once = pl.Buffered(1)
    kvblk = pl.BlockSpec((n_lat, D_GROUP), lambda b, i: (b, 0), pipeline_mode=once)
    cblk = pl.BlockSpec((None, past, D_GROUP), lambda b, i: (b, 0, 0))
    return pl.pallas_call(
        functools.partial(_lat_diff_kernel, lam_init=lam_init, n_lat=n_lat),
        grid=(nb, nq),
        in_specs=[pl.BlockSpec(lam_params.shape, lambda b, i: (0, 0)),
                  qblk, kvblk, kvblk, cblk, cblk,
                  pl.BlockSpec((tq, PAIR_W), lambda b, i: (i, 0)),
                  pl.BlockSpec((tq, PAIR_W), lambda b, i: (i, 0)),
                  pl.BlockSpec((n_lat, PAIR_W), lambda b, i: (0, 0), pipeline_mode=once),
                  pl.BlockSpec((n_lat, PAIR_W), lambda b, i: (0, 0), pipeline_mode=once)],
        out_specs=qblk,
        out_shape=jax.ShapeDtypeStruct((n, D_GROUP), F32),
        scratch_shapes=[pltpu.VMEM((n_lat + past, D_GROUP), BF16),
                        pltpu.VMEM((D_GROUP // PAIR_W, n_lat + past, 2 * PAIR_W), BF16)],
        compiler_params=_cparams("arbitrary", "arbitrary"),
        name="lat_diff_attention",
    )(lam_params, qd, kd, vd, cache_k, cache_v, cos, sin, cos, sin)


def _na_bias_kernel(rpb_ref, tt_ref):
    h = pl.program_id(0)
    lane = lax.broadcasted_iota(jnp.int32, (GRID_W, 2 * GRID_W), 1)
    qc = lax.broadcasted_iota(jnp.int32, (GRID_W, 2 * GRID_W), 0)
    dc = jnp.clip(lane % GRID_W - qc + (WIN_W - 1), 0, RPB_C - 1)
    first = lane < GRID_W
    for dr in range(RPB_R - 1):
        acc = jnp.zeros((GRID_W, 2 * GRID_W), F32)
        for d in range(RPB_C):
            val = jnp.where(first, rpb_ref[h * RPB_R + dr, d], rpb_ref[h * RPB_R + dr + 1, d])
            acc = jnp.where(dc == d, val, acc)
        tt_ref[0, dr] = acc


def _na_bias_table(rpb):
    return pl.pallas_call(
        _na_bias_kernel,
        grid=(N_HEADS,),
        in_specs=[pl.BlockSpec(memory_space=pltpu.SMEM)],
        out_specs=pl.BlockSpec((1, RPB_R - 1, GRID_W, 2 * GRID_W), lambda h: (h, 0, 0, 0)),
        out_shape=jax.ShapeDtypeStruct((N_HEADS, RPB_R - 1, GRID_W, 2 * GRID_W), F32),
        compiler_params=_cparams("arbitrary"),
        name="na_bias_table",
    )(rpb.reshape(N_HEADS * RPB_R, RPB_C))


def _lat_na_kernel(q_ref, k_ref, v_ref, ck_ref, cv_ref, tt_ref, o_ref):
    r = pl.program_id(1)
    r_start = jnp.clip(r - WIN_R // 2, 0, GRID_ROWS - WIN_R)
    dr0 = r_start - r + (WIN_R - 1)
    row0 = pl.multiple_of(r_start * GRID_W, GRID_W)
    nkeys = WIN_R * GRID_W
    past = ck_ref.shape[0]
    lane = lax.broadcasted_iota(jnp.int32, (GRID_W, nkeys + past), 1)
    kc = lane % GRID_W
    qc = lax.broadcasted_iota(jnp.int32, (GRID_W, nkeys + past), 0)
    c_start = jnp.clip(qc - WIN_W // 2, 0, GRID_W - WIN_W)
    visible = ((kc >= c_start) & (kc < c_start + WIN_W)) | (lane >= nkeys)
    no_bias = jnp.zeros((GRID_W, past), F32)
    q = [_split_pair(t) for t in _pair_tiles((q_ref[...] * ATTN_SCALE).astype(BF16))]
    kcat = _pair_tiles(jnp.concatenate([k_ref[pl.ds(row0, nkeys), :], ck_ref[...]], axis=0).astype(BF16))
    vcat = [_with_ones(t) for t in
            _pair_tiles(jnp.concatenate([v_ref[pl.ds(row0, nkeys), :], cv_ref[...]], axis=0).astype(BF16))]
    bias = [jnp.concatenate([tt_ref[h, dr0 + 2 * j] for j in range(WIN_R // 2)] + [no_bias], axis=1)
            for h in range(N_HEADS)]
    e = [[_exp_scores(jnp.where(visible, _bdot_nt(q[j][c], kcat[j]) + bias[2 * j + c], NEG_INF))
          for c in range(2)] for j in range(len(q))]
    out = [[jnp.dot(e[j][c], vcat[j], preferred_element_type=F32) for c in range(2)] for j in range(len(q))]
    for j in range(len(q)):
        o_ref[:, j * PAIR_W:(j + 1) * PAIR_W] = _merge_pair(out[j][0], out[j][1])


def _lat_na_attention(qn, kn, vn, cache_k, cache_v, tt, n_lat):
    n = qn.shape[0]
    nb = n // n_lat
    past = cache_k.shape[1]
    qblk = pl.BlockSpec((GRID_W, D_GROUP), lambda b, r: (b * GRID_ROWS + r, 0))
    kvblk = pl.BlockSpec((n_lat, D_GROUP), lambda b, r: (b, 0))
    cblk = pl.BlockSpec((None, past, D_GROUP), lambda b, r: (b, 0, 0))
    return pl.pallas_call(
        _lat_na_kernel,
        grid=(nb, GRID_ROWS),
        in_specs=[qblk, kvblk, kvblk, cblk, cblk,
                  pl.BlockSpec(tt.shape, lambda b, r: (0, 0, 0, 0))],
        out_specs=qblk,
        out_shape=jax.ShapeDtypeStruct((n, D_GROUP), F32),
        compiler_params=_cparams("arbitrary", "arbitrary"),
        name="lat_na_attention",
    )(qn, kn, vn, cache_k, cache_v, tt)


def _rec_mix(of_ref, ob_ref, rg_ref, yf_ref, yb_ref, xr_ref, xk_ref, xv_ref, lo_ref, rk_ref, lng_ref, lnb_ref,
             gup_ref):
    def head_sums(x):
        return jnp.concatenate([_pair_head_sum(t) for t in _pair_tiles(x)], axis=1)

    inv_d = 1.0 / HEAD_DIM
    o = of_ref[...] + ob_ref[...]
    o_ret = _silu(rg_ref[...]) * (o * lax.rsqrt(head_sums(o * o) * inv_d + RMS_EPS))
    y = yf_ref[...] + yb_ref[...]
    yc = y - head_sums(y) * inv_d
    var = head_sums(yc * yc) * inv_d
    yn = yc * lax.rsqrt(var + RWKV_GN_EPS) * lng_ref[...] + lnb_ref[...]
    bonus = head_sums(xr_ref[...] * rk_ref[...] * xk_ref[...]) * xv_ref[...]
    g_rw = _bdot(_sigmoid(lo_ref[:, 2 * D_LORA:]), gup_ref[...])
    return o_ret, (yn + bonus) * g_rw


def _tail_kernel(*refs, final, ff_chunk, n_mix):
    x_ref, mix_refs = refs[0], refs[1:1 + n_mix]
    wo_ref, gm_ref, g_ref, sh_ref, sc_ref, gf_ref, wi_ref, wf_ref, fg_ref, o_ref = refs[1 + n_mix:]
    if n_mix == 2:
        mix_a, mix_b = mix_refs[0][...], mix_refs[1][...]
    else:
        mix_a, mix_b = _rec_mix(*mix_refs)
    mix = _bdot(mix_a, wo_ref[0:D_GROUP, :]) + _bdot(mix_b, wo_ref[D_GROUP:, :])
    x1 = x_ref[...] + gm_ref[0] * mix
    h = _rms(x1) * g_ref[...]
    hb = (h * (1.0 + sc_ref[0]) + sh_ref[0]).astype(BF16)
    acc = jnp.zeros_like(x1)
    for c0 in range(0, D_FF, ff_chunk):
        gate = jnp.dot(hb, wi_ref[:, c0:c0 + ff_chunk], preferred_element_type=F32)
        up = jnp.dot(hb, wi_ref[:, D_FF + c0:D_FF + c0 + ff_chunk], preferred_element_type=F32)
        acc = acc + jnp.dot((_silu(gate) * up).astype(BF16), wf_ref[c0:c0 + ff_chunk, :],
                            preferred_element_type=F32)
    x2 = x1 + gf_ref[0] * acc
    if final:
        x2 = _rms(x2) * fg_ref[...]
    o_ref[...] = x2


def _layer_tail(x, mix_inputs, w_out, gate_mix, gain, shift, scale, gate_ffn, w_ffn_in_all, w_ffn_out_all, layer,
                final_gain, rows_per_mod, final, ff_chunk=1408):
    n, d = x.shape
    tm = TAIL_TM if len(mix_inputs) == 2 else TAIL_TM_REC
    tiles_per_mod = rows_per_mod // tm
    mod_spec = pl.BlockSpec((1, 1, d), lambda i: (i // tiles_per_mod, 0, 0))
    vec_spec = pl.BlockSpec((1, d), lambda i: (0, 0))
    resident = lambda a: pl.BlockSpec(a.shape, lambda i: (0,) * a.ndim, pipeline_mode=pl.Buffered(1))
    layer_slab = lambda a: pl.BlockSpec((None,) + a.shape[1:], lambda i: (layer, 0, 0),
                                        pipeline_mode=pl.Buffered(1))
    mix_specs = [pl.BlockSpec((tm, a.shape[1]), lambda i: (i, 0)) if a.shape[0] == n else resident(a)
                 for a in mix_inputs]
    return pl.pallas_call(
        functools.partial(_tail_kernel, final=final, ff_chunk=ff_chunk, n_mix=len(mix_inputs)),
        grid=(n // tm,),
        in_specs=[pl.BlockSpec((tm, d), lambda i: (i, 0))] + mix_specs
                 + [resident(w_out), mod_spec, vec_spec, mod_spec, mod_spec, mod_spec,
                    layer_slab(w_ffn_in_all), layer_slab(w_ffn_out_all), vec_spec],
        out_specs=pl.BlockSpec((tm, d), lambda i: (i, 0)),
        out_shape=jax.ShapeDtypeStruct((n, d), F32),
        compiler_params=_cparams("arbitrary"),
        name="layer_tail",
    )(x, *mix_inputs, w_out, gate_mix, gain.reshape(1, d), shift, scale, gate_ffn, w_ffn_in_all, w_ffn_out_all,
      final_gain.reshape(1, d))


def _log_sigmoid(x):
    return jnp.minimum(x, 0.0) - jnp.log(1.0 + jnp.exp(-jnp.abs(x)))


def _pair_block_diag(xp):
    left = lax.broadcasted_iota(jnp.int32, xp.shape, 1) < HEAD_DIM
    zero = jnp.zeros_like(xp)
    return jnp.concatenate([jnp.where(left, xp, zero), jnp.where(left, zero, xp)], axis=0)


def _states_to_pairs(state):
    nb = state.shape[0]
    s = state.reshape(nb, 2, N_HEADS // 2, 2, HEAD_DIM, HEAD_DIM)
    zero = jnp.zeros_like(s[:, :, :, 0])
    return jnp.concatenate([jnp.concatenate([s[:, :, :, 0], zero], axis=-1),
                            jnp.concatenate([zero, s[:, :, :, 1]], axis=-1)], axis=-2)


def _load_states(st, s0_ref):
    st[...] = jnp.zeros(st.shape, st.dtype) if s0_ref is None else s0_ref[0]


def _store_states(sf_ref, st):
    for d in range(st.shape[0]):
        for p in range(st.shape[1]):
            s = st[d, p]
            sf_ref[0, d, 2 * p] = s[:HEAD_DIM, :HEAD_DIM]
            sf_ref[0, d, 2 * p + 1] = s[HEAD_DIM:, HEAD_DIM:]


def _ret_kernel(*refs, zero_init):
    dlm_ref, dlq_ref, qf_ref, kf_ref, vf_ref, qb_ref, kb_ref, vb_ref = refs[:8]
    s0_ref = None if zero_init else refs[8]
    of_ref, ob_ref, sf_ref, st, mask_scr, qdec_scr, kdec_scr, cdec_scr = refs[8 if zero_init else 9:]
    c = pl.program_id(1)
    cs = RET_CHUNK
    pw = 2 * HEAD_DIM
    npair = N_HEADS // 2
    chains = [(d, p) for d in range(2) for p in range(npair)]

    @pl.when(c == 0)
    def _():
        _load_states(st, s0_ref)

    @pl.when((pl.program_id(0) == 0) & (c == 0))
    def _():
        row = lax.broadcasted_iota(jnp.int32, (cs, 2 * cs), 0)
        col = lax.broadcasted_iota(jnp.int32, (cs, 2 * cs), 1) % cs
        tok = lax.broadcasted_iota(jnp.int32, (cs, pw), 0)
        dist = [row - col, col - row]
        tq = [tok.astype(F32), (cs - 1 - tok).astype(F32)]
        for i, (d, p) in enumerate(chains):
            lgm = _log_sigmoid(dlm_ref[d, p])[0:1, :]
            lgq = _log_sigmoid(dlq_ref[d, p])
            mask_scr[i] = jnp.where(dist[d] >= 0, jnp.exp(jnp.maximum(dist[d], 0).astype(F32) * lgm), 0.0)
            qdec_scr[i] = jnp.exp((tq[d] + 1.0) * lgq[0:1, :])
            kdec_scr[i] = jnp.exp((cs - 1.0 - tq[d]) * lgq[0:1, :])
            cdec_scr[i] = jnp.exp(cs * lgq)

    brow = lax.broadcasted_iota(jnp.int32, (pw, pw), 0) // HEAD_DIM
    bcol = lax.broadcasted_iota(jnp.int32, (pw, pw), 1) // HEAD_DIM
    same_head = brow == bcol
    refs = [(qf_ref, kf_ref, vf_ref, of_ref), (qb_ref, kb_ref, vb_ref, ob_ref)]
    nsub = RET_CHUNKS_PER_STEP
    work = [(t, sub) for sub in range(nsub) for t in range(len(chains))]
    sl_of = lambda t: slice(chains[t][1] * pw, (chains[t][1] + 1) * pw)
    rows_of = lambda sub: pl.ds(sub * cs, cs)
    q = {(t, sub): refs[chains[t][0]][0][rows_of(sub), sl_of(t)] for t, sub in work}
    k = {(t, sub): refs[chains[t][0]][1][rows_of(sub), sl_of(t)] * ATTN_SCALE for t, sub in work}
    v = {(t, sub): refs[chains[t][0]][2][rows_of(sub), sl_of(t)] for t, sub in work}
    s = {w: _bdot_nt(q[w], _pair_block_diag(k[w])) * mask_scr[w[0]] for w in work}
    inner = {w: _bdot(s[w], _pair_block_diag(v[w])) for w in work}
    kv = {w: _bdot_tn(k[w] * kdec_scr[w[0]], v[w]) for w in work}
    for step in range(nsub):
        now = [(t, (step if chains[t][0] == 0 else nsub - 1 - step)) for t in range(len(chains))]
        state = {w: st[chains[w[0]]] for w in now}
        cross = {w: _bdot(q[w], state[w]) * qdec_scr[w[0]] for w in now}
        for w in now:
            t, sub = w
            d, p = chains[t]
            refs[d][3][rows_of(sub), sl_of(t)] = inner[w] + cross[w]
            st[d, p] = state[w] * cdec_scr[t][0:1, :] + jnp.where(same_head, kv[w], 0.0)

    @pl.when(c == pl.num_programs(1) - 1)
    def _():
        _store_states(sf_ref, st)


def _retention_scan(decay_logit, q, k, v, state0, seq):
    n = q.shape[0]
    nb = n // seq
    rows = RET_CHUNK * RET_CHUNKS_PER_STEP
    nc = seq // rows
    npair, pw = N_HEADS // 2, 2 * HEAD_DIM
    dl = decay_logit.astype(F32)
    dl_m = jnp.broadcast_to(jnp.repeat(dl, RET_CHUNK, axis=1).reshape(2, npair, 1, 2 * RET_CHUNK),
                            (2, npair, 8, 2 * RET_CHUNK))
    dl_q = jnp.broadcast_to(jnp.repeat(dl, HEAD_DIM, axis=1).reshape(2, npair, 1, pw), (2, npair, 8, pw))
    blk_f = pl.BlockSpec((rows, D_GROUP), lambda b, c: (b * nc + c, 0))
    blk_b = pl.BlockSpec((rows, D_GROUP), lambda b, c: (b * nc + nc - 1 - c, 0))
    st_in = ([] if state0 is None else
             [pl.BlockSpec((1, 2, npair, pw, pw), lambda b, c: (b, 0, 0, 0, 0))])
    st_out = pl.BlockSpec((1, 2, N_HEADS, HEAD_DIM, HEAD_DIM), lambda b, c: (b, 0, 0, 0, 0))
    return pl.pallas_call(
        functools.partial(_ret_kernel, zero_init=state0 is None),
        grid=(nb, nc),
        in_specs=[pl.BlockSpec(dl_m.shape, lambda b, c: (0, 0, 0, 0)),
                  pl.BlockSpec(dl_q.shape, lambda b, c: (0, 0, 0, 0)),
                  blk_f, blk_f, blk_f, blk_b, blk_b, blk_b] + st_in,
        out_specs=[blk_f, blk_b, st_out],
        out_shape=[jax.ShapeDtypeStruct((n, D_GROUP), F32), jax.ShapeDtypeStruct((n, D_GROUP), F32),
                   jax.ShapeDtypeStruct((nb, 2, N_HEADS, HEAD_DIM, HEAD_DIM), F32)],
        scratch_shapes=[pltpu.VMEM((2, npair, pw, pw), F32),
                        pltpu.VMEM((2 * npair, RET_CHUNK, 2 * RET_CHUNK), F32),
                        pltpu.VMEM((2 * npair, RET_CHUNK, pw), F32),
                        pltpu.VMEM((2 * npair, RET_CHUNK, pw), F32),
                        pltpu.VMEM((2 * npair, 8, pw), F32)],
        compiler_params=_cparams("arbitrary", "arbitrary"),
        name="retention_scan",
    )(dl_m, dl_q, q, k, v, q, k, v, *([] if state0 is None else [_states_to_pairs(state0)]))


def _rwkv_kernel(*refs, zero_init):
    (xrf_ref, xkf_ref, xvf_ref, lof_ref, xrb_ref, xkb_ref, xvb_ref, lob_ref,
     w0_ref, wup_ref, a0_ref, aup_ref, kk_ref, ka_ref) = refs[:14]
    s0_ref = None if zero_init else refs[14]
    yf_ref, yb_ref, sf_ref, st = refs[14 if zero_init else 15:]
    c = pl.program_id(1)
    cs = RWKV_CHUNK
    nsub = RWKV_CHUNKS_PER_STEP
    npair = N_HEADS // 2

    @pl.when(c == 0)
    def _():
        _load_states(st, s0_ref)

    row = lax.broadcasted_iota(jnp.int32, (cs, 3 * cs), 0)
    col = lax.broadcasted_iota(jnp.int32, (cs, 3 * cs), 1) % cs
    pw = 2 * HEAD_DIM
    prow = lax.broadcasted_iota(jnp.int32, (cs, pw), 0)
    pcol = lax.broadcasted_iota(jnp.int32, (cs, pw), 1) % HEAD_DIM
    eye = jnp.where(pcol == prow, 1.0, 0.0).astype(F32)
    sub_diag = (prow // RWKV_SUB) == (pcol // RWKV_SUB)
    brow = lax.broadcasted_iota(jnp.int32, (pw, pw), 0) // HEAD_DIM
    bcol = lax.broadcasted_iota(jnp.int32, (pw, pw), 1) // HEAD_DIM
    same_head = brow == bcol

    bd = _pair_block_diag

    def direction(d, sub, xr_ref, xk_ref, xv_ref, lo_ref):
        rows = pl.ds(sub * cs, cs)
        tri3 = jnp.where((row >= col) if d == 0 else (col >= row), 1.0, 0.0).astype(BF16)
        dist = (prow - pcol) if d == 0 else (pcol - prow)
        xk = xk_ref[rows, :]
        lo = lo_ref[rows, :]
        w_log = -math.exp(-0.5) * _sigmoid(w0_ref[d] + _bdot(jnp.tanh(lo[:, 0:D_LORA]), wup_ref[d]))
        a_all = _sigmoid(a0_ref[d] + _bdot(lo[:, D_LORA:2 * D_LORA], aup_ref[d]))
        w1 = w_log.astype(BF16)
        r1 = w_log - w1.astype(F32)
        w2 = r1.astype(BF16)
        w3 = (r1 - w2.astype(F32)).astype(BF16)
        cum = jnp.dot(tri3, jnp.concatenate([w1, w2, w3], axis=0), preferred_element_type=F32)
        cum_end = cum[cs - 1:cs, :] if d == 0 else cum[0:1, :]
        return dict(earlier=dist > 0, upto=dist >= 0, xr=xr_ref[rows, :], xv=xv_ref[rows, :], a_all=a_all,
                    e_incl=jnp.exp(cum), e_excl=jnp.exp(cum - w_log), e_neg=jnp.exp(-cum),
                    e_end=jnp.exp(cum_end - cum), w_end=jnp.exp(cum_end), kk_all=xk * kk_ref[...],
                    keff_all=xk * (1.0 + (a_all - 1.0) * ka_ref[...]))

    in_refs = [(xrf_ref, xkf_ref, xvf_ref, lof_ref), (xrb_ref, xkb_ref, xvb_ref, lob_ref)]
    dirs = {(d, sub): direction(d, sub, *in_refs[d]) for d in range(2) for sub in range(nsub)}
    y_refs = [yf_ref, yb_ref]

    chains = [(d, sub, p) for d in range(2) for sub in range(nsub) for p in range(npair)]
    pairs = range(len(chains))
    sls = [slice(p * pw, (p + 1) * pw) for _, _, p in chains]
    pick = lambda name: [dirs[d, sub][name][:, sls[i]] for i, (d, sub, _) in enumerate(chains)]
    earlier = [dirs[d, sub]["earlier"] for d, sub, _ in chains]
    upto = [dirs[d, sub]["upto"] for d, sub, _ in chains]
    cat = lambda x, y: jnp.concatenate([x, y], axis=0)

    head_sum = _pair_head_sum

    e_incl, e_excl, e_neg, e_end, w_end = (pick(k) for k in ("e_incl", "e_excl", "e_neg", "e_end", "w_end"))
    kk = [x * lax.rsqrt(head_sum(x * x) + 1e-12) for x in pick("kk_all")]
    b = [kk[p] * a for p, a in zip(pairs, pick("a_all"))]
    keff = pick("keff_all")
    v = pick("xv")
    a_t = [-kk[p] * e_excl[p] for p in pairs]
    r_t = [xr * e_incl[p] for p, xr in zip(pairs, pick("xr"))]
    ar = [cat(a_t[p], r_t[p]).astype(BF16) for p in pairs]
    g = [_bdot_nt(ar[p], cat(bd(b[p] * e_neg[p]), bd(keff[p] * e_neg[p]))) for p in pairs]
    n_mat = [jnp.where(earlier[p], g[p][:cs, :pw], 0.0) for p in pairs]
    m_rb = [jnp.where(upto[p], g[p][cs:, :pw], 0.0) for p in pairs]
    a_ak = [jnp.where(earlier[p], g[p][:cs, pw:], 0.0) for p in pairs]
    m_rk = [jnp.where(upto[p], g[p][cs:, pw:], 0.0) for p in pairs]
    bd_s = lambda xs: (bd(xs[0]), bd(xs[1]))
    n_d = [jnp.where(sub_diag, n, 0.0) for n in n_mat]
    n_o = [jnp.where(sub_diag, 0.0, n) for n in n_mat]
    n_s = [_split2(n) for n in n_d]
    x = [eye + n for n in n_d]
    pk = [_mm3s(ns, bd_s(ns)) for ns in n_s]
    for _ in range(RWKV_SUB_DOUBLINGS - 1):
        xp = [_mm3s(_split2(cat(x[p], pk[p])), bd_s(_split2(pk[p]))) for p in pairs]
        x = [x[p] + xp[p][:cs] for p in pairs]
        pk = [r[cs:] for r in xp]
    t_d = [x[p] + _mm3s(_split2(x[p]), bd_s(_split2(pk[p]))) for p in pairs]
    lcat = lambda x, y: jnp.concatenate([x, y], axis=1)
    m_o = [_bdot(t_d[p], bd(n_o[p])) for p in pairs]
    mz = [_bdot(m_o[p], lcat(bd(t_d[p]), bd(m_o[p]))) for p in pairs]
    z = [t_d[p] + mz[p][:, :pw] for p in pairs]
    t_inv = [(z[p] + _bdot(mz[p][:, pw:], bd(z[p]))).astype(BF16) for p in pairs]
    av = [_bdot(cat(a_ak[p], m_rk[p]), bd(v[p])) for p in pairs]
    tau = [_bdot(t_inv[p], lcat(bd(a_t[p]), bd(av[p][:cs]))) for p in pairs]
    bk_end = [cat(b[p] * e_end[p], keff[p] * e_end[p]) for p in pairs]
    for step in range(nsub):
        now = [i for i, (d, sub, _) in enumerate(chains) if sub == (step if d == 0 else nsub - 1 - step)]
        s_prev = {i: st[chains[i][0], chains[i][2]] for i in now}
        hs = {i: _bdot_nt(cat(tau[i][:, :pw], r_t[i]), s_prev[i]) for i in now}
        u = {i: hs[i][:cs] + tau[i][:, pw:] for i in now}
        y = {i: hs[i][cs:] + _bdot(m_rb[i], bd(u[i])) + av[i][cs:] for i in now}
        for i in now:
            d, sub, p = chains[i]
            y_refs[d][pl.ds(sub * cs, cs), sls[i]] = y[i]
            st[d, p] = s_prev[i] * w_end[i] + jnp.where(same_head, _bdot_tn(cat(u[i], v[i]), bk_end[i]), 0.0)

    @pl.when(c == pl.num_programs(1) - 1)
    def _():
        _store_states(sf_ref, st)


def _rwkv_scan(xr, xk, xv, lora, w0, w_up, a0, a_up, k_k, k_a, state0, seq):
    n = xr.shape[0]
    nb = n // seq
    blk = RWKV_CHUNK * RWKV_CHUNKS_PER_STEP
    nc = seq // blk

    fwd_idx = lambda b, c: (b * nc + c, 0)
    bwd_idx = lambda b, c: (b * nc + nc - 1 - c, 0)
    blk_f = pl.BlockSpec((blk, D_GROUP), fwd_idx)
    blk_b = pl.BlockSpec((blk, D_GROUP), bwd_idx)
    lblk_f = pl.BlockSpec((blk, lora.shape[1]), fwd_idx)
    lblk_b = pl.BlockSpec((blk, lora.shape[1]), bwd_idx)
    dvec = pl.BlockSpec((2, 1, D_GROUP), lambda b, c: (0, 0, 0))
    dmat = pl.BlockSpec((2, D_LORA, D_GROUP), lambda b, c: (0, 0, 0))
    vec = pl.BlockSpec((1, D_GROUP), lambda b, c: (0, 0))
    npair, pw = N_HEADS // 2, 2 * HEAD_DIM
    st_in = ([] if state0 is None else
             [pl.BlockSpec((1, 2, npair, pw, pw), lambda b, c: (b, 0, 0, 0, 0))])
    st_out = pl.BlockSpec((1, 2, N_HEADS, HEAD_DIM, HEAD_DIM), lambda b, c: (b, 0, 0, 0, 0))
    return pl.pallas_call(
        functools.partial(_rwkv_kernel, zero_init=state0 is None),
        grid=(nb, nc),
        in_specs=[blk_f, blk_f, blk_f, lblk_f, blk_b, blk_b, blk_b, lblk_b,
                  dvec, dmat, dvec, dmat, vec, vec] + st_in,
        out_specs=[blk_f, blk_b, st_out],
        out_shape=[jax.ShapeDtypeStruct((n, D_GROUP), F32), jax.ShapeDtypeStruct((n, D_GROUP), F32),
                   jax.ShapeDtypeStruct((nb, 2, N_HEADS, HEAD_DIM, HEAD_DIM), F32)],
        scratch_shapes=[pltpu.VMEM((2, npair, pw, pw), F32)],
        compiler_params=_cparams("arbitrary", "arbitrary"),
        name="rwkv7_scan",
    )(xr, xk, xv, lora, xr, xk, xv, lora, w0.reshape(2, 1, D_GROUP), w_up, a0.reshape(2, 1, D_GROUP), a_up,
      k_k.reshape(1, D_GROUP), k_a.reshape(1, D_GROUP), *([] if state0 is None else [_states_to_pairs(state0)]))


ATTN_WIDTHS = (D_GROUP,) * 6
REC_WIDTHS = (D_GROUP,) * 7 + (2 * D_LORA + D_LORA_G,)


def kernel(x_prompt, x_sample, cache_na_k, cache_na_v, cache_diff_k, cache_diff_v, state_ret, state_rwkv, c, c_ctx, norm_mix_g, norm_ffn_g, norm_final_g, w_ada, b_ada, w_in_attn, w_out_attn, na_rpb, diff_lq1, diff_lk1, diff_lq2, diff_lk2, w_in_rec, w_out_rec, ret_decay_logit, rw_w0, rw_w_up, rw_a0, rw_a_up, rw_g_up, rw_k_k, rw_k_a, rw_r_k, rw_ln_g, rw_ln_b, w_ffn_in, w_ffn_out):
    bp, seq, d = x_prompt.shape
    bl, n_lat, _ = x_sample.shape
    depth = w_ada.shape[0]
    ctx = x_prompt.reshape(bp * seq, d)
    lat = x_sample.reshape(bl * n_lat, d)

    cond8 = jnp.zeros((8, d), F32).at[0].set(c_ctx).at[1:1 + bl].set(c)
    mods = _ada_modulation(cond8, w_ada, b_ada).reshape(depth, 8, 6, d)
    w_ffn_in_bf16 = w_ffn_in.astype(BF16)
    w_ffn_out_bf16 = w_ffn_out.astype(BF16)

    outs = {}
    for layer in range(depth):
        m_ctx = [mods[layer, 0:1, j].reshape(1, 1, d) for j in range(6)]
        m_lat = [mods[layer, 1:1 + bl, j].reshape(bl, 1, d) for j in range(6)]
        if layer % 2 == 0:
            i = layer // 2
            lam_init = 0.8 - 0.6 * math.exp(-0.3 * layer)
            lam_params = jnp.stack([diff_lq1[i], diff_lk1[i], diff_lq2[i], diff_lk2[i]]).astype(F32)
            w_in = w_in_attn[i].astype(BF16)
            w_out = w_out_attn[i].astype(BF16)
            res = _ctx_attention(ctx, norm_mix_g[layer], m_ctx[0], m_ctx[1], w_in, lam_params, seq, lam_init)
            mix_ctx = res[:2]
            outs["na_k"], outs["na_v"], outs["df_k"], outs["df_v"] = res[2:]
            plat = _norm_mod_matmul(lat, norm_mix_g[layer], m_lat[0], m_lat[1], w_in, ATTN_WIDTHS, n_lat, PROJ_TM)
            past = cache_na_k.shape[2]
            tt = _na_bias_table(na_rpb[i])
            o_na = _lat_na_attention(plat[0], plat[1], plat[2],
                                     cache_na_k[:, i].reshape(bl, past, D_GROUP),
                                     cache_na_v[:, i].reshape(bl, past, D_GROUP), tt, n_lat)
            o_df = _lat_diff_attention(lam_params, plat[3], plat[4], plat[5],
                                       cache_diff_k[:, i].reshape(bl, past, D_GROUP),
                                       cache_diff_v[:, i].reshape(bl, past, D_GROUP), n_lat, lam_init)
            mix_lat = (o_na, o_df)
        else:
            j = layer // 2
            w_in = w_in_rec[j].astype(BF16)
            w_out = w_out_rec[j].astype(BF16)
            pc = _norm_mod_matmul(ctx, norm_mix_g[layer], m_ctx[0], m_ctx[1], w_in, REC_WIDTHS, bp * seq, PROJ_TM)
            plat = _norm_mod_matmul(lat, norm_mix_g[layer], m_lat[0], m_lat[1], w_in, REC_WIDTHS, n_lat, PROJ_TM)
            mixes = []
            for is_ctx, p, s_ret0, s_rw0, sq in ((True, pc, None, None, seq),
                                                 (False, plat, state_ret[:, j], state_rwkv[:, j], n_lat)):
                rq, rk, rv, rg, wr, wk, wv, lora = p
                o_f, o_b, s_ret = _retention_scan(ret_decay_logit[j], rq, rk, rv, s_ret0, sq)
                y_f, y_b, s_rw = _rwkv_scan(wr, wk, wv, lora, rw_w0[j], rw_w_up[j].astype(BF16), rw_a0[j],
                                            rw_a_up[j].astype(BF16), rw_k_k[j], rw_k_a[j], s_rw0, sq)
                mixes.append((o_f, o_b, rg, y_f, y_b, wr, wk, wv, lora, rw_r_k[j].reshape(1, D_GROUP),
                              rw_ln_g[j].reshape(1, D_GROUP), rw_ln_b[j].reshape(1, D_GROUP),
                              rw_g_up[j].astype(BF16)))
                if is_ctx:
                    outs["ret"], outs["rwkv"] = s_ret, s_rw
            mix_ctx, mix_lat = mixes
        final = layer == depth - 1
        ctx = _layer_tail(ctx, mix_ctx, w_out, m_ctx[2], norm_ffn_g[layer], m_ctx[3], m_ctx[4], m_ctx[5],
                          w_ffn_in_bf16, w_ffn_out_bf16, layer, norm_final_g, bp * seq, final)
        lat = _layer_tail(lat, mix_lat, w_out, m_lat[2], norm_ffn_g[layer], m_lat[3], m_lat[4], m_lat[5],
                          w_ffn_in_bf16, w_ffn_out_bf16, layer, norm_final_g, n_lat, final)

    y_prompt = ctx.reshape(bp, seq, d)
    y_sample = lat.reshape(bl, n_lat, d)
    new_cache_na_k = outs["na_k"].reshape(bp, 1, seq, N_HEADS, HEAD_DIM)
    new_cache_na_v = outs["na_v"].reshape(bp, 1, seq, N_HEADS, HEAD_DIM)
    new_cache_diff_k = outs["df_k"].reshape(bp, 1, seq, N_HEADS // 2, 2, HEAD_DIM)
    new_cache_diff_v = outs["df_v"].reshape(bp, 1, seq, N_HEADS // 2, 2 * HEAD_DIM)
    new_state_ret = outs["ret"].reshape(bp, 1, 2, N_HEADS, HEAD_DIM, HEAD_DIM)
    new_state_rwkv = outs["rwkv"].reshape(bp, 1, 2, N_HEADS, HEAD_DIM, HEAD_DIM)
    return (y_prompt, y_sample, new_cache_na_k, new_cache_na_v, new_cache_diff_k, new_cache_diff_v,
            new_state_ret, new_state_rwkv)
```

```python
import functools
import math

import jax
import jax.numpy as jnp
from jax import lax
from jax.experimental import pallas as pl
from jax.experimental.pallas import tpu as pltpu

F32 = jnp.float32
BF16 = jnp.bfloat16

HEAD_DIM = 64
N_HEADS = 8
D_GROUP = 512
GRID_W = 64
GRID_ROWS = 32
WIN_R = 8
WIN_W = 16
RPB_R = 15
RPB_C = 31
D_FF = 2816
FFN_CHUNK = D_FF // 2
D_LORA = 64
D_LORA_G = 128
ROPE_BASE = 10000.0
RMS_EPS = 1e-6
RWKV_GN_EPS = 64e-5
NEG_INF = -1e30
ATTN_SCALE = HEAD_DIM ** -0.5
RET_CHUNK = 128
RET_CHUNKS_PER_STEP = 2
RWKV_CHUNK = 64
RWKV_CHUNKS_PER_STEP = 4
RWKV_SUB = 16
RWKV_SUB_DOUBLINGS = 3
V7X_VMEM_BYTES = 64 * 1024 * 1024
VMEM_LIMIT = V7X_VMEM_BYTES - 8 * 1024 * 1024
ADA_TN = 1536
TAIL_TM = 512
TAIL_TM_REC = 256
PROJ_TM = 512
LAT_DIFF_TQ = 256


def _cparams(*sem):
    return pltpu.CompilerParams(dimension_semantics=sem, vmem_limit_bytes=VMEM_LIMIT)


def _sigmoid(x):
    return 1.0 / (1.0 + jnp.exp(-x))


def _silu(x):
    return x * _sigmoid(x)


def _rms(x, eps=RMS_EPS):
    return x * lax.rsqrt(jnp.mean(x * x, axis=-1, keepdims=True) + eps)


def _bdot(a, b):
    return jnp.dot(a.astype(BF16), b.astype(BF16), preferred_element_type=F32)


def _bdot_nt(a, b):
    return lax.dot_general(a.astype(BF16), b.astype(BF16), (((1,), (1,)), ((), ())),
                           preferred_element_type=F32)


def _bdot_tn(a, b):
    return lax.dot_general(a.astype(BF16), b.astype(BF16), (((0,), (0,)), ((), ())),
                           preferred_element_type=F32)


def _split2(x):
    hi = x.astype(BF16)
    lo = (x - hi.astype(F32)).astype(BF16)
    return hi, lo


def _mm3s(a_split, b_split):
    ah, al = a_split
    bh, bl = b_split
    n = bh.shape[1]
    rhs = jnp.concatenate([jnp.concatenate([bh, bl], axis=1),
                           jnp.concatenate([bh, jnp.zeros_like(bl)], axis=1)], axis=0)
    out = jnp.dot(jnp.concatenate([ah, al], axis=1), rhs, preferred_element_type=F32)
    return out[:, :n] + out[:, n:]


def _diff_lambda(lp, lam_init):
    s1 = jnp.sum(lp[0:1, :] * lp[1:2, :], axis=-1, keepdims=True)
    s2 = jnp.sum(lp[2:3, :] * lp[3:4, :], axis=-1, keepdims=True)
    return jnp.exp(s1) - jnp.exp(s2) + lam_init


def _ada_kernel(c_ref, w_ref, b_ref, o_ref):
    s = _silu(c_ref[...])
    o_ref[0] = _bdot(s, w_ref[0]) + b_ref[0]


def _ada_modulation(cond8, w_ada, b_ada):
    depth, d, n = w_ada.shape
    tn = ADA_TN
    return pl.pallas_call(
        _ada_kernel,
        grid=(depth, n // tn),
        in_specs=[pl.BlockSpec((8, d), lambda l, j: (0, 0)),
                  pl.BlockSpec((1, d, tn), lambda l, j: (l, 0, j)),
                  pl.BlockSpec((1, 1, tn), lambda l, j: (l, 0, j))],
        out_specs=pl.BlockSpec((1, 8, tn), lambda l, j: (l, 0, j)),
        out_shape=jax.ShapeDtypeStruct((depth, 8, n), F32),
        compiler_params=_cparams("arbitrary", "arbitrary"),
        name="ada_modulation",
    )(cond8, w_ada, b_ada.reshape(depth, 1, n))


def _norm_mod(x_ref, g_ref, sh_ref, sc_ref):
    h = _rms(x_ref[...]) * g_ref[...]
    return (h * (1.0 + sc_ref[0]) + sh_ref[0]).astype(BF16)


def _nmm_kernel(x_ref, g_ref, sh_ref, sc_ref, w_ref, *o_refs, widths):
    hb = _norm_mod(x_ref, g_ref, sh_ref, sc_ref)
    off = 0
    for o_ref, w in zip(o_refs, widths):
        o_ref[...] = jnp.dot(hb, w_ref[:, off:off + w], preferred_element_type=F32)
        off += w


def _norm_mod_matmul(x, gain, shift, scale, w_bf16, widths, rows_per_mod, tm):
    n, d = x.shape
    tiles_per_mod = rows_per_mod // tm
    mod_spec = pl.BlockSpec((1, 1, d), lambda i: (i // tiles_per_mod, 0, 0))
    return pl.pallas_call(
        functools.partial(_nmm_kernel, widths=widths),
        grid=(n // tm,),
        in_specs=[pl.BlockSpec((tm, d), lambda i: (i, 0)),
                  pl.BlockSpec((1, d), lambda i: (0, 0)),
                  mod_spec, mod_spec,
                  pl.BlockSpec(w_bf16.shape, lambda i: (0, 0), pipeline_mode=pl.Buffered(1))],
        out_specs=[pl.BlockSpec((tm, w), lambda i: (i, 0)) for w in widths],
        out_shape=[jax.ShapeDtypeStruct((n, w), F32) for w in widths],
        compiler_params=_cparams("arbitrary"),
        name="norm_mod_matmul",
    )(x, gain.reshape(1, d), shift, scale, w_bf16)


PAIR_W = 2 * HEAD_DIM


def _pair_tiles(x):
    return [x[:, j * PAIR_W:(j + 1) * PAIR_W] for j in range(x.shape[1] // PAIR_W)]


def _split_pair(tile):
    left = lax.broadcasted_iota(jnp.int32, tile.shape, 1) < HEAD_DIM
    zero = jnp.zeros_like(tile)
    return jnp.where(left, tile, zero), jnp.where(left, zero, tile)


def _pair_head_sum(tile):
    left = lax.broadcasted_iota(jnp.int32, tile.shape, 1) < HEAD_DIM
    zero = jnp.zeros_like(tile)
    s_left = jnp.sum(jnp.where(left, tile, zero), axis=-1, keepdims=True)
    s_right = jnp.sum(jnp.where(left, zero, tile), axis=-1, keepdims=True)
    return jnp.where(left, s_left, s_right)


def _with_ones(v):
    return jnp.concatenate([v, jnp.ones_like(v)], axis=1)


def _exp_scores(s):
    return jnp.exp(s - jnp.max(s, axis=-1, keepdims=True)).astype(BF16)


def _merge_pair(out_even, out_odd):
    left = lax.broadcasted_iota(jnp.int32, (out_even.shape[0], PAIR_W), 1) < HEAD_DIM
    return jnp.where(left, out_even[:, :PAIR_W] * (1.0 / out_even[:, PAIR_W:]),
                     out_odd[:, :PAIR_W] * (1.0 / out_odd[:, PAIR_W:]))


def _diff_combine(out1, out2, lam, lam_init):
    o = out1[:, :PAIR_W] * (1.0 / out1[:, PAIR_W:]) - out2[:, :PAIR_W] * (lam / out2[:, PAIR_W:])
    return _rms(o) * (1.0 - lam_init)


CTX_CACHE_SHAPES = ((N_HEADS, HEAD_DIM), (N_HEADS, HEAD_DIM),
                    (N_HEADS // 2, 2, HEAD_DIM), (N_HEADS // 2, 2 * HEAD_DIM))


def _ctx_attn_kernel(x_ref, g_ref, sh_ref, sc_ref, w_ref, lp_ref, on_ref, od_ref, *cache_refs, lam_init):
    hb = _norm_mod(x_ref, g_ref, sh_ref, sc_ref)
    qn, kn, vn, qd, kd, vd = (jnp.dot(hb, w_ref[:, i * D_GROUP:(i + 1) * D_GROUP], preferred_element_type=F32)
                              for i in range(6))
    for c_ref, val in zip(cache_refs, (kn, vn, kd, vd)):
        c_ref[...] = val.reshape(c_ref.shape)
    tiles = range(D_GROUP // PAIR_W)
    lam = _diff_lambda(lp_ref[...], lam_init)
    q = [_split_pair(t) for t in _pair_tiles((qn * ATTN_SCALE).astype(BF16))
         + _pair_tiles((qd * ATTN_SCALE).astype(BF16))]
    k = _pair_tiles(kn.astype(BF16)) + _pair_tiles(kd.astype(BF16))
    v = [_with_ones(t) for t in _pair_tiles(vn.astype(BF16)) + _pair_tiles(vd.astype(BF16))]
    e = [[_exp_scores(_bdot_nt(q[j][c], k[j])) for c in range(2)] for j in range(2 * len(tiles))]
    out = [[jnp.dot(e[j][c], v[j], preferred_element_type=F32) for c in range(2)] for j in range(2 * len(tiles))]
    for j in tiles:
        sl = slice(j * PAIR_W, (j + 1) * PAIR_W)
        on_ref[:, sl] = _merge_pair(out[j][0], out[j][1])
        od_ref[:, sl] = _diff_combine(out[len(tiles) + j][0], out[len(tiles) + j][1], lam, lam_init)


def _ctx_attention(x, gain, shift, scale, w_bf16, lam_params, seq, lam_init):
    n, d = x.shape
    blk = pl.BlockSpec((seq, D_GROUP), lambda b: (b, 0))
    mod_spec = pl.BlockSpec((1, 1, d), lambda b: (0, 0, 0))
    cache_specs = [pl.BlockSpec((seq,) + dims, lambda b, nd=len(dims): (b,) + (0,) * nd)
                   for dims in CTX_CACHE_SHAPES]
    return pl.pallas_call(
        functools.partial(_ctx_attn_kernel, lam_init=lam_init),
        grid=(n // seq,),
        in_specs=[pl.BlockSpec((seq, d), lambda b: (b, 0)),
                  pl.BlockSpec((1, d), lambda b: (0, 0)),
                  mod_spec, mod_spec,
                  pl.BlockSpec(w_bf16.shape, lambda b: (0, 0), pipeline_mode=pl.Buffered(1)),
                  pl.BlockSpec(lam_params.shape, lambda b: (0, 0))],
        out_specs=[blk, blk] + cache_specs,
        out_shape=([jax.ShapeDtypeStruct((n, D_GROUP), F32)] * 2
                   + [jax.ShapeDtypeStruct((n,) + dims, F32) for dims in CTX_CACHE_SHAPES]),
        compiler_params=_cparams("arbitrary"),
        name="ctx_attention",
    )(x, gain.reshape(1, d), shift, scale, w_bf16, lam_params)


def _rope_tables(n):
    quarter = HEAD_DIM // 4
    pos = jnp.arange(n)
    inv_freq = ROPE_BASE ** (-jnp.arange(quarter, dtype=F32) / quarter)
    lane = jnp.arange(HEAD_DIM)
    p = jnp.where(lane[None, :] < HEAD_DIM // 2, (pos // GRID_W)[:, None], (pos % GRID_W)[:, None]).astype(F32)
    ang = p * inv_freq[lane % quarter][None, :]
    sign = jnp.where((lane % (2 * quarter)) < quarter, -1.0, 1.0).astype(F32)
    cos = jnp.tile(jnp.cos(ang), (1, 2))
    sin = jnp.tile(jnp.sin(ang) * sign[None, :], (1, 2))
    return cos, sin


def _rope(x, cos, sin_signed):
    quarter = HEAD_DIM // 4
    lane = lax.broadcasted_iota(jnp.int32, cos.shape, 1)
    first = (lane % (2 * quarter)) < quarter
    tiles = []
    for t in _pair_tiles(x):
        partner = jnp.where(first, pltpu.roll(t, PAIR_W - quarter, 1), pltpu.roll(t, quarter, 1))
        tiles.append(t * cos + partner * sin_signed)
    return jnp.concatenate(tiles, axis=1)


def _lat_diff_kernel(lp_ref, q_ref, k_ref, v_ref, ck_ref, cv_ref, cosq_ref, sinq_ref, cosk_ref, sink_ref,
                     o_ref, kall, vall, *, lam_init, n_lat):
    @pl.when(pl.program_id(1) == 0)
    def _():
        kall[0:n_lat, :] = _rope(k_ref[...], cosk_ref[...], sink_ref[...]).astype(BF16)
        kall[n_lat:, :] = ck_ref[...].astype(BF16)
        for j, (vt, ct) in enumerate(zip(_pair_tiles(v_ref[...].astype(BF16)),
                                         _pair_tiles(cv_ref[...].astype(BF16)))):
            vall[j, 0:n_lat, :] = _with_ones(vt)
            vall[j, n_lat:, :] = _with_ones(ct)

    lam = _diff_lambda(lp_ref[...], lam_init)
    q = (_rope(q_ref[...], cosq_ref[...], sinq_ref[...]) * ATTN_SCALE).astype(BF16)
    qs = [_split_pair(qt) for qt in _pair_tiles(q)]
    sls = [slice(j * PAIR_W, (j + 1) * PAIR_W) for j in range(len(qs))]
    for group in ((0, 1), (2, 3)):
        e = {j: [_exp_scores(_bdot_nt(qs[j][c], kall[:, sls[j]])) for c in range(2)] for j in group}
        out = {j: [jnp.dot(e[j][c], vall[j], preferred_element_type=F32) for c in range(2)] for j in group}
        for j in group:
            o_ref[:, sls[j]] = _diff_combine(out[j][0], out[j][1], lam, lam_init)


def _lat_diff_attention(lam_params, qd, kd, vd, cache_k, cache_v, n_lat, lam_init, tq=LAT_DIFF_TQ):
    n = qd.shape[0]
    nb = n // n_lat
    nq = n_lat // tq
    past = cache_k.shape[1]
    cos, sin = _rope_tables(n_lat)
    qblk = pl.BlockSpec((tq, D_GROUP), lambda b, i: (b * nq + i, 0))
    once = pl.Buffered(1)
    kvblk = pl.BlockSpec((n_lat, D_GROUP), lambda b, i: (b, 0), pipeline_mode=once)
    cblk = pl.BlockSpec((None, past, D_GROUP), lambda b, i: (b, 0, 0))
    return pl.pallas_call(
        functools.partial(_lat_diff_kernel, lam_init=lam_init, n_lat=n_lat),
        grid=(nb, nq),
        in_specs=[pl.BlockSpec(lam_params.shape, lambda b, i: (0, 0)),
                  qblk, kvblk, kvblk, cblk, cblk,
                  pl.BlockSpec((tq, PAIR_W), lambda b, i: (i, 0)),
                  pl.BlockSpec((tq, PAIR_W), lambda b, i: (i, 0)),
                  pl.BlockSpec((n_lat, PAIR_W), lambda b, i: (0, 0), pipeline_mode=once),
                  pl.BlockSpec((n_lat, PAIR_W), lambda b, i: (0, 0), pipeline_mode=once)],
        out_specs=qblk,
        out_shape=jax.ShapeDtypeStruct((n, D_GROUP), F32),
        scratch_shapes=[pltpu.VMEM((n_lat + past, D_GROUP), BF16),
                        pltpu.VMEM((D_GROUP // PAIR_W, n_lat + past, 2 * PAIR_W), BF16)],
        compiler_params=_cparams("arbitrary", "arbitrary"),
        name="lat_diff_attention",
    )(lam_params, qd, kd, vd, cache_k, cache_v, cos, sin, cos, sin)


NA_Q_ROWS = 4
NA_KEY_ROWS = WIN_R + NA_Q_ROWS
NA_TABLE = RPB_R + 1


def _na_bias_kernel(rpb_ref, tt_ref):
    h = pl.program_id(0)
    lane = lax.broadcasted_iota(jnp.int32, (GRID_W, 2 * GRID_W), 1)
    qc = lax.broadcasted_iota(jnp.int32, (GRID_W, 2 * GRID_W), 0)
    dc = jnp.clip(lane % GRID_W - qc + (WIN_W - 1), 0, RPB_C - 1)
    first = lane < GRID_W
    for k in range(NA_TABLE):
        dr_a, dr_b = min(max(k - 1, 0), RPB_R - 1), min(k, RPB_R - 1)
        acc = jnp.zeros((GRID_W, 2 * GRID_W), F32)
        for d in range(RPB_C):
            val = jnp.where(first, rpb_ref[h * RPB_R + dr_a, d], rpb_ref[h * RPB_R + dr_b, d])
            acc = jnp.where(dc == d, val, acc)
        tt_ref[0, k] = acc


def _na_bias_table(rpb):
    return pl.pallas_call(
        _na_bias_kernel,
        grid=(N_HEADS,),
        in_specs=[pl.BlockSpec(memory_space=pltpu.SMEM)],
        out_specs=pl.BlockSpec((1, NA_TABLE, GRID_W, 2 * GRID_W), lambda h: (h, 0, 0, 0)),
        out_shape=jax.ShapeDtypeStruct((N_HEADS, NA_TABLE, GRID_W, 2 * GRID_W), F32),
        compiler_params=_cparams("arbitrary"),
        name="na_bias_table",
    )(rpb.reshape(N_HEADS * RPB_R, RPB_C))


def _lat_na_kernel(q_ref, k_ref, v_ref, ck_ref, cv_ref, tt_ref, o_ref):
    step = pl.program_id(1)
    past = ck_ref.shape[0]
    nwin = NA_KEY_ROWS * GRID_W
    r_q = [NA_Q_ROWS * step + a for a in range(NA_Q_ROWS)]
    r_start = [jnp.clip(r - WIN_R // 2, 0, GRID_ROWS - WIN_R) for r in r_q]
    ws = jnp.minimum(r_start[0], GRID_ROWS - NA_KEY_ROWS)
    row0 = pl.multiple_of(ws * GRID_W, GRID_W)
    lane = lax.broadcasted_iota(jnp.int32, (GRID_W, past + nwin), 1)
    key_row = ws + (lane - past) // GRID_W
    kc = lane % GRID_W
    qc = lax.broadcasted_iota(jnp.int32, (GRID_W, past + nwin), 0)
    c_start = jnp.clip(qc - WIN_W // 2, 0, GRID_W - WIN_W)
    col_ok = (kc >= c_start) & (kc < c_start + WIN_W)
    visible = jnp.concatenate(
        [(lane < past) | (col_ok & (key_row >= rs) & (key_row < rs + WIN_R)) for rs in r_start], axis=0)
    no_bias = jnp.zeros((GRID_W, past), F32)

    def bias_rows(h, r):
        tiles = [tt_ref[h, jnp.clip(ws + 2 * j - r + WIN_R, 0, NA_TABLE - 1)] for j in range(NA_KEY_ROWS // 2)]
        return jnp.concatenate([no_bias] + tiles, axis=1)

    q = [_split_pair(t) for t in _pair_tiles((q_ref[...] * ATTN_SCALE).astype(BF16))]
    kcat = _pair_tiles(jnp.concatenate([ck_ref[...], k_ref[pl.ds(row0, nwin), :]], axis=0).astype(BF16))
    vcat = [_with_ones(t) for t in
            _pair_tiles(jnp.concatenate([cv_ref[...], v_ref[pl.ds(row0, nwin), :]], axis=0).astype(BF16))]
    bias = [jnp.concatenate([bias_rows(h, r) for r in r_q], axis=0) for h in range(N_HEADS)]
    e = [[_exp_scores(jnp.where(visible, _bdot_nt(q[j][c], kcat[j]) + bias[2 * j + c], NEG_INF))
          for c in range(2)] for j in range(len(q))]
    out = [[jnp.dot(e[j][c], vcat[j], preferred_element_type=F32) for c in range(2)] for j in range(len(q))]
    for j in range(len(q)):
        o_ref[:, j * PAIR_W:(j + 1) * PAIR_W] = _merge_pair(out[j][0], out[j][1])


def _lat_na_attention(qn, kn, vn, cache_k, cache_v, tt, n_lat):
    n = qn.shape[0]
    nb = n // n_lat
    past = cache_k.shape[1]
    steps = GRID_ROWS // NA_Q_ROWS
    qblk = pl.BlockSpec((NA_Q_ROWS * GRID_W, D_GROUP), lambda b, r: (b * steps + r, 0))
    kvblk = pl.BlockSpec((n_lat, D_GROUP), lambda b, r: (b, 0))
    cblk = pl.BlockSpec((None, past, D_GROUP), lambda b, r: (b, 0, 0))
    return pl.pallas_call(
        _lat_na_kernel,
        grid=(nb, steps),
        in_specs=[qblk, kvblk, kvblk, cblk, cblk,
                  pl.BlockSpec(tt.shape, lambda b, r: (0, 0, 0, 0))],
        out_specs=qblk,
        out_shape=jax.ShapeDtypeStruct((n, D_GROUP), F32),
        compiler_params=_cparams("arbitrary", "arbitrary"),
        name="lat_na_attention",
    )(qn, kn, vn, cache_k, cache_v, tt)


def _rec_mix(of_ref, ob_ref, rg_ref, yf_ref, yb_ref, xr_ref, xk_ref, xv_ref, lo_ref, rk_ref, lng_ref, lnb_ref,
             gup_ref):
    def head_sums(x):
        return jnp.concatenate([_pair_head_sum(t) for t in _pair_tiles(x)], axis=1)

    inv_d = 1.0 / HEAD_DIM
    o = of_ref[...] + ob_ref[...]
    o_ret = _silu(rg_ref[...]) * (o * lax.rsqrt(head_sums(o * o) * inv_d + RMS_EPS))
    y = yf_ref[...] + yb_ref[...]
    yc = y - head_sums(y) * inv_d
    var = head_sums(yc * yc) * inv_d
    yn = yc * lax.rsqrt(var + RWKV_GN_EPS) * lng_ref[...] + lnb_ref[...]
    bonus = head_sums(xr_ref[...] * rk_ref[...] * xk_ref[...]) * xv_ref[...]
    g_rw = _bdot(_sigmoid(lo_ref[:, 2 * D_LORA:]), gup_ref[...])
    return o_ret, (yn + bonus) * g_rw


def _tail_kernel(*refs, final, ff_chunk, n_mix):
    x_ref, mix_refs = refs[0], refs[1:1 + n_mix]
    wo_ref, gm_ref, g_ref, sh_ref, sc_ref, gf_ref, wi_ref, wf_ref, fg_ref, o_ref = refs[1 + n_mix:]
    if n_mix == 2:
        mix_a, mix_b = mix_refs[0][...], mix_refs[1][...]
    else:
        mix_a, mix_b = _rec_mix(*mix_refs)
    mix = _bdot(mix_a, wo_ref[0:D_GROUP, :]) + _bdot(mix_b, wo_ref[D_GROUP:, :])
    x1 = x_ref[...] + gm_ref[0] * mix
    h = _rms(x1) * g_ref[...]
    hb = (h * (1.0 + sc_ref[0]) + sh_ref[0]).astype(BF16)
    acc = jnp.zeros_like(x1)
    for c0 in range(0, D_FF, ff_chunk):
        gate = jnp.dot(hb, wi_ref[:, c0:c0 + ff_chunk], preferred_element_type=F32)
        up = jnp.dot(hb, wi_ref[:, D_FF + c0:D_FF + c0 + ff_chunk], preferred_element_type=F32)
        acc = acc + jnp.dot((_silu(gate) * up).astype(BF16), wf_ref[c0:c0 + ff_chunk, :],
                            preferred_element_type=F32)
    x2 = x1 + gf_ref[0] * acc
    if final:
        x2 = _rms(x2) * fg_ref[...]
    o_ref[...] = x2


def _layer_tail(x, mix_inputs, w_out, gate_mix, gain, shift, scale, gate_ffn, w_ffn_in_all, w_ffn_out_all, layer,
                final_gain, rows_per_mod, final, ff_chunk=FFN_CHUNK):
    n, d = x.shape
    tm = TAIL_TM if len(mix_inputs) == 2 else TAIL_TM_REC
    tiles_per_mod = rows_per_mod // tm
    mod_spec = pl.BlockSpec((1, 1, d), lambda i: (i // tiles_per_mod, 0, 0))
    vec_spec = pl.BlockSpec((1, d), lambda i: (0, 0))
    resident = lambda a: pl.BlockSpec(a.shape, lambda i: (0,) * a.ndim, pipeline_mode=pl.Buffered(1))
    layer_slab = lambda a: pl.BlockSpec((None,) + a.shape[1:], lambda i: (layer, 0, 0),
                                        pipeline_mode=pl.Buffered(1))
    mix_specs = [pl.BlockSpec((tm, a.shape[1]), lambda i: (i, 0)) if a.shape[0] == n else resident(a)
                 for a in mix_inputs]
    return pl.pallas_call(
        functools.partial(_tail_kernel, final=final, ff_chunk=ff_chunk, n_mix=len(mix_inputs)),
        grid=(n // tm,),
        in_specs=[pl.BlockSpec((tm, d), lambda i: (i, 0))] + mix_specs
                 + [resident(w_out), mod_spec, vec_spec, mod_spec, mod_spec, mod_spec,
                    layer_slab(w_ffn_in_all), layer_slab(w_ffn_out_all), vec_spec],
        out_specs=pl.BlockSpec((tm, d), lambda i: (i, 0)),
        out_shape=jax.ShapeDtypeStruct((n, d), F32),
        compiler_params=_cparams("arbitrary"),
        name="layer_tail",
    )(x, *mix_inputs, w_out, gate_mix, gain.reshape(1, d), shift, scale, gate_ffn, w_ffn_in_all, w_ffn_out_all,
      final_gain.reshape(1, d))


def _log_sigmoid(x):
    return jnp.minimum(x, 0.0) - jnp.log(1.0 + jnp.exp(-jnp.abs(x)))


def _pair_block_diag(xp):
    left = lax.broadcasted_iota(jnp.int32, xp.shape, 1) < HEAD_DIM
    zero = jnp.zeros_like(xp)
    return jnp.concatenate([jnp.where(left, xp, zero), jnp.where(left, zero, xp)], axis=0)


def _states_to_pairs(state):
    nb = state.shape[0]
    s = state.reshape(nb, 2, N_HEADS // 2, 2, HEAD_DIM, HEAD_DIM)
    zero = jnp.zeros_like(s[:, :, :, 0])
    return jnp.concatenate([jnp.concatenate([s[:, :, :, 0], zero], axis=-1),
                            jnp.concatenate([zero, s[:, :, :, 1]], axis=-1)], axis=-2)


def _load_states(st, s0_ref):
    st[...] = jnp.zeros(st.shape, st.dtype) if s0_ref is None else s0_ref[0]


def _store_states(sf_ref, st):
    for d in range(st.shape[0]):
        for p in range(st.shape[1]):
            s = st[d, p]
            sf_ref[0, d, 2 * p] = s[:HEAD_DIM, :HEAD_DIM]
            sf_ref[0, d, 2 * p + 1] = s[HEAD_DIM:, HEAD_DIM:]


def _ret_kernel(*refs, zero_init):
    dlm_ref, dlq_ref, qf_ref, kf_ref, vf_ref, qb_ref, kb_ref, vb_ref = refs[:8]
    s0_ref = None if zero_init else refs[8]
    of_ref, ob_ref, sf_ref, st, mask_scr, qdec_scr, kdec_scr, cdec_scr = refs[8 if zero_init else 9:]
    c = pl.program_id(1)
    cs = RET_CHUNK
    pw = 2 * HEAD_DIM
    npair = N_HEADS // 2
    chains = [(d, p) for d in range(2) for p in range(npair)]

    @pl.when(c == 0)
    def _():
        _load_states(st, s0_ref)

    @pl.when((pl.program_id(0) == 0) & (c == 0))
    def _():
        row = lax.broadcasted_iota(jnp.int32, (cs, 2 * cs), 0)
        col = lax.broadcasted_iota(jnp.int32, (cs, 2 * cs), 1) % cs
        tok = lax.broadcasted_iota(jnp.int32, (cs, pw), 0)
        dist = [row - col, col - row]
        tq = [tok.astype(F32), (cs - 1 - tok).astype(F32)]
        for i, (d, p) in enumerate(chains):
            lgm = _log_sigmoid(dlm_ref[d, p])[0:1, :]
            lgq = _log_sigmoid(dlq_ref[d, p])
            mask_scr[i] = jnp.where(dist[d] >= 0, jnp.exp(jnp.maximum(dist[d], 0).astype(F32) * lgm), 0.0)
            qdec_scr[i] = jnp.exp((tq[d] + 1.0) * lgq[0:1, :])
            kdec_scr[i] = jnp.exp((cs - 1.0 - tq[d]) * lgq[0:1, :])
            cdec_scr[i] = jnp.exp(cs * lgq)

    brow = lax.broadcasted_iota(jnp.int32, (pw, pw), 0) // HEAD_DIM
    bcol = lax.broadcasted_iota(jnp.int32, (pw, pw), 1) // HEAD_DIM
    same_head = brow == bcol
    refs = [(qf_ref, kf_ref, vf_ref, of_ref), (qb_ref, kb_ref, vb_ref, ob_ref)]
    nsub = RET_CHUNKS_PER_STEP
    work = [(t, sub) for sub in range(nsub) for t in range(len(chains))]
    sl_of = lambda t: slice(chains[t][1] * pw, (chains[t][1] + 1) * pw)
    rows_of = lambda sub: pl.ds(sub * cs, cs)
    q = {(t, sub): refs[chains[t][0]][0][rows_of(sub), sl_of(t)] for t, sub in work}
    k = {(t, sub): refs[chains[t][0]][1][rows_of(sub), sl_of(t)] * ATTN_SCALE for t, sub in work}
    v = {(t, sub): refs[chains[t][0]][2][rows_of(sub), sl_of(t)] for t, sub in work}
    s = {w: _bdot_nt(q[w], _pair_block_diag(k[w])) * mask_scr[w[0]] for w in work}
    inner = {w: _bdot(s[w], _pair_block_diag(v[w])) for w in work}
    kv = {w: _bdot_tn(k[w] * kdec_scr[w[0]], v[w]) for w in work}
    for step in range(nsub):
        now = [(t, (step if chains[t][0] == 0 else nsub - 1 - step)) for t in range(len(chains))]
        state = {w: st[chains[w[0]]] for w in now}
        cross = {w: _bdot(q[w], state[w]) * qdec_scr[w[0]] for w in now}
        for w in now:
            t, sub = w
            d, p = chains[t]
            refs[d][3][rows_of(sub), sl_of(t)] = inner[w] + cross[w]
            st[d, p] = state[w] * cdec_scr[t][0:1, :] + jnp.where(same_head, kv[w], 0.0)

    @pl.when(c == pl.num_programs(1) - 1)
    def _():
        _store_states(sf_ref, st)


def _retention_scan(decay_logit, q, k, v, state0, seq):
    n = q.shape[0]
    nb = n // seq
    rows = RET_CHUNK * RET_CHUNKS_PER_STEP
    nc = seq // rows
    npair, pw = N_HEADS // 2, 2 * HEAD_DIM
    dl = decay_logit.astype(F32)
    dl_m = jnp.broadcast_to(jnp.repeat(dl, RET_CHUNK, axis=1).reshape(2, npair, 1, 2 * RET_CHUNK),
                            (2, npair, 8, 2 * RET_CHUNK))
    dl_q = jnp.broadcast_to(jnp.repeat(dl, HEAD_DIM, axis=1).reshape(2, npair, 1, pw), (2, npair, 8, pw))
    blk_f = pl.BlockSpec((rows, D_GROUP), lambda b, c: (b * nc + c, 0))
    blk_b = pl.BlockSpec((rows, D_GROUP), lambda b, c: (b * nc + nc - 1 - c, 0))
    st_in = ([] if state0 is None else
             [pl.BlockSpec((1, 2, npair, pw, pw), lambda b, c: (b, 0, 0, 0, 0))])
    st_out = pl.BlockSpec((1, 2, N_HEADS, HEAD_DIM, HEAD_DIM), lambda b, c: (b, 0, 0, 0, 0))
    return pl.pallas_call(
        functools.partial(_ret_kernel, zero_init=state0 is None),
        grid=(nb, nc),
        in_specs=[pl.BlockSpec(dl_m.shape, lambda b, c: (0, 0, 0, 0)),
                  pl.BlockSpec(dl_q.shape, lambda b, c: (0, 0, 0, 0)),
                  blk_f, blk_f, blk_f, blk_b, blk_b, blk_b] + st_in,
        out_specs=[blk_f, blk_b, st_out],
        out_shape=[jax.ShapeDtypeStruct((n, D_GROUP), F32), jax.ShapeDtypeStruct((n, D_GROUP), F32),
                   jax.ShapeDtypeStruct((nb, 2, N_HEADS, HEAD_DIM, HEAD_DIM), F32)],
        scratch_shapes=[pltpu.VMEM((2, npair, pw, pw), F32),
                        pltpu.VMEM((2 * npair, RET_CHUNK, 2 * RET_CHUNK), F32),
                        pltpu.VMEM((2 * npair, RET_CHUNK, pw), F32),
                        pltpu.VMEM((2 * npair, RET_CHUNK, pw), F32),
                        pltpu.VMEM((2 * npair, 8, pw), F32)],
        compiler_params=_cparams("arbitrary", "arbitrary"),
        name="retention_scan",
    )(dl_m, dl_q, q, k, v, q, k, v, *([] if state0 is None else [_states_to_pairs(state0)]))


def _rwkv_kernel(*refs, zero_init):
    (xrf_ref, xkf_ref, xvf_ref, lof_ref, xrb_ref, xkb_ref, xvb_ref, lob_ref,
     w0_ref, wup_ref, a0_ref, aup_ref, kk_ref, ka_ref) = refs[:14]
    s0_ref = None if zero_init else refs[14]
    yf_ref, yb_ref, sf_ref, st = refs[14 if zero_init else 15:]
    c = pl.program_id(1)
    cs = RWKV_CHUNK
    nsub = RWKV_CHUNKS_PER_STEP
    npair = N_HEADS // 2

    @pl.when(c == 0)
    def _():
        _load_states(st, s0_ref)

    row = lax.broadcasted_iota(jnp.int32, (cs, 3 * cs), 0)
    col = lax.broadcasted_iota(jnp.int32, (cs, 3 * cs), 1) % cs
    pw = 2 * HEAD_DIM
    prow = lax.broadcasted_iota(jnp.int32, (cs, pw), 0)
    pcol = lax.broadcasted_iota(jnp.int32, (cs, pw), 1) % HEAD_DIM
    eye = jnp.where(pcol == prow, 1.0, 0.0).astype(F32)
    sub_diag = (prow // RWKV_SUB) == (pcol // RWKV_SUB)
    brow = lax.broadcasted_iota(jnp.int32, (pw, pw), 0) // HEAD_DIM
    bcol = lax.broadcasted_iota(jnp.int32, (pw, pw), 1) // HEAD_DIM
    same_head = brow == bcol

    bd = _pair_block_diag

    def direction(d, sub, xr_ref, xk_ref, xv_ref, lo_ref):
        rows = pl.ds(sub * cs, cs)
        tri3 = jnp.where((row >= col) if d == 0 else (col >= row), 1.0, 0.0).astype(BF16)
        dist = (prow - pcol) if d == 0 else (pcol - prow)
        xk = xk_ref[rows, :]
        lo = lo_ref[rows, :]
        w_log = -math.exp(-0.5) * _sigmoid(w0_ref[d] + _bdot(jnp.tanh(lo[:, 0:D_LORA]), wup_ref[d]))
        a_all = _sigmoid(a0_ref[d] + _bdot(lo[:, D_LORA:2 * D_LORA], aup_ref[d]))
        w1 = w_log.astype(BF16)
        r1 = w_log - w1.astype(F32)
        w2 = r1.astype(BF16)
        w3 = (r1 - w2.astype(F32)).astype(BF16)
        cum = jnp.dot(tri3, jnp.concatenate([w1, w2, w3], axis=0), preferred_element_type=F32)
        cum_end = cum[cs - 1:cs, :] if d == 0 else cum[0:1, :]
        return dict(earlier=dist > 0, upto=dist >= 0, xr=xr_ref[rows, :], xv=xv_ref[rows, :], a_all=a_all,
                    e_incl=jnp.exp(cum), e_excl=jnp.exp(cum - w_log), e_neg=jnp.exp(-cum),
                    e_end=jnp.exp(cum_end - cum), w_end=jnp.exp(cum_end), kk_all=xk * kk_ref[...],
                    keff_all=xk * (1.0 + (a_all - 1.0) * ka_ref[...]))

    in_refs = [(xrf_ref, xkf_ref, xvf_ref, lof_ref), (xrb_ref, xkb_ref, xvb_ref, lob_ref)]
    dirs = {(d, sub): direction(d, sub, *in_refs[d]) for d in range(2) for sub in range(nsub)}
    y_refs = [yf_ref, yb_ref]

    chains = [(d, sub, p) for d in range(2) for sub in range(nsub) for p in range(npair)]
    pairs = range(len(chains))
    sls = [slice(p * pw, (p + 1) * pw) for _, _, p in chains]
    pick = lambda name: [dirs[d, sub][name][:, sls[i]] for i, (d, sub, _) in enumerate(chains)]
    earlier = [dirs[d, sub]["earlier"] for d, sub, _ in chains]
    upto = [dirs[d, sub]["upto"] for d, sub, _ in chains]
    cat = lambda x, y: jnp.concatenate([x, y], axis=0)

    head_sum = _pair_head_sum

    e_incl, e_excl, e_neg, e_end, w_end = (pick(k) for k in ("e_incl", "e_excl", "e_neg", "e_end", "w_end"))
    kk = [x * lax.rsqrt(head_sum(x * x) + 1e-12) for x in pick("kk_all")]
    b = [kk[p] * a for p, a in zip(pairs, pick("a_all"))]
    keff = pick("keff_all")
    v = pick("xv")
    a_t = [-kk[p] * e_excl[p] for p in pairs]
    r_t = [xr * e_incl[p] for p, xr in zip(pairs, pick("xr"))]
    ar = [cat(a_t[p], r_t[p]).astype(BF16) for p in pairs]
    g = [_bdot_nt(ar[p], cat(bd(b[p] * e_neg[p]), bd(keff[p] * e_neg[p]))) for p in pairs]
    n_mat = [jnp.where(earlier[p], g[p][:cs, :pw], 0.0) for p in pairs]
    m_rb = [jnp.where(upto[p], g[p][cs:, :pw], 0.0) for p in pairs]
    a_ak = [jnp.where(earlier[p], g[p][:cs, pw:], 0.0) for p in pairs]
    m_rk = [jnp.where(upto[p], g[p][cs:, pw:], 0.0) for p in pairs]
    bd_s = lambda xs: (bd(xs[0]), bd(xs[1]))
    n_d = [jnp.where(sub_diag, n, 0.0) for n in n_mat]
    n_o = [jnp.where(sub_diag, 0.0, n) for n in n_mat]
    n_s = [_split2(n) for n in n_d]
    x = [eye + n for n in n_d]
    pk = [_mm3s(ns, bd_s(ns)) for ns in n_s]
    for _ in range(RWKV_SUB_DOUBLINGS - 1):
        xp = [_mm3s(_split2(cat(x[p], pk[p])), bd_s(_split2(pk[p]))) for p in pairs]
        x = [x[p] + xp[p][:cs] for p in pairs]
        pk = [r[cs:] for r in xp]
    t_d = [x[p] + _mm3s(_split2(x[p]), bd_s(_split2(pk[p]))) for p in pairs]
    lcat = lambda x, y: jnp.concatenate([x, y], axis=1)
    m_o = [_bdot(t_d[p], bd(n_o[p])) for p in pairs]
    mz = [_bdot(m_o[p], lcat(bd(t_d[p]), bd(m_o[p]))) for p in pairs]
    z = [t_d[p] + mz[p][:, :pw] for p in pairs]
    t_inv = [(z[p] + _bdot(mz[p][:, pw:], bd(z[p]))).astype(BF16) for p in pairs]
    av = [_bdot(cat(a_ak[p], m_rk[p]), bd(v[p])) for p in pairs]
    tau = [_bdot(t_inv[p], lcat(bd(a_t[p]), bd(av[p][:cs]))) for p in pairs]
    bk_end = [cat(b[p] * e_end[p], keff[p] * e_end[p]) for p in pairs]
    for step in range(nsub):
        now = [i for i, (d, sub, _) in enumerate(chains) if sub == (step if d == 0 else nsub - 1 - step)]
        s_prev = {i: st[chains[i][0], chains[i][2]] for i in now}
        hs = {i: _bdot_nt(cat(tau[i][:, :pw], r_t[i]), s_prev[i]) for i in now}
        u = {i: hs[i][:cs] + tau[i][:, pw:] for i in now}
        y = {i: hs[i][cs:] + _bdot(m_rb[i], bd(u[i])) + av[i][cs:] for i in now}
        for i in now:
            d, sub, p = chains[i]
            y_refs[d][pl.ds(sub * cs, cs), sls[i]] = y[i]
            st[d, p] = s_prev[i] * w_end[i] + jnp.where(same_head, _bdot_tn(cat(u[i], v[i]), bk_end[i]), 0.0)

    @pl.when(c == pl.num_programs(1) - 1)
    def _():
        _store_states(sf_ref, st)


def _rwkv_scan(xr, xk, xv, lora, w0, w_up, a0, a_up, k_k, k_a, state0, seq):
    n = xr.shape[0]
    nb = n // seq
    blk = RWKV_CHUNK * RWKV_CHUNKS_PER_STEP
    nc = seq // blk

    fwd_idx = lambda b, c: (b * nc + c, 0)
    bwd_idx = lambda b, c: (b * nc + nc - 1 - c, 0)
    blk_f = pl.BlockSpec((blk, D_GROUP), fwd_idx)
    blk_b = pl.BlockSpec((blk, D_GROUP), bwd_idx)
    lblk_f = pl.BlockSpec((blk, lora.shape[1]), fwd_idx)
    lblk_b = pl.BlockSpec((blk, lora.shape[1]), bwd_idx)
    dvec = pl.BlockSpec((2, 1, D_GROUP), lambda b, c: (0, 0, 0))
    dmat = pl.BlockSpec((2, D_LORA, D_GROUP), lambda b, c: (0, 0, 0))
    vec = pl.BlockSpec((1, D_GROUP), lambda b, c: (0, 0))
    npair, pw = N_HEADS // 2, 2 * HEAD_DIM
    st_in = ([] if state0 is None else
             [pl.BlockSpec((1, 2, npair, pw, pw), lambda b, c: (b, 0, 0, 0, 0))])
    st_out = pl.BlockSpec((1, 2, N_HEADS, HEAD_DIM, HEAD_DIM), lambda b, c: (b, 0, 0, 0, 0))
    return pl.pallas_call(
        functools.partial(_rwkv_kernel, zero_init=state0 is None),
        grid=(nb, nc),
        in_specs=[blk_f, blk_f, blk_f, lblk_f, blk_b, blk_b, blk_b, lblk_b,
                  dvec, dmat, dvec, dmat, vec, vec] + st_in,
        out_specs=[blk_f, blk_b, st_out],
        out_shape=[jax.ShapeDtypeStruct((n, D_GROUP), F32), jax.ShapeDtypeStruct((n, D_GROUP), F32),
                   jax.ShapeDtypeStruct((nb, 2, N_HEADS, HEAD_DIM, HEAD_DIM), F32)],
        scratch_shapes=[pltpu.VMEM((2, npair, pw, pw), F32)],
        compiler_params=_cparams("arbitrary", "arbitrary"),
        name="rwkv7_scan",
    )(xr, xk, xv, lora, xr, xk, xv, lora, w0.reshape(2, 1, D_GROUP), w_up, a0.reshape(2, 1, D_GROUP), a_up,
      k_k.reshape(1, D_GROUP), k_a.reshape(1, D_GROUP), *([] if state0 is None else [_states_to_pairs(state0)]))


ATTN_WIDTHS = (D_GROUP,) * 6
REC_WIDTHS = (D_GROUP,) * 7 + (2 * D_LORA + D_LORA_G,)


def kernel(x_prompt, x_sample, cache_na_k, cache_na_v, cache_diff_k, cache_diff_v, state_ret, state_rwkv, c, c_ctx, norm_mix_g, norm_ffn_g, norm_final_g, w_ada, b_ada, w_in_attn, w_out_attn, na_rpb, diff_lq1, diff_lk1, diff_lq2, diff_lk2, w_in_rec, w_out_rec, ret_decay_logit, rw_w0, rw_w_up, rw_a0, rw_a_up, rw_g_up, rw_k_k, rw_k_a, rw_r_k, rw_ln_g, rw_ln_b, w_ffn_in, w_ffn_out):
    bp, seq, d = x_prompt.shape
    bl, n_lat, _ = x_sample.shape
    depth = w_ada.shape[0]
    ctx = x_prompt.reshape(bp * seq, d)
    lat = x_sample.reshape(bl * n_lat, d)

    cond8 = jnp.zeros((8, d), F32).at[0].set(c_ctx).at[1:1 + bl].set(c)
    mods = _ada_modulation(cond8, w_ada, b_ada).reshape(depth, 8, 6, d)
    w_ffn_in_bf16 = w_ffn_in.astype(BF16)
    w_ffn_out_bf16 = w_ffn_out.astype(BF16)

    outs = {}
    for layer in range(depth):
        m_ctx = [mods[layer, 0:1, j].reshape(1, 1, d) for j in range(6)]
        m_lat = [mods[layer, 1:1 + bl, j].reshape(bl, 1, d) for j in range(6)]
        if layer % 2 == 0:
            i = layer // 2
            lam_init = 0.8 - 0.6 * math.exp(-0.3 * layer)
            lam_params = jnp.stack([diff_lq1[i], diff_lk1[i], diff_lq2[i], diff_lk2[i]]).astype(F32)
            w_in = w_in_attn[i].astype(BF16)
            w_out = w_out_attn[i].astype(BF16)
            res = _ctx_attention(ctx, norm_mix_g[layer], m_ctx[0], m_ctx[1], w_in, lam_params, seq, lam_init)
            mix_ctx = res[:2]
            outs["na_k"], outs["na_v"], outs["df_k"], outs["df_v"] = res[2:]
            plat = _norm_mod_matmul(lat, norm_mix_g[layer], m_lat[0], m_lat[1], w_in, ATTN_WIDTHS, n_lat, PROJ_TM)
            past = cache_na_k.shape[2]
            tt = _na_bias_table(na_rpb[i])
            o_na = _lat_na_attention(plat[0], plat[1], plat[2],
                                     cache_na_k[:, i].reshape(bl, past, D_GROUP),
                                     cache_na_v[:, i].reshape(bl, past, D_GROUP), tt, n_lat)
            o_df = _lat_diff_attention(lam_params, plat[3], plat[4], plat[5],
                                       cache_diff_k[:, i].reshape(bl, past, D_GROUP),
                                       cache_diff_v[:, i].reshape(bl, past, D_GROUP), n_lat, lam_init)
            mix_lat = (o_na, o_df)
        else:
            j = layer // 2
            w_in = w_in_rec[j].astype(BF16)
            w_out = w_out_rec[j].astype(BF16)
            pc = _norm_mod_matmul(ctx, norm_mix_g[layer], m_ctx[0], m_ctx[1], w_in, REC_WIDTHS, bp * seq, PROJ_TM)
            plat = _norm_mod_matmul(lat, norm_mix_g[layer], m_lat[0], m_lat[1], w_in, REC_WIDTHS, n_lat, PROJ_TM)
            mixes = []
            for is_ctx, p, s_ret0, s_rw0, sq in ((True, pc, None, None, seq),
                                                 (False, plat, state_ret[:, j], state_rwkv[:, j], n_lat)):
                rq, rk, rv, rg, wr, wk, wv, lora = p
                o_f, o_b, s_ret = _retention_scan(ret_decay_logit[j], rq, rk, rv, s_ret0, sq)
                y_f, y_b, s_rw = _rwkv_scan(wr, wk, wv, lora, rw_w0[j], rw_w_up[j].astype(BF16), rw_a0[j],
                                            rw_a_up[j].astype(BF16), rw_k_k[j], rw_k_a[j], s_rw0, sq)
                mixes.append((o_f, o_b, rg, y_f, y_b, wr, wk, wv, lora, rw_r_k[j].reshape(1, D_GROUP),
                              rw_ln_g[j].reshape(1, D_GROUP), rw_ln_b[j].reshape(1, D_GROUP),
                              rw_g_up[j].astype(BF16)))
                if is_ctx:
                    outs["ret"], outs["rwkv"] = s_ret, s_rw
            mix_ctx, mix_lat = mixes
        final = layer == depth - 1
        ctx = _layer_tail(ctx, mix_ctx, w_out, m_ctx[2], norm_ffn_g[layer], m_ctx[3], m_ctx[4], m_ctx[5],
                          w_ffn_in_bf16, w_ffn_out_bf16, layer, norm_final_g, bp * seq, final)
        lat = _layer_tail(lat, mix_lat, w_out, m_lat[2], norm_ffn_g[layer], m_lat[3], m_lat[4], m_lat[5],
                          w_ffn_in_bf16, w_ffn_out_bf16, layer, norm_final_g, n_lat, final)

    y_prompt = ctx.reshape(bp, seq, d)
    y_sample = lat.reshape(bl, n_lat, d)
    new_cache_na_k = outs["na_k"].reshape(bp, 1, seq, N_HEADS, HEAD_DIM)
    new_cache_na_v = outs["na_v"].reshape(bp, 1, seq, N_HEADS, HEAD_DIM)
    new_cache_diff_k = outs["df_k"].reshape(bp, 1, seq, N_HEADS // 2, 2, HEAD_DIM)
    new_cache_diff_v = outs["df_v"].reshape(bp, 1, seq, N_HEADS // 2, 2 * HEAD_DIM)
    new_state_ret = outs["ret"].reshape(bp, 1, 2, N_HEADS, HEAD_DIM, HEAD_DIM)
    new_state_rwkv = outs["rwkv"].reshape(bp, 1, 2, N_HEADS, HEAD_DIM, HEAD_DIM)
    return (y_prompt, y_sample, new_cache_na_k, new_cache_na_v, new_cache_diff_k, new_cache_diff_v,
            new_state_ret, new_state_rwkv)
```

```python
import functools
import math

import jax
import jax.numpy as jnp
from jax import lax
from jax.experimental import pallas as pl
from jax.experimental.pallas import tpu as pltpu

F32 = jnp.float32
BF16 = jnp.bfloat16

HEAD_DIM = 64
N_HEADS = 8
D_GROUP = 512
GRID_W = 64
GRID_ROWS = 32
WIN_R = 8
WIN_W = 16
RPB_R = 15
RPB_C = 31
D_FF = 2816
FFN_CHUNK = D_FF // 2
D_LORA = 64
D_LORA_G = 128
ROPE_BASE = 10000.0
RMS_EPS = 1e-6
RWKV_GN_EPS = 64e-5
NEG_INF = -1e30
ATTN_SCALE = HEAD_DIM ** -0.5
RET_CHUNK = 128
RET_CHUNKS_PER_STEP = 2
RWKV_CHUNK = 64
RWKV_CHUNKS_PER_STEP = 4
RWKV_SUB = 16
RWKV_SUB_DOUBLINGS = 3
V7X_VMEM_BYTES = 64 * 1024 * 1024
VMEM_LIMIT = V7X_VMEM_BYTES - 8 * 1024 * 1024
ADA_TN = 1536
TAIL_TM = 512
TAIL_TM_REC = 256
PROJ_TM = 512
CTX_SEQS_PER_STEP = 1
LAT_DIFF_TQ = 256


def _cparams(*sem):
    return pltpu.CompilerParams(dimension_semantics=sem, vmem_limit_bytes=VMEM_LIMIT)


def _sigmoid(x):
    return 1.0 / (1.0 + jnp.exp(-x))


def _silu(x):
    return x * _sigmoid(x)


def _rms(x, eps=RMS_EPS):
    return x * lax.rsqrt(jnp.mean(x * x, axis=-1, keepdims=True) + eps)


def _bdot(a, b):
    return jnp.dot(a.astype(BF16), b.astype(BF16), preferred_element_type=F32)


def _bdot_nt(a, b):
    return lax.dot_general(a.astype(BF16), b.astype(BF16), (((1,), (1,)), ((), ())),
                           preferred_element_type=F32)


def _bdot_tn(a, b):
    return lax.dot_general(a.astype(BF16), b.astype(BF16), (((0,), (0,)), ((), ())),
                           preferred_element_type=F32)


def _split2(x):
    hi = x.astype(BF16)
    lo = (x - hi.astype(F32)).astype(BF16)
    return hi, lo


def _mm3s(a_split, b_split):
    ah, al = a_split
    bh, bl = b_split
    n = bh.shape[1]
    rhs = jnp.concatenate([jnp.concatenate([bh, bl], axis=1),
                           jnp.concatenate([bh, jnp.zeros_like(bl)], axis=1)], axis=0)
    out = jnp.dot(jnp.concatenate([ah, al], axis=1), rhs, preferred_element_type=F32)
    return out[:, :n] + out[:, n:]


def _diff_lambda(lp, lam_init):
    s1 = jnp.sum(lp[0:1, :] * lp[1:2, :], axis=-1, keepdims=True)
    s2 = jnp.sum(lp[2:3, :] * lp[3:4, :], axis=-1, keepdims=True)
    return jnp.exp(s1) - jnp.exp(s2) + lam_init


def _ada_kernel(c_ref, w_ref, b_ref, o_ref):
    s = _silu(c_ref[...])
    o_ref[0] = _bdot(s, w_ref[0]) + b_ref[0]


def _ada_modulation(cond8, w_ada, b_ada):
    depth, d, n = w_ada.shape
    tn = ADA_TN
    return pl.pallas_call(
        _ada_kernel,
        grid=(depth, n // tn),
        in_specs=[pl.BlockSpec((8, d), lambda l, j: (0, 0)),
                  pl.BlockSpec((1, d, tn), lambda l, j: (l, 0, j)),
                  pl.BlockSpec((1, 1, tn), lambda l, j: (l, 0, j))],
        out_specs=pl.BlockSpec((1, 8, tn), lambda l, j: (l, 0, j)),
        out_shape=jax.ShapeDtypeStruct((depth, 8, n), F32),
        compiler_params=_cparams("arbitrary", "arbitrary"),
        name="ada_modulation",
    )(cond8, w_ada, b_ada.reshape(depth, 1, n))


def _norm_mod(x_ref, g_ref, sh_ref, sc_ref):
    h = _rms(x_ref[...]) * g_ref[...]
    return (h * (1.0 + sc_ref[0]) + sh_ref[0]).astype(BF16)


def _nmm_kernel(x_ref, g_ref, sh_ref, sc_ref, w_ref, *o_refs, widths):
    hb = _norm_mod(x_ref, g_ref, sh_ref, sc_ref)
    off = 0
    for o_ref, w in zip(o_refs, widths):
        o_ref[...] = jnp.dot(hb, w_ref[:, off:off + w], preferred_element_type=F32)
        off += w


def _norm_mod_matmul(x, gain, shift, scale, w_bf16, widths, rows_per_mod, tm):
    n, d = x.shape
    tiles_per_mod = rows_per_mod // tm
    mod_spec = pl.BlockSpec((1, 1, d), lambda i: (i // tiles_per_mod, 0, 0))
    return pl.pallas_call(
        functools.partial(_nmm_kernel, widths=widths),
        grid=(n // tm,),
        in_specs=[pl.BlockSpec((tm, d), lambda i: (i, 0)),
                  pl.BlockSpec((1, d), lambda i: (0, 0)),
                  mod_spec, mod_spec,
                  pl.BlockSpec(w_bf16.shape, lambda i: (0, 0), pipeline_mode=pl.Buffered(1))],
        out_specs=[pl.BlockSpec((tm, w), lambda i: (i, 0)) for w in widths],
        out_shape=[jax.ShapeDtypeStruct((n, w), F32) for w in widths],
        compiler_params=_cparams("arbitrary"),
        name="norm_mod_matmul",
    )(x, gain.reshape(1, d), shift, scale, w_bf16)


PAIR_W = 2 * HEAD_DIM


def _pair_tiles(x):
    return [x[:, j * PAIR_W:(j + 1) * PAIR_W] for j in range(x.shape[1] // PAIR_W)]


def _split_pair(tile):
    left = lax.broadcasted_iota(jnp.int32, tile.shape, 1) < HEAD_DIM
    zero = jnp.zeros_like(tile)
    return jnp.where(left, tile, zero), jnp.where(left, zero, tile)


def _pair_head_sum(tile):
    left = lax.broadcasted_iota(jnp.int32, tile.shape, 1) < HEAD_DIM
    zero = jnp.zeros_like(tile)
    s_left = jnp.sum(jnp.where(left, tile, zero), axis=-1, keepdims=True)
    s_right = jnp.sum(jnp.where(left, zero, tile), axis=-1, keepdims=True)
    return jnp.where(left, s_left, s_right)


def _with_ones(v):
    return jnp.concatenate([v, jnp.ones_like(v)], axis=1)


def _exp_scores(s):
    return jnp.exp(s - jnp.max(s, axis=-1, keepdims=True)).astype(BF16)


def _merge_pair(out_even, out_odd):
    left = lax.broadcasted_iota(jnp.int32, (out_even.shape[0], PAIR_W), 1) < HEAD_DIM
    return jnp.where(left, out_even[:, :PAIR_W] * (1.0 / out_even[:, PAIR_W:]),
                     out_odd[:, :PAIR_W] * (1.0 / out_odd[:, PAIR_W:]))


def _diff_combine(out1, out2, lam, lam_init):
    o = out1[:, :PAIR_W] * (1.0 / out1[:, PAIR_W:]) - out2[:, :PAIR_W] * (lam / out2[:, PAIR_W:])
    return _rms(o) * (1.0 - lam_init)


CTX_CACHE_SHAPES = ((N_HEADS, HEAD_DIM), (N_HEADS, HEAD_DIM),
                    (N_HEADS // 2, 2, HEAD_DIM), (N_HEADS // 2, 2 * HEAD_DIM))


def _ctx_attn_kernel(x_ref, g_ref, sh_ref, sc_ref, w_ref, lp_ref, on_ref, od_ref, *cache_refs, lam_init, seq):
    hb = _norm_mod(x_ref, g_ref, sh_ref, sc_ref)
    proj = [jnp.dot(hb, w_ref[:, i * D_GROUP:(i + 1) * D_GROUP], preferred_element_type=F32) for i in range(6)]
    for c_ref, val in zip(cache_refs, (proj[1], proj[2], proj[4], proj[5])):
        c_ref[...] = val.reshape(c_ref.shape)
    tiles = range(D_GROUP // PAIR_W)
    lam = _diff_lambda(lp_ref[...], lam_init)
    for s in range(hb.shape[0] // seq):
        rows = slice(s * seq, (s + 1) * seq)
        qn, kn, vn, qd, kd, vd = (p[rows] for p in proj)
        q = [_split_pair(t) for t in _pair_tiles((qn * ATTN_SCALE).astype(BF16))
             + _pair_tiles((qd * ATTN_SCALE).astype(BF16))]
        k = _pair_tiles(kn.astype(BF16)) + _pair_tiles(kd.astype(BF16))
        v = [_with_ones(t) for t in _pair_tiles(vn.astype(BF16)) + _pair_tiles(vd.astype(BF16))]
        e = [[_exp_scores(_bdot_nt(q[j][c], k[j])) for c in range(2)] for j in range(2 * len(tiles))]
        out = [[jnp.dot(e[j][c], v[j], preferred_element_type=F32) for c in range(2)]
               for j in range(2 * len(tiles))]
        for j in tiles:
            sl = slice(j * PAIR_W, (j + 1) * PAIR_W)
            on_ref[rows, sl] = _merge_pair(out[j][0], out[j][1])
            od_ref[rows, sl] = _diff_combine(out[len(tiles) + j][0], out[len(tiles) + j][1], lam, lam_init)


def _ctx_attention(x, gain, shift, scale, w_bf16, lam_params, seq, lam_init):
    n, d = x.shape
    tm = seq * CTX_SEQS_PER_STEP
    blk = pl.BlockSpec((tm, D_GROUP), lambda b: (b, 0))
    mod_spec = pl.BlockSpec((1, 1, d), lambda b: (0, 0, 0))
    cache_specs = [pl.BlockSpec((tm,) + dims, lambda b, nd=len(dims): (b,) + (0,) * nd)
                   for dims in CTX_CACHE_SHAPES]
    return pl.pallas_call(
        functools.partial(_ctx_attn_kernel, lam_init=lam_init, seq=seq),
        grid=(n // tm,),
        in_specs=[pl.BlockSpec((tm, d), lambda b: (b, 0)),
                  pl.BlockSpec((1, d), lambda b: (0, 0)),
                  mod_spec, mod_spec,
                  pl.BlockSpec(w_bf16.shape, lambda b: (0, 0), pipeline_mode=pl.Buffered(1)),
                  pl.BlockSpec(lam_params.shape, lambda b: (0, 0))],
        out_specs=[blk, blk] + cache_specs,
        out_shape=([jax.ShapeDtypeStruct((n, D_GROUP), F32)] * 2
                   + [jax.ShapeDtypeStruct((n,) + dims, F32) for dims in CTX_CACHE_SHAPES]),
        compiler_params=_cparams("arbitrary"),
        name="ctx_attention",
    )(x, gain.reshape(1, d), shift, scale, w_bf16, lam_params)


def _rope_tables(n):
    quarter = HEAD_DIM // 4
    pos = jnp.arange(n)
    inv_freq = ROPE_BASE ** (-jnp.arange(quarter, dtype=F32) / quarter)
    lane = jnp.arange(HEAD_DIM)
    p = jnp.where(lane[None, :] < HEAD_DIM // 2, (pos // GRID_W)[:, None], (pos % GRID_W)[:, None]).astype(F32)
    ang = p * inv_freq[lane % quarter][None, :]
    sign = jnp.where((lane % (2 * quarter)) < quarter, -1.0, 1.0).astype(F32)
    cos = jnp.tile(jnp.cos(ang), (1, 2))
    sin = jnp.tile(jnp.sin(ang) * sign[None, :], (1, 2))
    return cos, sin


def _rope(x, cos, sin_signed):
    quarter = HEAD_DIM // 4
    lane = lax.broadcasted_iota(jnp.int32, cos.shape, 1)
    first = (lane % (2 * quarter)) < quarter
    tiles = []
    for t in _pair_tiles(x):
        partner = jnp.where(first, pltpu.roll(t, PAIR_W - quarter, 1), pltpu.roll(t, quarter, 1))
        tiles.append(t * cos + partner * sin_signed)
    return jnp.concatenate(tiles, axis=1)


def _lat_diff_kernel(lp_ref, q_ref, k_ref, v_ref, ck_ref, cv_ref, cosq_ref, sinq_ref, cosk_ref, sink_ref,
                     o_ref, kall, vall, *, lam_init, n_lat):
    @pl.when(pl.program_id(1) == 0)
    def _():
        kall[0:n_lat, :] = _rope(k_ref[...], cosk_ref[...], sink_ref[...]).astype(BF16)
        kall[n_lat:, :] = ck_ref[...].astype(BF16)
        for j, (vt, ct) in enumerate(zip(_pair_tiles(v_ref[...].astype(BF16)),
                                         _pair_tiles(cv_ref[...].astype(BF16)))):
            vall[j, 0:n_lat, :] = _with_ones(vt)
            vall[j, n_lat:, :] = _with_ones(ct)

    lam = _diff_lambda(lp_ref[...], lam_init)
    q = (_rope(q_ref[...], cosq_ref[...], sinq_ref[...]) * ATTN_SCALE).astype(BF16)
    qs = [_split_pair(qt) for qt in _pair_tiles(q)]
    sls = [slice(j * PAIR_W, (j + 1) * PAIR_W) for j in range(len(qs))]
    for group in ((0, 1), (2, 3)):
        e = {j: [_exp_scores(_bdot_nt(qs[j][c], kall[:, sls[j]])) for c in range(2)] for j in group}
        out = {j: [jnp.dot(e[j][c], vall[j], preferred_element_type=F32) for c in range(2)] for j in group}
        for j in group:
            o_ref[:, sls[j]] = _diff_combine(out[j][0], out[j][1], lam, lam_init)


def _lat_diff_attention(lam_params, qd, kd, vd, cache_k, cache_v, n_lat, lam_init, tq=LAT_DIFF_TQ):
    n = qd.shape[0]
    nb = n // n_lat
    nq = n_lat // tq
    past = cache_k.shape[1]
    cos, sin = _rope_tables(n_lat)
    qblk = pl.BlockSpec((tq, D_GROUP), lambda b, i: (b * nq + i, 0))
    once = pl.Buffered(1)
    kvblk = pl.BlockSpec((n_lat, D_GROUP), lambda b, i: (b, 0), pipeline_mode=once)
    cblk = pl.BlockSpec((None, past, D_GROUP), lambda b, i: (b, 0, 0))
    return pl.pallas_call(
        functools.partial(_lat_diff_kernel, lam_init=lam_init, n_lat=n_lat),
        grid=(nb, nq),
        in_specs=[pl.BlockSpec(lam_params.shape, lambda b, i: (0, 0)),
                  qblk, kvblk, kvblk, cblk, cblk,
                  pl.BlockSpec((tq, PAIR_W), lambda b, i: (i, 0)),
                  pl.BlockSpec((tq, PAIR_W), lambda b, i: (i, 0)),
                  pl.BlockSpec((n_lat, PAIR_W), lambda b, i: (0, 0), pipeline_mode=once),
                  pl.BlockSpec((n_lat, PAIR_W), lambda b, i: (0, 0), pipeline_mode=once)],
        out_specs=qblk,
        out_shape=jax.ShapeDtypeStruct((n, D_GROUP), F32),
        scratch_shapes=[pltpu.VMEM((n_lat + past, D_GROUP), BF16),
                        pltpu.VMEM((D_GROUP // PAIR_W, n_lat + past, 2 * PAIR_W), BF16)],
        compiler_params=_cparams("arbitrary", "arbitrary"),
        name="lat_diff_attention",
    )(lam_params, qd, kd, vd, cache_k, cache_v, cos, sin, cos, sin)


NA_Q_ROWS = 4
NA_KEY_ROWS = WIN_R + NA_Q_ROWS
NA_TABLE = RPB_R + 1


def _na_bias_kernel(rpb_ref, tt_ref):
    h = pl.program_id(0)
    lane = lax.broadcasted_iota(jnp.int32, (GRID_W, 2 * GRID_W), 1)
    qc = lax.broadcasted_iota(jnp.int32, (GRID_W, 2 * GRID_W), 0)
    dc = jnp.clip(lane % GRID_W - qc + (WIN_W - 1), 0, RPB_C - 1)
    first = lane < GRID_W
    rows = []
    for dr in range(RPB_R):
        acc = jnp.zeros((GRID_W, 2 * GRID_W), F32)
        for d in range(RPB_C):
            acc = jnp.where(dc == d, rpb_ref[h * RPB_R + dr, d], acc)
        rows.append(acc)
    for k in range(NA_TABLE):
        tt_ref[0, k] = jnp.where(first, rows[min(max(k - 1, 0), RPB_R - 1)], rows[min(k, RPB_R - 1)])


def _na_bias_table(rpb):
    return pl.pallas_call(
        _na_bias_kernel,
        grid=(N_HEADS,),
        in_specs=[pl.BlockSpec(memory_space=pltpu.SMEM)],
        out_specs=pl.BlockSpec((1, NA_TABLE, GRID_W, 2 * GRID_W), lambda h: (h, 0, 0, 0)),
        out_shape=jax.ShapeDtypeStruct((N_HEADS, NA_TABLE, GRID_W, 2 * GRID_W), F32),
        compiler_params=_cparams("arbitrary"),
        name="na_bias_table",
    )(rpb.reshape(N_HEADS * RPB_R, RPB_C))


def _lat_na_kernel(q_ref, k_ref, v_ref, ck_ref, cv_ref, tt_ref, o_ref):
    step = pl.program_id(1)
    past = ck_ref.shape[0]
    nwin = NA_KEY_ROWS * GRID_W
    r_q = [NA_Q_ROWS * step + a for a in range(NA_Q_ROWS)]
    r_start = [jnp.clip(r - WIN_R // 2, 0, GRID_ROWS - WIN_R) for r in r_q]
    ws = jnp.minimum(r_start[0], GRID_ROWS - NA_KEY_ROWS)
    row0 = pl.multiple_of(ws * GRID_W, GRID_W)
    lane = lax.broadcasted_iota(jnp.int32, (GRID_W, past + nwin), 1)
    key_row = ws + (lane - past) // GRID_W
    kc = lane % GRID_W
    qc = lax.broadcasted_iota(jnp.int32, (GRID_W, past + nwin), 0)
    c_start = jnp.clip(qc - WIN_W // 2, 0, GRID_W - WIN_W)
    col_ok = (kc >= c_start) & (kc < c_start + WIN_W)
    visible = jnp.concatenate(
        [(lane < past) | (col_ok & (key_row >= rs) & (key_row < rs + WIN_R)) for rs in r_start], axis=0)
    no_bias = jnp.zeros((GRID_W, past), F32)

    def bias_rows(h, r):
        tiles = [tt_ref[h, jnp.clip(ws + 2 * j - r + WIN_R, 0, NA_TABLE - 1)] for j in range(NA_KEY_ROWS // 2)]
        return jnp.concatenate([no_bias] + tiles, axis=1)

    q = [_split_pair(t) for t in _pair_tiles((q_ref[...] * ATTN_SCALE).astype(BF16))]
    kcat = _pair_tiles(jnp.concatenate([ck_ref[...], k_ref[pl.ds(row0, nwin), :]], axis=0).astype(BF16))
    vcat = [_with_ones(t) for t in
            _pair_tiles(jnp.concatenate([cv_ref[...], v_ref[pl.ds(row0, nwin), :]], axis=0).astype(BF16))]
    bias = [jnp.concatenate([bias_rows(h, r) for r in r_q], axis=0) for h in range(N_HEADS)]
    e = [[_exp_scores(jnp.where(visible, _bdot_nt(q[j][c], kcat[j]) + bias[2 * j + c], NEG_INF))
          for c in range(2)] for j in range(len(q))]
    out = [[jnp.dot(e[j][c], vcat[j], preferred_element_type=F32) for c in range(2)] for j in range(len(q))]
    for j in range(len(q)):
        o_ref[:, j * PAIR_W:(j + 1) * PAIR_W] = _merge_pair(out[j][0], out[j][1])


def _lat_na_attention(qn, kn, vn, cache_k, cache_v, tt, n_lat):
    n = qn.shape[0]
    nb = n // n_lat
    past = cache_k.shape[1]
    steps = GRID_ROWS // NA_Q_ROWS
    qblk = pl.BlockSpec((NA_Q_ROWS * GRID_W, D_GROUP), lambda b, r: (b * steps + r, 0))
    kvblk = pl.BlockSpec((n_lat, D_GROUP), lambda b, r: (b, 0))
    cblk = pl.BlockSpec((None, past, D_GROUP), lambda b, r: (b, 0, 0))
    return pl.pallas_call(
        _lat_na_kernel,
        grid=(nb, steps),
        in_specs=[qblk, kvblk, kvblk, cblk, cblk,
                  pl.BlockSpec(tt.shape, lambda b, r: (0, 0, 0, 0))],
        out_specs=qblk,
        out_shape=jax.ShapeDtypeStruct((n, D_GROUP), F32),
        compiler_params=_cparams("arbitrary", "arbitrary"),
        name="lat_na_attention",
    )(qn, kn, vn, cache_k, cache_v, tt)


def _rec_mix(of_ref, ob_ref, rg_ref, yf_ref, yb_ref, xr_ref, xk_ref, xv_ref, lo_ref, rk_ref, lng_ref, lnb_ref,
             gup_ref):
    def head_sums(x):
        return jnp.concatenate([_pair_head_sum(t) for t in _pair_tiles(x)], axis=1)

    inv_d = 1.0 / HEAD_DIM
    o = of_ref[...] + ob_ref[...]
    o_ret = _silu(rg_ref[...]) * (o * lax.rsqrt(head_sums(o * o) * inv_d + RMS_EPS))
    y = yf_ref[...] + yb_ref[...]
    yc = y - head_sums(y) * inv_d
    var = head_sums(yc * yc) * inv_d
    yn = yc * lax.rsqrt(var + RWKV_GN_EPS) * lng_ref[...] + lnb_ref[...]
    bonus = head_sums(xr_ref[...] * rk_ref[...] * xk_ref[...]) * xv_ref[...]
    g_rw = _bdot(_sigmoid(lo_ref[:, 2 * D_LORA:]), gup_ref[...])
    return o_ret, (yn + bonus) * g_rw


def _tail_kernel(*refs, final, ff_chunk, n_mix):
    x_ref, mix_refs = refs[0], refs[1:1 + n_mix]
    wo_ref, gm_ref, g_ref, sh_ref, sc_ref, gf_ref, wi_ref, wf_ref, fg_ref, o_ref = refs[1 + n_mix:]
    if n_mix == 2:
        mix_a, mix_b = mix_refs[0][...], mix_refs[1][...]
    else:
        mix_a, mix_b = _rec_mix(*mix_refs)
    mix = _bdot(mix_a, wo_ref[0:D_GROUP, :]) + _bdot(mix_b, wo_ref[D_GROUP:, :])
    x1 = x_ref[...] + gm_ref[0] * mix
    h = _rms(x1) * g_ref[...]
    hb = (h * (1.0 + sc_ref[0]) + sh_ref[0]).astype(BF16)
    acc = jnp.zeros_like(x1)
    for c0 in range(0, D_FF, ff_chunk):
        gate = jnp.dot(hb, wi_ref[:, c0:c0 + ff_chunk], preferred_element_type=F32)
        up = jnp.dot(hb, wi_ref[:, D_FF + c0:D_FF + c0 + ff_chunk], preferred_element_type=F32)
        acc = acc + jnp.dot((_silu(gate) * up).astype(BF16), wf_ref[c0:c0 + ff_chunk, :],
                            preferred_element_type=F32)
    x2 = x1 + gf_ref[0] * acc
    if final:
        x2 = _rms(x2) * fg_ref[...]
    o_ref[...] = x2


def _layer_tail(x, mix_inputs, w_out, gate_mix, gain, shift, scale, gate_ffn, w_ffn_in_all, w_ffn_out_all, layer,
                final_gain, rows_per_mod, final, ff_chunk=FFN_CHUNK):
    n, d = x.shape
    tm = TAIL_TM if len(mix_inputs) == 2 else TAIL_TM_REC
    tiles_per_mod = rows_per_mod // tm
    mod_spec = pl.BlockSpec((1, 1, d), lambda i: (i // tiles_per_mod, 0, 0))
    vec_spec = pl.BlockSpec((1, d), lambda i: (0, 0))
    resident = lambda a: pl.BlockSpec(a.shape, lambda i: (0,) * a.ndim, pipeline_mode=pl.Buffered(1))
    layer_slab = lambda a: pl.BlockSpec((None,) + a.shape[1:], lambda i: (layer, 0, 0),
                                        pipeline_mode=pl.Buffered(1))
    mix_specs = [pl.BlockSpec((tm, a.shape[1]), lambda i: (i, 0)) if a.shape[0] == n else resident(a)
                 for a in mix_inputs]
    return pl.pallas_call(
        functools.partial(_tail_kernel, final=final, ff_chunk=ff_chunk, n_mix=len(mix_inputs)),
        grid=(n // tm,),
        in_specs=[pl.BlockSpec((tm, d), lambda i: (i, 0))] + mix_specs
                 + [resident(w_out), mod_spec, vec_spec, mod_spec, mod_spec, mod_spec,
                    layer_slab(w_ffn_in_all), layer_slab(w_ffn_out_all), vec_spec],
        out_specs=pl.BlockSpec((tm, d), lambda i: (i, 0)),
        out_shape=jax.ShapeDtypeStruct((n, d), F32),
        compiler_params=_cparams("arbitrary"),
        name="layer_tail",
    )(x, *mix_inputs, w_out, gate_mix, gain.reshape(1, d), shift, scale, gate_ffn, w_ffn_in_all, w_ffn_out_all,
      final_gain.reshape(1, d))


def _log_sigmoid(x):
    return jnp.minimum(x, 0.0) - jnp.log(1.0 + jnp.exp(-jnp.abs(x)))


def _pair_block_diag(xp):
    left = lax.broadcasted_iota(jnp.int32, xp.shape, 1) < HEAD_DIM
    zero = jnp.zeros_like(xp)
    return jnp.concatenate([jnp.where(left, xp, zero), jnp.where(left, zero, xp)], axis=0)


def _states_to_pairs(state):
    nb = state.shape[0]
    s = state.reshape(nb, 2, N_HEADS // 2, 2, HEAD_DIM, HEAD_DIM)
    zero = jnp.zeros_like(s[:, :, :, 0])
    return jnp.concatenate([jnp.concatenate([s[:, :, :, 0], zero], axis=-1),
                            jnp.concatenate([zero, s[:, :, :, 1]], axis=-1)], axis=-2)


def _load_states(st, s0_ref):
    st[...] = jnp.zeros(st.shape, st.dtype) if s0_ref is None else s0_ref[0]


def _store_states(sf_ref, st):
    for d in range(st.shape[0]):
        for p in range(st.shape[1]):
            s = st[d, p]
            sf_ref[0, d, 2 * p] = s[:HEAD_DIM, :HEAD_DIM]
            sf_ref[0, d, 2 * p + 1] = s[HEAD_DIM:, HEAD_DIM:]


def _ret_kernel(*refs, zero_init):
    dlm_ref, dlq_ref, qf_ref, kf_ref, vf_ref, qb_ref, kb_ref, vb_ref = refs[:8]
    s0_ref = None if zero_init else refs[8]
    of_ref, ob_ref, sf_ref, st, mask_scr, qdec_scr, kdec_scr, cdec_scr = refs[8 if zero_init else 9:]
    c = pl.program_id(1)
    cs = RET_CHUNK
    pw = 2 * HEAD_DIM
    npair = N_HEADS // 2
    chains = [(d, p) for d in range(2) for p in range(npair)]

    @pl.when(c == 0)
    def _():
        _load_states(st, s0_ref)

    @pl.when((pl.program_id(0) == 0) & (c == 0))
    def _():
        row = lax.broadcasted_iota(jnp.int32, (cs, 2 * cs), 0)
        col = lax.broadcasted_iota(jnp.int32, (cs, 2 * cs), 1) % cs
        tok = lax.broadcasted_iota(jnp.int32, (cs, pw), 0)
        dist = [row - col, col - row]
        tq = [tok.astype(F32), (cs - 1 - tok).astype(F32)]
        for i, (d, p) in enumerate(chains):
            lgm = _log_sigmoid(dlm_ref[d, p])[0:1, :]
            lgq = _log_sigmoid(dlq_ref[d, p])
            mask_scr[i] = jnp.where(dist[d] >= 0, jnp.exp(jnp.maximum(dist[d], 0).astype(F32) * lgm), 0.0)
            qdec_scr[i] = jnp.exp((tq[d] + 1.0) * lgq[0:1, :])
            kdec_scr[i] = jnp.exp((cs - 1.0 - tq[d]) * lgq[0:1, :])
            cdec_scr[i] = jnp.exp(cs * lgq)

    brow = lax.broadcasted_iota(jnp.int32, (pw, pw), 0) // HEAD_DIM
    bcol = lax.broadcasted_iota(jnp.int32, (pw, pw), 1) // HEAD_DIM
    same_head = brow == bcol
    refs = [(qf_ref, kf_ref, vf_ref, of_ref), (qb_ref, kb_ref, vb_ref, ob_ref)]
    nsub = RET_CHUNKS_PER_STEP
    work = [(t, sub) for sub in range(nsub) for t in range(len(chains))]
    sl_of = lambda t: slice(chains[t][1] * pw, (chains[t][1] + 1) * pw)
    rows_of = lambda sub: pl.ds(sub * cs, cs)
    q = {(t, sub): refs[chains[t][0]][0][rows_of(sub), sl_of(t)] for t, sub in work}
    k = {(t, sub): refs[chains[t][0]][1][rows_of(sub), sl_of(t)] * ATTN_SCALE for t, sub in work}
    v = {(t, sub): refs[chains[t][0]][2][rows_of(sub), sl_of(t)] for t, sub in work}
    s = {w: _bdot_nt(q[w], _pair_block_diag(k[w])) * mask_scr[w[0]] for w in work}
    inner = {w: _bdot(s[w], _pair_block_diag(v[w])) for w in work}
    kv = {w: _bdot_tn(k[w] * kdec_scr[w[0]], v[w]) for w in work}
    for step in range(nsub):
        now = [(t, (step if chains[t][0] == 0 else nsub - 1 - step)) for t in range(len(chains))]
        state = {w: st[chains[w[0]]] for w in now}
        cross = {w: _bdot(q[w], state[w]) * qdec_scr[w[0]] for w in now}
        for w in now:
            t, sub = w
            d, p = chains[t]
            refs[d][3][rows_of(sub), sl_of(t)] = inner[w] + cross[w]
            st[d, p] = state[w] * cdec_scr[t][0:1, :] + jnp.where(same_head, kv[w], 0.0)

    @pl.when(c == pl.num_programs(1) - 1)
    def _():
        _store_states(sf_ref, st)


def _retention_scan(decay_logit, q, k, v, state0, seq):
    n = q.shape[0]
    nb = n // seq
    rows = RET_CHUNK * RET_CHUNKS_PER_STEP
    nc = seq // rows
    npair, pw = N_HEADS // 2, 2 * HEAD_DIM
    dl = decay_logit.astype(F32)
    dl_m = jnp.broadcast_to(jnp.repeat(dl, RET_CHUNK, axis=1).reshape(2, npair, 1, 2 * RET_CHUNK),
                            (2, npair, 8, 2 * RET_CHUNK))
    dl_q = jnp.broadcast_to(jnp.repeat(dl, HEAD_DIM, axis=1).reshape(2, npair, 1, pw), (2, npair, 8, pw))
    blk_f = pl.BlockSpec((rows, D_GROUP), lambda b, c: (b * nc + c, 0))
    blk_b = pl.BlockSpec((rows, D_GROUP), lambda b, c: (b * nc + nc - 1 - c, 0))
    st_in = ([] if state0 is None else
             [pl.BlockSpec((1, 2, npair, pw, pw), lambda b, c: (b, 0, 0, 0, 0))])
    st_out = pl.BlockSpec((1, 2, N_HEADS, HEAD_DIM, HEAD_DIM), lambda b, c: (b, 0, 0, 0, 0))
    return pl.pallas_call(
        functools.partial(_ret_kernel, zero_init=state0 is None),
        grid=(nb, nc),
        in_specs=[pl.BlockSpec(dl_m.shape, lambda b, c: (0, 0, 0, 0)),
                  pl.BlockSpec(dl_q.shape, lambda b, c: (0, 0, 0, 0)),
                  blk_f, blk_f, blk_f, blk_b, blk_b, blk_b] + st_in,
        out_specs=[blk_f, blk_b, st_out],
        out_shape=[jax.ShapeDtypeStruct((n, D_GROUP), F32), jax.ShapeDtypeStruct((n, D_GROUP), F32),
                   jax.ShapeDtypeStruct((nb, 2, N_HEADS, HEAD_DIM, HEAD_DIM), F32)],
        scratch_shapes=[pltpu.VMEM((2, npair, pw, pw), F32),
                        pltpu.VMEM((2 * npair, RET_CHUNK, 2 * RET_CHUNK), F32),
                        pltpu.VMEM((2 * npair, RET_CHUNK, pw), F32),
                        pltpu.VMEM((2 * npair, RET_CHUNK, pw), F32),
                        pltpu.VMEM((2 * npair, 8, pw), F32)],
        compiler_params=_cparams("arbitrary", "arbitrary"),
        name="retention_scan",
    )(dl_m, dl_q, q, k, v, q, k, v, *([] if state0 is None else [_states_to_pairs(state0)]))


def _rwkv_kernel(*refs, zero_init):
    (xrf_ref, xkf_ref, xvf_ref, lof_ref, xrb_ref, xkb_ref, xvb_ref, lob_ref,
     w0_ref, wup_ref, a0_ref, aup_ref, kk_ref, ka_ref) = refs[:14]
    s0_ref = None if zero_init else refs[14]
    yf_ref, yb_ref, sf_ref, st = refs[14 if zero_init else 15:]
    c = pl.program_id(1)
    cs = RWKV_CHUNK
    nsub = RWKV_CHUNKS_PER_STEP
    npair = N_HEADS // 2

    @pl.when(c == 0)
    def _():
        _load_states(st, s0_ref)

    row = lax.broadcasted_iota(jnp.int32, (cs, 3 * cs), 0)
    col = lax.broadcasted_iota(jnp.int32, (cs, 3 * cs), 1) % cs
    pw = 2 * HEAD_DIM
    prow = lax.broadcasted_iota(jnp.int32, (cs, pw), 0)
    pcol = lax.broadcasted_iota(jnp.int32, (cs, pw), 1) % HEAD_DIM
    eye = jnp.where(pcol == prow, 1.0, 0.0).astype(F32)
    sub_diag = (prow // RWKV_SUB) == (pcol // RWKV_SUB)
    brow = lax.broadcasted_iota(jnp.int32, (pw, pw), 0) // HEAD_DIM
    bcol = lax.broadcasted_iota(jnp.int32, (pw, pw), 1) // HEAD_DIM
    same_head = brow == bcol

    bd = _pair_block_diag

    def direction(d, sub, xr_ref, xk_ref, xv_ref, lo_ref):
        rows = pl.ds(sub * cs, cs)
        tri3 = jnp.where((row >= col) if d == 0 else (col >= row), 1.0, 0.0).astype(BF16)
        dist = (prow - pcol) if d == 0 else (pcol - prow)
        xk = xk_ref[rows, :]
        lo = lo_ref[rows, :]
        w_log = -math.exp(-0.5) * _sigmoid(w0_ref[d] + _bdot(jnp.tanh(lo[:, 0:D_LORA]), wup_ref[d]))
        a_all = _sigmoid(a0_ref[d] + _bdot(lo[:, D_LORA:2 * D_LORA], aup_ref[d]))
        w1 = w_log.astype(BF16)
        r1 = w_log - w1.astype(F32)
        w2 = r1.astype(BF16)
        w3 = (r1 - w2.astype(F32)).astype(BF16)
        cum = jnp.dot(tri3, jnp.concatenate([w1, w2, w3], axis=0), preferred_element_type=F32)
        cum_end = cum[cs - 1:cs, :] if d == 0 else cum[0:1, :]
        return dict(earlier=dist > 0, upto=dist >= 0, xr=xr_ref[rows, :], xv=xv_ref[rows, :], a_all=a_all,
                    e_incl=jnp.exp(cum), e_excl=jnp.exp(cum - w_log), e_neg=jnp.exp(-cum),
                    e_end=jnp.exp(cum_end - cum), w_end=jnp.exp(cum_end), kk_all=xk * kk_ref[...],
                    keff_all=xk * (1.0 + (a_all - 1.0) * ka_ref[...]))

    in_refs = [(xrf_ref, xkf_ref, xvf_ref, lof_ref), (xrb_ref, xkb_ref, xvb_ref, lob_ref)]
    dirs = {(d, sub): direction(d, sub, *in_refs[d]) for d in range(2) for sub in range(nsub)}
    y_refs = [yf_ref, yb_ref]

    chains = [(d, sub, p) for d in range(2) for sub in range(nsub) for p in range(npair)]
    pairs = range(len(chains))
    sls = [slice(p * pw, (p + 1) * pw) for _, _, p in chains]
    pick = lambda name: [dirs[d, sub][name][:, sls[i]] for i, (d, sub, _) in enumerate(chains)]
    earlier = [dirs[d, sub]["earlier"] for d, sub, _ in chains]
    upto = [dirs[d, sub]["upto"] for d, sub, _ in chains]
    cat = lambda x, y: jnp.concatenate([x, y], axis=0)

    head_sum = _pair_head_sum

    e_incl, e_excl, e_neg, e_end, w_end = (pick(k) for k in ("e_incl", "e_excl", "e_neg", "e_end", "w_end"))
    kk = [x * lax.rsqrt(head_sum(x * x) + 1e-12) for x in pick("kk_all")]
    b = [kk[p] * a for p, a in zip(pairs, pick("a_all"))]
    keff = pick("keff_all")
    v = pick("xv")
    a_t = [-kk[p] * e_excl[p] for p in pairs]
    r_t = [xr * e_incl[p] for p, xr in zip(pairs, pick("xr"))]
    ar = [cat(a_t[p], r_t[p]).astype(BF16) for p in pairs]
    g = [_bdot_nt(ar[p], cat(bd(b[p] * e_neg[p]), bd(keff[p] * e_neg[p]))) for p in pairs]
    n_mat = [jnp.where(earlier[p], g[p][:cs, :pw], 0.0) for p in pairs]
    m_rb = [jnp.where(upto[p], g[p][cs:, :pw], 0.0) for p in pairs]
    a_ak = [jnp.where(earlier[p], g[p][:cs, pw:], 0.0) for p in pairs]
    m_rk = [jnp.where(upto[p], g[p][cs:, pw:], 0.0) for p in pairs]
    bd_s = lambda xs: (bd(xs[0]), bd(xs[1]))
    n_d = [jnp.where(sub_diag, n, 0.0) for n in n_mat]
    n_o = [jnp.where(sub_diag, 0.0, n) for n in n_mat]
    n_s = [_split2(n) for n in n_d]
    x = [eye + n for n in n_d]
    pk = [_mm3s(ns, bd_s(ns)) for ns in n_s]
    for _ in range(RWKV_SUB_DOUBLINGS - 1):
        xp = [_mm3s(_split2(cat(x[p], pk[p])), bd_s(_split2(pk[p]))) for p in pairs]
        x = [x[p] + xp[p][:cs] for p in pairs]
        pk = [r[cs:] for r in xp]
    t_d = [x[p] + _mm3s(_split2(x[p]), bd_s(_split2(pk[p]))) for p in pairs]
    lcat = lambda x, y: jnp.concatenate([x, y], axis=1)
    m_o = [_bdot(t_d[p], bd(n_o[p])) for p in pairs]
    mz = [_bdot(m_o[p], lcat(bd(t_d[p]), bd(m_o[p]))) for p in pairs]
    z = [t_d[p] + mz[p][:, :pw] for p in pairs]
    t_inv = [(z[p] + _bdot(mz[p][:, pw:], bd(z[p]))).astype(BF16) for p in pairs]
    av = [_bdot(cat(a_ak[p], m_rk[p]), bd(v[p])) for p in pairs]
    tau = [_bdot(t_inv[p], lcat(bd(a_t[p]), bd(av[p][:cs]))) for p in pairs]
    bk_end = [cat(b[p] * e_end[p], keff[p] * e_end[p]) for p in pairs]
    for step in range(nsub):
        now = [i for i, (d, sub, _) in enumerate(chains) if sub == (step if d == 0 else nsub - 1 - step)]
        s_prev = {i: st[chains[i][0], chains[i][2]] for i in now}
        hs = {i: _bdot_nt(cat(tau[i][:, :pw], r_t[i]), s_prev[i]) for i in now}
        u = {i: hs[i][:cs] + tau[i][:, pw:] for i in now}
        y = {i: hs[i][cs:] + _bdot(m_rb[i], bd(u[i])) + av[i][cs:] for i in now}
        for i in now:
            d, sub, p = chains[i]
            y_refs[d][pl.ds(sub * cs, cs), sls[i]] = y[i]
            st[d, p] = s_prev[i] * w_end[i] + jnp.where(same_head, _bdot_tn(cat(u[i], v[i]), bk_end[i]), 0.0)

    @pl.when(c == pl.num_programs(1) - 1)
    def _():
        _store_states(sf_ref, st)


def _rwkv_scan(xr, xk, xv, lora, w0, w_up, a0, a_up, k_k, k_a, state0, seq):
    n = xr.shape[0]
    nb = n // seq
    blk = RWKV_CHUNK * RWKV_CHUNKS_PER_STEP
    nc = seq // blk

    fwd_idx = lambda b, c: (b * nc + c, 0)
    bwd_idx = lambda b, c: (b * nc + nc - 1 - c, 0)
    blk_f = pl.BlockSpec((blk, D_GROUP), fwd_idx)
    blk_b = pl.BlockSpec((blk, D_GROUP), bwd_idx)
    lblk_f = pl.BlockSpec((blk, lora.shape[1]), fwd_idx)
    lblk_b = pl.BlockSpec((blk, lora.shape[1]), bwd_idx)
    dvec = pl.BlockSpec((2, 1, D_GROUP), lambda b, c: (0, 0, 0))
    dmat = pl.BlockSpec((2, D_LORA, D_GROUP), lambda b, c: (0, 0, 0))
    vec = pl.BlockSpec((1, D_GROUP), lambda b, c: (0, 0))
    npair, pw = N_HEADS // 2, 2 * HEAD_DIM
    st_in = ([] if state0 is None else
             [pl.BlockSpec((1, 2, npair, pw, pw), lambda b, c: (b, 0, 0, 0, 0))])
    st_out = pl.BlockSpec((1, 2, N_HEADS, HEAD_DIM, HEAD_DIM), lambda b, c: (b, 0, 0, 0, 0))
    return pl.pallas_call(
        functools.partial(_rwkv_kernel, zero_init=state0 is None),
        grid=(nb, nc),
        in_specs=[blk_f, blk_f, blk_f, lblk_f, blk_b, blk_b, blk_b, lblk_b,
                  dvec, dmat, dvec, dmat, vec, vec] + st_in,
        out_specs=[blk_f, blk_b, st_out],
        out_shape=[jax.ShapeDtypeStruct((n, D_GROUP), F32), jax.ShapeDtypeStruct((n, D_GROUP), F32),
                   jax.ShapeDtypeStruct((nb, 2, N_HEADS, HEAD_DIM, HEAD_DIM), F32)],
        scratch_shapes=[pltpu.VMEM((2, npair, pw, pw), F32)],
        compiler_params=_cparams("arbitrary", "arbitrary"),
        name="rwkv7_scan",
    )(xr, xk, xv, lora, xr, xk, xv, lora, w0.reshape(2, 1, D_GROUP), w_up, a0.reshape(2, 1, D_GROUP), a_up,
      k_k.reshape(1, D_GROUP), k_a.reshape(1, D_GROUP), *([] if state0 is None else [_states_to_pairs(state0)]))


ATTN_WIDTHS = (D_GROUP,) * 6
REC_WIDTHS = (D_GROUP,) * 7 + (2 * D_LORA + D_LORA_G,)


def kernel(x_prompt, x_sample, cache_na_k, cache_na_v, cache_diff_k, cache_diff_v, state_ret, state_rwkv, c, c_ctx, norm_mix_g, norm_ffn_g, norm_final_g, w_ada, b_ada, w_in_attn, w_out_attn, na_rpb, diff_lq1, diff_lk1, diff_lq2, diff_lk2, w_in_rec, w_out_rec, ret_decay_logit, rw_w0, rw_w_up, rw_a0, rw_a_up, rw_g_up, rw_k_k, rw_k_a, rw_r_k, rw_ln_g, rw_ln_b, w_ffn_in, w_ffn_out):
    bp, seq, d = x_prompt.shape
    bl, n_lat, _ = x_sample.shape
    depth = w_ada.shape[0]
    ctx = x_prompt.reshape(bp * seq, d)
    lat = x_sample.reshape(bl * n_lat, d)

    cond8 = jnp.zeros((8, d), F32).at[0].set(c_ctx).at[1:1 + bl].set(c)
    mods = _ada_modulation(cond8, w_ada, b_ada).reshape(depth, 8, 6, d)
    w_ffn_in_bf16 = w_ffn_in.astype(BF16)
    w_ffn_out_bf16 = w_ffn_out.astype(BF16)

    outs = {}
    for layer in range(depth):
        m_ctx = [mods[layer, 0:1, j].reshape(1, 1, d) for j in range(6)]
        m_lat = [mods[layer, 1:1 + bl, j].reshape(bl, 1, d) for j in range(6)]
        if layer % 2 == 0:
            i = layer // 2
            lam_init = 0.8 - 0.6 * math.exp(-0.3 * layer)
            lam_params = jnp.stack([diff_lq1[i], diff_lk1[i], diff_lq2[i], diff_lk2[i]]).astype(F32)
            w_in = w_in_attn[i].astype(BF16)
            w_out = w_out_attn[i].astype(BF16)
            res = _ctx_attention(ctx, norm_mix_g[layer], m_ctx[0], m_ctx[1], w_in, lam_params, seq, lam_init)
            mix_ctx = res[:2]
            outs["na_k"], outs["na_v"], outs["df_k"], outs["df_v"] = res[2:]
            plat = _norm_mod_matmul(lat, norm_mix_g[layer], m_lat[0], m_lat[1], w_in, ATTN_WIDTHS, n_lat, PROJ_TM)
            past = cache_na_k.shape[2]
            tt = _na_bias_table(na_rpb[i])
            o_na = _lat_na_attention(plat[0], plat[1], plat[2],
                                     cache_na_k[:, i].reshape(bl, past, D_GROUP),
                                     cache_na_v[:, i].reshape(bl, past, D_GROUP), tt, n_lat)
            o_df = _lat_diff_attention(lam_params, plat[3], plat[4], plat[5],
                                       cache_diff_k[:, i].reshape(bl, past, D_GROUP),
                                       cache_diff_v[:, i].reshape(bl, past, D_GROUP), n_lat, lam_init)
            mix_lat = (o_na, o_df)
        else:
            j = layer // 2
            w_in = w_in_rec[j].astype(BF16)
            w_out = w_out_rec[j].astype(BF16)
            pc = _norm_mod_matmul(ctx, norm_mix_g[layer], m_ctx[0], m_ctx[1], w_in, REC_WIDTHS, bp * seq, PROJ_TM)
            plat = _norm_mod_matmul(lat, norm_mix_g[layer], m_lat[0], m_lat[1], w_in, REC_WIDTHS, n_lat, PROJ_TM)
            mixes = []
            for is_ctx, p, s_ret0, s_rw0, sq in ((True, pc, None, None, seq),
                                                 (False, plat, state_ret[:, j], state_rwkv[:, j], n_lat)):
                rq, rk, rv, rg, wr, wk, wv, lora = p
                o_f, o_b, s_ret = _retention_scan(ret_decay_logit[j], rq, rk, rv, s_ret0, sq)
                y_f, y_b, s_rw = _rwkv_scan(wr, wk, wv, lora, rw_w0[j], rw_w_up[j].astype(BF16), rw_a0[j],
                                            rw_a_up[j].astype(BF16), rw_k_k[j], rw_k_a[j], s_rw0, sq)
                mixes.append((o_f, o_b, rg, y_f, y_b, wr, wk, wv, lora, rw_r_k[j].reshape(1, D_GROUP),
                              rw_ln_g[j].reshape(1, D_GROUP), rw_ln_b[j].reshape(1, D_GROUP),
                              rw_g_up[j].astype(BF16)))
                if is_ctx:
                    outs["ret"], outs["rwkv"] = s_ret, s_rw
            mix_ctx, mix_lat = mixes
        final = layer == depth - 1
        ctx = _layer_tail(ctx, mix_ctx, w_out, m_ctx[2], norm_ffn_g[layer], m_ctx[3], m_ctx[4], m_ctx[5],
                          w_ffn_in_bf16, w_ffn_out_bf16, layer, norm_final_g, bp * seq, final)
        lat = _layer_tail(lat, mix_lat, w_out, m_lat[2], norm_ffn_g[layer], m_lat[3], m_lat[4], m_lat[5],
                          w_ffn_in_bf16, w_ffn_out_bf16, layer, norm_final_g, n_lat, final)

    y_prompt = ctx.reshape(bp, seq, d)
    y_sample = lat.reshape(bl, n_lat, d)
    new_cache_na_k = outs["na_k"].reshape(bp, 1, seq, N_HEADS, HEAD_DIM)
    new_cache_na_v = outs["na_v"].reshape(bp, 1, seq, N_HEADS, HEAD_DIM)
    new_cache_diff_k = outs["df_k"].reshape(bp, 1, seq, N_HEADS // 2, 2, HEAD_DIM)
    new_cache_diff_v = outs["df_v"].reshape(bp, 1, seq, N_HEADS // 2, 2 * HEAD_DIM)
    new_state_ret = outs["ret"].reshape(bp, 1, 2, N_HEADS, HEAD_DIM, HEAD_DIM)
    new_state_rwkv = outs["rwkv"].reshape(bp, 1, 2, N_HEADS, HEAD_DIM, HEAD_DIM)
    return (y_prompt, y_sample, new_cache_na_k, new_cache_na_v, new_cache_diff_k, new_cache_diff_v,
            new_state_ret, new_state_rwkv)
```

```python
import functools
import math

import jax
import jax.numpy as jnp
from jax import lax
from jax.experimental import pallas as pl
from jax.experimental.pallas import tpu as pltpu

F32 = jnp.float32
BF16 = jnp.bfloat16

HEAD_DIM = 64
N_HEADS = 8
D_GROUP = 512
GRID_W = 64
GRID_ROWS = 32
WIN_R = 8
WIN_W = 16
RPB_R = 15
RPB_C = 31
D_FF = 2816
FFN_CHUNK = D_FF // 2
D_LORA = 64
D_LORA_G = 128
ROPE_BASE = 10000.0
RMS_EPS = 1e-6
RWKV_GN_EPS = 64e-5
NEG_INF = -1e30
ATTN_SCALE = HEAD_DIM ** -0.5
RET_CHUNK = 128
RET_CHUNKS_PER_STEP = 2
RWKV_CHUNK = 64
RWKV_CHUNKS_PER_STEP = 4
RWKV_SUB = 16
RWKV_SUB_DOUBLINGS = 3
V7X_VMEM_BYTES = 64 * 1024 * 1024
VMEM_LIMIT = V7X_VMEM_BYTES - 8 * 1024 * 1024
CAST_BLOCK_BYTES = 6 * 1024 * 1024
ADA_TN = 1536
TAIL_TM = 512
TAIL_TM_REC = 256
PROJ_TM = 512
CTX_SEQS_PER_STEP = 1
LAT_DIFF_TQ = 256


def _cparams(*sem):
    return pltpu.CompilerParams(dimension_semantics=sem, vmem_limit_bytes=VMEM_LIMIT)


def _sigmoid(x):
    return 1.0 / (1.0 + jnp.exp(-x))


def _silu(x):
    return x * _sigmoid(x)


def _rms(x, eps=RMS_EPS):
    return x * lax.rsqrt(jnp.mean(x * x, axis=-1, keepdims=True) + eps)


def _bdot(a, b):
    return jnp.dot(a.astype(BF16), b.astype(BF16), preferred_element_type=F32)


def _bdot_nt(a, b):
    return lax.dot_general(a.astype(BF16), b.astype(BF16), (((1,), (1,)), ((), ())),
                           preferred_element_type=F32)


def _bdot_tn(a, b):
    return lax.dot_general(a.astype(BF16), b.astype(BF16), (((0,), (0,)), ((), ())),
                           preferred_element_type=F32)


def _split2(x):
    hi = x.astype(BF16)
    lo = (x - hi.astype(F32)).astype(BF16)
    return hi, lo


def _mm3s(a_split, b_split):
    ah, al = a_split
    bh, bl = b_split
    n = bh.shape[1]
    rhs = jnp.concatenate([jnp.concatenate([bh, bl], axis=1),
                           jnp.concatenate([bh, jnp.zeros_like(bl)], axis=1)], axis=0)
    out = jnp.dot(jnp.concatenate([ah, al], axis=1), rhs, preferred_element_type=F32)
    return out[:, :n] + out[:, n:]


def _diff_lambda(lp, lam_init):
    s1 = jnp.sum(lp[0:1, :] * lp[1:2, :], axis=-1, keepdims=True)
    s2 = jnp.sum(lp[2:3, :] * lp[3:4, :], axis=-1, keepdims=True)
    return jnp.exp(s1) - jnp.exp(s2) + lam_init


def _cast_kernel(w_ref, o_ref):
    o_ref[...] = w_ref[...].astype(BF16)


def _cast_bf16(w):
    nl, r, c = w.shape
    rows = max(k for k in range(16, r + 1, 16) if r % k == 0 and k * c * 4 <= CAST_BLOCK_BYTES)
    blk = pl.BlockSpec((1, rows, c), lambda l, i: (l, i, 0))
    return pl.pallas_call(
        _cast_kernel,
        grid=(nl, r // rows),
        in_specs=[blk],
        out_specs=blk,
        out_shape=jax.ShapeDtypeStruct(w.shape, BF16),
        compiler_params=_cparams("arbitrary", "arbitrary"),
        name="cast_bf16",
    )(w)


def _ada_kernel(c_ref, w_ref, b_ref, o_ref):
    s = _silu(c_ref[...])
    o_ref[0] = _bdot(s, w_ref[0]) + b_ref[0]


def _ada_modulation(cond8, w_ada, b_ada):
    depth, d, n = w_ada.shape
    tn = ADA_TN
    return pl.pallas_call(
        _ada_kernel,
        grid=(depth, n // tn),
        in_specs=[pl.BlockSpec((8, d), lambda l, j: (0, 0)),
                  pl.BlockSpec((1, d, tn), lambda l, j: (l, 0, j)),
                  pl.BlockSpec((1, 1, tn), lambda l, j: (l, 0, j))],
        out_specs=pl.BlockSpec((1, 8, tn), lambda l, j: (l, 0, j)),
        out_shape=jax.ShapeDtypeStruct((depth, 8, n), F32),
        compiler_params=_cparams("arbitrary", "arbitrary"),
        name="ada_modulation",
    )(cond8, w_ada, b_ada.reshape(depth, 1, n))


def _norm_mod(x_ref, g_ref, sh_ref, sc_ref):
    h = _rms(x_ref[...]) * g_ref[...]
    return (h * (1.0 + sc_ref[0]) + sh_ref[0]).astype(BF16)


def _nmm_kernel(x_ref, g_ref, sh_ref, sc_ref, w_ref, *o_refs, widths):
    hb = _norm_mod(x_ref, g_ref, sh_ref, sc_ref)
    off = 0
    for o_ref, w in zip(o_refs, widths):
        o_ref[...] = jnp.dot(hb, w_ref[:, off:off + w], preferred_element_type=F32)
        off += w


def _norm_mod_matmul(x, gain, shift, scale, w_bf16, widths, rows_per_mod, tm):
    n, d = x.shape
    tiles_per_mod = rows_per_mod // tm
    mod_spec = pl.BlockSpec((1, 1, d), lambda i: (i // tiles_per_mod, 0, 0))
    return pl.pallas_call(
        functools.partial(_nmm_kernel, widths=widths),
        grid=(n // tm,),
        in_specs=[pl.BlockSpec((tm, d), lambda i: (i, 0)),
                  pl.BlockSpec((1, d), lambda i: (0, 0)),
                  mod_spec, mod_spec,
                  pl.BlockSpec(w_bf16.shape, lambda i: (0, 0), pipeline_mode=pl.Buffered(1))],
        out_specs=[pl.BlockSpec((tm, w), lambda i: (i, 0)) for w in widths],
        out_shape=[jax.ShapeDtypeStruct((n, w), F32) for w in widths],
        compiler_params=_cparams("arbitrary"),
        name="norm_mod_matmul",
    )(x, gain.reshape(1, d), shift, scale, w_bf16)


PAIR_W = 2 * HEAD_DIM


def _pair_tiles(x):
    return [x[:, j * PAIR_W:(j + 1) * PAIR_W] for j in range(x.shape[1] // PAIR_W)]


def _split_pair(tile):
    left = lax.broadcasted_iota(jnp.int32, tile.shape, 1) < HEAD_DIM
    zero = jnp.zeros_like(tile)
    return jnp.where(left, tile, zero), jnp.where(left, zero, tile)


def _pair_head_sum(tile):
    left = lax.broadcasted_iota(jnp.int32, tile.shape, 1) < HEAD_DIM
    zero = jnp.zeros_like(tile)
    s_left = jnp.sum(jnp.where(left, tile, zero), axis=-1, keepdims=True)
    s_right = jnp.sum(jnp.where(left, zero, tile), axis=-1, keepdims=True)
    return jnp.where(left, s_left, s_right)


def _with_ones(v):
    return jnp.concatenate([v, jnp.ones_like(v)], axis=1)


def _exp_scores(s):
    return jnp.exp(s - jnp.max(s, axis=-1, keepdims=True)).astype(BF16)


def _merge_pair(out_even, out_odd):
    left = lax.broadcasted_iota(jnp.int32, (out_even.shape[0], PAIR_W), 1) < HEAD_DIM
    return jnp.where(left, out_even[:, :PAIR_W] * (1.0 / out_even[:, PAIR_W:]),
                     out_odd[:, :PAIR_W] * (1.0 / out_odd[:, PAIR_W:]))


def _diff_combine(out1, out2, lam, lam_init):
    o = out1[:, :PAIR_W] * (1.0 / out1[:, PAIR_W:]) - out2[:, :PAIR_W] * (lam / out2[:, PAIR_W:])
    return _rms(o) * (1.0 - lam_init)


CTX_CACHE_SHAPES = ((N_HEADS, HEAD_DIM), (N_HEADS, HEAD_DIM),
                    (N_HEADS // 2, 2, HEAD_DIM), (N_HEADS // 2, 2 * HEAD_DIM))


def _ctx_attn_kernel(x_ref, g_ref, sh_ref, sc_ref, w_ref, lp_ref, on_ref, od_ref, *cache_refs, lam_init, seq):
    hb = _norm_mod(x_ref, g_ref, sh_ref, sc_ref)
    proj = [jnp.dot(hb, w_ref[:, i * D_GROUP:(i + 1) * D_GROUP], preferred_element_type=F32) for i in range(6)]
    for c_ref, val in zip(cache_refs, (proj[1], proj[2], proj[4], proj[5])):
        c_ref[...] = val.reshape(c_ref.shape)
    tiles = range(D_GROUP // PAIR_W)
    lam = _diff_lambda(lp_ref[...], lam_init)
    for s in range(hb.shape[0] // seq):
        rows = slice(s * seq, (s + 1) * seq)
        qn, kn, vn, qd, kd, vd = (p[rows] for p in proj)
        q = [_split_pair(t) for t in _pair_tiles((qn * ATTN_SCALE).astype(BF16))
             + _pair_tiles((qd * ATTN_SCALE).astype(BF16))]
        k = _pair_tiles(kn.astype(BF16)) + _pair_tiles(kd.astype(BF16))
        v = [_with_ones(t) for t in _pair_tiles(vn.astype(BF16)) + _pair_tiles(vd.astype(BF16))]
        e = [[_exp_scores(_bdot_nt(q[j][c], k[j])) for c in range(2)] for j in range(2 * len(tiles))]
        out = [[jnp.dot(e[j][c], v[j], preferred_element_type=F32) for c in range(2)]
               for j in range(2 * len(tiles))]
        for j in tiles:
            sl = slice(j * PAIR_W, (j + 1) * PAIR_W)
            on_ref[rows, sl] = _merge_pair(out[j][0], out[j][1])
            od_ref[rows, sl] = _diff_combine(out[len(tiles) + j][0], out[len(tiles) + j][1], lam, lam_init)


def _ctx_attention(x, gain, shift, scale, w_bf16, lam_params, seq, lam_init):
    n, d = x.shape
    tm = seq * CTX_SEQS_PER_STEP
    blk = pl.BlockSpec((tm, D_GROUP), lambda b: (b, 0))
    mod_spec = pl.BlockSpec((1, 1, d), lambda b: (0, 0, 0))
    cache_specs = [pl.BlockSpec((tm,) + dims, lambda b, nd=len(dims): (b,) + (0,) * nd)
                   for dims in CTX_CACHE_SHAPES]
    return pl.pallas_call(
        functools.partial(_ctx_attn_kernel, lam_init=lam_init, seq=seq),
        grid=(n // tm,),
        in_specs=[pl.BlockSpec((tm, d), lambda b: (b, 0)),
                  pl.BlockSpec((1, d), lambda b: (0, 0)),
                  mod_spec, mod_spec,
                  pl.BlockSpec(w_bf16.shape, lambda b: (0, 0), pipeline_mode=pl.Buffered(1)),
                  pl.BlockSpec(lam_params.shape, lambda b: (0, 0))],
        out_specs=[blk, blk] + cache_specs,
        out_shape=([jax.ShapeDtypeStruct((n, D_GROUP), F32)] * 2
                   + [jax.ShapeDtypeStruct((n,) + dims, F32) for dims in CTX_CACHE_SHAPES]),
        compiler_params=_cparams("arbitrary"),
        name="ctx_attention",
    )(x, gain.reshape(1, d), shift, scale, w_bf16, lam_params)


def _rope_tables(n):
    quarter = HEAD_DIM // 4
    pos = jnp.arange(n)
    inv_freq = ROPE_BASE ** (-jnp.arange(quarter, dtype=F32) / quarter)
    lane = jnp.arange(HEAD_DIM)
    p = jnp.where(lane[None, :] < HEAD_DIM // 2, (pos // GRID_W)[:, None], (pos % GRID_W)[:, None]).astype(F32)
    ang = p * inv_freq[lane % quarter][None, :]
    sign = jnp.where((lane % (2 * quarter)) < quarter, -1.0, 1.0).astype(F32)
    cos = jnp.tile(jnp.cos(ang), (1, 2))
    sin = jnp.tile(jnp.sin(ang) * sign[None, :], (1, 2))
    return cos, sin


def _rope(x, cos, sin_signed):
    quarter = HEAD_DIM // 4
    lane = lax.broadcasted_iota(jnp.int32, cos.shape, 1)
    first = (lane % (2 * quarter)) < quarter
    tiles = []
    for t in _pair_tiles(x):
        partner = jnp.where(first, pltpu.roll(t, PAIR_W - quarter, 1), pltpu.roll(t, quarter, 1))
        tiles.append(t * cos + partner * sin_signed)
    return jnp.concatenate(tiles, axis=1)


def _lat_diff_kernel(lp_ref, q_ref, k_ref, v_ref, ck_ref, cv_ref, cosq_ref, sinq_ref, cosk_ref, sink_ref,
                     o_ref, kall, vall, *, lam_init, n_lat):
    @pl.when(pl.program_id(1) == 0)
    def _():
        kall[0:n_lat, :] = _rope(k_ref[...], cosk_ref[...], sink_ref[...]).astype(BF16)
        kall[n_lat:, :] = ck_ref[...].astype(BF16)
        for j, (vt, ct) in enumerate(zip(_pair_tiles(v_ref[...].astype(BF16)),
                                         _pair_tiles(cv_ref[...].astype(BF16)))):
            vall[j, 0:n_lat, :] = _with_ones(vt)
            vall[j, n_lat:, :] = _with_ones(ct)

    lam = _diff_lambda(lp_ref[...], lam_init)
    q = (_rope(q_ref[...], cosq_ref[...], sinq_ref[...]) * ATTN_SCALE).astype(BF16)
    qs = [_split_pair(qt) for qt in _pair_tiles(q)]
    sls = [slice(j * PAIR_W, (j + 1) * PAIR_W) for j in range(len(qs))]
    for group in ((0, 1), (2, 3)):
        e = {j: [_exp_scores(_bdot_nt(qs[j][c], kall[:, sls[j]])) for c in range(2)] for j in group}
        out = {j: [jnp.dot(e[j][c], vall[j], preferred_element_type=F32) for c in range(2)] for j in group}
        for j in group:
            o_ref[:, sls[j]] = _diff_combine(out[j][0], out[j][1], lam, lam_init)


def _lat_diff_attention(lam_params, qd, kd, vd, cache_k, cache_v, n_lat, lam_init, tq=LAT_DIFF_TQ):
    n = qd.shape[0]
    nb = n // n_lat
    nq = n_lat // tq
    past = cache_k.shape[1]
    cos, sin = _rope_tables(n_lat)
    qblk = pl.BlockSpec((tq, D_GROUP), lambda b, i: (b * nq + i, 0))
    once = pl.Buffered(1)
    kvblk = pl.BlockSpec((n_lat, D_GROUP), lambda b, i: (b, 0), pipeline_mode=once)
    cblk = pl.BlockSpec((None, past, D_GROUP), lambda b, i: (b, 0, 0))
    return pl.pallas_call(
        functools.partial(_lat_diff_kernel, lam_init=lam_init, n_lat=n_lat),
        grid=(nb, nq),
        in_specs=[pl.BlockSpec(lam_params.shape, lambda b, i: (0, 0)),
                  qblk, kvblk, kvblk, cblk, cblk,
                  pl.BlockSpec((tq, PAIR_W), lambda b, i: (i, 0)),
                  pl.BlockSpec((tq, PAIR_W), lambda b, i: (i, 0)),
                  pl.BlockSpec((n_lat, PAIR_W), lambda b, i: (0, 0), pipeline_mode=once),
                  pl.BlockSpec((n_lat, PAIR_W), lambda b, i: (0, 0), pipeline_mode=once)],
        out_specs=qblk,
        out_shape=jax.ShapeDtypeStruct((n, D_GROUP), F32),
        scratch_shapes=[pltpu.VMEM((n_lat + past, D_GROUP), BF16),
                        pltpu.VMEM((D_GROUP // PAIR_W, n_lat + past, 2 * PAIR_W), BF16)],
        compiler_params=_cparams("arbitrary", "arbitrary"),
        name="lat_diff_attention",
    )(lam_params, qd, kd, vd, cache_k, cache_v, cos, sin, cos, sin)


NA_Q_ROWS = 4
NA_KEY_ROWS = WIN_R + NA_Q_ROWS
NA_TABLE = RPB_R + 1


def _na_bias_kernel(rpb_ref, tt_ref):
    h = pl.program_id(0)
    lane = lax.broadcasted_iota(jnp.int32, (GRID_W, 2 * GRID_W), 1)
    qc = lax.broadcasted_iota(jnp.int32, (GRID_W, 2 * GRID_W), 0)
    dc = jnp.clip(lane % GRID_W - qc + (WIN_W - 1), 0, RPB_C - 1)
    first = lane < GRID_W
    rows = []
    for dr in range(RPB_R):
        acc = jnp.zeros((GRID_W, 2 * GRID_W), F32)
        for d in range(RPB_C):
            acc = jnp.where(dc == d, rpb_ref[h * RPB_R + dr, d], acc)
        rows.append(acc)
    for k in range(NA_TABLE):
        tt_ref[0, k] = jnp.where(first, rows[min(max(k - 1, 0), RPB_R - 1)], rows[min(k, RPB_R - 1)])


def _na_bias_table(rpb):
    return pl.pallas_call(
        _na_bias_kernel,
        grid=(N_HEADS,),
        in_specs=[pl.BlockSpec(memory_space=pltpu.SMEM)],
        out_specs=pl.BlockSpec((1, NA_TABLE, GRID_W, 2 * GRID_W), lambda h: (h, 0, 0, 0)),
        out_shape=jax.ShapeDtypeStruct((N_HEADS, NA_TABLE, GRID_W, 2 * GRID_W), F32),
        compiler_params=_cparams("arbitrary"),
        name="na_bias_table",
    )(rpb.reshape(N_HEADS * RPB_R, RPB_C))


def _lat_na_kernel(q_ref, k_ref, v_ref, ck_ref, cv_ref, tt_ref, o_ref):
    step = pl.program_id(1)
    past = ck_ref.shape[0]
    nwin = NA_KEY_ROWS * GRID_W
    r_q = [NA_Q_ROWS * step + a for a in range(NA_Q_ROWS)]
    r_start = [jnp.clip(r - WIN_R // 2, 0, GRID_ROWS - WIN_R) for r in r_q]
    ws = jnp.minimum(r_start[0], GRID_ROWS - NA_KEY_ROWS)
    row0 = pl.multiple_of(ws * GRID_W, GRID_W)
    lane = lax.broadcasted_iota(jnp.int32, (GRID_W, past + nwin), 1)
    key_row = ws + (lane - past) // GRID_W
    kc = lane % GRID_W
    qc = lax.broadcasted_iota(jnp.int32, (GRID_W, past + nwin), 0)
    c_start = jnp.clip(qc - WIN_W // 2, 0, GRID_W - WIN_W)
    col_ok = (kc >= c_start) & (kc < c_start + WIN_W)
    visible = jnp.concatenate(
        [(lane < past) | (col_ok & (key_row >= rs) & (key_row < rs + WIN_R)) for rs in r_start], axis=0)
    no_bias = jnp.zeros((GRID_W, past), F32)

    def bias_rows(h, r):
        tiles = [tt_ref[h, jnp.clip(ws + 2 * j - r + WIN_R, 0, NA_TABLE - 1)] for j in range(NA_KEY_ROWS // 2)]
        return jnp.concatenate([no_bias] + tiles, axis=1)

    q = [_split_pair(t) for t in _pair_tiles((q_ref[...] * ATTN_SCALE).astype(BF16))]
    kcat = _pair_tiles(jnp.concatenate([ck_ref[...], k_ref[pl.ds(row0, nwin), :]], axis=0).astype(BF16))
    vcat = [_with_ones(t) for t in
            _pair_tiles(jnp.concatenate([cv_ref[...], v_ref[pl.ds(row0, nwin), :]], axis=0).astype(BF16))]
    bias = [jnp.concatenate([bias_rows(h, r) for r in r_q], axis=0) for h in range(N_HEADS)]
    e = [[_exp_scores(jnp.where(visible, _bdot_nt(q[j][c], kcat[j]) + bias[2 * j + c], NEG_INF))
          for c in range(2)] for j in range(len(q))]
    out = [[jnp.dot(e[j][c], vcat[j], preferred_element_type=F32) for c in range(2)] for j in range(len(q))]
    for j in range(len(q)):
        o_ref[:, j * PAIR_W:(j + 1) * PAIR_W] = _merge_pair(out[j][0], out[j][1])


def _lat_na_attention(qn, kn, vn, cache_k, cache_v, tt, n_lat):
    n = qn.shape[0]
    nb = n // n_lat
    past = cache_k.shape[1]
    steps = GRID_ROWS // NA_Q_ROWS
    qblk = pl.BlockSpec((NA_Q_ROWS * GRID_W, D_GROUP), lambda b, r: (b * steps + r, 0))
    kvblk = pl.BlockSpec((n_lat, D_GROUP), lambda b, r: (b, 0))
    cblk = pl.BlockSpec((None, past, D_GROUP), lambda b, r: (b, 0, 0))
    return pl.pallas_call(
        _lat_na_kernel,
        grid=(nb, steps),
        in_specs=[qblk, kvblk, kvblk, cblk, cblk,
                  pl.BlockSpec(tt.shape, lambda b, r: (0, 0, 0, 0))],
        out_specs=qblk,
        out_shape=jax.ShapeDtypeStruct((n, D_GROUP), F32),
        compiler_params=_cparams("arbitrary", "arbitrary"),
        name="lat_na_attention",
    )(qn, kn, vn, cache_k, cache_v, tt)


def _rec_mix(of_ref, ob_ref, rg_ref, yf_ref, yb_ref, xr_ref, xk_ref, xv_ref, lo_ref, rk_ref, lng_ref, lnb_ref,
             gup_ref):
    def head_sums(x):
        return jnp.concatenate([_pair_head_sum(t) for t in _pair_tiles(x)], axis=1)

    inv_d = 1.0 / HEAD_DIM
    o = of_ref[...] + ob_ref[...]
    o_ret = _silu(rg_ref[...]) * (o * lax.rsqrt(head_sums(o * o) * inv_d + RMS_EPS))
    y = yf_ref[...] + yb_ref[...]
    yc = y - head_sums(y) * inv_d
    var = head_sums(yc * yc) * inv_d
    yn = yc * lax.rsqrt(var + RWKV_GN_EPS) * lng_ref[...] + lnb_ref[...]
    bonus = head_sums(xr_ref[...] * rk_ref[...] * xk_ref[...]) * xv_ref[...]
    g_rw = _bdot(_sigmoid(lo_ref[:, 2 * D_LORA:]), gup_ref[...])
    return o_ret, (yn + bonus) * g_rw


N_REC_TOKEN_INPUTS = 9
N_REC_PARAMS = 4


def _tail_kernel(*refs, final, ff_chunk, recurrent):
    x_ref = refs[0]
    if not recurrent:
        n_mix = 2
        mix_a, mix_b = refs[1][...], refs[2][...]
    else:
        n_mix = 2 * N_REC_TOKEN_INPUTS + N_REC_PARAMS
        first_refs = refs[1:1 + N_REC_TOKEN_INPUTS]
        next_refs = refs[1 + N_REC_TOKEN_INPUTS:1 + 2 * N_REC_TOKEN_INPUTS]
        param_refs = refs[1 + 2 * N_REC_TOKEN_INPUTS:1 + n_mix]
        mixa_scr, mixb_scr = refs[-2:]
        refs = refs[:-2]

        @pl.when(pl.program_id(0) == 0)
        def _():
            a0, b0 = _rec_mix(*first_refs, *param_refs)
            mixa_scr[...] = a0
            mixb_scr[...] = b0

        mix_a, mix_b = mixa_scr[...], mixb_scr[...]
    wo_ref, gm_ref, g_ref, sh_ref, sc_ref, gf_ref, wi_ref, wf_ref, fg_ref, o_ref = refs[1 + n_mix:]
    if recurrent:
        a1, b1 = _rec_mix(*next_refs, *param_refs)
        mixa_scr[...] = a1
        mixb_scr[...] = b1
    mix = _bdot(mix_a, wo_ref[0:D_GROUP, :]) + _bdot(mix_b, wo_ref[D_GROUP:, :])
    x1 = x_ref[...] + gm_ref[0] * mix
    h = _rms(x1) * g_ref[...]
    hb = (h * (1.0 + sc_ref[0]) + sh_ref[0]).astype(BF16)
    acc = jnp.zeros_like(x1)
    for c0 in range(0, D_FF, ff_chunk):
        gate = jnp.dot(hb, wi_ref[:, c0:c0 + ff_chunk], preferred_element_type=F32)
        up = jnp.dot(hb, wi_ref[:, D_FF + c0:D_FF + c0 + ff_chunk], preferred_element_type=F32)
        acc = acc + jnp.dot((_silu(gate) * up).astype(BF16), wf_ref[c0:c0 + ff_chunk, :],
                            preferred_element_type=F32)
    x2 = x1 + gf_ref[0] * acc
    if final:
        x2 = _rms(x2) * fg_ref[...]
    o_ref[...] = x2


def _layer_tail(x, mix_inputs, w_out, gate_mix, gain, shift, scale, gate_ffn, w_ffn_in_all, w_ffn_out_all, layer,
                final_gain, rows_per_mod, final, ff_chunk=FFN_CHUNK):
    n, d = x.shape
    recurrent = len(mix_inputs) != 2
    tm = TAIL_TM_REC if recurrent else TAIL_TM
    tiles_per_mod = rows_per_mod // tm
    last = n // tm - 1
    mod_spec = pl.BlockSpec((1, 1, d), lambda i: (i // tiles_per_mod, 0, 0))
    vec_spec = pl.BlockSpec((1, d), lambda i: (0, 0))
    resident = lambda a: pl.BlockSpec(a.shape, lambda i: (0,) * a.ndim, pipeline_mode=pl.Buffered(1))
    layer_slab = lambda a: pl.BlockSpec((None,) + a.shape[1:], lambda i: (layer, 0, 0),
                                        pipeline_mode=pl.Buffered(1))
    if recurrent:
        tok, params = mix_inputs[:N_REC_TOKEN_INPUTS], mix_inputs[N_REC_TOKEN_INPUTS:]
        mix_specs = ([pl.BlockSpec((tm, a.shape[1]), lambda i: (0, 0), pipeline_mode=pl.Buffered(1)) for a in tok]
                     + [pl.BlockSpec((tm, a.shape[1]), lambda i: (jnp.minimum(i + 1, last), 0)) for a in tok]
                     + [resident(a) for a in params])
        mix_args = list(tok) + list(tok) + list(params)
        scratch = [pltpu.VMEM((tm, D_GROUP), F32)] * 2
    else:
        mix_specs = [pl.BlockSpec((tm, D_GROUP), lambda i: (i, 0))] * 2
        mix_args = list(mix_inputs)
        scratch = []
    return pl.pallas_call(
        functools.partial(_tail_kernel, final=final, ff_chunk=ff_chunk, recurrent=recurrent),
        grid=(n // tm,),
        in_specs=[pl.BlockSpec((tm, d), lambda i: (i, 0))] + mix_specs
                 + [resident(w_out), mod_spec, vec_spec, mod_spec, mod_spec, mod_spec,
                    layer_slab(w_ffn_in_all), layer_slab(w_ffn_out_all), vec_spec],
        out_specs=pl.BlockSpec((tm, d), lambda i: (i, 0)),
        out_shape=jax.ShapeDtypeStruct((n, d), F32),
        scratch_shapes=scratch,
        compiler_params=_cparams("arbitrary"),
        name="layer_tail",
    )(x, *mix_args, w_out, gate_mix, gain.reshape(1, d), shift, scale, gate_ffn, w_ffn_in_all, w_ffn_out_all,
      final_gain.reshape(1, d))


def _log_sigmoid(x):
    return jnp.minimum(x, 0.0) - jnp.log(1.0 + jnp.exp(-jnp.abs(x)))


def _pair_block_diag(xp):
    left = lax.broadcasted_iota(jnp.int32, xp.shape, 1) < HEAD_DIM
    zero = jnp.zeros_like(xp)
    return jnp.concatenate([jnp.where(left, xp, zero), jnp.where(left, zero, xp)], axis=0)


def _states_to_pairs(state):
    nb = state.shape[0]
    s = state.reshape(nb, 2, N_HEADS // 2, 2, HEAD_DIM, HEAD_DIM)
    zero = jnp.zeros_like(s[:, :, :, 0])
    return jnp.concatenate([jnp.concatenate([s[:, :, :, 0], zero], axis=-1),
                            jnp.concatenate([zero, s[:, :, :, 1]], axis=-1)], axis=-2)


def _load_states(st, s0_ref):
    st[...] = jnp.zeros(st.shape, st.dtype) if s0_ref is None else s0_ref[0]


def _store_states(sf_ref, st):
    for d in range(st.shape[0]):
        for p in range(st.shape[1]):
            s = st[d, p]
            sf_ref[0, d, 2 * p] = s[:HEAD_DIM, :HEAD_DIM]
            sf_ref[0, d, 2 * p + 1] = s[HEAD_DIM:, HEAD_DIM:]


def _ret_kernel(*refs, zero_init):
    dlm_ref, dlq_ref, qf_ref, kf_ref, vf_ref, qb_ref, kb_ref, vb_ref = refs[:8]
    s0_ref = None if zero_init else refs[8]
    of_ref, ob_ref, sf_ref, st, mask_scr, qdec_scr, kdec_scr, cdec_scr = refs[8 if zero_init else 9:]
    c = pl.program_id(1)
    cs = RET_CHUNK
    pw = 2 * HEAD_DIM
    npair = N_HEADS // 2
    chains = [(d, p) for d in range(2) for p in range(npair)]

    @pl.when(c == 0)
    def _():
        _load_states(st, s0_ref)

    @pl.when((pl.program_id(0) == 0) & (c == 0))
    def _():
        row = lax.broadcasted_iota(jnp.int32, (cs, 2 * cs), 0)
        col = lax.broadcasted_iota(jnp.int32, (cs, 2 * cs), 1) % cs
        tok = lax.broadcasted_iota(jnp.int32, (cs, pw), 0)
        dist = [row - col, col - row]
        tq = [tok.astype(F32), (cs - 1 - tok).astype(F32)]
        for i, (d, p) in enumerate(chains):
            lgm = _log_sigmoid(dlm_ref[d, p])[0:1, :]
            lgq = _log_sigmoid(dlq_ref[d, p])
            mask_scr[i] = jnp.where(dist[d] >= 0, jnp.exp(jnp.maximum(dist[d], 0).astype(F32) * lgm), 0.0)
            qdec_scr[i] = jnp.exp((tq[d] + 1.0) * lgq[0:1, :])
            kdec_scr[i] = jnp.exp((cs - 1.0 - tq[d]) * lgq[0:1, :])
            cdec_scr[i] = jnp.exp(cs * lgq)

    brow = lax.broadcasted_iota(jnp.int32, (pw, pw), 0) // HEAD_DIM
    bcol = lax.broadcasted_iota(jnp.int32, (pw, pw), 1) // HEAD_DIM
    same_head = brow == bcol
    refs = [(qf_ref, kf_ref, vf_ref, of_ref), (qb_ref, kb_ref, vb_ref, ob_ref)]
    nsub = RET_CHUNKS_PER_STEP
    work = [(t, sub) for sub in range(nsub) for t in range(len(chains))]
    sl_of = lambda t: slice(chains[t][1] * pw, (chains[t][1] + 1) * pw)
    rows_of = lambda sub: pl.ds(sub * cs, cs)
    q = {(t, sub): refs[chains[t][0]][0][rows_of(sub), sl_of(t)] for t, sub in work}
    k = {(t, sub): refs[chains[t][0]][1][rows_of(sub), sl_of(t)] * ATTN_SCALE for t, sub in work}
    v = {(t, sub): refs[chains[t][0]][2][rows_of(sub), sl_of(t)] for t, sub in work}
    s = {w: _bdot_nt(q[w], _pair_block_diag(k[w])) * mask_scr[w[0]] for w in work}
    inner = {w: _bdot(s[w], _pair_block_diag(v[w])) for w in work}
    kv = {w: _bdot_tn(k[w] * kdec_scr[w[0]], v[w]) for w in work}
    for step in range(nsub):
        now = [(t, (step if chains[t][0] == 0 else nsub - 1 - step)) for t in range(len(chains))]
        state = {w: st[chains[w[0]]] for w in now}
        cross = {w: _bdot(q[w], state[w]) * qdec_scr[w[0]] for w in now}
        for w in now:
            t, sub = w
            d, p = chains[t]
            refs[d][3][rows_of(sub), sl_of(t)] = inner[w] + cross[w]
            st[d, p] = state[w] * cdec_scr[t][0:1, :] + jnp.where(same_head, kv[w], 0.0)

    @pl.when(c == pl.num_programs(1) - 1)
    def _():
        _store_states(sf_ref, st)


def _retention_scan(decay_logit, q, k, v, state0, seq):
    n = q.shape[0]
    nb = n // seq
    rows = RET_CHUNK * RET_CHUNKS_PER_STEP
    nc = seq // rows
    npair, pw = N_HEADS // 2, 2 * HEAD_DIM
    dl = decay_logit.astype(F32)
    dl_m = jnp.broadcast_to(jnp.repeat(dl, RET_CHUNK, axis=1).reshape(2, npair, 1, 2 * RET_CHUNK),
                            (2, npair, 8, 2 * RET_CHUNK))
    dl_q = jnp.broadcast_to(jnp.repeat(dl, HEAD_DIM, axis=1).reshape(2, npair, 1, pw), (2, npair, 8, pw))
    blk_f = pl.BlockSpec((rows, D_GROUP), lambda b, c: (b * nc + c, 0))
    blk_b = pl.BlockSpec((rows, D_GROUP), lambda b, c: (b * nc + nc - 1 - c, 0))
    st_in = ([] if state0 is None else
             [pl.BlockSpec((1, 2, npair, pw, pw), lambda b, c: (b, 0, 0, 0, 0))])
    st_out = pl.BlockSpec((1, 2, N_HEADS, HEAD_DIM, HEAD_DIM), lambda b, c: (b, 0, 0, 0, 0))
    return pl.pallas_call(
        functools.partial(_ret_kernel, zero_init=state0 is None),
        grid=(nb, nc),
        in_specs=[pl.BlockSpec(dl_m.shape, lambda b, c: (0, 0, 0, 0)),
                  pl.BlockSpec(dl_q.shape, lambda b, c: (0, 0, 0, 0)),
                  blk_f, blk_f, blk_f, blk_b, blk_b, blk_b] + st_in,
        out_specs=[blk_f, blk_b, st_out],
        out_shape=[jax.ShapeDtypeStruct((n, D_GROUP), F32), jax.ShapeDtypeStruct((n, D_GROUP), F32),
                   jax.ShapeDtypeStruct((nb, 2, N_HEADS, HEAD_DIM, HEAD_DIM), F32)],
        scratch_shapes=[pltpu.VMEM((2, npair, pw, pw), F32),
                        pltpu.VMEM((2 * npair, RET_CHUNK, 2 * RET_CHUNK), F32),
                        pltpu.VMEM((2 * npair, RET_CHUNK, pw), F32),
                        pltpu.VMEM((2 * npair, RET_CHUNK, pw), F32),
                        pltpu.VMEM((2 * npair, 8, pw), F32)],
        compiler_params=_cparams("arbitrary", "arbitrary"),
        name="retention_scan",
    )(dl_m, dl_q, q, k, v, q, k, v, *([] if state0 is None else [_states_to_pairs(state0)]))


def _rwkv_kernel(*refs, zero_init):
    (xrf_ref, xkf_ref, xvf_ref, lof_ref, xrb_ref, xkb_ref, xvb_ref, lob_ref,
     w0_ref, wup_ref, a0_ref, aup_ref, kk_ref, ka_ref) = refs[:14]
    s0_ref = None if zero_init else refs[14]
    yf_ref, yb_ref, sf_ref, st = refs[14 if zero_init else 15:]
    c = pl.program_id(1)
    cs = RWKV_CHUNK
    nsub = RWKV_CHUNKS_PER_STEP
    npair = N_HEADS // 2

    @pl.when(c == 0)
    def _():
        _load_states(st, s0_ref)

    row = lax.broadcasted_iota(jnp.int32, (cs, 3 * cs), 0)
    col = lax.broadcasted_iota(jnp.int32, (cs, 3 * cs), 1) % cs
    pw = 2 * HEAD_DIM
    prow = lax.broadcasted_iota(jnp.int32, (cs, pw), 0)
    pcol = lax.broadcasted_iota(jnp.int32, (cs, pw), 1) % HEAD_DIM
    eye = jnp.where(pcol == prow, 1.0, 0.0).astype(F32)
    sub_diag = (prow // RWKV_SUB) == (pcol // RWKV_SUB)
    brow = lax.broadcasted_iota(jnp.int32, (pw, pw), 0) // HEAD_DIM
    bcol = lax.broadcasted_iota(jnp.int32, (pw, pw), 1) // HEAD_DIM
    same_head = brow == bcol

    bd = _pair_block_diag

    def direction(d, sub, xr_ref, xk_ref, xv_ref, lo_ref):
        rows = pl.ds(sub * cs, cs)
        tri3 = jnp.where((row >= col) if d == 0 else (col >= row), 1.0, 0.0).astype(BF16)
        dist = (prow - pcol) if d == 0 else (pcol - prow)
        xk = xk_ref[rows, :]
        lo = lo_ref[rows, :]
        w_log = -math.exp(-0.5) * _sigmoid(w0_ref[d] + _bdot(jnp.tanh(lo[:, 0:D_LORA]), wup_ref[d]))
        a_all = _sigmoid(a0_ref[d] + _bdot(lo[:, D_LORA:2 * D_LORA], aup_ref[d]))
        w1 = w_log.astype(BF16)
        r1 = w_log - w1.astype(F32)
        w2 = r1.astype(BF16)
        w3 = (r1 - w2.astype(F32)).astype(BF16)
        cum = jnp.dot(tri3, jnp.concatenate([w1, w2, w3], axis=0), preferred_element_type=F32)
        cum_end = cum[cs - 1:cs, :] if d == 0 else cum[0:1, :]
        return dict(earlier=dist > 0, upto=dist >= 0, xr=xr_ref[rows, :], xv=xv_ref[rows, :], a_all=a_all,
                    e_incl=jnp.exp(cum), e_excl=jnp.exp(cum - w_log), e_neg=jnp.exp(-cum),
                    e_end=jnp.exp(cum_end - cum), w_end=jnp.exp(cum_end), kk_all=xk * kk_ref[...],
                    keff_all=xk * (1.0 + (a_all - 1.0) * ka_ref[...]))

    in_refs = [(xrf_ref, xkf_ref, xvf_ref, lof_ref), (xrb_ref, xkb_ref, xvb_ref, lob_ref)]
    dirs = {(d, sub): direction(d, sub, *in_refs[d]) for d in range(2) for sub in range(nsub)}
    y_refs = [yf_ref, yb_ref]

    chains = [(d, sub, p) for d in range(2) for sub in range(nsub) for p in range(npair)]
    pairs = range(len(chains))
    sls = [slice(p * pw, (p + 1) * pw) for _, _, p in chains]
    pick = lambda name: [dirs[d, sub][name][:, sls[i]] for i, (d, sub, _) in enumerate(chains)]
    earlier = [dirs[d, sub]["earlier"] for d, sub, _ in chains]
    upto = [dirs[d, sub]["upto"] for d, sub, _ in chains]
    cat = lambda x, y: jnp.concatenate([x, y], axis=0)

    head_sum = _pair_head_sum

    e_incl, e_excl, e_neg, e_end, w_end = (pick(k) for k in ("e_incl", "e_excl", "e_neg", "e_end", "w_end"))
    kk = [x * lax.rsqrt(head_sum(x * x) + 1e-12) for x in pick("kk_all")]
    b = [kk[p] * a for p, a in zip(pairs, pick("a_all"))]
    keff = pick("keff_all")
    v = pick("xv")
    a_t = [-kk[p] * e_excl[p] for p in pairs]
    r_t = [xr * e_incl[p] for p, xr in zip(pairs, pick("xr"))]
    ar = [cat(a_t[p], r_t[p]).astype(BF16) for p in pairs]
    g = [_bdot_nt(ar[p], cat(bd(b[p] * e_neg[p]), bd(keff[p] * e_neg[p]))) for p in pairs]
    n_mat = [jnp.where(earlier[p], g[p][:cs, :pw], 0.0) for p in pairs]
    m_rb = [jnp.where(upto[p], g[p][cs:, :pw], 0.0) for p in pairs]
    a_ak = [jnp.where(earlier[p], g[p][:cs, pw:], 0.0) for p in pairs]
    m_rk = [jnp.where(upto[p], g[p][cs:, pw:], 0.0) for p in pairs]
    bd_s = lambda xs: (bd(xs[0]), bd(xs[1]))
    n_d = [jnp.where(sub_diag, n, 0.0) for n in n_mat]
    n_o = [jnp.where(sub_diag, 0.0, n) for n in n_mat]
    n_s = [_split2(n) for n in n_d]
    x = [eye + n for n in n_d]
    pk = [_mm3s(ns, bd_s(ns)) for ns in n_s]
    for _ in range(RWKV_SUB_DOUBLINGS - 1):
        xp = [_mm3s(_split2(cat(x[p], pk[p])), bd_s(_split2(pk[p]))) for p in pairs]
        x = [x[p] + xp[p][:cs] for p in pairs]
        pk = [r[cs:] for r in xp]
    t_d = [x[p] + _mm3s(_split2(x[p]), bd_s(_split2(pk[p]))) for p in pairs]
    lcat = lambda x, y: jnp.concatenate([x, y], axis=1)
    m_o = [_bdot(t_d[p], bd(n_o[p])) for p in pairs]
    mz = [_bdot(m_o[p], lcat(bd(t_d[p]), bd(m_o[p]))) for p in pairs]
    z = [t_d[p] + mz[p][:, :pw] for p in pairs]
    t_inv = [(z[p] + _bdot(mz[p][:, pw:], bd(z[p]))).astype(BF16) for p in pairs]
    av = [_bdot(cat(a_ak[p], m_rk[p]), bd(v[p])) for p in pairs]
    tau = [_bdot(t_inv[p], lcat(bd(a_t[p]), bd(av[p][:cs]))) for p in pairs]
    bk_end = [cat(b[p] * e_end[p], keff[p] * e_end[p]) for p in pairs]
    for step in range(nsub):
        now = [i for i, (d, sub, _) in enumerate(chains) if sub == (step if d == 0 else nsub - 1 - step)]
        s_prev = {i: st[chains[i][0], chains[i][2]] for i in now}
        hs = {i: _bdot_nt(cat(tau[i][:, :pw], r_t[i]), s_prev[i]) for i in now}
        u = {i: hs[i][:cs] + tau[i][:, pw:] for i in now}
        y = {i: hs[i][cs:] + _bdot(m_rb[i], bd(u[i])) + av[i][cs:] for i in now}
        for i in now:
            d, sub, p = chains[i]
            y_refs[d][pl.ds(sub * cs, cs), sls[i]] = y[i]
            st[d, p] = s_prev[i] * w_end[i] + jnp.where(same_head, _bdot_tn(cat(u[i], v[i]), bk_end[i]), 0.0)

    @pl.when(c == pl.num_programs(1) - 1)
    def _():
        _store_states(sf_ref, st)


def _rwkv_scan(xr, xk, xv, lora, w0, w_up, a0, a_up, k_k, k_a, state0, seq):
    n = xr.shape[0]
    nb = n // seq
    blk = RWKV_CHUNK * RWKV_CHUNKS_PER_STEP
    nc = seq // blk

    fwd_idx = lambda b, c: (b * nc + c, 0)
    bwd_idx = lambda b, c: (b * nc + nc - 1 - c, 0)
    blk_f = pl.BlockSpec((blk, D_GROUP), fwd_idx)
    blk_b = pl.BlockSpec((blk, D_GROUP), bwd_idx)
    lblk_f = pl.BlockSpec((blk, lora.shape[1]), fwd_idx)
    lblk_b = pl.BlockSpec((blk, lora.shape[1]), bwd_idx)
    dvec = pl.BlockSpec((2, 1, D_GROUP), lambda b, c: (0, 0, 0))
    dmat = pl.BlockSpec((2, D_LORA, D_GROUP), lambda b, c: (0, 0, 0))
    vec = pl.BlockSpec((1, D_GROUP), lambda b, c: (0, 0))
    npair, pw = N_HEADS // 2, 2 * HEAD_DIM
    st_in = ([] if state0 is None else
             [pl.BlockSpec((1, 2, npair, pw, pw), lambda b, c: (b, 0, 0, 0, 0))])
    st_out = pl.BlockSpec((1, 2, N_HEADS, HEAD_DIM, HEAD_DIM), lambda b, c: (b, 0, 0, 0, 0))
    return pl.pallas_call(
        functools.partial(_rwkv_kernel, zero_init=state0 is None),
        grid=(nb, nc),
        in_specs=[blk_f, blk_f, blk_f, lblk_f, blk_b, blk_b, blk_b, lblk_b,
                  dvec, dmat, dvec, dmat, vec, vec] + st_in,
        out_specs=[blk_f, blk_b, st_out],
        out_shape=[jax.ShapeDtypeStruct((n, D_GROUP), F32), jax.ShapeDtypeStruct((n, D_GROUP), F32),
                   jax.ShapeDtypeStruct((nb, 2, N_HEADS, HEAD_DIM, HEAD_DIM), F32)],
        scratch_shapes=[pltpu.VMEM((2, npair, pw, pw), F32)],
        compiler_params=_cparams("arbitrary", "arbitrary"),
        name="rwkv7_scan",
    )(xr, xk, xv, lora, xr, xk, xv, lora, w0.reshape(2, 1, D_GROUP), w_up, a0.reshape(2, 1, D_GROUP), a_up,
      k_k.reshape(1, D_GROUP), k_a.reshape(1, D_GROUP), *([] if state0 is None else [_states_to_pairs(state0)]))


ATTN_WIDTHS = (D_GROUP,) * 6
REC_WIDTHS = (D_GROUP,) * 7 + (2 * D_LORA + D_LORA_G,)


def kernel(x_prompt, x_sample, cache_na_k, cache_na_v, cache_diff_k, cache_diff_v, state_ret, state_rwkv, c, c_ctx, norm_mix_g, norm_ffn_g, norm_final_g, w_ada, b_ada, w_in_attn, w_out_attn, na_rpb, diff_lq1, diff_lk1, diff_lq2, diff_lk2, w_in_rec, w_out_rec, ret_decay_logit, rw_w0, rw_w_up, rw_a0, rw_a_up, rw_g_up, rw_k_k, rw_k_a, rw_r_k, rw_ln_g, rw_ln_b, w_ffn_in, w_ffn_out):
    bp, seq, d = x_prompt.shape
    bl, n_lat, _ = x_sample.shape
    depth = w_ada.shape[0]
    ctx = x_prompt.reshape(bp * seq, d)
    lat = x_sample.reshape(bl * n_lat, d)

    cond8 = jnp.zeros((8, d), F32).at[0].set(c_ctx).at[1:1 + bl].set(c)
    mods = _ada_modulation(cond8, w_ada, b_ada).reshape(depth, 8, 6, d)
    w_ffn_in_bf16 = _cast_bf16(w_ffn_in)
    w_ffn_out_bf16 = _cast_bf16(w_ffn_out)

    outs = {}
    for layer in range(depth):
        m_ctx = [mods[layer, 0:1, j].reshape(1, 1, d) for j in range(6)]
        m_lat = [mods[layer, 1:1 + bl, j].reshape(bl, 1, d) for j in range(6)]
        if layer % 2 == 0:
            i = layer // 2
            lam_init = 0.8 - 0.6 * math.exp(-0.3 * layer)
            lam_params = jnp.stack([diff_lq1[i], diff_lk1[i], diff_lq2[i], diff_lk2[i]]).astype(F32)
            w_in = w_in_attn[i].astype(BF16)
            w_out = w_out_attn[i].astype(BF16)
            res = _ctx_attention(ctx, norm_mix_g[layer], m_ctx[0], m_ctx[1], w_in, lam_params, seq, lam_init)
            mix_ctx = res[:2]
            outs["na_k"], outs["na_v"], outs["df_k"], outs["df_v"] = res[2:]
            plat = _norm_mod_matmul(lat, norm_mix_g[layer], m_lat[0], m_lat[1], w_in, ATTN_WIDTHS, n_lat, PROJ_TM)
            past = cache_na_k.shape[2]
            tt = _na_bias_table(na_rpb[i])
            o_na = _lat_na_attention(plat[0], plat[1], plat[2],
                                     cache_na_k[:, i].reshape(bl, past, D_GROUP),
                                     cache_na_v[:, i].reshape(bl, past, D_GROUP), tt, n_lat)
            o_df = _lat_diff_attention(lam_params, plat[3], plat[4], plat[5],
                                       cache_diff_k[:, i].reshape(bl, past, D_GROUP),
                                       cache_diff_v[:, i].reshape(bl, past, D_GROUP), n_lat, lam_init)
            mix_lat = (o_na, o_df)
        else:
            j = layer // 2
            w_in = w_in_rec[j].astype(BF16)
            w_out = w_out_rec[j].astype(BF16)
            pc = _norm_mod_matmul(ctx, norm_mix_g[layer], m_ctx[0], m_ctx[1], w_in, REC_WIDTHS, bp * seq, PROJ_TM)
            plat = _norm_mod_matmul(lat, norm_mix_g[layer], m_lat[0], m_lat[1], w_in, REC_WIDTHS, n_lat, PROJ_TM)
            mixes = []
            for is_ctx, p, s_ret0, s_rw0, sq in ((True, pc, None, None, seq),
                                                 (False, plat, state_ret[:, j], state_rwkv[:, j], n_lat)):
                rq, rk, rv, rg, wr, wk, wv, lora = p
                o_f, o_b, s_ret = _retention_scan(ret_decay_logit[j], rq, rk, rv, s_ret0, sq)
                y_f, y_b, s_rw = _rwkv_scan(wr, wk, wv, lora, rw_w0[j], rw_w_up[j].astype(BF16), rw_a0[j],
                                            rw_a_up[j].astype(BF16), rw_k_k[j], rw_k_a[j], s_rw0, sq)
                mixes.append((o_f, o_b, rg, y_f, y_b, wr, wk, wv, lora, rw_r_k[j].reshape(1, D_GROUP),
                              rw_ln_g[j].reshape(1, D_GROUP), rw_ln_b[j].reshape(1, D_GROUP),
                              rw_g_up[j].astype(BF16)))
                if is_ctx:
                    outs["ret"], outs["rwkv"] = s_ret, s_rw
            mix_ctx, mix_lat = mixes
        final = layer == depth - 1
        ctx = _layer_tail(ctx, mix_ctx, w_out, m_ctx[2], norm_ffn_g[layer], m_ctx[3], m_ctx[4], m_ctx[5],
                          w_ffn_in_bf16, w_ffn_out_bf16, layer, norm_final_g, bp * seq, final)
        lat = _layer_tail(lat, mix_lat, w_out, m_lat[2], norm_ffn_g[layer], m_lat[3], m_lat[4], m_lat[5],
                          w_ffn_in_bf16, w_ffn_out_bf16, layer, norm_final_g, n_lat, final)

    y_prompt = ctx.reshape(bp, seq, d)
    y_sample = lat.reshape(bl, n_lat, d)
    new_cache_na_k = outs["na_k"].reshape(bp, 1, seq, N_HEADS, HEAD_DIM)
    new_cache_na_v = outs["na_v"].reshape(bp, 1, seq, N_HEADS, HEAD_DIM)
    new_cache_diff_k = outs["df_k"].reshape(bp, 1, seq, N_HEADS // 2, 2, HEAD_DIM)
    new_cache_diff_v = outs["df_v"].reshape(bp, 1, seq, N_HEADS // 2, 2 * HEAD_DIM)
    new_state_ret = outs["ret"].reshape(bp, 1, 2, N_HEADS, HEAD_DIM, HEAD_DIM)
    new_state_rwkv = outs["rwkv"].reshape(bp, 1, 2, N_HEADS, HEAD_DIM, HEAD_DIM)
    return (y_prompt, y_sample, new_cache_na_k, new_cache_na_v, new_cache_diff_k, new_cache_diff_v,
            new_state_ret, new_state_rwkv)
```

```python
import functools
import math

import jax
import jax.numpy as jnp
from jax import lax
from jax.experimental import pallas as pl
from jax.experimental.pallas import tpu as pltpu

F32 = jnp.float32
BF16 = jnp.bfloat16

HEAD_DIM = 64
N_HEADS = 8
D_GROUP = 512
GRID_W = 64
GRID_ROWS = 32
WIN_R = 8
WIN_W = 16
RPB_R = 15
RPB_C = 31
D_FF = 2816
FFN_CHUNK = D_FF // 2
D_LORA = 64
D_LORA_G = 128
ROPE_BASE = 10000.0
RMS_EPS = 1e-6
RWKV_GN_EPS = 64e-5
NEG_INF = -1e30
ATTN_SCALE = HEAD_DIM ** -0.5
RET_CHUNK = 128
RET_CHUNKS_PER_STEP = 2
RWKV_CHUNK = 64
RWKV_CHUNKS_PER_STEP = 4
RWKV_SUB = 16
RWKV_SUB_DOUBLINGS = RWKV_SUB.bit_length() - 2
assert RWKV_CHUNK == 4 * RWKV_SUB
V7X_VMEM_BYTES = 64 * 1024 * 1024
VMEM_LIMIT = V7X_VMEM_BYTES - 8 * 1024 * 1024
ADA_TN = 1536
TAIL_TM = 512
TAIL_TM_REC = 256
PROJ_TM = 512
CTX_SEQS_PER_STEP = 1
LAT_DIFF_TQ = 256


def _cparams(*sem):
    return pltpu.CompilerParams(dimension_semantics=sem, vmem_limit_bytes=VMEM_LIMIT)


def _sigmoid(x):
    return 1.0 / (1.0 + jnp.exp(-x))


def _silu(x):
    return x * _sigmoid(x)


def _rms(x, eps=RMS_EPS):
    return x * lax.rsqrt(jnp.mean(x * x, axis=-1, keepdims=True) + eps)


def _bdot(a, b):
    return jnp.dot(a.astype(BF16), b.astype(BF16), preferred_element_type=F32)


def _bdot_nt(a, b):
    return lax.dot_general(a.astype(BF16), b.astype(BF16), (((1,), (1,)), ((), ())),
                           preferred_element_type=F32)


def _bdot_tn(a, b):
    return lax.dot_general(a.astype(BF16), b.astype(BF16), (((0,), (0,)), ((), ())),
                           preferred_element_type=F32)


def _split2(x):
    hi = x.astype(BF16)
    lo = (x - hi.astype(F32)).astype(BF16)
    return hi, lo


def _mm3s(a_split, b_split):
    ah, al = a_split
    bh, bl = b_split
    n = bh.shape[1]
    rhs = jnp.concatenate([jnp.concatenate([bh, bl], axis=1),
                           jnp.concatenate([bh, jnp.zeros_like(bl)], axis=1)], axis=0)
    out = jnp.dot(jnp.concatenate([ah, al], axis=1), rhs, preferred_element_type=F32)
    return out[:, :n] + out[:, n:]


def _diff_lambda(lp, lam_init):
    s1 = jnp.sum(lp[0:1, :] * lp[1:2, :], axis=-1, keepdims=True)
    s2 = jnp.sum(lp[2:3, :] * lp[3:4, :], axis=-1, keepdims=True)
    return jnp.exp(s1) - jnp.exp(s2) + lam_init


def _ada_kernel(c_ref, w_ref, b_ref, o_ref):
    s = _silu(c_ref[...])
    o_ref[0] = _bdot(s, w_ref[0]) + b_ref[0]


def _ada_modulation(cond8, w_ada, b_ada):
    depth, d, n = w_ada.shape
    tn = ADA_TN
    return pl.pallas_call(
        _ada_kernel,
        grid=(depth, n // tn),
        in_specs=[pl.BlockSpec((8, d), lambda l, j: (0, 0)),
                  pl.BlockSpec((1, d, tn), lambda l, j: (l, 0, j)),
                  pl.BlockSpec((1, 1, tn), lambda l, j: (l, 0, j))],
        out_specs=pl.BlockSpec((1, 8, tn), lambda l, j: (l, 0, j)),
        out_shape=jax.ShapeDtypeStruct((depth, 8, n), F32),
        compiler_params=_cparams("arbitrary", "arbitrary"),
        name="ada_modulation",
    )(cond8, w_ada, b_ada.reshape(depth, 1, n))


def _norm_mod(x_ref, g_ref, sh_ref, sc_ref):
    h = _rms(x_ref[...]) * g_ref[...]
    return (h * (1.0 + sc_ref[0]) + sh_ref[0]).astype(BF16)


def _nmm_kernel(x_ref, g_ref, sh_ref, sc_ref, w_ref, *o_refs, widths):
    hb = _norm_mod(x_ref, g_ref, sh_ref, sc_ref)
    off = 0
    for o_ref, w in zip(o_refs, widths):
        o_ref[...] = jnp.dot(hb, w_ref[:, off:off + w], preferred_element_type=F32)
        off += w


def _norm_mod_matmul(x, gain, shift, scale, w_bf16, widths, rows_per_mod, tm):
    n, d = x.shape
    tiles_per_mod = rows_per_mod // tm
    mod_spec = pl.BlockSpec((1, 1, d), lambda i: (i // tiles_per_mod, 0, 0))
    return pl.pallas_call(
        functools.partial(_nmm_kernel, widths=widths),
        grid=(n // tm,),
        in_specs=[pl.BlockSpec((tm, d), lambda i: (i, 0)),
                  pl.BlockSpec((1, d), lambda i: (0, 0)),
                  mod_spec, mod_spec,
                  pl.BlockSpec(w_bf16.shape, lambda i: (0, 0), pipeline_mode=pl.Buffered(1))],
        out_specs=[pl.BlockSpec((tm, w), lambda i: (i, 0)) for w in widths],
        out_shape=[jax.ShapeDtypeStruct((n, w), F32) for w in widths],
        compiler_params=_cparams("arbitrary"),
        name="norm_mod_matmul",
    )(x, gain.reshape(1, d), shift, scale, w_bf16)


PAIR_W = 2 * HEAD_DIM


def _pair_tiles(x):
    return [x[:, j * PAIR_W:(j + 1) * PAIR_W] for j in range(x.shape[1] // PAIR_W)]


def _split_pair(tile):
    left = lax.broadcasted_iota(jnp.int32, tile.shape, 1) < HEAD_DIM
    zero = jnp.zeros_like(tile)
    return jnp.where(left, tile, zero), jnp.where(left, zero, tile)


def _pair_head_sum(tile):
    left = lax.broadcasted_iota(jnp.int32, tile.shape, 1) < HEAD_DIM
    zero = jnp.zeros_like(tile)
    s_left = jnp.sum(jnp.where(left, tile, zero), axis=-1, keepdims=True)
    s_right = jnp.sum(jnp.where(left, zero, tile), axis=-1, keepdims=True)
    return jnp.where(left, s_left, s_right)


def _with_ones(v):
    return jnp.concatenate([v, jnp.ones_like(v)], axis=1)


def _exp_scores(s):
    return jnp.exp(s - jnp.max(s, axis=-1, keepdims=True)).astype(BF16)


def _merge_pair(out_even, out_odd):
    left = lax.broadcasted_iota(jnp.int32, (out_even.shape[0], PAIR_W), 1) < HEAD_DIM
    return jnp.where(left, out_even[:, :PAIR_W] * (1.0 / out_even[:, PAIR_W:]),
                     out_odd[:, :PAIR_W] * (1.0 / out_odd[:, PAIR_W:]))


def _diff_combine(out1, out2, lam, lam_init):
    o = out1[:, :PAIR_W] * (1.0 / out1[:, PAIR_W:]) - out2[:, :PAIR_W] * (lam / out2[:, PAIR_W:])
    return _rms(o) * (1.0 - lam_init)


CTX_CACHE_SHAPES = ((N_HEADS, HEAD_DIM), (N_HEADS, HEAD_DIM),
                    (N_HEADS // 2, 2, HEAD_DIM), (N_HEADS // 2, 2 * HEAD_DIM))


def _ctx_attn_kernel(x_ref, g_ref, sh_ref, sc_ref, w_ref, lp_ref, on_ref, od_ref, *cache_refs, lam_init, seq):
    hb = _norm_mod(x_ref, g_ref, sh_ref, sc_ref)
    proj = [jnp.dot(hb, w_ref[:, i * D_GROUP:(i + 1) * D_GROUP], preferred_element_type=F32)
            for i in range(len(ATTN_WIDTHS))]
    for c_ref, val in zip(cache_refs, (proj[1], proj[2], proj[4], proj[5])):
        c_ref[...] = val.reshape(c_ref.shape)
    tiles = range(D_GROUP // PAIR_W)
    lam = _diff_lambda(lp_ref[...], lam_init)
    for s in range(hb.shape[0] // seq):
        rows = slice(s * seq, (s + 1) * seq)
        qn, kn, vn, qd, kd, vd = (p[rows] for p in proj)
        q = [_split_pair(t) for t in _pair_tiles((qn * ATTN_SCALE).astype(BF16))
             + _pair_tiles((qd * ATTN_SCALE).astype(BF16))]
        k = _pair_tiles(kn.astype(BF16)) + _pair_tiles(kd.astype(BF16))
        v = [_with_ones(t) for t in _pair_tiles(vn.astype(BF16)) + _pair_tiles(vd.astype(BF16))]
        e = [[_exp_scores(_bdot_nt(q[j][c], k[j])) for c in range(2)] for j in range(2 * len(tiles))]
        out = [[jnp.dot(e[j][c], v[j], preferred_element_type=F32) for c in range(2)]
               for j in range(2 * len(tiles))]
        for j in tiles:
            sl = slice(j * PAIR_W, (j + 1) * PAIR_W)
            on_ref[rows, sl] = _merge_pair(out[j][0], out[j][1])
            od_ref[rows, sl] = _diff_combine(out[len(tiles) + j][0], out[len(tiles) + j][1], lam, lam_init)


def _ctx_attention(x, gain, shift, scale, w_bf16, lam_params, seq, lam_init):
    n, d = x.shape
    tm = seq * CTX_SEQS_PER_STEP
    blk = pl.BlockSpec((tm, D_GROUP), lambda b: (b, 0))
    mod_spec = pl.BlockSpec((1, 1, d), lambda b: (0, 0, 0))
    cache_specs = [pl.BlockSpec((tm,) + dims, lambda b, nd=len(dims): (b,) + (0,) * nd)
                   for dims in CTX_CACHE_SHAPES]
    return pl.pallas_call(
        functools.partial(_ctx_attn_kernel, lam_init=lam_init, seq=seq),
        grid=(n // tm,),
        in_specs=[pl.BlockSpec((tm, d), lambda b: (b, 0)),
                  pl.BlockSpec((1, d), lambda b: (0, 0)),
                  mod_spec, mod_spec,
                  pl.BlockSpec(w_bf16.shape, lambda b: (0, 0), pipeline_mode=pl.Buffered(1)),
                  pl.BlockSpec(lam_params.shape, lambda b: (0, 0))],
        out_specs=[blk, blk] + cache_specs,
        out_shape=([jax.ShapeDtypeStruct((n, D_GROUP), F32)] * 2
                   + [jax.ShapeDtypeStruct((n,) + dims, F32) for dims in CTX_CACHE_SHAPES]),
        compiler_params=_cparams("arbitrary"),
        name="ctx_attention",
    )(x, gain.reshape(1, d), shift, scale, w_bf16, lam_params)


def _rope_tables(n):
    quarter = HEAD_DIM // 4
    pos = jnp.arange(n)
    inv_freq = ROPE_BASE ** (-jnp.arange(quarter, dtype=F32) / quarter)
    lane = jnp.arange(HEAD_DIM)
    p = jnp.where(lane[None, :] < HEAD_DIM // 2, (pos // GRID_W)[:, None], (pos % GRID_W)[:, None]).astype(F32)
    ang = p * inv_freq[lane % quarter][None, :]
    sign = jnp.where((lane % (2 * quarter)) < quarter, -1.0, 1.0).astype(F32)
    cos = jnp.tile(jnp.cos(ang), (1, 2))
    sin = jnp.tile(jnp.sin(ang) * sign[None, :], (1, 2))
    return cos, sin


def _rope(x, cos, sin_signed):
    quarter = HEAD_DIM // 4
    lane = lax.broadcasted_iota(jnp.int32, cos.shape, 1)
    first = (lane % (2 * quarter)) < quarter
    tiles = []
    for t in _pair_tiles(x):
        partner = jnp.where(first, pltpu.roll(t, PAIR_W - quarter, 1), pltpu.roll(t, quarter, 1))
        tiles.append(t * cos + partner * sin_signed)
    return jnp.concatenate(tiles, axis=1)


def _lat_diff_kernel(lp_ref, q_ref, k_ref, v_ref, ck_ref, cv_ref, cosq_ref, sinq_ref, cosk_ref, sink_ref,
                     o_ref, kall, vall, *, lam_init, n_lat):
    @pl.when(pl.program_id(1) == 0)
    def _():
        kall[0:n_lat, :] = _rope(k_ref[...], cosk_ref[...], sink_ref[...]).astype(BF16)
        kall[n_lat:, :] = ck_ref[...].astype(BF16)
        for j, (vt, ct) in enumerate(zip(_pair_tiles(v_ref[...].astype(BF16)),
                                         _pair_tiles(cv_ref[...].astype(BF16)))):
            vall[j, 0:n_lat, :] = _with_ones(vt)
            vall[j, n_lat:, :] = _with_ones(ct)

    lam = _diff_lambda(lp_ref[...], lam_init)
    q = (_rope(q_ref[...], cosq_ref[...], sinq_ref[...]) * ATTN_SCALE).astype(BF16)
    qs = [_split_pair(qt) for qt in _pair_tiles(q)]
    sls = [slice(j * PAIR_W, (j + 1) * PAIR_W) for j in range(len(qs))]
    for group in ((0, 1), (2, 3)):
        e = {j: [_exp_scores(_bdot_nt(qs[j][c], kall[:, sls[j]])) for c in range(2)] for j in group}
        out = {j: [jnp.dot(e[j][c], vall[j], preferred_element_type=F32) for c in range(2)] for j in group}
        for j in group:
            o_ref[:, sls[j]] = _diff_combine(out[j][0], out[j][1], lam, lam_init)


def _lat_diff_attention(lam_params, qd, kd, vd, cache_k, cache_v, n_lat, lam_init, tq=LAT_DIFF_TQ):
    n = qd.shape[0]
    nb = n // n_lat
    nq = n_lat // tq
    past = cache_k.shape[1]
    cos, sin = _rope_tables(n_lat)
    qblk = pl.BlockSpec((tq, D_GROUP), lambda b, i: (b * nq + i, 0))
    once = pl.Buffered(1)
    kvblk = pl.BlockSpec((n_lat, D_GROUP), lambda b, i: (b, 0), pipeline_mode=once)
    cblk = pl.BlockSpec((None, past, D_GROUP), lambda b, i: (b, 0, 0))
    return pl.pallas_call(
        functools.partial(_lat_diff_kernel, lam_init=lam_init, n_lat=n_lat),
        grid=(nb, nq),
        in_specs=[pl.BlockSpec(lam_params.shape, lambda b, i: (0, 0)),
                  qblk, kvblk, kvblk, cblk, cblk,
                  pl.BlockSpec((tq, PAIR_W), lambda b, i: (i, 0)),
                  pl.BlockSpec((tq, PAIR_W), lambda b, i: (i, 0)),
                  pl.BlockSpec((n_lat, PAIR_W), lambda b, i: (0, 0), pipeline_mode=once),
                  pl.BlockSpec((n_lat, PAIR_W), lambda b, i: (0, 0), pipeline_mode=once)],
        out_specs=qblk,
        out_shape=jax.ShapeDtypeStruct((n, D_GROUP), F32),
        scratch_shapes=[pltpu.VMEM((n_lat + past, D_GROUP), BF16),
                        pltpu.VMEM((D_GROUP // PAIR_W, n_lat + past, 2 * PAIR_W), BF16)],
        compiler_params=_cparams("arbitrary", "arbitrary"),
        name="lat_diff_attention",
    )(lam_params, qd, kd, vd, cache_k, cache_v, cos, sin, cos, sin)


NA_Q_ROWS = 4
NA_KEY_ROWS = WIN_R + NA_Q_ROWS
NA_TABLE = RPB_R + 1


def _na_bias_kernel(rpb_ref, tt_ref):
    h = pl.program_id(0)
    lane = lax.broadcasted_iota(jnp.int32, (GRID_W, 2 * GRID_W), 1)
    qc = lax.broadcasted_iota(jnp.int32, (GRID_W, 2 * GRID_W), 0)
    dc = jnp.clip(lane % GRID_W - qc + (WIN_W - 1), 0, RPB_C - 1)
    first = lane < GRID_W
    rows = []
    for dr in range(RPB_R):
        acc = jnp.zeros((GRID_W, 2 * GRID_W), F32)
        for d in range(RPB_C):
            acc = jnp.where(dc == d, rpb_ref[h * RPB_R + dr, d], acc)
        rows.append(acc)
    for k in range(NA_TABLE):
        tt_ref[0, k] = jnp.where(first, rows[min(max(k - 1, 0), RPB_R - 1)], rows[min(k, RPB_R - 1)])


def _na_bias_table(rpb):
    return pl.pallas_call(
        _na_bias_kernel,
        grid=(N_HEADS,),
        in_specs=[pl.BlockSpec(memory_space=pltpu.SMEM)],
        out_specs=pl.BlockSpec((1, NA_TABLE, GRID_W, 2 * GRID_W), lambda h: (h, 0, 0, 0)),
        out_shape=jax.ShapeDtypeStruct((N_HEADS, NA_TABLE, GRID_W, 2 * GRID_W), F32),
        compiler_params=_cparams("arbitrary"),
        name="na_bias_table",
    )(rpb.reshape(N_HEADS * RPB_R, RPB_C))


def _lat_na_kernel(q_ref, k_ref, v_ref, ck_ref, cv_ref, tt_ref, o_ref):
    step = pl.program_id(1)
    past = ck_ref.shape[0]
    nwin = NA_KEY_ROWS * GRID_W
    r_q = [NA_Q_ROWS * step + a for a in range(NA_Q_ROWS)]
    r_start = [jnp.clip(r - WIN_R // 2, 0, GRID_ROWS - WIN_R) for r in r_q]
    ws = jnp.minimum(r_start[0], GRID_ROWS - NA_KEY_ROWS)
    row0 = pl.multiple_of(ws * GRID_W, GRID_W)
    lane = lax.broadcasted_iota(jnp.int32, (GRID_W, past + nwin), 1)
    key_row = ws + (lane - past) // GRID_W
    kc = lane % GRID_W
    qc = lax.broadcasted_iota(jnp.int32, (GRID_W, past + nwin), 0)
    c_start = jnp.clip(qc - WIN_W // 2, 0, GRID_W - WIN_W)
    col_ok = (kc >= c_start) & (kc < c_start + WIN_W)
    visible = jnp.concatenate(
        [(lane < past) | (col_ok & (key_row >= rs) & (key_row < rs + WIN_R)) for rs in r_start], axis=0)
    no_bias = jnp.zeros((GRID_W, past), F32)

    def bias_rows(h, r):
        tiles = [tt_ref[h, jnp.clip(ws + 2 * j - r + WIN_R, 0, NA_TABLE - 1)] for j in range(NA_KEY_ROWS // 2)]
        return jnp.concatenate([no_bias] + tiles, axis=1)

    q = [_split_pair(t) for t in _pair_tiles((q_ref[...] * ATTN_SCALE).astype(BF16))]
    kcat = _pair_tiles(jnp.concatenate([ck_ref[...], k_ref[pl.ds(row0, nwin), :]], axis=0).astype(BF16))
    vcat = [_with_ones(t) for t in
            _pair_tiles(jnp.concatenate([cv_ref[...], v_ref[pl.ds(row0, nwin), :]], axis=0).astype(BF16))]
    bias = [jnp.concatenate([bias_rows(h, r) for r in r_q], axis=0) for h in range(N_HEADS)]
    e = [[_exp_scores(jnp.where(visible, _bdot_nt(q[j][c], kcat[j]) + bias[2 * j + c], NEG_INF))
          for c in range(2)] for j in range(len(q))]
    out = [[jnp.dot(e[j][c], vcat[j], preferred_element_type=F32) for c in range(2)] for j in range(len(q))]
    for j in range(len(q)):
        o_ref[:, j * PAIR_W:(j + 1) * PAIR_W] = _merge_pair(out[j][0], out[j][1])


def _lat_na_attention(qn, kn, vn, cache_k, cache_v, tt, n_lat):
    n = qn.shape[0]
    nb = n // n_lat
    past = cache_k.shape[1]
    steps = GRID_ROWS // NA_Q_ROWS
    qblk = pl.BlockSpec((NA_Q_ROWS * GRID_W, D_GROUP), lambda b, r: (b * steps + r, 0))
    kvblk = pl.BlockSpec((n_lat, D_GROUP), lambda b, r: (b, 0))
    cblk = pl.BlockSpec((None, past, D_GROUP), lambda b, r: (b, 0, 0))
    return pl.pallas_call(
        _lat_na_kernel,
        grid=(nb, steps),
        in_specs=[qblk, kvblk, kvblk, cblk, cblk,
                  pl.BlockSpec(tt.shape, lambda b, r: (0, 0, 0, 0))],
        out_specs=qblk,
        out_shape=jax.ShapeDtypeStruct((n, D_GROUP), F32),
        compiler_params=_cparams("arbitrary", "arbitrary"),
        name="lat_na_attention",
    )(qn, kn, vn, cache_k, cache_v, tt)


def _rec_mix(of_ref, ob_ref, rg_ref, yf_ref, yb_ref, xr_ref, xk_ref, xv_ref, lo_ref, rk_ref, lng_ref, lnb_ref,
             gup_ref):
    def head_sums(x):
        return jnp.concatenate([_pair_head_sum(t) for t in _pair_tiles(x)], axis=1)

    inv_d = 1.0 / HEAD_DIM
    o = of_ref[...] + ob_ref[...]
    o_ret = _silu(rg_ref[...]) * (o * lax.rsqrt(head_sums(o * o) * inv_d + RMS_EPS))
    y = yf_ref[...] + yb_ref[...]
    yc = y - head_sums(y) * inv_d
    var = head_sums(yc * yc) * inv_d
    yn = yc * lax.rsqrt(var + RWKV_GN_EPS) * lng_ref[...] + lnb_ref[...]
    bonus = head_sums(xr_ref[...] * rk_ref[...] * xk_ref[...]) * xv_ref[...]
    g_rw = _bdot(_sigmoid(lo_ref[:, 2 * D_LORA:]), gup_ref[...])
    return o_ret, (yn + bonus) * g_rw


N_REC_TOKEN_INPUTS = 9
N_REC_PARAMS = 4


def _tail_kernel(*refs, final, ff_chunk, recurrent):
    x_ref = refs[0]
    if not recurrent:
        n_mix = 2
        mix_a, mix_b = refs[1][...], refs[2][...]
    else:
        n_mix = 2 * N_REC_TOKEN_INPUTS + N_REC_PARAMS
        first_refs = refs[1:1 + N_REC_TOKEN_INPUTS]
        next_refs = refs[1 + N_REC_TOKEN_INPUTS:1 + 2 * N_REC_TOKEN_INPUTS]
        param_refs = refs[1 + 2 * N_REC_TOKEN_INPUTS:1 + n_mix]
        mixa_scr, mixb_scr = refs[-2:]
        refs = refs[:-2]

        @pl.when(pl.program_id(0) == 0)
        def _():
            a0, b0 = _rec_mix(*first_refs, *param_refs)
            mixa_scr[...] = a0
            mixb_scr[...] = b0

        mix_a, mix_b = mixa_scr[...], mixb_scr[...]
    wo_ref, gm_ref, g_ref, sh_ref, sc_ref, gf_ref, wi_ref, wf_ref, fg_ref, o_ref = refs[1 + n_mix:]
    if recurrent:
        a1, b1 = _rec_mix(*next_refs, *param_refs)
        mixa_scr[...] = a1
        mixb_scr[...] = b1
    mix = _bdot(mix_a, wo_ref[0:D_GROUP, :]) + _bdot(mix_b, wo_ref[D_GROUP:, :])
    x1 = x_ref[...] + gm_ref[0] * mix
    h = _rms(x1) * g_ref[...]
    hb = (h * (1.0 + sc_ref[0]) + sh_ref[0]).astype(BF16)
    acc = jnp.zeros_like(x1)
    for c0 in range(0, D_FF, ff_chunk):
        gate = jnp.dot(hb, wi_ref[:, c0:c0 + ff_chunk], preferred_element_type=F32)
        up = jnp.dot(hb, wi_ref[:, D_FF + c0:D_FF + c0 + ff_chunk], preferred_element_type=F32)
        acc = acc + jnp.dot((_silu(gate) * up).astype(BF16), wf_ref[c0:c0 + ff_chunk, :],
                            preferred_element_type=F32)
    x2 = x1 + gf_ref[0] * acc
    if final:
        x2 = _rms(x2) * fg_ref[...]
    o_ref[...] = x2


def _layer_tail(x, mix_inputs, w_out, gate_mix, gain, shift, scale, gate_ffn, w_ffn_in_all, w_ffn_out_all, layer,
                final_gain, rows_per_mod, final, ff_chunk=FFN_CHUNK):
    n, d = x.shape
    recurrent = len(mix_inputs) != 2
    tm = TAIL_TM_REC if recurrent else TAIL_TM
    tiles_per_mod = rows_per_mod // tm
    last = n // tm - 1
    mod_spec = pl.BlockSpec((1, 1, d), lambda i: (i // tiles_per_mod, 0, 0))
    vec_spec = pl.BlockSpec((1, d), lambda i: (0, 0))
    resident = lambda a: pl.BlockSpec(a.shape, lambda i: (0,) * a.ndim, pipeline_mode=pl.Buffered(1))
    layer_slab = lambda a: pl.BlockSpec((None,) + a.shape[1:], lambda i: (layer, 0, 0),
                                        pipeline_mode=pl.Buffered(1))
    if recurrent:
        tok, params = mix_inputs[:N_REC_TOKEN_INPUTS], mix_inputs[N_REC_TOKEN_INPUTS:]
        mix_specs = ([pl.BlockSpec((tm, a.shape[1]), lambda i: (0, 0), pipeline_mode=pl.Buffered(1)) for a in tok]
                     + [pl.BlockSpec((tm, a.shape[1]), lambda i: (jnp.minimum(i + 1, last), 0)) for a in tok]
                     + [resident(a) for a in params])
        mix_args = list(tok) + list(tok) + list(params)
        scratch = [pltpu.VMEM((tm, D_GROUP), F32)] * 2
    else:
        mix_specs = [pl.BlockSpec((tm, D_GROUP), lambda i: (i, 0))] * 2
        mix_args = list(mix_inputs)
        scratch = []
    return pl.pallas_call(
        functools.partial(_tail_kernel, final=final, ff_chunk=ff_chunk, recurrent=recurrent),
        grid=(n // tm,),
        in_specs=[pl.BlockSpec((tm, d), lambda i: (i, 0))] + mix_specs
                 + [resident(w_out), mod_spec, vec_spec, mod_spec, mod_spec, mod_spec,
                    layer_slab(w_ffn_in_all), layer_slab(w_ffn_out_all), vec_spec],
        out_specs=pl.BlockSpec((tm, d), lambda i: (i, 0)),
        out_shape=jax.ShapeDtypeStruct((n, d), F32),
        scratch_shapes=scratch,
        compiler_params=_cparams("arbitrary"),
        name="layer_tail",
    )(x, *mix_args, w_out, gate_mix, gain.reshape(1, d), shift, scale, gate_ffn, w_ffn_in_all, w_ffn_out_all,
      final_gain.reshape(1, d))


def _log_sigmoid(x):
    return jnp.minimum(x, 0.0) - jnp.log(1.0 + jnp.exp(-jnp.abs(x)))


def _pair_block_diag(xp):
    left = lax.broadcasted_iota(jnp.int32, xp.shape, 1) < HEAD_DIM
    zero = jnp.zeros_like(xp)
    return jnp.concatenate([jnp.where(left, xp, zero), jnp.where(left, zero, xp)], axis=0)


def _states_to_pairs(state):
    nb = state.shape[0]
    s = state.reshape(nb, 2, N_HEADS // 2, 2, HEAD_DIM, HEAD_DIM)
    zero = jnp.zeros_like(s[:, :, :, 0])
    return jnp.concatenate([jnp.concatenate([s[:, :, :, 0], zero], axis=-1),
                            jnp.concatenate([zero, s[:, :, :, 1]], axis=-1)], axis=-2)


def _load_states(st, s0_ref):
    st[...] = jnp.zeros(st.shape, st.dtype) if s0_ref is None else s0_ref[0]


def _store_states(sf_ref, st):
    for d in range(st.shape[0]):
        for p in range(st.shape[1]):
            s = st[d, p]
            sf_ref[0, d, 2 * p] = s[:HEAD_DIM, :HEAD_DIM]
            sf_ref[0, d, 2 * p + 1] = s[HEAD_DIM:, HEAD_DIM:]


def _ret_kernel(*refs, zero_init):
    dlm_ref, dlq_ref, qf_ref, kf_ref, vf_ref, qb_ref, kb_ref, vb_ref = refs[:8]
    s0_ref = None if zero_init else refs[8]
    of_ref, ob_ref, sf_ref, st, mask_scr, qdec_scr, kdec_scr, cdec_scr = refs[8 if zero_init else 9:]
    c = pl.program_id(1)
    cs = RET_CHUNK
    pw = 2 * HEAD_DIM
    npair = N_HEADS // 2
    chains = [(d, p) for d in range(2) for p in range(npair)]

    @pl.when(c == 0)
    def _():
        _load_states(st, s0_ref)

    @pl.when((pl.program_id(0) == 0) & (c == 0))
    def _():
        row = lax.broadcasted_iota(jnp.int32, (cs, 2 * cs), 0)
        col = lax.broadcasted_iota(jnp.int32, (cs, 2 * cs), 1) % cs
        tok = lax.broadcasted_iota(jnp.int32, (cs, pw), 0)
        dist = [row - col, col - row]
        tq = [tok.astype(F32), (cs - 1 - tok).astype(F32)]
        for i, (d, p) in enumerate(chains):
            lgm = _log_sigmoid(dlm_ref[d, p])[0:1, :]
            lgq = _log_sigmoid(dlq_ref[d, p])
            mask_scr[i] = jnp.where(dist[d] >= 0, jnp.exp(jnp.maximum(dist[d], 0).astype(F32) * lgm), 0.0)
            qdec_scr[i] = jnp.exp((tq[d] + 1.0) * lgq[0:1, :])
            kdec_scr[i] = jnp.exp((cs - 1.0 - tq[d]) * lgq[0:1, :])
            cdec_scr[i] = jnp.exp(cs * lgq)

    brow = lax.broadcasted_iota(jnp.int32, (pw, pw), 0) // HEAD_DIM
    bcol = lax.broadcasted_iota(jnp.int32, (pw, pw), 1) // HEAD_DIM
    same_head = brow == bcol
    refs = [(qf_ref, kf_ref, vf_ref, of_ref), (qb_ref, kb_ref, vb_ref, ob_ref)]
    nsub = RET_CHUNKS_PER_STEP
    work = [(t, sub) for sub in range(nsub) for t in range(len(chains))]
    sl_of = lambda t: slice(chains[t][1] * pw, (chains[t][1] + 1) * pw)
    rows_of = lambda sub: pl.ds(sub * cs, cs)
    q = {(t, sub): refs[chains[t][0]][0][rows_of(sub), sl_of(t)] for t, sub in work}
    k = {(t, sub): refs[chains[t][0]][1][rows_of(sub), sl_of(t)] * ATTN_SCALE for t, sub in work}
    v = {(t, sub): refs[chains[t][0]][2][rows_of(sub), sl_of(t)] for t, sub in work}
    s = {w: _bdot_nt(q[w], _pair_block_diag(k[w])) * mask_scr[w[0]] for w in work}
    inner = {w: _bdot(s[w], _pair_block_diag(v[w])) for w in work}
    kv = {w: _bdot_tn(k[w] * kdec_scr[w[0]], v[w]) for w in work}
    for step in range(nsub):
        now = [(t, (step if chains[t][0] == 0 else nsub - 1 - step)) for t in range(len(chains))]
        state = {w: st[chains[w[0]]] for w in now}
        cross = {w: _bdot(q[w], state[w]) * qdec_scr[w[0]] for w in now}
        for w in now:
            t, sub = w
            d, p = chains[t]
            refs[d][3][rows_of(sub), sl_of(t)] = inner[w] + cross[w]
            st[d, p] = state[w] * cdec_scr[t][0:1, :] + jnp.where(same_head, kv[w], 0.0)

    @pl.when(c == pl.num_programs(1) - 1)
    def _():
        _store_states(sf_ref, st)


def _retention_scan(decay_logit, q, k, v, state0, seq):
    n = q.shape[0]
    nb = n // seq
    rows = RET_CHUNK * RET_CHUNKS_PER_STEP
    nc = seq // rows
    npair, pw = N_HEADS // 2, 2 * HEAD_DIM
    dl = decay_logit.astype(F32)
    dl_m = jnp.broadcast_to(jnp.repeat(dl, RET_CHUNK, axis=1).reshape(2, npair, 1, 2 * RET_CHUNK),
                            (2, npair, 8, 2 * RET_CHUNK))
    dl_q = jnp.broadcast_to(jnp.repeat(dl, HEAD_DIM, axis=1).reshape(2, npair, 1, pw), (2, npair, 8, pw))
    blk_f = pl.BlockSpec((rows, D_GROUP), lambda b, c: (b * nc + c, 0))
    blk_b = pl.BlockSpec((rows, D_GROUP), lambda b, c: (b * nc + nc - 1 - c, 0))
    st_in = ([] if state0 is None else
             [pl.BlockSpec((1, 2, npair, pw, pw), lambda b, c: (b, 0, 0, 0, 0))])
    st_out = pl.BlockSpec((1, 2, N_HEADS, HEAD_DIM, HEAD_DIM), lambda b, c: (b, 0, 0, 0, 0))
    return pl.pallas_call(
        functools.partial(_ret_kernel, zero_init=state0 is None),
        grid=(nb, nc),
        in_specs=[pl.BlockSpec(dl_m.shape, lambda b, c: (0, 0, 0, 0)),
                  pl.BlockSpec(dl_q.shape, lambda b, c: (0, 0, 0, 0)),
                  blk_f, blk_f, blk_f, blk_b, blk_b, blk_b] + st_in,
        out_specs=[blk_f, blk_b, st_out],
        out_shape=[jax.ShapeDtypeStruct((n, D_GROUP), F32), jax.ShapeDtypeStruct((n, D_GROUP), F32),
                   jax.ShapeDtypeStruct((nb, 2, N_HEADS, HEAD_DIM, HEAD_DIM), F32)],
        scratch_shapes=[pltpu.VMEM((2, npair, pw, pw), F32),
                        pltpu.VMEM((2 * npair, RET_CHUNK, 2 * RET_CHUNK), F32),
                        pltpu.VMEM((2 * npair, RET_CHUNK, pw), F32),
                        pltpu.VMEM((2 * npair, RET_CHUNK, pw), F32),
                        pltpu.VMEM((2 * npair, 8, pw), F32)],
        compiler_params=_cparams("arbitrary", "arbitrary"),
        name="retention_scan",
    )(dl_m, dl_q, q, k, v, q, k, v, *([] if state0 is None else [_states_to_pairs(state0)]))


def _rwkv_kernel(*refs, zero_init):
    (xrf_ref, xkf_ref, xvf_ref, lof_ref, xrb_ref, xkb_ref, xvb_ref, lob_ref,
     w0_ref, wup_ref, a0_ref, aup_ref, kk_ref, ka_ref) = refs[:14]
    s0_ref = None if zero_init else refs[14]
    yf_ref, yb_ref, sf_ref, st = refs[14 if zero_init else 15:]
    c = pl.program_id(1)
    cs = RWKV_CHUNK
    nsub = RWKV_CHUNKS_PER_STEP
    npair = N_HEADS // 2

    @pl.when(c == 0)
    def _():
        _load_states(st, s0_ref)

    row = lax.broadcasted_iota(jnp.int32, (cs, 3 * cs), 0)
    col = lax.broadcasted_iota(jnp.int32, (cs, 3 * cs), 1) % cs
    pw = 2 * HEAD_DIM
    prow = lax.broadcasted_iota(jnp.int32, (cs, pw), 0)
    pcol = lax.broadcasted_iota(jnp.int32, (cs, pw), 1) % HEAD_DIM
    eye = jnp.where(pcol == prow, 1.0, 0.0).astype(F32)
    sub_diag = (prow // RWKV_SUB) == (pcol // RWKV_SUB)
    brow = lax.broadcasted_iota(jnp.int32, (pw, pw), 0) // HEAD_DIM
    bcol = lax.broadcasted_iota(jnp.int32, (pw, pw), 1) // HEAD_DIM
    same_head = brow == bcol

    bd = _pair_block_diag

    def direction(d, sub, xr_ref, xk_ref, xv_ref, lo_ref):
        rows = pl.ds(sub * cs, cs)
        tri3 = jnp.where((row >= col) if d == 0 else (col >= row), 1.0, 0.0).astype(BF16)
        dist = (prow - pcol) if d == 0 else (pcol - prow)
        xk = xk_ref[rows, :]
        lo = lo_ref[rows, :]
        w_log = -math.exp(-0.5) * _sigmoid(w0_ref[d] + _bdot(jnp.tanh(lo[:, 0:D_LORA]), wup_ref[d]))
        a_all = _sigmoid(a0_ref[d] + _bdot(lo[:, D_LORA:2 * D_LORA], aup_ref[d]))
        w1 = w_log.astype(BF16)
        r1 = w_log - w1.astype(F32)
        w2 = r1.astype(BF16)
        w3 = (r1 - w2.astype(F32)).astype(BF16)
        cum = jnp.dot(tri3, jnp.concatenate([w1, w2, w3], axis=0), preferred_element_type=F32)
        cum_end = cum[cs - 1:cs, :] if d == 0 else cum[0:1, :]
        return dict(earlier=dist > 0, upto=dist >= 0, xr=xr_ref[rows, :], xv=xv_ref[rows, :], a_all=a_all,
                    e_incl=jnp.exp(cum), e_excl=jnp.exp(cum - w_log), e_neg=jnp.exp(-cum),
                    e_end=jnp.exp(cum_end - cum), w_end=jnp.exp(cum_end), kk_all=xk * kk_ref[...],
                    keff_all=xk * (1.0 + (a_all - 1.0) * ka_ref[...]))

    in_refs = [(xrf_ref, xkf_ref, xvf_ref, lof_ref), (xrb_ref, xkb_ref, xvb_ref, lob_ref)]
    dirs = {(d, sub): direction(d, sub, *in_refs[d]) for d in range(2) for sub in range(nsub)}
    y_refs = [yf_ref, yb_ref]

    chains = [(d, sub, p) for d in range(2) for sub in range(nsub) for p in range(npair)]
    pairs = range(len(chains))
    sls = [slice(p * pw, (p + 1) * pw) for _, _, p in chains]
    pick = lambda name: [dirs[d, sub][name][:, sls[i]] for i, (d, sub, _) in enumerate(chains)]
    earlier = [dirs[d, sub]["earlier"] for d, sub, _ in chains]
    upto = [dirs[d, sub]["upto"] for d, sub, _ in chains]
    cat = lambda x, y: jnp.concatenate([x, y], axis=0)

    head_sum = _pair_head_sum

    e_incl, e_excl, e_neg, e_end, w_end = (pick(k) for k in ("e_incl", "e_excl", "e_neg", "e_end", "w_end"))
    kk = [x * lax.rsqrt(head_sum(x * x) + 1e-12) for x in pick("kk_all")]
    b = [kk[p] * a for p, a in zip(pairs, pick("a_all"))]
    keff = pick("keff_all")
    v = pick("xv")
    a_t = [-kk[p] * e_excl[p] for p in pairs]
    r_t = [xr * e_incl[p] for p, xr in zip(pairs, pick("xr"))]
    ar = [cat(a_t[p], r_t[p]).astype(BF16) for p in pairs]
    g = [_bdot_nt(ar[p], cat(bd(b[p] * e_neg[p]), bd(keff[p] * e_neg[p]))) for p in pairs]
    n_mat = [jnp.where(earlier[p], g[p][:cs, :pw], 0.0) for p in pairs]
    m_rb = [jnp.where(upto[p], g[p][cs:, :pw], 0.0) for p in pairs]
    a_ak = [jnp.where(earlier[p], g[p][:cs, pw:], 0.0) for p in pairs]
    m_rk = [jnp.where(upto[p], g[p][cs:, pw:], 0.0) for p in pairs]
    bd_s = lambda xs: (bd(xs[0]), bd(xs[1]))
    n_d = [jnp.where(sub_diag, n, 0.0) for n in n_mat]
    n_o = [jnp.where(sub_diag, 0.0, n) for n in n_mat]
    n_s = [_split2(n) for n in n_d]
    x = [eye + n for n in n_d]
    pk = [_mm3s(ns, bd_s(ns)) for ns in n_s]
    for _ in range(RWKV_SUB_DOUBLINGS - 1):
        xp = [_mm3s(_split2(cat(x[p], pk[p])), bd_s(_split2(pk[p]))) for p in pairs]
        x = [x[p] + xp[p][:cs] for p in pairs]
        pk = [r[cs:] for r in xp]
    t_d = [x[p] + _mm3s(_split2(x[p]), bd_s(_split2(pk[p]))) for p in pairs]
    lcat = lambda x, y: jnp.concatenate([x, y], axis=1)
    m_o = [_bdot(t_d[p], bd(n_o[p])) for p in pairs]
    mz = [_bdot(m_o[p], lcat(bd(t_d[p]), bd(m_o[p]))) for p in pairs]
    z = [t_d[p] + mz[p][:, :pw] for p in pairs]
    t_inv = [(z[p] + _bdot(mz[p][:, pw:], bd(z[p]))).astype(BF16) for p in pairs]
    av = [_bdot(cat(a_ak[p], m_rk[p]), bd(v[p])) for p in pairs]
    tau = [_bdot(t_inv[p], lcat(bd(a_t[p]), bd(av[p][:cs]))) for p in pairs]
    bk_end = [cat(b[p] * e_end[p], keff[p] * e_end[p]) for p in pairs]
    for step in range(nsub):
        now = [i for i, (d, sub, _) in enumerate(chains) if sub == (step if d == 0 else nsub - 1 - step)]
        s_prev = {i: st[chains[i][0], chains[i][2]] for i in now}
        hs = {i: _bdot_nt(cat(tau[i][:, :pw], r_t[i]), s_prev[i]) for i in now}
        u = {i: hs[i][:cs] + tau[i][:, pw:] for i in now}
        y = {i: hs[i][cs:] + _bdot(m_rb[i], bd(u[i])) + av[i][cs:] for i in now}
        for i in now:
            d, sub, p = chains[i]
            y_refs[d][pl.ds(sub * cs, cs), sls[i]] = y[i]
            st[d, p] = s_prev[i] * w_end[i] + jnp.where(same_head, _bdot_tn(cat(u[i], v[i]), bk_end[i]), 0.0)

    @pl.when(c == pl.num_programs(1) - 1)
    def _():
        _store_states(sf_ref, st)


def _rwkv_scan(xr, xk, xv, lora, w0, w_up, a0, a_up, k_k, k_a, state0, seq):
    n = xr.shape[0]
    nb = n // seq
    blk = RWKV_CHUNK * RWKV_CHUNKS_PER_STEP
    nc = seq // blk

    fwd_idx = lambda b, c: (b * nc + c, 0)
    bwd_idx = lambda b, c: (b * nc + nc - 1 - c, 0)
    blk_f = pl.BlockSpec((blk, D_GROUP), fwd_idx)
    blk_b = pl.BlockSpec((blk, D_GROUP), bwd_idx)
    lblk_f = pl.BlockSpec((blk, lora.shape[1]), fwd_idx)
    lblk_b = pl.BlockSpec((blk, lora.shape[1]), bwd_idx)
    dvec = pl.BlockSpec((2, 1, D_GROUP), lambda b, c: (0, 0, 0))
    dmat = pl.BlockSpec((2, D_LORA, D_GROUP), lambda b, c: (0, 0, 0))
    vec = pl.BlockSpec((1, D_GROUP), lambda b, c: (0, 0))
    npair, pw = N_HEADS // 2, 2 * HEAD_DIM
    st_in = ([] if state0 is None else
             [pl.BlockSpec((1, 2, npair, pw, pw), lambda b, c: (b, 0, 0, 0, 0))])
    st_out = pl.BlockSpec((1, 2, N_HEADS, HEAD_DIM, HEAD_DIM), lambda b, c: (b, 0, 0, 0, 0))
    return pl.pallas_call(
        functools.partial(_rwkv_kernel, zero_init=state0 is None),
        grid=(nb, nc),
        in_specs=[blk_f, blk_f, blk_f, lblk_f, blk_b, blk_b, blk_b, lblk_b,
                  dvec, dmat, dvec, dmat, vec, vec] + st_in,
        out_specs=[blk_f, blk_b, st_out],
        out_shape=[jax.ShapeDtypeStruct((n, D_GROUP), F32), jax.ShapeDtypeStruct((n, D_GROUP), F32),
                   jax.ShapeDtypeStruct((nb, 2, N_HEADS, HEAD_DIM, HEAD_DIM), F32)],
        scratch_shapes=[pltpu.VMEM((2, npair, pw, pw), F32)],
        compiler_params=_cparams("arbitrary", "arbitrary"),
        name="rwkv7_scan",
    )(xr, xk, xv, lora, xr, xk, xv, lora, w0.reshape(2, 1, D_GROUP), w_up, a0.reshape(2, 1, D_GROUP), a_up,
      k_k.reshape(1, D_GROUP), k_a.reshape(1, D_GROUP), *([] if state0 is None else [_states_to_pairs(state0)]))


ATTN_WIDTHS = (D_GROUP,) * 6
REC_WIDTHS = (D_GROUP,) * 7 + (2 * D_LORA + D_LORA_G,)


def kernel(x_prompt, x_sample, cache_na_k, cache_na_v, cache_diff_k, cache_diff_v, state_ret, state_rwkv, c, c_ctx, norm_mix_g, norm_ffn_g, norm_final_g, w_ada, b_ada, w_in_attn, w_out_attn, na_rpb, diff_lq1, diff_lk1, diff_lq2, diff_lk2, w_in_rec, w_out_rec, ret_decay_logit, rw_w0, rw_w_up, rw_a0, rw_a_up, rw_g_up, rw_k_k, rw_k_a, rw_r_k, rw_ln_g, rw_ln_b, w_ffn_in, w_ffn_out):
    bp, seq, d = x_prompt.shape
    bl, n_lat, _ = x_sample.shape
    depth = w_ada.shape[0]
    assert n_lat == GRID_ROWS * GRID_W and 1 + bl <= 8
    assert seq % (RWKV_CHUNK * RWKV_CHUNKS_PER_STEP) == 0 and seq % (RET_CHUNK * RET_CHUNKS_PER_STEP) == 0
    assert n_lat % (RWKV_CHUNK * RWKV_CHUNKS_PER_STEP) == 0 and n_lat % (RET_CHUNK * RET_CHUNKS_PER_STEP) == 0
    assert (bp * seq) % TAIL_TM == 0 and n_lat % TAIL_TM == 0 and n_lat % LAT_DIFF_TQ == 0
    assert w_in_attn.shape[2] == sum(ATTN_WIDTHS) and w_in_rec.shape[2] == sum(REC_WIDTHS)
    assert w_ffn_in.shape[2] == 2 * D_FF and w_ada.shape[2] % ADA_TN == 0
    ctx = x_prompt.reshape(bp * seq, d)
    lat = x_sample.reshape(bl * n_lat, d)

    cond8 = jnp.zeros((8, d), F32).at[0].set(c_ctx).at[1:1 + bl].set(c)
    mods = _ada_modulation(cond8, w_ada, b_ada).reshape(depth, 8, 6, d)
    w_ffn_in_bf16 = w_ffn_in.astype(BF16)
    w_ffn_out_bf16 = w_ffn_out.astype(BF16)

    outs = {}
    for layer in range(depth):
        m_ctx = [mods[layer, 0:1, j].reshape(1, 1, d) for j in range(6)]
        m_lat = [mods[layer, 1:1 + bl, j].reshape(bl, 1, d) for j in range(6)]
        if layer % 2 == 0:
            i = layer // 2
            lam_init = 0.8 - 0.6 * math.exp(-0.3 * layer)
            lam_params = jnp.stack([diff_lq1[i], diff_lk1[i], diff_lq2[i], diff_lk2[i]]).astype(F32)
            w_in = w_in_attn[i].astype(BF16)
            w_out = w_out_attn[i].astype(BF16)
            res = _ctx_attention(ctx, norm_mix_g[layer], m_ctx[0], m_ctx[1], w_in, lam_params, seq, lam_init)
            mix_ctx = res[:2]
            outs["na_k"], outs["na_v"], outs["df_k"], outs["df_v"] = res[2:]
            plat = _norm_mod_matmul(lat, norm_mix_g[layer], m_lat[0], m_lat[1], w_in, ATTN_WIDTHS, n_lat, PROJ_TM)
            past = cache_na_k.shape[2]
            tt = _na_bias_table(na_rpb[i])
            o_na = _lat_na_attention(plat[0], plat[1], plat[2],
                                     cache_na_k[:, i].reshape(bl, past, D_GROUP),
                                     cache_na_v[:, i].reshape(bl, past, D_GROUP), tt, n_lat)
            o_df = _lat_diff_attention(lam_params, plat[3], plat[4], plat[5],
                                       cache_diff_k[:, i].reshape(bl, past, D_GROUP),
                                       cache_diff_v[:, i].reshape(bl, past, D_GROUP), n_lat, lam_init)
            mix_lat = (o_na, o_df)
        else:
            j = layer // 2
            w_in = w_in_rec[j].astype(BF16)
            w_out = w_out_rec[j].astype(BF16)
            pc = _norm_mod_matmul(ctx, norm_mix_g[layer], m_ctx[0], m_ctx[1], w_in, REC_WIDTHS, bp * seq, PROJ_TM)
            plat = _norm_mod_matmul(lat, norm_mix_g[layer], m_lat[0], m_lat[1], w_in, REC_WIDTHS, n_lat, PROJ_TM)
            mixes = []
            for is_ctx, p, s_ret0, s_rw0, sq in ((True, pc, None, None, seq),
                                                 (False, plat, state_ret[:, j], state_rwkv[:, j], n_lat)):
                rq, rk, rv, rg, wr, wk, wv, lora = p
                o_f, o_b, s_ret = _retention_scan(ret_decay_logit[j], rq, rk, rv, s_ret0, sq)
                y_f, y_b, s_rw = _rwkv_scan(wr, wk, wv, lora, rw_w0[j], rw_w_up[j].astype(BF16), rw_a0[j],
                                            rw_a_up[j].astype(BF16), rw_k_k[j], rw_k_a[j], s_rw0, sq)
                mixes.append((o_f, o_b, rg, y_f, y_b, wr, wk, wv, lora, rw_r_k[j].reshape(1, D_GROUP),
                              rw_ln_g[j].reshape(1, D_GROUP), rw_ln_b[j].reshape(1, D_GROUP),
                              rw_g_up[j].astype(BF16)))
                if is_ctx:
                    outs["ret"], outs["rwkv"] = s_ret, s_rw
            mix_ctx, mix_lat = mixes
        final = layer == depth - 1
        ctx = _layer_tail(ctx, mix_ctx, w_out, m_ctx[2], norm_ffn_g[layer], m_ctx[3], m_ctx[4], m_ctx[5],
                          w_ffn_in_bf16, w_ffn_out_bf16, layer, norm_final_g, bp * seq, final)
        lat = _layer_tail(lat, mix_lat, w_out, m_lat[2], norm_ffn_g[layer], m_lat[3], m_lat[4], m_lat[5],
                          w_ffn_in_bf16, w_ffn_out_bf16, layer, norm_final_g, n_lat, final)

    y_prompt = ctx.reshape(bp, seq, d)
    y_sample = lat.reshape(bl, n_lat, d)
    new_cache_na_k = outs["na_k"].reshape(bp, 1, seq, N_HEADS, HEAD_DIM)
    new_cache_na_v = outs["na_v"].reshape(bp, 1, seq, N_HEADS, HEAD_DIM)
    new_cache_diff_k = outs["df_k"].reshape(bp, 1, seq, N_HEADS // 2, 2, HEAD_DIM)
    new_cache_diff_v = outs["df_v"].reshape(bp, 1, seq, N_HEADS // 2, 2 * HEAD_DIM)
    new_state_ret = outs["ret"].reshape(bp, 1, 2, N_HEADS, HEAD_DIM, HEAD_DIM)
    new_state_rwkv = outs["rwkv"].reshape(bp, 1, 2, N_HEADS, HEAD_DIM, HEAD_DIM)
    return (y_prompt, y_sample, new_cache_na_k, new_cache_na_v, new_cache_diff_k, new_cache_diff_v,
            new_state_ret, new_state_rwkv)
```

```python
import functools
import math

import jax
import jax.numpy as jnp
from jax import lax
from jax.experimental import pallas as pl
from jax.experimental.pallas import tpu as pltpu

F32 = jnp.float32
BF16 = jnp.bfloat16

HEAD_DIM = 64
N_HEADS = 8
D_GROUP = 512
GRID_W = 64
GRID_ROWS = 32
WIN_R = 8
WIN_W = 16
RPB_R = 15
RPB_C = 31
D_FF = 2816
FFN_CHUNK = D_FF // 2
D_LORA = 64
D_LORA_G = 128
ROPE_BASE = 10000.0
RMS_EPS = 1e-6
RWKV_GN_EPS = 64e-5
NEG_INF = -1e30
ATTN_SCALE = HEAD_DIM ** -0.5
RET_CHUNK = 128
RET_CHUNKS_PER_STEP = 2
RWKV_CHUNK = 64
RWKV_CHUNKS_PER_STEP = 4
RWKV_SUB = 16
RWKV_SUB_DOUBLINGS = RWKV_SUB.bit_length() - 2
assert RWKV_CHUNK == 4 * RWKV_SUB
V7X_VMEM_BYTES = 64 * 1024 * 1024
VMEM_LIMIT = V7X_VMEM_BYTES - 8 * 1024 * 1024
ADA_TN = 1536
TAIL_TM = 512
TAIL_TM_REC = 256
PROJ_TM = 512
CTX_SEQS_PER_STEP = 1
LAT_DIFF_TQ = 256


def _cparams(*sem):
    return pltpu.CompilerParams(dimension_semantics=sem, vmem_limit_bytes=VMEM_LIMIT)


def _sigmoid(x):
    return 1.0 / (1.0 + jnp.exp(-x))


def _silu(x):
    return x * _sigmoid(x)


def _rms(x, eps=RMS_EPS):
    return x * lax.rsqrt(jnp.mean(x * x, axis=-1, keepdims=True) + eps)


def _bdot(a, b):
    return jnp.dot(a.astype(BF16), b.astype(BF16), preferred_element_type=F32)


def _bdot_nt(a, b):
    return lax.dot_general(a.astype(BF16), b.astype(BF16), (((1,), (1,)), ((), ())),
                           preferred_element_type=F32)


def _bdot_tn(a, b):
    return lax.dot_general(a.astype(BF16), b.astype(BF16), (((0,), (0,)), ((), ())),
                           preferred_element_type=F32)


def _split2(x):
    hi = x.astype(BF16)
    lo = (x - hi.astype(F32)).astype(BF16)
    return hi, lo


def _mm3s(a_split, b_split):
    ah, al = a_split
    bh, bl = b_split
    n = bh.shape[1]
    rhs = jnp.concatenate([jnp.concatenate([bh, bl], axis=1),
                           jnp.concatenate([bh, jnp.zeros_like(bl)], axis=1)], axis=0)
    out = jnp.dot(jnp.concatenate([ah, al], axis=1), rhs, preferred_element_type=F32)
    return out[:, :n] + out[:, n:]


def _diff_lambda(lp, lam_init):
    s1 = jnp.sum(lp[0:1, :] * lp[1:2, :], axis=-1, keepdims=True)
    s2 = jnp.sum(lp[2:3, :] * lp[3:4, :], axis=-1, keepdims=True)
    return jnp.exp(s1) - jnp.exp(s2) + lam_init


def _ada_kernel(c_ref, w_ref, b_ref, o_ref):
    s = _silu(c_ref[...])
    o_ref[0] = _bdot(s, w_ref[0]) + b_ref[0]


def _ada_modulation(cond8, w_ada, b_ada):
    depth, d, n = w_ada.shape
    tn = ADA_TN
    return pl.pallas_call(
        _ada_kernel,
        grid=(depth, n // tn),
        in_specs=[pl.BlockSpec((8, d), lambda l, j: (0, 0)),
                  pl.BlockSpec((1, d, tn), lambda l, j: (l, 0, j)),
                  pl.BlockSpec((1, 1, tn), lambda l, j: (l, 0, j))],
        out_specs=pl.BlockSpec((1, 8, tn), lambda l, j: (l, 0, j)),
        out_shape=jax.ShapeDtypeStruct((depth, 8, n), F32),
        compiler_params=_cparams("arbitrary", "arbitrary"),
        name="ada_modulation",
    )(cond8, w_ada, b_ada.reshape(depth, 1, n))


def _norm_mod(x_ref, g_ref, sh_ref, sc_ref):
    h = _rms(x_ref[...]) * g_ref[...]
    return (h * (1.0 + sc_ref[0]) + sh_ref[0]).astype(BF16)


def _nmm_kernel(x_ref, g_ref, sh_ref, sc_ref, w_ref, *o_refs, widths):
    hb = _norm_mod(x_ref, g_ref, sh_ref, sc_ref)
    off = 0
    for o_ref, w in zip(o_refs, widths):
        o_ref[...] = jnp.dot(hb, w_ref[:, off:off + w], preferred_element_type=F32)
        off += w


def _norm_mod_matmul(x, gain, shift, scale, w_bf16, widths, rows_per_mod, tm):
    n, d = x.shape
    tiles_per_mod = rows_per_mod // tm
    mod_spec = pl.BlockSpec((1, 1, d), lambda i: (i // tiles_per_mod, 0, 0))
    return pl.pallas_call(
        functools.partial(_nmm_kernel, widths=widths),
        grid=(n // tm,),
        in_specs=[pl.BlockSpec((tm, d), lambda i: (i, 0)),
                  pl.BlockSpec((1, d), lambda i: (0, 0)),
                  mod_spec, mod_spec,
                  pl.BlockSpec(w_bf16.shape, lambda i: (0, 0), pipeline_mode=pl.Buffered(1))],
        out_specs=[pl.BlockSpec((tm, w), lambda i: (i, 0)) for w in widths],
        out_shape=[jax.ShapeDtypeStruct((n, w), F32) for w in widths],
        compiler_params=_cparams("arbitrary"),
        name="norm_mod_matmul",
    )(x, gain.reshape(1, d), shift, scale, w_bf16)


PAIR_W = 2 * HEAD_DIM


def _pair_tiles(x):
    return [x[:, j * PAIR_W:(j + 1) * PAIR_W] for j in range(x.shape[1] // PAIR_W)]


def _split_pair(tile):
    left = lax.broadcasted_iota(jnp.int32, tile.shape, 1) < HEAD_DIM
    zero = jnp.zeros_like(tile)
    return jnp.where(left, tile, zero), jnp.where(left, zero, tile)


def _pair_head_sum(tile):
    left = lax.broadcasted_iota(jnp.int32, tile.shape, 1) < HEAD_DIM
    zero = jnp.zeros_like(tile)
    s_left = jnp.sum(jnp.where(left, tile, zero), axis=-1, keepdims=True)
    s_right = jnp.sum(jnp.where(left, zero, tile), axis=-1, keepdims=True)
    return jnp.where(left, s_left, s_right)


def _with_ones(v):
    return jnp.concatenate([v, jnp.ones_like(v)], axis=1)


def _exp_scores(s):
    return jnp.exp(s - jnp.max(s, axis=-1, keepdims=True)).astype(BF16)


def _merge_pair(out_even, out_odd):
    left = lax.broadcasted_iota(jnp.int32, (out_even.shape[0], PAIR_W), 1) < HEAD_DIM
    return jnp.where(left, out_even[:, :PAIR_W] * (1.0 / out_even[:, PAIR_W:]),
                     out_odd[:, :PAIR_W] * (1.0 / out_odd[:, PAIR_W:]))


def _diff_combine(out1, out2, lam, lam_init):
    o = out1[:, :PAIR_W] * (1.0 / out1[:, PAIR_W:]) - out2[:, :PAIR_W] * (lam / out2[:, PAIR_W:])
    return _rms(o) * (1.0 - lam_init)


CTX_CACHE_SHAPES = ((N_HEADS, HEAD_DIM), (N_HEADS, HEAD_DIM),
                    (N_HEADS // 2, 2, HEAD_DIM), (N_HEADS // 2, 2 * HEAD_DIM))


def _ctx_attn_kernel(x_ref, g_ref, sh_ref, sc_ref, w_ref, lp_ref, on_ref, od_ref, *cache_refs, lam_init, seq):
    hb = _norm_mod(x_ref, g_ref, sh_ref, sc_ref)
    proj = [jnp.dot(hb, w_ref[:, i * D_GROUP:(i + 1) * D_GROUP], preferred_element_type=F32)
            for i in range(len(ATTN_WIDTHS))]
    for c_ref, val in zip(cache_refs, (proj[1], proj[2], proj[4], proj[5])):
        c_ref[...] = val.reshape(c_ref.shape)
    tiles = range(D_GROUP // PAIR_W)
    lam = _diff_lambda(lp_ref[...], lam_init)
    for s in range(hb.shape[0] // seq):
        rows = slice(s * seq, (s + 1) * seq)
        qn, kn, vn, qd, kd, vd = (p[rows] for p in proj)
        q = [_split_pair(t) for t in _pair_tiles((qn * ATTN_SCALE).astype(BF16))
             + _pair_tiles((qd * ATTN_SCALE).astype(BF16))]
        k = _pair_tiles(kn.astype(BF16)) + _pair_tiles(kd.astype(BF16))
        v = [_with_ones(t) for t in _pair_tiles(vn.astype(BF16)) + _pair_tiles(vd.astype(BF16))]
        e = [[_exp_scores(_bdot_nt(q[j][c], k[j])) for c in range(2)] for j in range(2 * len(tiles))]
        out = [[jnp.dot(e[j][c], v[j], preferred_element_type=F32) for c in range(2)]
               for j in range(2 * len(tiles))]
        for j in tiles:
            sl = slice(j * PAIR_W, (j + 1) * PAIR_W)
            on_ref[rows, sl] = _merge_pair(out[j][0], out[j][1])
            od_ref[rows, sl] = _diff_combine(out[len(tiles) + j][0], out[len(tiles) + j][1], lam, lam_init)


def _ctx_attention(x, gain, shift, scale, w_bf16, lam_params, seq, lam_init):
    n, d = x.shape
    tm = seq * CTX_SEQS_PER_STEP
    blk = pl.BlockSpec((tm, D_GROUP), lambda b: (b, 0))
    mod_spec = pl.BlockSpec((1, 1, d), lambda b: (0, 0, 0))
    cache_specs = [pl.BlockSpec((tm,) + dims, lambda b, nd=len(dims): (b,) + (0,) * nd)
                   for dims in CTX_CACHE_SHAPES]
    return pl.pallas_call(
        functools.partial(_ctx_attn_kernel, lam_init=lam_init, seq=seq),
        grid=(n // tm,),
        in_specs=[pl.BlockSpec((tm, d), lambda b: (b, 0)),
                  pl.BlockSpec((1, d), lambda b: (0, 0)),
                  mod_spec, mod_spec,
                  pl.BlockSpec(w_bf16.shape, lambda b: (0, 0), pipeline_mode=pl.Buffered(1)),
                  pl.BlockSpec(lam_params.shape, lambda b: (0, 0))],
        out_specs=[blk, blk] + cache_specs,
        out_shape=([jax.ShapeDtypeStruct((n, D_GROUP), F32)] * 2
                   + [jax.ShapeDtypeStruct((n,) + dims, F32) for dims in CTX_CACHE_SHAPES]),
        compiler_params=_cparams("arbitrary"),
        name="ctx_attention",
    )(x, gain.reshape(1, d), shift, scale, w_bf16, lam_params)


def _rope_tables(n):
    quarter = HEAD_DIM // 4
    pos = jnp.arange(n)
    inv_freq = ROPE_BASE ** (-jnp.arange(quarter, dtype=F32) / quarter)
    lane = jnp.arange(HEAD_DIM)
    p = jnp.where(lane[None, :] < HEAD_DIM // 2, (pos // GRID_W)[:, None], (pos % GRID_W)[:, None]).astype(F32)
    ang = p * inv_freq[lane % quarter][None, :]
    sign = jnp.where((lane % (2 * quarter)) < quarter, -1.0, 1.0).astype(F32)
    cos = jnp.tile(jnp.cos(ang), (1, 2))
    sin = jnp.tile(jnp.sin(ang) * sign[None, :], (1, 2))
    return cos, sin


def _rope(x, cos, sin_signed):
    quarter = HEAD_DIM // 4
    lane = lax.broadcasted_iota(jnp.int32, cos.shape, 1)
    first = (lane % (2 * quarter)) < quarter
    tiles = []
    for t in _pair_tiles(x):
        partner = jnp.where(first, pltpu.roll(t, PAIR_W - quarter, 1), pltpu.roll(t, quarter, 1))
        tiles.append(t * cos + partner * sin_signed)
    return jnp.concatenate(tiles, axis=1)


def _lat_diff_kernel(lp_ref, q_ref, k_ref, v_ref, ck_ref, cv_ref, cosq_ref, sinq_ref, cosk_ref, sink_ref,
                     o_ref, kall, vall, *, lam_init, n_lat):
    @pl.when(pl.program_id(1) == 0)
    def _():
        kall[0:n_lat, :] = _rope(k_ref[...], cosk_ref[...], sink_ref[...]).astype(BF16)
        kall[n_lat:, :] = ck_ref[...].astype(BF16)
        for j, (vt, ct) in enumerate(zip(_pair_tiles(v_ref[...].astype(BF16)),
                                         _pair_tiles(cv_ref[...].astype(BF16)))):
            vall[j, 0:n_lat, :] = _with_ones(vt)
            vall[j, n_lat:, :] = _with_ones(ct)

    lam = _diff_lambda(lp_ref[...], lam_init)
    q = (_rope(q_ref[...], cosq_ref[...], sinq_ref[...]) * ATTN_SCALE).astype(BF16)
    qs = [_split_pair(qt) for qt in _pair_tiles(q)]
    sls = [slice(j * PAIR_W, (j + 1) * PAIR_W) for j in range(len(qs))]
    for group in ((0, 1), (2, 3)):
        e = {j: [_exp_scores(_bdot_nt(qs[j][c], kall[:, sls[j]])) for c in range(2)] for j in group}
        out = {j: [jnp.dot(e[j][c], vall[j], preferred_element_type=F32) for c in range(2)] for j in group}
        for j in group:
            o_ref[:, sls[j]] = _diff_combine(out[j][0], out[j][1], lam, lam_init)


def _lat_diff_attention(lam_params, qd, kd, vd, cache_k, cache_v, n_lat, lam_init, tq=LAT_DIFF_TQ):
    n = qd.shape[0]
    nb = n // n_lat
    nq = n_lat // tq
    past = cache_k.shape[1]
    cos, sin = _rope_tables(n_lat)
    qblk = pl.BlockSpec((tq, D_GROUP), lambda b, i: (b * nq + i, 0))
    once = pl.Buffered(1)
    kvblk = pl.BlockSpec((n_lat, D_GROUP), lambda b, i: (b, 0), pipeline_mode=once)
    cblk = pl.BlockSpec((None, past, D_GROUP), lambda b, i: (b, 0, 0))
    return pl.pallas_call(
        functools.partial(_lat_diff_kernel, lam_init=lam_init, n_lat=n_lat),
        grid=(nb, nq),
        in_specs=[pl.BlockSpec(lam_params.shape, lambda b, i: (0, 0)),
                  qblk, kvblk, kvblk, cblk, cblk,
                  pl.BlockSpec((tq, PAIR_W), lambda b, i: (i, 0)),
                  pl.BlockSpec((tq, PAIR_W), lambda b, i: (i, 0)),
                  pl.BlockSpec((n_lat, PAIR_W), lambda b, i: (0, 0), pipeline_mode=once),
                  pl.BlockSpec((n_lat, PAIR_W), lambda b, i: (0, 0), pipeline_mode=once)],
        out_specs=qblk,
        out_shape=jax.ShapeDtypeStruct((n, D_GROUP), F32),
        scratch_shapes=[pltpu.VMEM((n_lat + past, D_GROUP), BF16),
                        pltpu.VMEM((D_GROUP // PAIR_W, n_lat + past, 2 * PAIR_W), BF16)],
        compiler_params=_cparams("arbitrary", "arbitrary"),
        name="lat_diff_attention",
    )(lam_params, qd, kd, vd, cache_k, cache_v, cos, sin, cos, sin)


NA_Q_ROWS = 4
NA_KEY_ROWS = WIN_R + NA_Q_ROWS
NA_TABLE = RPB_R + 1


def _na_bias_kernel(rpb_ref, tt_ref):
    h = pl.program_id(0)
    lane = lax.broadcasted_iota(jnp.int32, (GRID_W, 2 * GRID_W), 1)
    qc = lax.broadcasted_iota(jnp.int32, (GRID_W, 2 * GRID_W), 0)
    dc = jnp.clip(lane % GRID_W - qc + (WIN_W - 1), 0, RPB_C - 1)
    first = lane < GRID_W
    rows = []
    for dr in range(RPB_R):
        acc = jnp.zeros((GRID_W, 2 * GRID_W), F32)
        for d in range(RPB_C):
            acc = jnp.where(dc == d, rpb_ref[h * RPB_R + dr, d], acc)
        rows.append(acc)
    for k in range(NA_TABLE):
        tt_ref[0, k] = jnp.where(first, rows[min(max(k - 1, 0), RPB_R - 1)], rows[min(k, RPB_R - 1)])


def _na_bias_table(rpb):
    return pl.pallas_call(
        _na_bias_kernel,
        grid=(N_HEADS,),
        in_specs=[pl.BlockSpec(memory_space=pltpu.SMEM)],
        out_specs=pl.BlockSpec((1, NA_TABLE, GRID_W, 2 * GRID_W), lambda h: (h, 0, 0, 0)),
        out_shape=jax.ShapeDtypeStruct((N_HEADS, NA_TABLE, GRID_W, 2 * GRID_W), F32),
        compiler_params=_cparams("arbitrary"),
        name="na_bias_table",
    )(rpb.reshape(N_HEADS * RPB_R, RPB_C))


def _lat_na_kernel(q_ref, k_ref, v_ref, ck_ref, cv_ref, tt_ref, o_ref):
    step = pl.program_id(1)
    past = ck_ref.shape[0]
    nwin = NA_KEY_ROWS * GRID_W
    r_q = [NA_Q_ROWS * step + a for a in range(NA_Q_ROWS)]
    r_start = [jnp.clip(r - WIN_R // 2, 0, GRID_ROWS - WIN_R) for r in r_q]
    ws = jnp.minimum(r_start[0], GRID_ROWS - NA_KEY_ROWS)
    row0 = pl.multiple_of(ws * GRID_W, GRID_W)
    lane = lax.broadcasted_iota(jnp.int32, (GRID_W, past + nwin), 1)
    key_row = ws + (lane - past) // GRID_W
    kc = lane % GRID_W
    qc = lax.broadcasted_iota(jnp.int32, (GRID_W, past + nwin), 0)
    c_start = jnp.clip(qc - WIN_W // 2, 0, GRID_W - WIN_W)
    col_ok = (kc >= c_start) & (kc < c_start + WIN_W)
    visible = jnp.concatenate(
        [(lane < past) | (col_ok & (key_row >= rs) & (key_row < rs + WIN_R)) for rs in r_start], axis=0)
    no_bias = jnp.zeros((GRID_W, past), F32)

    def bias_rows(h, r):
        tiles = [tt_ref[h, jnp.clip(ws + 2 * j - r + WIN_R, 0, NA_TABLE - 1)] for j in range(NA_KEY_ROWS // 2)]
        return jnp.concatenate([no_bias] + tiles, axis=1)

    q = [_split_pair(t) for t in _pair_tiles((q_ref[...] * ATTN_SCALE).astype(BF16))]
    kcat = _pair_tiles(jnp.concatenate([ck_ref[...], k_ref[pl.ds(row0, nwin), :]], axis=0).astype(BF16))
    vcat = [_with_ones(t) for t in
            _pair_tiles(jnp.concatenate([cv_ref[...], v_ref[pl.ds(row0, nwin), :]], axis=0).astype(BF16))]
    bias = [jnp.concatenate([bias_rows(h, r) for r in r_q], axis=0) for h in range(N_HEADS)]
    e = [[_exp_scores(jnp.where(visible, _bdot_nt(q[j][c], kcat[j]) + bias[2 * j + c], NEG_INF))
          for c in range(2)] for j in range(len(q))]
    out = [[jnp.dot(e[j][c], vcat[j], preferred_element_type=F32) for c in range(2)] for j in range(len(q))]
    for j in range(len(q)):
        o_ref[:, j * PAIR_W:(j + 1) * PAIR_W] = _merge_pair(out[j][0], out[j][1])


def _lat_na_attention(qn, kn, vn, cache_k, cache_v, tt, n_lat):
    n = qn.shape[0]
    nb = n // n_lat
    past = cache_k.shape[1]
    steps = GRID_ROWS // NA_Q_ROWS
    qblk = pl.BlockSpec((NA_Q_ROWS * GRID_W, D_GROUP), lambda b, r: (b * steps + r, 0))
    kvblk = pl.BlockSpec((n_lat, D_GROUP), lambda b, r: (b, 0))
    cblk = pl.BlockSpec((None, past, D_GROUP), lambda b, r: (b, 0, 0))
    return pl.pallas_call(
        _lat_na_kernel,
        grid=(nb, steps),
        in_specs=[qblk, kvblk, kvblk, cblk, cblk,
                  pl.BlockSpec(tt.shape, lambda b, r: (0, 0, 0, 0))],
        out_specs=qblk,
        out_shape=jax.ShapeDtypeStruct((n, D_GROUP), F32),
        compiler_params=_cparams("arbitrary", "arbitrary"),
        name="lat_na_attention",
    )(qn, kn, vn, cache_k, cache_v, tt)


def _rec_mix(of_ref, ob_ref, rg_ref, yf_ref, yb_ref, xr_ref, xk_ref, xv_ref, lo_ref, rk_ref, lng_ref, lnb_ref,
             gup_ref):
    def head_sums(x):
        return jnp.concatenate([_pair_head_sum(t) for t in _pair_tiles(x)], axis=1)

    inv_d = 1.0 / HEAD_DIM
    o = of_ref[...] + ob_ref[...]
    o_ret = _silu(rg_ref[...]) * (o * lax.rsqrt(head_sums(o * o) * inv_d + RMS_EPS))
    y = yf_ref[...] + yb_ref[...]
    yc = y - head_sums(y) * inv_d
    var = head_sums(yc * yc) * inv_d
    yn = yc * lax.rsqrt(var + RWKV_GN_EPS) * lng_ref[...] + lnb_ref[...]
    bonus = head_sums(xr_ref[...] * rk_ref[...] * xk_ref[...]) * xv_ref[...]
    g_rw = _bdot(_sigmoid(lo_ref[:, 2 * D_LORA:]), gup_ref[...])
    return o_ret, (yn + bonus) * g_rw


N_REC_TOKEN_INPUTS = 9
N_REC_PARAMS = 4


def _tail_kernel(*refs, final, ff_chunk, recurrent):
    x_ref = refs[0]
    if not recurrent:
        n_mix = 2
        mix_a, mix_b = refs[1][...], refs[2][...]
    else:
        n_mix = 2 * N_REC_TOKEN_INPUTS + N_REC_PARAMS
        first_refs = refs[1:1 + N_REC_TOKEN_INPUTS]
        next_refs = refs[1 + N_REC_TOKEN_INPUTS:1 + 2 * N_REC_TOKEN_INPUTS]
        param_refs = refs[1 + 2 * N_REC_TOKEN_INPUTS:1 + n_mix]
        mixa_scr, mixb_scr = refs[-2:]
        refs = refs[:-2]

        @pl.when(pl.program_id(0) == 0)
        def _():
            a0, b0 = _rec_mix(*first_refs, *param_refs)
            mixa_scr[...] = a0
            mixb_scr[...] = b0

        mix_a, mix_b = mixa_scr[...], mixb_scr[...]
    wo_ref, gm_ref, g_ref, sh_ref, sc_ref, gf_ref, wi_ref, wf_ref, fg_ref, o_ref = refs[1 + n_mix:]
    if recurrent:
        a1, b1 = _rec_mix(*next_refs, *param_refs)
        mixa_scr[...] = a1
        mixb_scr[...] = b1
    mix = _bdot(mix_a, wo_ref[0:D_GROUP, :]) + _bdot(mix_b, wo_ref[D_GROUP:, :])
    x1 = x_ref[...] + gm_ref[0] * mix
    h = _rms(x1) * g_ref[...]
    hb = (h * (1.0 + sc_ref[0]) + sh_ref[0]).astype(BF16)
    acc = jnp.zeros_like(x1)
    for c0 in range(0, D_FF, ff_chunk):
        gate = jnp.dot(hb, wi_ref[:, c0:c0 + ff_chunk], preferred_element_type=F32)
        up = jnp.dot(hb, wi_ref[:, D_FF + c0:D_FF + c0 + ff_chunk], preferred_element_type=F32)
        acc = acc + jnp.dot((_silu(gate) * up).astype(BF16), wf_ref[c0:c0 + ff_chunk, :],
                            preferred_element_type=F32)
    x2 = x1 + gf_ref[0] * acc
    if final:
        x2 = _rms(x2) * fg_ref[...]
    o_ref[...] = x2


def _layer_tail(x, mix_inputs, w_out, gate_mix, gain, shift, scale, gate_ffn, w_ffn_in_all, w_ffn_out_all, layer,
                final_gain, rows_per_mod, final, ff_chunk=FFN_CHUNK):
    n, d = x.shape
    recurrent = len(mix_inputs) != 2
    tm = TAIL_TM_REC if recurrent else TAIL_TM
    tiles_per_mod = rows_per_mod // tm
    last = n // tm - 1
    mod_spec = pl.BlockSpec((1, 1, d), lambda i: (i // tiles_per_mod, 0, 0))
    vec_spec = pl.BlockSpec((1, d), lambda i: (0, 0))
    resident = lambda a: pl.BlockSpec(a.shape, lambda i: (0,) * a.ndim, pipeline_mode=pl.Buffered(1))
    layer_slab = lambda a: pl.BlockSpec((None,) + a.shape[1:], lambda i: (layer, 0, 0),
                                        pipeline_mode=pl.Buffered(1))
    if recurrent:
        tok, params = mix_inputs[:N_REC_TOKEN_INPUTS], mix_inputs[N_REC_TOKEN_INPUTS:]
        mix_specs = ([pl.BlockSpec((tm, a.shape[1]), lambda i: (0, 0), pipeline_mode=pl.Buffered(1)) for a in tok]
                     + [pl.BlockSpec((tm, a.shape[1]), lambda i: (jnp.minimum(i + 1, last), 0)) for a in tok]
                     + [resident(a) for a in params])
        mix_args = list(tok) + list(tok) + list(params)
        scratch = [pltpu.VMEM((tm, D_GROUP), F32)] * 2
    else:
        mix_specs = [pl.BlockSpec((tm, D_GROUP), lambda i: (i, 0))] * 2
        mix_args = list(mix_inputs)
        scratch = []
    return pl.pallas_call(
        functools.partial(_tail_kernel, final=final, ff_chunk=ff_chunk, recurrent=recurrent),
        grid=(n // tm,),
        in_specs=[pl.BlockSpec((tm, d), lambda i: (i, 0))] + mix_specs
                 + [resident(w_out), mod_spec, vec_spec, mod_spec, mod_spec, mod_spec,
                    layer_slab(w_ffn_in_all), layer_slab(w_ffn_out_all), vec_spec],
        out_specs=pl.BlockSpec((tm, d), lambda i: (i, 0)),
        out_shape=jax.ShapeDtypeStruct((n, d), F32),
        scratch_shapes=scratch,
        compiler_params=_cparams("arbitrary"),
        name="layer_tail",
    )(x, *mix_args, w_out, gate_mix, gain.reshape(1, d), shift, scale, gate_ffn, w_ffn_in_all, w_ffn_out_all,
      final_gain.reshape(1, d))


def _log_sigmoid(x):
    return jnp.minimum(x, 0.0) - jnp.log(1.0 + jnp.exp(-jnp.abs(x)))


def _pair_block_diag(xp):
    left = lax.broadcasted_iota(jnp.int32, xp.shape, 1) < HEAD_DIM
    zero = jnp.zeros_like(xp)
    return jnp.concatenate([jnp.where(left, xp, zero), jnp.where(left, zero, xp)], axis=0)


def _states_to_pairs(state):
    nb = state.shape[0]
    s = state.reshape(nb, 2, N_HEADS // 2, 2, HEAD_DIM, HEAD_DIM)
    zero = jnp.zeros_like(s[:, :, :, 0])
    return jnp.concatenate([jnp.concatenate([s[:, :, :, 0], zero], axis=-1),
                            jnp.concatenate([zero, s[:, :, :, 1]], axis=-1)], axis=-2)


def _load_states(st, s0_ref):
    st[...] = jnp.zeros(st.shape, st.dtype) if s0_ref is None else s0_ref[0]


def _store_states(sf_ref, st):
    for d in range(st.shape[0]):
        for p in range(st.shape[1]):
            s = st[d, p]
            sf_ref[0, d, 2 * p] = s[:HEAD_DIM, :HEAD_DIM]
            sf_ref[0, d, 2 * p + 1] = s[HEAD_DIM:, HEAD_DIM:]


def _ret_kernel(*refs, zero_init, shared_block):
    if shared_block:
        refs = refs[:5] + refs[2:5] + refs[5:]
    dlm_ref, dlq_ref, qf_ref, kf_ref, vf_ref, qb_ref, kb_ref, vb_ref = refs[:8]
    s0_ref = None if zero_init else refs[8]
    of_ref, ob_ref, sf_ref, st, mask_scr, qdec_scr, kdec_scr, cdec_scr = refs[8 if zero_init else 9:]
    c = pl.program_id(1)
    cs = RET_CHUNK
    pw = 2 * HEAD_DIM
    npair = N_HEADS // 2
    chains = [(d, p) for d in range(2) for p in range(npair)]

    @pl.when(c == 0)
    def _():
        _load_states(st, s0_ref)

    @pl.when((pl.program_id(0) == 0) & (c == 0))
    def _():
        row = lax.broadcasted_iota(jnp.int32, (cs, 2 * cs), 0)
        col = lax.broadcasted_iota(jnp.int32, (cs, 2 * cs), 1) % cs
        tok = lax.broadcasted_iota(jnp.int32, (cs, pw), 0)
        dist = [row - col, col - row]
        tq = [tok.astype(F32), (cs - 1 - tok).astype(F32)]
        for i, (d, p) in enumerate(chains):
            lgm = _log_sigmoid(dlm_ref[d, p])[0:1, :]
            lgq = _log_sigmoid(dlq_ref[d, p])
            mask_scr[i] = jnp.where(dist[d] >= 0, jnp.exp(jnp.maximum(dist[d], 0).astype(F32) * lgm), 0.0)
            qdec_scr[i] = jnp.exp((tq[d] + 1.0) * lgq[0:1, :])
            kdec_scr[i] = jnp.exp((cs - 1.0 - tq[d]) * lgq[0:1, :])
            cdec_scr[i] = jnp.exp(cs * lgq)

    brow = lax.broadcasted_iota(jnp.int32, (pw, pw), 0) // HEAD_DIM
    bcol = lax.broadcasted_iota(jnp.int32, (pw, pw), 1) // HEAD_DIM
    same_head = brow == bcol
    refs = [(qf_ref, kf_ref, vf_ref, of_ref), (qb_ref, kb_ref, vb_ref, ob_ref)]
    nsub = RET_CHUNKS_PER_STEP
    work = [(t, sub) for sub in range(nsub) for t in range(len(chains))]
    sl_of = lambda t: slice(chains[t][1] * pw, (chains[t][1] + 1) * pw)
    rows_of = lambda sub: pl.ds(sub * cs, cs)
    q = {(t, sub): refs[chains[t][0]][0][rows_of(sub), sl_of(t)] for t, sub in work}
    k = {(t, sub): refs[chains[t][0]][1][rows_of(sub), sl_of(t)] * ATTN_SCALE for t, sub in work}
    v = {(t, sub): refs[chains[t][0]][2][rows_of(sub), sl_of(t)] for t, sub in work}
    s = {w: _bdot_nt(q[w], _pair_block_diag(k[w])) * mask_scr[w[0]] for w in work}
    inner = {w: _bdot(s[w], _pair_block_diag(v[w])) for w in work}
    kv = {w: _bdot_tn(k[w] * kdec_scr[w[0]], v[w]) for w in work}
    for step in range(nsub):
        now = [(t, (step if chains[t][0] == 0 else nsub - 1 - step)) for t in range(len(chains))]
        state = {w: st[chains[w[0]]] for w in now}
        cross = {w: _bdot(q[w], state[w]) * qdec_scr[w[0]] for w in now}
        for w in now:
            t, sub = w
            d, p = chains[t]
            refs[d][3][rows_of(sub), sl_of(t)] = inner[w] + cross[w]
            st[d, p] = state[w] * cdec_scr[t][0:1, :] + jnp.where(same_head, kv[w], 0.0)

    @pl.when(c == pl.num_programs(1) - 1)
    def _():
        _store_states(sf_ref, st)


def _retention_scan(decay_logit, q, k, v, state0, seq):
    n = q.shape[0]
    nb = n // seq
    rows = RET_CHUNK * RET_CHUNKS_PER_STEP
    nc = seq // rows
    npair, pw = N_HEADS // 2, 2 * HEAD_DIM
    dl = decay_logit.astype(F32)
    dl_m = jnp.broadcast_to(jnp.repeat(dl, RET_CHUNK, axis=1).reshape(2, npair, 1, 2 * RET_CHUNK),
                            (2, npair, 8, 2 * RET_CHUNK))
    dl_q = jnp.broadcast_to(jnp.repeat(dl, HEAD_DIM, axis=1).reshape(2, npair, 1, pw), (2, npair, 8, pw))
    blk_f = pl.BlockSpec((rows, D_GROUP), lambda b, c: (b * nc + c, 0))
    blk_b = pl.BlockSpec((rows, D_GROUP), lambda b, c: (b * nc + nc - 1 - c, 0))
    st_in = ([] if state0 is None else
             [pl.BlockSpec((1, 2, npair, pw, pw), lambda b, c: (b, 0, 0, 0, 0))])
    st_out = pl.BlockSpec((1, 2, N_HEADS, HEAD_DIM, HEAD_DIM), lambda b, c: (b, 0, 0, 0, 0))
    shared = nc == 1
    bwd_specs, bwd_args = ([], []) if shared else ([blk_b] * 3, [q, k, v])
    return pl.pallas_call(
        functools.partial(_ret_kernel, zero_init=state0 is None, shared_block=shared),
        grid=(nb, nc),
        in_specs=[pl.BlockSpec(dl_m.shape, lambda b, c: (0, 0, 0, 0)),
                  pl.BlockSpec(dl_q.shape, lambda b, c: (0, 0, 0, 0)),
                  blk_f, blk_f, blk_f] + bwd_specs + st_in,
        out_specs=[blk_f, blk_b, st_out],
        out_shape=[jax.ShapeDtypeStruct((n, D_GROUP), F32), jax.ShapeDtypeStruct((n, D_GROUP), F32),
                   jax.ShapeDtypeStruct((nb, 2, N_HEADS, HEAD_DIM, HEAD_DIM), F32)],
        scratch_shapes=[pltpu.VMEM((2, npair, pw, pw), F32),
                        pltpu.VMEM((2 * npair, RET_CHUNK, 2 * RET_CHUNK), F32),
                        pltpu.VMEM((2 * npair, RET_CHUNK, pw), F32),
                        pltpu.VMEM((2 * npair, RET_CHUNK, pw), F32),
                        pltpu.VMEM((2 * npair, 8, pw), F32)],
        compiler_params=_cparams("arbitrary", "arbitrary"),
        name="retention_scan",
    )(dl_m, dl_q, q, k, v, *bwd_args, *([] if state0 is None else [_states_to_pairs(state0)]))


def _rwkv_kernel(*refs, zero_init, shared_block):
    if shared_block:
        refs = refs[:4] + refs[:4] + refs[4:]
    (xrf_ref, xkf_ref, xvf_ref, lof_ref, xrb_ref, xkb_ref, xvb_ref, lob_ref,
     w0_ref, wup_ref, a0_ref, aup_ref, kk_ref, ka_ref) = refs[:14]
    s0_ref = None if zero_init else refs[14]
    yf_ref, yb_ref, sf_ref, st = refs[14 if zero_init else 15:]
    c = pl.program_id(1)
    cs = RWKV_CHUNK
    nsub = RWKV_CHUNKS_PER_STEP
    npair = N_HEADS // 2

    @pl.when(c == 0)
    def _():
        _load_states(st, s0_ref)

    row = lax.broadcasted_iota(jnp.int32, (cs, 3 * cs), 0)
    col = lax.broadcasted_iota(jnp.int32, (cs, 3 * cs), 1) % cs
    pw = 2 * HEAD_DIM
    prow = lax.broadcasted_iota(jnp.int32, (cs, pw), 0)
    pcol = lax.broadcasted_iota(jnp.int32, (cs, pw), 1) % HEAD_DIM
    eye = jnp.where(pcol == prow, 1.0, 0.0).astype(F32)
    sub_diag = (prow // RWKV_SUB) == (pcol // RWKV_SUB)
    brow = lax.broadcasted_iota(jnp.int32, (pw, pw), 0) // HEAD_DIM
    bcol = lax.broadcasted_iota(jnp.int32, (pw, pw), 1) // HEAD_DIM
    same_head = brow == bcol

    bd = _pair_block_diag

    def direction(d, sub, xr_ref, xk_ref, xv_ref, lo_ref):
        rows = pl.ds(sub * cs, cs)
        tri3 = jnp.where((row >= col) if d == 0 else (col >= row), 1.0, 0.0).astype(BF16)
        dist = (prow - pcol) if d == 0 else (pcol - prow)
        xk = xk_ref[rows, :]
        lo = lo_ref[rows, :]
        w_log = -math.exp(-0.5) * _sigmoid(w0_ref[d] + _bdot(jnp.tanh(lo[:, 0:D_LORA]), wup_ref[d]))
        a_all = _sigmoid(a0_ref[d] + _bdot(lo[:, D_LORA:2 * D_LORA], aup_ref[d]))
        w1 = w_log.astype(BF16)
        r1 = w_log - w1.astype(F32)
        w2 = r1.astype(BF16)
        w3 = (r1 - w2.astype(F32)).astype(BF16)
        cum = jnp.dot(tri3, jnp.concatenate([w1, w2, w3], axis=0), preferred_element_type=F32)
        cum_end = cum[cs - 1:cs, :] if d == 0 else cum[0:1, :]
        return dict(earlier=dist > 0, upto=dist >= 0, xr=xr_ref[rows, :], xv=xv_ref[rows, :], a_all=a_all,
                    e_incl=jnp.exp(cum), e_excl=jnp.exp(cum - w_log), e_neg=jnp.exp(-cum),
                    e_end=jnp.exp(cum_end - cum), w_end=jnp.exp(cum_end), kk_all=xk * kk_ref[...],
                    keff_all=xk * (1.0 + (a_all - 1.0) * ka_ref[...]))

    in_refs = [(xrf_ref, xkf_ref, xvf_ref, lof_ref), (xrb_ref, xkb_ref, xvb_ref, lob_ref)]
    dirs = {(d, sub): direction(d, sub, *in_refs[d]) for d in range(2) for sub in range(nsub)}
    y_refs = [yf_ref, yb_ref]

    chains = [(d, sub, p) for d in range(2) for sub in range(nsub) for p in range(npair)]
    pairs = range(len(chains))
    sls = [slice(p * pw, (p + 1) * pw) for _, _, p in chains]
    pick = lambda name: [dirs[d, sub][name][:, sls[i]] for i, (d, sub, _) in enumerate(chains)]
    earlier = [dirs[d, sub]["earlier"] for d, sub, _ in chains]
    upto = [dirs[d, sub]["upto"] for d, sub, _ in chains]
    cat = lambda x, y: jnp.concatenate([x, y], axis=0)

    head_sum = _pair_head_sum

    e_incl, e_excl, e_neg, e_end, w_end = (pick(k) for k in ("e_incl", "e_excl", "e_neg", "e_end", "w_end"))
    kk = [x * lax.rsqrt(head_sum(x * x) + 1e-12) for x in pick("kk_all")]
    b = [kk[p] * a for p, a in zip(pairs, pick("a_all"))]
    keff = pick("keff_all")
    v = pick("xv")
    a_t = [-kk[p] * e_excl[p] for p in pairs]
    r_t = [xr * e_incl[p] for p, xr in zip(pairs, pick("xr"))]
    ar = [cat(a_t[p], r_t[p]).astype(BF16) for p in pairs]
    g = [_bdot_nt(ar[p], cat(bd(b[p] * e_neg[p]), bd(keff[p] * e_neg[p]))) for p in pairs]
    n_mat = [jnp.where(earlier[p], g[p][:cs, :pw], 0.0) for p in pairs]
    m_rb = [jnp.where(upto[p], g[p][cs:, :pw], 0.0) for p in pairs]
    a_ak = [jnp.where(earlier[p], g[p][:cs, pw:], 0.0) for p in pairs]
    m_rk = [jnp.where(upto[p], g[p][cs:, pw:], 0.0) for p in pairs]
    bd_s = lambda xs: (bd(xs[0]), bd(xs[1]))
    n_d = [jnp.where(sub_diag, n, 0.0) for n in n_mat]
    n_o = [jnp.where(sub_diag, 0.0, n) for n in n_mat]
    n_s = [_split2(n) for n in n_d]
    x = [eye + n for n in n_d]
    pk = [_mm3s(ns, bd_s(ns)) for ns in n_s]
    for _ in range(RWKV_SUB_DOUBLINGS - 1):
        xp = [_mm3s(_split2(cat(x[p], pk[p])), bd_s(_split2(pk[p]))) for p in pairs]
        x = [x[p] + xp[p][:cs] for p in pairs]
        pk = [r[cs:] for r in xp]
    t_d = [x[p] + _mm3s(_split2(x[p]), bd_s(_split2(pk[p]))) for p in pairs]
    lcat = lambda x, y: jnp.concatenate([x, y], axis=1)
    m_o = [_bdot(t_d[p], bd(n_o[p])) for p in pairs]
    mz = [_bdot(m_o[p], lcat(bd(t_d[p]), bd(m_o[p]))) for p in pairs]
    z = [t_d[p] + mz[p][:, :pw] for p in pairs]
    t_inv = [(z[p] + _bdot(mz[p][:, pw:], bd(z[p]))).astype(BF16) for p in pairs]
    av = [_bdot(cat(a_ak[p], m_rk[p]), bd(v[p])) for p in pairs]
    tau = [_bdot(t_inv[p], lcat(bd(a_t[p]), bd(av[p][:cs]))) for p in pairs]
    bk_end = [cat(b[p] * e_end[p], keff[p] * e_end[p]) for p in pairs]
    for step in range(nsub):
        now = [i for i, (d, sub, _) in enumerate(chains) if sub == (step if d == 0 else nsub - 1 - step)]
        s_prev = {i: st[chains[i][0], chains[i][2]] for i in now}
        hs = {i: _bdot_nt(cat(tau[i][:, :pw], r_t[i]), s_prev[i]) for i in now}
        u = {i: hs[i][:cs] + tau[i][:, pw:] for i in now}
        y = {i: hs[i][cs:] + _bdot(m_rb[i], bd(u[i])) + av[i][cs:] for i in now}
        for i in now:
            d, sub, p = chains[i]
            y_refs[d][pl.ds(sub * cs, cs), sls[i]] = y[i]
            st[d, p] = s_prev[i] * w_end[i] + jnp.where(same_head, _bdot_tn(cat(u[i], v[i]), bk_end[i]), 0.0)

    @pl.when(c == pl.num_programs(1) - 1)
    def _():
        _store_states(sf_ref, st)


def _rwkv_scan(xr, xk, xv, lora, w0, w_up, a0, a_up, k_k, k_a, state0, seq):
    n = xr.shape[0]
    nb = n // seq
    blk = RWKV_CHUNK * RWKV_CHUNKS_PER_STEP
    nc = seq // blk

    fwd_idx = lambda b, c: (b * nc + c, 0)
    bwd_idx = lambda b, c: (b * nc + nc - 1 - c, 0)
    blk_f = pl.BlockSpec((blk, D_GROUP), fwd_idx)
    blk_b = pl.BlockSpec((blk, D_GROUP), bwd_idx)
    lblk_f = pl.BlockSpec((blk, lora.shape[1]), fwd_idx)
    lblk_b = pl.BlockSpec((blk, lora.shape[1]), bwd_idx)
    dvec = pl.BlockSpec((2, 1, D_GROUP), lambda b, c: (0, 0, 0))
    dmat = pl.BlockSpec((2, D_LORA, D_GROUP), lambda b, c: (0, 0, 0))
    vec = pl.BlockSpec((1, D_GROUP), lambda b, c: (0, 0))
    npair, pw = N_HEADS // 2, 2 * HEAD_DIM
    st_in = ([] if state0 is None else
             [pl.BlockSpec((1, 2, npair, pw, pw), lambda b, c: (b, 0, 0, 0, 0))])
    st_out = pl.BlockSpec((1, 2, N_HEADS, HEAD_DIM, HEAD_DIM), lambda b, c: (b, 0, 0, 0, 0))
    shared = nc == 1
    bwd_specs, bwd_args = ([], []) if shared else ([blk_b, blk_b, blk_b, lblk_b], [xr, xk, xv, lora])
    return pl.pallas_call(
        functools.partial(_rwkv_kernel, zero_init=state0 is None, shared_block=shared),
        grid=(nb, nc),
        in_specs=[blk_f, blk_f, blk_f, lblk_f] + bwd_specs + [dvec, dmat, dvec, dmat, vec, vec] + st_in,
        out_specs=[blk_f, blk_b, st_out],
        out_shape=[jax.ShapeDtypeStruct((n, D_GROUP), F32), jax.ShapeDtypeStruct((n, D_GROUP), F32),
                   jax.ShapeDtypeStruct((nb, 2, N_HEADS, HEAD_DIM, HEAD_DIM), F32)],
        scratch_shapes=[pltpu.VMEM((2, npair, pw, pw), F32)],
        compiler_params=_cparams("arbitrary", "arbitrary"),
        name="rwkv7_scan",
    )(xr, xk, xv, lora, *bwd_args, w0.reshape(2, 1, D_GROUP), w_up, a0.reshape(2, 1, D_GROUP), a_up,
      k_k.reshape(1, D_GROUP), k_a.reshape(1, D_GROUP), *([] if state0 is None else [_states_to_pairs(state0)]))


ATTN_WIDTHS = (D_GROUP,) * 6
REC_WIDTHS = (D_GROUP,) * 7 + (2 * D_LORA + D_LORA_G,)


def kernel(x_prompt, x_sample, cache_na_k, cache_na_v, cache_diff_k, cache_diff_v, state_ret, state_rwkv, c, c_ctx, norm_mix_g, norm_ffn_g, norm_final_g, w_ada, b_ada, w_in_attn, w_out_attn, na_rpb, diff_lq1, diff_lk1, diff_lq2, diff_lk2, w_in_rec, w_out_rec, ret_decay_logit, rw_w0, rw_w_up, rw_a0, rw_a_up, rw_g_up, rw_k_k, rw_k_a, rw_r_k, rw_ln_g, rw_ln_b, w_ffn_in, w_ffn_out):
    bp, seq, d = x_prompt.shape
    bl, n_lat, _ = x_sample.shape
    depth = w_ada.shape[0]
    assert n_lat == GRID_ROWS * GRID_W and 1 + bl <= 8
    assert seq % (RWKV_CHUNK * RWKV_CHUNKS_PER_STEP) == 0 and seq % (RET_CHUNK * RET_CHUNKS_PER_STEP) == 0
    assert n_lat % (RWKV_CHUNK * RWKV_CHUNKS_PER_STEP) == 0 and n_lat % (RET_CHUNK * RET_CHUNKS_PER_STEP) == 0
    assert (bp * seq) % TAIL_TM == 0 and n_lat % TAIL_TM == 0 and n_lat % LAT_DIFF_TQ == 0
    assert w_in_attn.shape[2] == sum(ATTN_WIDTHS) and w_in_rec.shape[2] == sum(REC_WIDTHS)
    assert w_ffn_in.shape[2] == 2 * D_FF and w_ada.shape[2] % ADA_TN == 0
    ctx = x_prompt.reshape(bp * seq, d)
    lat = x_sample.reshape(bl * n_lat, d)

    cond8 = jnp.zeros((8, d), F32).at[0].set(c_ctx).at[1:1 + bl].set(c)
    mods = _ada_modulation(cond8, w_ada, b_ada).reshape(depth, 8, 6, d)
    w_ffn_in_bf16 = w_ffn_in.astype(BF16)
    w_ffn_out_bf16 = w_ffn_out.astype(BF16)

    outs = {}
    for layer in range(depth):
        m_ctx = [mods[layer, 0:1, j].reshape(1, 1, d) for j in range(6)]
        m_lat = [mods[layer, 1:1 + bl, j].reshape(bl, 1, d) for j in range(6)]
        if layer % 2 == 0:
            i = layer // 2
            lam_init = 0.8 - 0.6 * math.exp(-0.3 * layer)
            lam_params = jnp.stack([diff_lq1[i], diff_lk1[i], diff_lq2[i], diff_lk2[i]]).astype(F32)
            w_in = w_in_attn[i].astype(BF16)
            w_out = w_out_attn[i].astype(BF16)
            res = _ctx_attention(ctx, norm_mix_g[layer], m_ctx[0], m_ctx[1], w_in, lam_params, seq, lam_init)
            mix_ctx = res[:2]
            outs["na_k"], outs["na_v"], outs["df_k"], outs["df_v"] = res[2:]
            plat = _norm_mod_matmul(lat, norm_mix_g[layer], m_lat[0], m_lat[1], w_in, ATTN_WIDTHS, n_lat, PROJ_TM)
            past = cache_na_k.shape[2]
            tt = _na_bias_table(na_rpb[i])
            o_na = _lat_na_attention(plat[0], plat[1], plat[2],
                                     cache_na_k[:, i].reshape(bl, past, D_GROUP),
                                     cache_na_v[:, i].reshape(bl, past, D_GROUP), tt, n_lat)
            o_df = _lat_diff_attention(lam_params, plat[3], plat[4], plat[5],
                                       cache_diff_k[:, i].reshape(bl, past, D_GROUP),
                                       cache_diff_v[:, i].reshape(bl, past, D_GROUP), n_lat, lam_init)
            mix_lat = (o_na, o_df)
        else:
            j = layer // 2
            w_in = w_in_rec[j].astype(BF16)
            w_out = w_out_rec[j].astype(BF16)
            pc = _norm_mod_matmul(ctx, norm_mix_g[layer], m_ctx[0], m_ctx[1], w_in, REC_WIDTHS, bp * seq, PROJ_TM)
            plat = _norm_mod_matmul(lat, norm_mix_g[layer], m_lat[0], m_lat[1], w_in, REC_WIDTHS, n_lat, PROJ_TM)
            mixes = []
            for is_ctx, p, s_ret0, s_rw0, sq in ((True, pc, None, None, seq),
                                                 (False, plat, state_ret[:, j], state_rwkv[:, j], n_lat)):
                rq, rk, rv, rg, wr, wk, wv, lora = p
                o_f, o_b, s_ret = _retention_scan(ret_decay_logit[j], rq, rk, rv, s_ret0, sq)
                y_f, y_b, s_rw = _rwkv_scan(wr, wk, wv, lora, rw_w0[j], rw_w_up[j].astype(BF16), rw_a0[j],
                                            rw_a_up[j].astype(BF16), rw_k_k[j], rw_k_a[j], s_rw0, sq)
                mixes.append((o_f, o_b, rg, y_f, y_b, wr, wk, wv, lora, rw_r_k[j].reshape(1, D_GROUP),
                              rw_ln_g[j].reshape(1, D_GROUP), rw_ln_b[j].reshape(1, D_GROUP),
                              rw_g_up[j].astype(BF16)))
                if is_ctx:
                    outs["ret"], outs["rwkv"] = s_ret, s_rw
            mix_ctx, mix_lat = mixes
        final = layer == depth - 1
        ctx = _layer_tail(ctx, mix_ctx, w_out, m_ctx[2], norm_ffn_g[layer], m_ctx[3], m_ctx[4], m_ctx[5],
                          w_ffn_in_bf16, w_ffn_out_bf16, layer, norm_final_g, bp * seq, final)
        lat = _layer_tail(lat, mix_lat, w_out, m_lat[2], norm_ffn_g[layer], m_lat[3], m_lat[4], m_lat[5],
                          w_ffn_in_bf16, w_ffn_out_bf16, layer, norm_final_g, n_lat, final)

    y_prompt = ctx.reshape(bp, seq, d)
    y_sample = lat.reshape(bl, n_lat, d)
    new_cache_na_k = outs["na_k"].reshape(bp, 1, seq, N_HEADS, HEAD_DIM)
    new_cache_na_v = outs["na_v"].reshape(bp, 1, seq, N_HEADS, HEAD_DIM)
    new_cache_diff_k = outs["df_k"].reshape(bp, 1, seq, N_HEADS // 2, 2, HEAD_DIM)
    new_cache_diff_v = outs["df_v"].reshape(bp, 1, seq, N_HEADS // 2, 2 * HEAD_DIM)
    new_state_ret = outs["ret"].reshape(bp, 1, 2, N_HEADS, HEAD_DIM, HEAD_DIM)
    new_state_rwkv = outs["rwkv"].reshape(bp, 1, 2, N_HEADS, HEAD_DIM, HEAD_DIM)
    return (y_prompt, y_sample, new_cache_na_k, new_cache_na_v, new_cache_diff_k, new_cache_diff_v,
            new_state_ret, new_state_rwkv)
```

```python
import functools
import math

import jax
import jax.numpy as jnp
from jax import lax
from jax.experimental import pallas as pl
from jax.experimental.pallas import tpu as pltpu

F32 = jnp.float32
BF16 = jnp.bfloat16

HEAD_DIM = 64
N_HEADS = 8
D_GROUP = 512
GRID_W = 64
GRID_ROWS = 32
WIN_R = 8
WIN_W = 16
RPB_R = 15
RPB_C = 31
D_FF = 2816
FFN_CHUNK = D_FF // 2
D_LORA = 64
D_LORA_G = 128
ROPE_BASE = 10000.0
RMS_EPS = 1e-6
RWKV_GN_EPS = 64e-5
NEG_INF = -1e30
ATTN_SCALE = HEAD_DIM ** -0.5
RET_CHUNK = 128
RET_CHUNKS_PER_STEP = 2
RWKV_CHUNK = 64
RWKV_CHUNKS_PER_STEP = 4
RWKV_SUB = 16
RWKV_SUB_DOUBLINGS = RWKV_SUB.bit_length() - 2
assert RWKV_CHUNK == 4 * RWKV_SUB
V7X_VMEM_BYTES = 64 * 1024 * 1024
VMEM_LIMIT = V7X_VMEM_BYTES - 8 * 1024 * 1024
ADA_TN = 1536
TAIL_TM = 512
TAIL_TM_REC = 256
PROJ_TM = 512
CTX_SEQS_PER_STEP = 1
LAT_DIFF_TQ = 256


def _cparams(*sem):
    return pltpu.CompilerParams(dimension_semantics=sem, vmem_limit_bytes=VMEM_LIMIT)


def _sigmoid(x):
    return 1.0 / (1.0 + jnp.exp(-x))


def _silu(x):
    return x * _sigmoid(x)


def _rms(x, eps=RMS_EPS):
    return x * lax.rsqrt(jnp.mean(x * x, axis=-1, keepdims=True) + eps)


def _bdot(a, b):
    return jnp.dot(a.astype(BF16), b.astype(BF16), preferred_element_type=F32)


def _bdot_nt(a, b):
    return lax.dot_general(a.astype(BF16), b.astype(BF16), (((1,), (1,)), ((), ())),
                           preferred_element_type=F32)


def _bdot_tn(a, b):
    return lax.dot_general(a.astype(BF16), b.astype(BF16), (((0,), (0,)), ((), ())),
                           preferred_element_type=F32)


def _split2(x):
    hi = x.astype(BF16)
    lo = (x - hi.astype(F32)).astype(BF16)
    return hi, lo


def _mm3s(a_split, b_split):
    ah, al = a_split
    bh, bl = b_split
    n = bh.shape[1]
    rhs = jnp.concatenate([jnp.concatenate([bh, bl], axis=1),
                           jnp.concatenate([bh, jnp.zeros_like(bl)], axis=1)], axis=0)
    out = jnp.dot(jnp.concatenate([ah, al], axis=1), rhs, preferred_element_type=F32)
    return out[:, :n] + out[:, n:]


def _diff_lambda(lp, lam_init):
    s1 = jnp.sum(lp[0:1, :] * lp[1:2, :], axis=-1, keepdims=True)
    s2 = jnp.sum(lp[2:3, :] * lp[3:4, :], axis=-1, keepdims=True)
    return jnp.exp(s1) - jnp.exp(s2) + lam_init


def _ada_kernel(c_ref, w_ref, b_ref, o_ref):
    s = _silu(c_ref[...])
    o_ref[0] = _bdot(s, w_ref[0]) + b_ref[0]


def _ada_modulation(cond8, w_ada, b_ada):
    depth, d, n = w_ada.shape
    tn = ADA_TN
    return pl.pallas_call(
        _ada_kernel,
        grid=(depth, n // tn),
        in_specs=[pl.BlockSpec((8, d), lambda l, j: (0, 0)),
                  pl.BlockSpec((1, d, tn), lambda l, j: (l, 0, j)),
                  pl.BlockSpec((1, 1, tn), lambda l, j: (l, 0, j))],
        out_specs=pl.BlockSpec((1, 8, tn), lambda l, j: (l, 0, j)),
        out_shape=jax.ShapeDtypeStruct((depth, 8, n), F32),
        compiler_params=_cparams("arbitrary", "arbitrary"),
        name="ada_modulation",
    )(cond8, w_ada, b_ada.reshape(depth, 1, n))


def _norm_mod(x_ref, g_ref, sh_ref, sc_ref):
    h = _rms(x_ref[...]) * g_ref[...]
    return (h * (1.0 + sc_ref[0]) + sh_ref[0]).astype(BF16)


def _nmm_kernel(x_ref, g_ref, sh_ref, sc_ref, w_ref, *o_refs, widths):
    hb = _norm_mod(x_ref, g_ref, sh_ref, sc_ref)
    off = 0
    for o_ref, w in zip(o_refs, widths):
        o_ref[...] = jnp.dot(hb, w_ref[:, off:off + w], preferred_element_type=F32).astype(o_ref.dtype)
        off += w


def _norm_mod_matmul(x, gain, shift, scale, w_bf16, widths, rows_per_mod, tm, bf16_groups=()):
    n, d = x.shape
    tiles_per_mod = rows_per_mod // tm
    mod_spec = pl.BlockSpec((1, 1, d), lambda i: (i // tiles_per_mod, 0, 0))
    return pl.pallas_call(
        functools.partial(_nmm_kernel, widths=widths),
        grid=(n // tm,),
        in_specs=[pl.BlockSpec((tm, d), lambda i: (i, 0)),
                  pl.BlockSpec((1, d), lambda i: (0, 0)),
                  mod_spec, mod_spec,
                  pl.BlockSpec(w_bf16.shape, lambda i: (0, 0), pipeline_mode=pl.Buffered(1))],
        out_specs=[pl.BlockSpec((tm, w), lambda i: (i, 0)) for w in widths],
        out_shape=[jax.ShapeDtypeStruct((n, w), BF16 if i in bf16_groups else F32) for i, w in enumerate(widths)],
        compiler_params=_cparams("arbitrary"),
        name="norm_mod_matmul",
    )(x, gain.reshape(1, d), shift, scale, w_bf16)


PAIR_W = 2 * HEAD_DIM


def _pair_tiles(x):
    return [x[:, j * PAIR_W:(j + 1) * PAIR_W] for j in range(x.shape[1] // PAIR_W)]


def _split_pair(tile):
    left = lax.broadcasted_iota(jnp.int32, tile.shape, 1) < HEAD_DIM
    zero = jnp.zeros_like(tile)
    return jnp.where(left, tile, zero), jnp.where(left, zero, tile)


def _pair_head_sum(tile):
    left = lax.broadcasted_iota(jnp.int32, tile.shape, 1) < HEAD_DIM
    zero = jnp.zeros_like(tile)
    s_left = jnp.sum(jnp.where(left, tile, zero), axis=-1, keepdims=True)
    s_right = jnp.sum(jnp.where(left, zero, tile), axis=-1, keepdims=True)
    return jnp.where(left, s_left, s_right)


def _with_ones(v):
    return jnp.concatenate([v, jnp.ones_like(v)], axis=1)


def _exp_scores(s):
    return jnp.exp(s - jnp.max(s, axis=-1, keepdims=True)).astype(BF16)


def _merge_pair(out_even, out_odd):
    left = lax.broadcasted_iota(jnp.int32, (out_even.shape[0], PAIR_W), 1) < HEAD_DIM
    return jnp.where(left, out_even[:, :PAIR_W] * (1.0 / out_even[:, PAIR_W:]),
                     out_odd[:, :PAIR_W] * (1.0 / out_odd[:, PAIR_W:]))


def _diff_combine(out1, out2, lam, lam_init):
    o = out1[:, :PAIR_W] * (1.0 / out1[:, PAIR_W:]) - out2[:, :PAIR_W] * (lam / out2[:, PAIR_W:])
    return _rms(o) * (1.0 - lam_init)


CTX_CACHE_SHAPES = ((N_HEADS, HEAD_DIM), (N_HEADS, HEAD_DIM),
                    (N_HEADS // 2, 2, HEAD_DIM), (N_HEADS // 2, 2 * HEAD_DIM))


def _ctx_attn_kernel(x_ref, g_ref, sh_ref, sc_ref, w_ref, lp_ref, on_ref, od_ref, *cache_refs, lam_init, seq):
    hb = _norm_mod(x_ref, g_ref, sh_ref, sc_ref)
    proj = [jnp.dot(hb, w_ref[:, i * D_GROUP:(i + 1) * D_GROUP], preferred_element_type=F32)
            for i in range(len(ATTN_WIDTHS))]
    for c_ref, val in zip(cache_refs, (proj[1], proj[2], proj[4], proj[5])):
        c_ref[...] = val.reshape(c_ref.shape)
    tiles = range(D_GROUP // PAIR_W)
    lam = _diff_lambda(lp_ref[...], lam_init)
    for s in range(hb.shape[0] // seq):
        rows = slice(s * seq, (s + 1) * seq)
        qn, kn, vn, qd, kd, vd = (p[rows] for p in proj)
        q = [_split_pair(t) for t in _pair_tiles((qn * ATTN_SCALE).astype(BF16))
             + _pair_tiles((qd * ATTN_SCALE).astype(BF16))]
        k = _pair_tiles(kn.astype(BF16)) + _pair_tiles(kd.astype(BF16))
        v = [_with_ones(t) for t in _pair_tiles(vn.astype(BF16)) + _pair_tiles(vd.astype(BF16))]
        e = [[_exp_scores(_bdot_nt(q[j][c], k[j])) for c in range(2)] for j in range(2 * len(tiles))]
        out = [[jnp.dot(e[j][c], v[j], preferred_element_type=F32) for c in range(2)]
               for j in range(2 * len(tiles))]
        for j in tiles:
            sl = slice(j * PAIR_W, (j + 1) * PAIR_W)
            on_ref[rows, sl] = _merge_pair(out[j][0], out[j][1])
            od_ref[rows, sl] = _diff_combine(out[len(tiles) + j][0], out[len(tiles) + j][1], lam, lam_init)


def _ctx_attention(x, gain, shift, scale, w_bf16, lam_params, seq, lam_init):
    n, d = x.shape
    tm = seq * CTX_SEQS_PER_STEP
    blk = pl.BlockSpec((tm, D_GROUP), lambda b: (b, 0))
    mod_spec = pl.BlockSpec((1, 1, d), lambda b: (0, 0, 0))
    cache_specs = [pl.BlockSpec((tm,) + dims, lambda b, nd=len(dims): (b,) + (0,) * nd)
                   for dims in CTX_CACHE_SHAPES]
    return pl.pallas_call(
        functools.partial(_ctx_attn_kernel, lam_init=lam_init, seq=seq),
        grid=(n // tm,),
        in_specs=[pl.BlockSpec((tm, d), lambda b: (b, 0)),
                  pl.BlockSpec((1, d), lambda b: (0, 0)),
                  mod_spec, mod_spec,
                  pl.BlockSpec(w_bf16.shape, lambda b: (0, 0), pipeline_mode=pl.Buffered(1)),
                  pl.BlockSpec(lam_params.shape, lambda b: (0, 0))],
        out_specs=[blk, blk] + cache_specs,
        out_shape=([jax.ShapeDtypeStruct((n, D_GROUP), F32)] * 2
                   + [jax.ShapeDtypeStruct((n,) + dims, F32) for dims in CTX_CACHE_SHAPES]),
        compiler_params=_cparams("arbitrary"),
        name="ctx_attention",
    )(x, gain.reshape(1, d), shift, scale, w_bf16, lam_params)


def _rope_tables(n):
    quarter = HEAD_DIM // 4
    pos = jnp.arange(n)
    inv_freq = ROPE_BASE ** (-jnp.arange(quarter, dtype=F32) / quarter)
    lane = jnp.arange(HEAD_DIM)
    p = jnp.where(lane[None, :] < HEAD_DIM // 2, (pos // GRID_W)[:, None], (pos % GRID_W)[:, None]).astype(F32)
    ang = p * inv_freq[lane % quarter][None, :]
    sign = jnp.where((lane % (2 * quarter)) < quarter, -1.0, 1.0).astype(F32)
    cos = jnp.tile(jnp.cos(ang), (1, 2))
    sin = jnp.tile(jnp.sin(ang) * sign[None, :], (1, 2))
    return cos, sin


def _rope(x, cos, sin_signed):
    quarter = HEAD_DIM // 4
    lane = lax.broadcasted_iota(jnp.int32, cos.shape, 1)
    first = (lane % (2 * quarter)) < quarter
    tiles = []
    for t in _pair_tiles(x):
        partner = jnp.where(first, pltpu.roll(t, PAIR_W - quarter, 1), pltpu.roll(t, quarter, 1))
        tiles.append(t * cos + partner * sin_signed)
    return jnp.concatenate(tiles, axis=1)


def _lat_diff_kernel(lp_ref, q_ref, k_ref, v_ref, ck_ref, cv_ref, cosq_ref, sinq_ref, cosk_ref, sink_ref,
                     o_ref, kall, vall, *, lam_init, n_lat):
    @pl.when(pl.program_id(1) == 0)
    def _():
        kall[0:n_lat, :] = _rope(k_ref[...], cosk_ref[...], sink_ref[...]).astype(BF16)
        kall[n_lat:, :] = ck_ref[...].astype(BF16)
        for j, (vt, ct) in enumerate(zip(_pair_tiles(v_ref[...].astype(BF16)),
                                         _pair_tiles(cv_ref[...].astype(BF16)))):
            vall[j, 0:n_lat, :] = _with_ones(vt)
            vall[j, n_lat:, :] = _with_ones(ct)

    lam = _diff_lambda(lp_ref[...], lam_init)
    q = (_rope(q_ref[...], cosq_ref[...], sinq_ref[...]) * ATTN_SCALE).astype(BF16)
    qs = [_split_pair(qt) for qt in _pair_tiles(q)]
    sls = [slice(j * PAIR_W, (j + 1) * PAIR_W) for j in range(len(qs))]
    for group in ((0, 1), (2, 3)):
        e = {j: [_exp_scores(_bdot_nt(qs[j][c], kall[:, sls[j]])) for c in range(2)] for j in group}
        out = {j: [jnp.dot(e[j][c], vall[j], preferred_element_type=F32) for c in range(2)] for j in group}
        for j in group:
            o_ref[:, sls[j]] = _diff_combine(out[j][0], out[j][1], lam, lam_init)


def _lat_diff_attention(lam_params, qd, kd, vd, cache_k, cache_v, n_lat, lam_init, tq=LAT_DIFF_TQ):
    n = qd.shape[0]
    nb = n // n_lat
    nq = n_lat // tq
    past = cache_k.shape[1]
    cos, sin = _rope_tables(n_lat)
    qblk = pl.BlockSpec((tq, D_GROUP), lambda b, i: (b * nq + i, 0))
    once = pl.Buffered(1)
    kvblk = pl.BlockSpec((n_lat, D_GROUP), lambda b, i: (b, 0), pipeline_mode=once)
    cblk = pl.BlockSpec((None, past, D_GROUP), lambda b, i: (b, 0, 0))
    return pl.pallas_call(
        functools.partial(_lat_diff_kernel, lam_init=lam_init, n_lat=n_lat),
        grid=(nb, nq),
        in_specs=[pl.BlockSpec(lam_params.shape, lambda b, i: (0, 0)),
                  qblk, kvblk, kvblk, cblk, cblk,
                  pl.BlockSpec((tq, PAIR_W), lambda b, i: (i, 0)),
                  pl.BlockSpec((tq, PAIR_W), lambda b, i: (i, 0)),
                  pl.BlockSpec((n_lat, PAIR_W), lambda b, i: (0, 0), pipeline_mode=once),
                  pl.BlockSpec((n_lat, PAIR_W), lambda b, i: (0, 0), pipeline_mode=once)],
        out_specs=qblk,
        out_shape=jax.ShapeDtypeStruct((n, D_GROUP), F32),
        scratch_shapes=[pltpu.VMEM((n_lat + past, D_GROUP), BF16),
                        pltpu.VMEM((D_GROUP // PAIR_W, n_lat + past, 2 * PAIR_W), BF16)],
        compiler_params=_cparams("arbitrary", "arbitrary"),
        name="lat_diff_attention",
    )(lam_params, qd, kd, vd, cache_k, cache_v, cos, sin, cos, sin)


NA_Q_ROWS = 4
NA_KEY_ROWS = WIN_R + NA_Q_ROWS
NA_TABLE = RPB_R + 1


def _na_bias_kernel(rpb_ref, tt_ref):
    h = pl.program_id(0)
    lane = lax.broadcasted_iota(jnp.int32, (GRID_W, 2 * GRID_W), 1)
    qc = lax.broadcasted_iota(jnp.int32, (GRID_W, 2 * GRID_W), 0)
    dc = jnp.clip(lane % GRID_W - qc + (WIN_W - 1), 0, RPB_C - 1)
    first = lane < GRID_W
    rows = []
    for dr in range(RPB_R):
        acc = jnp.zeros((GRID_W, 2 * GRID_W), F32)
        for d in range(RPB_C):
            acc = jnp.where(dc == d, rpb_ref[h * RPB_R + dr, d], acc)
        rows.append(acc)
    for k in range(NA_TABLE):
        tt_ref[0, k] = jnp.where(first, rows[min(max(k - 1, 0), RPB_R - 1)], rows[min(k, RPB_R - 1)])


def _na_bias_table(rpb):
    return pl.pallas_call(
        _na_bias_kernel,
        grid=(N_HEADS,),
        in_specs=[pl.BlockSpec(memory_space=pltpu.SMEM)],
        out_specs=pl.BlockSpec((1, NA_TABLE, GRID_W, 2 * GRID_W), lambda h: (h, 0, 0, 0)),
        out_shape=jax.ShapeDtypeStruct((N_HEADS, NA_TABLE, GRID_W, 2 * GRID_W), F32),
        compiler_params=_cparams("arbitrary"),
        name="na_bias_table",
    )(rpb.reshape(N_HEADS * RPB_R, RPB_C))


def _lat_na_kernel(q_ref, k_ref, v_ref, ck_ref, cv_ref, tt_ref, o_ref):
    step = pl.program_id(1)
    past = ck_ref.shape[0]
    nwin = NA_KEY_ROWS * GRID_W
    r_q = [NA_Q_ROWS * step + a for a in range(NA_Q_ROWS)]
    r_start = [jnp.clip(r - WIN_R // 2, 0, GRID_ROWS - WIN_R) for r in r_q]
    ws = jnp.minimum(r_start[0], GRID_ROWS - NA_KEY_ROWS)
    row0 = pl.multiple_of(ws * GRID_W, GRID_W)
    lane = lax.broadcasted_iota(jnp.int32, (GRID_W, past + nwin), 1)
    key_row = ws + (lane - past) // GRID_W
    kc = lane % GRID_W
    qc = lax.broadcasted_iota(jnp.int32, (GRID_W, past + nwin), 0)
    c_start = jnp.clip(qc - WIN_W // 2, 0, GRID_W - WIN_W)
    col_ok = (kc >= c_start) & (kc < c_start + WIN_W)
    visible = jnp.concatenate(
        [(lane < past) | (col_ok & (key_row >= rs) & (key_row < rs + WIN_R)) for rs in r_start], axis=0)
    no_bias = jnp.zeros((GRID_W, past), F32)

    def bias_rows(h, r):
        tiles = [tt_ref[h, jnp.clip(ws + 2 * j - r + WIN_R, 0, NA_TABLE - 1)] for j in range(NA_KEY_ROWS // 2)]
        return jnp.concatenate([no_bias] + tiles, axis=1)

    q = [_split_pair(t) for t in _pair_tiles((q_ref[...] * ATTN_SCALE).astype(BF16))]
    kcat = _pair_tiles(jnp.concatenate([ck_ref[...], k_ref[pl.ds(row0, nwin), :]], axis=0).astype(BF16))
    vcat = [_with_ones(t) for t in
            _pair_tiles(jnp.concatenate([cv_ref[...], v_ref[pl.ds(row0, nwin), :]], axis=0).astype(BF16))]
    bias = [jnp.concatenate([bias_rows(h, r) for r in r_q], axis=0) for h in range(N_HEADS)]
    e = [[_exp_scores(jnp.where(visible, _bdot_nt(q[j][c], kcat[j]) + bias[2 * j + c], NEG_INF))
          for c in range(2)] for j in range(len(q))]
    out = [[jnp.dot(e[j][c], vcat[j], preferred_element_type=F32) for c in range(2)] for j in range(len(q))]
    for j in range(len(q)):
        o_ref[:, j * PAIR_W:(j + 1) * PAIR_W] = _merge_pair(out[j][0], out[j][1])


def _lat_na_attention(qn, kn, vn, cache_k, cache_v, tt, n_lat):
    n = qn.shape[0]
    nb = n // n_lat
    past = cache_k.shape[1]
    steps = GRID_ROWS // NA_Q_ROWS
    qblk = pl.BlockSpec((NA_Q_ROWS * GRID_W, D_GROUP), lambda b, r: (b * steps + r, 0))
    kvblk = pl.BlockSpec((n_lat, D_GROUP), lambda b, r: (b, 0))
    cblk = pl.BlockSpec((None, past, D_GROUP), lambda b, r: (b, 0, 0))
    return pl.pallas_call(
        _lat_na_kernel,
        grid=(nb, steps),
        in_specs=[qblk, kvblk, kvblk, cblk, cblk,
                  pl.BlockSpec(tt.shape, lambda b, r: (0, 0, 0, 0))],
        out_specs=qblk,
        out_shape=jax.ShapeDtypeStruct((n, D_GROUP), F32),
        compiler_params=_cparams("arbitrary", "arbitrary"),
        name="lat_na_attention",
    )(qn, kn, vn, cache_k, cache_v, tt)


def _rec_mix(of_ref, ob_ref, rg_ref, yf_ref, yb_ref, xr_ref, xk_ref, xv_ref, lo_ref, rk_ref, lng_ref, lnb_ref,
             gup_ref):
    def head_sums(x):
        return jnp.concatenate([_pair_head_sum(t) for t in _pair_tiles(x)], axis=1)

    inv_d = 1.0 / HEAD_DIM
    o = of_ref[...] + ob_ref[...]
    o_ret = _silu(rg_ref[...]) * (o * lax.rsqrt(head_sums(o * o) * inv_d + RMS_EPS))
    y = yf_ref[...] + yb_ref[...]
    yc = y - head_sums(y) * inv_d
    var = head_sums(yc * yc) * inv_d
    yn = yc * lax.rsqrt(var + RWKV_GN_EPS) * lng_ref[...] + lnb_ref[...]
    bonus = head_sums(xr_ref[...] * rk_ref[...] * xk_ref[...]) * xv_ref[...]
    g_rw = _bdot(_sigmoid(lo_ref[:, 2 * D_LORA:]), gup_ref[...])
    return o_ret, (yn + bonus) * g_rw


N_REC_TOKEN_INPUTS = 9
N_REC_PARAMS = 4


def _tail_kernel(*refs, final, ff_chunk, recurrent):
    x_ref = refs[0]
    if not recurrent:
        n_mix = 2
        mix_a, mix_b = refs[1][...], refs[2][...]
    else:
        n_mix = 2 * N_REC_TOKEN_INPUTS + N_REC_PARAMS
        first_refs = refs[1:1 + N_REC_TOKEN_INPUTS]
        next_refs = refs[1 + N_REC_TOKEN_INPUTS:1 + 2 * N_REC_TOKEN_INPUTS]
        param_refs = refs[1 + 2 * N_REC_TOKEN_INPUTS:1 + n_mix]
        mixa_scr, mixb_scr = refs[-2:]
        refs = refs[:-2]

        @pl.when(pl.program_id(0) == 0)
        def _():
            a0, b0 = _rec_mix(*first_refs, *param_refs)
            mixa_scr[...] = a0
            mixb_scr[...] = b0

        mix_a, mix_b = mixa_scr[...], mixb_scr[...]
    wo_ref, gm_ref, g_ref, sh_ref, sc_ref, gf_ref, wi_ref, wf_ref, fg_ref, o_ref = refs[1 + n_mix:]
    if recurrent:
        a1, b1 = _rec_mix(*next_refs, *param_refs)
        mixa_scr[...] = a1
        mixb_scr[...] = b1
    mix = _bdot(mix_a, wo_ref[0:D_GROUP, :]) + _bdot(mix_b, wo_ref[D_GROUP:, :])
    x1 = x_ref[...] + gm_ref[0] * mix
    h = _rms(x1) * g_ref[...]
    hb = (h * (1.0 + sc_ref[0]) + sh_ref[0]).astype(BF16)
    acc = jnp.zeros_like(x1)
    for c0 in range(0, D_FF, ff_chunk):
        gate = jnp.dot(hb, wi_ref[:, c0:c0 + ff_chunk], preferred_element_type=F32)
        up = jnp.dot(hb, wi_ref[:, D_FF + c0:D_FF + c0 + ff_chunk], preferred_element_type=F32)
        acc = acc + jnp.dot((_silu(gate) * up).astype(BF16), wf_ref[c0:c0 + ff_chunk, :],
                            preferred_element_type=F32)
    x2 = x1 + gf_ref[0] * acc
    if final:
        x2 = _rms(x2) * fg_ref[...]
    o_ref[...] = x2


def _layer_tail(x, mix_inputs, w_out, gate_mix, gain, shift, scale, gate_ffn, w_ffn_in_all, w_ffn_out_all, layer,
                final_gain, rows_per_mod, final, ff_chunk=FFN_CHUNK):
    n, d = x.shape
    recurrent = len(mix_inputs) != 2
    tm = TAIL_TM_REC if recurrent else TAIL_TM
    tiles_per_mod = rows_per_mod // tm
    last = n // tm - 1
    mod_spec = pl.BlockSpec((1, 1, d), lambda i: (i // tiles_per_mod, 0, 0))
    vec_spec = pl.BlockSpec((1, d), lambda i: (0, 0))
    resident = lambda a: pl.BlockSpec(a.shape, lambda i: (0,) * a.ndim, pipeline_mode=pl.Buffered(1))
    layer_slab = lambda a: pl.BlockSpec((None,) + a.shape[1:], lambda i: (layer, 0, 0),
                                        pipeline_mode=pl.Buffered(1))
    if recurrent:
        tok, params = mix_inputs[:N_REC_TOKEN_INPUTS], mix_inputs[N_REC_TOKEN_INPUTS:]
        mix_specs = ([pl.BlockSpec((tm, a.shape[1]), lambda i: (0, 0), pipeline_mode=pl.Buffered(1)) for a in tok]
                     + [pl.BlockSpec((tm, a.shape[1]), lambda i: (jnp.minimum(i + 1, last), 0)) for a in tok]
                     + [resident(a) for a in params])
        mix_args = list(tok) + list(tok) + list(params)
        scratch = [pltpu.VMEM((tm, D_GROUP), F32)] * 2
    else:
        mix_specs = [pl.BlockSpec((tm, D_GROUP), lambda i: (i, 0))] * 2
        mix_args = list(mix_inputs)
        scratch = []
    return pl.pallas_call(
        functools.partial(_tail_kernel, final=final, ff_chunk=ff_chunk, recurrent=recurrent),
        grid=(n // tm,),
        in_specs=[pl.BlockSpec((tm, d), lambda i: (i, 0))] + mix_specs
                 + [resident(w_out), mod_spec, vec_spec, mod_spec, mod_spec, mod_spec,
                    layer_slab(w_ffn_in_all), layer_slab(w_ffn_out_all), vec_spec],
        out_specs=pl.BlockSpec((tm, d), lambda i: (i, 0)),
        out_shape=jax.ShapeDtypeStruct((n, d), F32),
        scratch_shapes=scratch,
        compiler_params=_cparams("arbitrary"),
        name="layer_tail",
    )(x, *mix_args, w_out, gate_mix, gain.reshape(1, d), shift, scale, gate_ffn, w_ffn_in_all, w_ffn_out_all,
      final_gain.reshape(1, d))


def _log_sigmoid(x):
    return jnp.minimum(x, 0.0) - jnp.log(1.0 + jnp.exp(-jnp.abs(x)))


def _pair_block_diag(xp):
    left = lax.broadcasted_iota(jnp.int32, xp.shape, 1) < HEAD_DIM
    zero = jnp.zeros_like(xp)
    return jnp.concatenate([jnp.where(left, xp, zero), jnp.where(left, zero, xp)], axis=0)


def _states_to_pairs(state):
    nb = state.shape[0]
    s = state.reshape(nb, 2, N_HEADS // 2, 2, HEAD_DIM, HEAD_DIM)
    zero = jnp.zeros_like(s[:, :, :, 0])
    return jnp.concatenate([jnp.concatenate([s[:, :, :, 0], zero], axis=-1),
                            jnp.concatenate([zero, s[:, :, :, 1]], axis=-1)], axis=-2)


def _load_states(st, s0_ref):
    st[...] = jnp.zeros(st.shape, st.dtype) if s0_ref is None else s0_ref[0]


def _store_states(sf_ref, st):
    for d in range(st.shape[0]):
        for p in range(st.shape[1]):
            s = st[d, p]
            sf_ref[0, d, 2 * p] = s[:HEAD_DIM, :HEAD_DIM]
            sf_ref[0, d, 2 * p + 1] = s[HEAD_DIM:, HEAD_DIM:]


def _ret_kernel(*refs, zero_init, shared_block):
    if shared_block:
        refs = refs[:5] + refs[2:5] + refs[5:]
    dlm_ref, dlq_ref, qf_ref, kf_ref, vf_ref, qb_ref, kb_ref, vb_ref = refs[:8]
    s0_ref = None if zero_init else refs[8]
    of_ref, ob_ref, sf_ref, st, mask_scr, qdec_scr, kdec_scr, cdec_scr = refs[8 if zero_init else 9:]
    c = pl.program_id(1)
    cs = RET_CHUNK
    pw = 2 * HEAD_DIM
    npair = N_HEADS // 2
    chains = [(d, p) for d in range(2) for p in range(npair)]

    @pl.when(c == 0)
    def _():
        _load_states(st, s0_ref)

    @pl.when((pl.program_id(0) == 0) & (c == 0))
    def _():
        row = lax.broadcasted_iota(jnp.int32, (cs, 2 * cs), 0)
        col = lax.broadcasted_iota(jnp.int32, (cs, 2 * cs), 1) % cs
        tok = lax.broadcasted_iota(jnp.int32, (cs, pw), 0)
        dist = [row - col, col - row]
        tq = [tok.astype(F32), (cs - 1 - tok).astype(F32)]
        for i, (d, p) in enumerate(chains):
            lgm = _log_sigmoid(dlm_ref[d, p])[0:1, :]
            lgq = _log_sigmoid(dlq_ref[d, p])
            mask_scr[i] = jnp.where(dist[d] >= 0, jnp.exp(jnp.maximum(dist[d], 0).astype(F32) * lgm), 0.0)
            qdec_scr[i] = jnp.exp((tq[d] + 1.0) * lgq[0:1, :])
            kdec_scr[i] = jnp.exp((cs - 1.0 - tq[d]) * lgq[0:1, :])
            cdec_scr[i] = jnp.exp(cs * lgq)

    brow = lax.broadcasted_iota(jnp.int32, (pw, pw), 0) // HEAD_DIM
    bcol = lax.broadcasted_iota(jnp.int32, (pw, pw), 1) // HEAD_DIM
    same_head = brow == bcol
    refs = [(qf_ref, kf_ref, vf_ref, of_ref), (qb_ref, kb_ref, vb_ref, ob_ref)]
    nsub = RET_CHUNKS_PER_STEP
    work = [(t, sub) for sub in range(nsub) for t in range(len(chains))]
    sl_of = lambda t: slice(chains[t][1] * pw, (chains[t][1] + 1) * pw)
    rows_of = lambda sub: pl.ds(sub * cs, cs)
    q = {(t, sub): refs[chains[t][0]][0][rows_of(sub), sl_of(t)] for t, sub in work}
    k = {(t, sub): refs[chains[t][0]][1][rows_of(sub), sl_of(t)] * ATTN_SCALE for t, sub in work}
    v = {(t, sub): refs[chains[t][0]][2][rows_of(sub), sl_of(t)] for t, sub in work}
    s = {w: _bdot_nt(q[w], _pair_block_diag(k[w])) * mask_scr[w[0]] for w in work}
    inner = {w: _bdot(s[w], _pair_block_diag(v[w])) for w in work}
    kv = {w: _bdot_tn(k[w] * kdec_scr[w[0]], v[w]) for w in work}
    for step in range(nsub):
        now = [(t, (step if chains[t][0] == 0 else nsub - 1 - step)) for t in range(len(chains))]
        state = {w: st[chains[w[0]]] for w in now}
        cross = {w: _bdot(q[w], state[w]) * qdec_scr[w[0]] for w in now}
        for w in now:
            t, sub = w
            d, p = chains[t]
            refs[d][3][rows_of(sub), sl_of(t)] = inner[w] + cross[w]
            st[d, p] = state[w] * cdec_scr[t][0:1, :] + jnp.where(same_head, kv[w], 0.0)

    @pl.when(c == pl.num_programs(1) - 1)
    def _():
        _store_states(sf_ref, st)


def _retention_scan(decay_logit, q, k, v, state0, seq):
    n = q.shape[0]
    nb = n // seq
    rows = RET_CHUNK * RET_CHUNKS_PER_STEP
    nc = seq // rows
    npair, pw = N_HEADS // 2, 2 * HEAD_DIM
    dl = decay_logit.astype(F32)
    dl_m = jnp.broadcast_to(jnp.repeat(dl, RET_CHUNK, axis=1).reshape(2, npair, 1, 2 * RET_CHUNK),
                            (2, npair, 8, 2 * RET_CHUNK))
    dl_q = jnp.broadcast_to(jnp.repeat(dl, HEAD_DIM, axis=1).reshape(2, npair, 1, pw), (2, npair, 8, pw))
    blk_f = pl.BlockSpec((rows, D_GROUP), lambda b, c: (b * nc + c, 0))
    blk_b = pl.BlockSpec((rows, D_GROUP), lambda b, c: (b * nc + nc - 1 - c, 0))
    st_in = ([] if state0 is None else
             [pl.BlockSpec((1, 2, npair, pw, pw), lambda b, c: (b, 0, 0, 0, 0))])
    st_out = pl.BlockSpec((1, 2, N_HEADS, HEAD_DIM, HEAD_DIM), lambda b, c: (b, 0, 0, 0, 0))
    shared = nc == 1
    bwd_specs, bwd_args = ([], []) if shared else ([blk_b] * 3, [q, k, v])
    return pl.pallas_call(
        functools.partial(_ret_kernel, zero_init=state0 is None, shared_block=shared),
        grid=(nb, nc),
        in_specs=[pl.BlockSpec(dl_m.shape, lambda b, c: (0, 0, 0, 0)),
                  pl.BlockSpec(dl_q.shape, lambda b, c: (0, 0, 0, 0)),
                  blk_f, blk_f, blk_f] + bwd_specs + st_in,
        out_specs=[blk_f, blk_b, st_out],
        out_shape=[jax.ShapeDtypeStruct((n, D_GROUP), F32), jax.ShapeDtypeStruct((n, D_GROUP), F32),
                   jax.ShapeDtypeStruct((nb, 2, N_HEADS, HEAD_DIM, HEAD_DIM), F32)],
        scratch_shapes=[pltpu.VMEM((2, npair, pw, pw), F32),
                        pltpu.VMEM((2 * npair, RET_CHUNK, 2 * RET_CHUNK), F32),
                        pltpu.VMEM((2 * npair, RET_CHUNK, pw), F32),
                        pltpu.VMEM((2 * npair, RET_CHUNK, pw), F32),
                        pltpu.VMEM((2 * npair, 8, pw), F32)],
        compiler_params=_cparams("arbitrary", "arbitrary"),
        name="retention_scan",
    )(dl_m, dl_q, q, k, v, *bwd_args, *([] if state0 is None else [_states_to_pairs(state0)]))


def _rwkv_kernel(*refs, zero_init, shared_block):
    if shared_block:
        refs = refs[:4] + refs[:4] + refs[4:]
    (xrf_ref, xkf_ref, xvf_ref, lof_ref, xrb_ref, xkb_ref, xvb_ref, lob_ref,
     w0_ref, wup_ref, a0_ref, aup_ref, kk_ref, ka_ref) = refs[:14]
    s0_ref = None if zero_init else refs[14]
    yf_ref, yb_ref, sf_ref, st = refs[14 if zero_init else 15:]
    c = pl.program_id(1)
    cs = RWKV_CHUNK
    nsub = RWKV_CHUNKS_PER_STEP
    npair = N_HEADS // 2

    @pl.when(c == 0)
    def _():
        _load_states(st, s0_ref)

    row = lax.broadcasted_iota(jnp.int32, (cs, 3 * cs), 0)
    col = lax.broadcasted_iota(jnp.int32, (cs, 3 * cs), 1) % cs
    pw = 2 * HEAD_DIM
    prow = lax.broadcasted_iota(jnp.int32, (cs, pw), 0)
    pcol = lax.broadcasted_iota(jnp.int32, (cs, pw), 1) % HEAD_DIM
    eye = jnp.where(pcol == prow, 1.0, 0.0).astype(F32)
    sub_diag = (prow // RWKV_SUB) == (pcol // RWKV_SUB)
    brow = lax.broadcasted_iota(jnp.int32, (pw, pw), 0) // HEAD_DIM
    bcol = lax.broadcasted_iota(jnp.int32, (pw, pw), 1) // HEAD_DIM
    same_head = brow == bcol

    bd = _pair_block_diag

    def direction(d, sub, xr_ref, xk_ref, xv_ref, lo_ref):
        rows = pl.ds(sub * cs, cs)
        tri3 = jnp.where((row >= col) if d == 0 else (col >= row), 1.0, 0.0).astype(BF16)
        dist = (prow - pcol) if d == 0 else (pcol - prow)
        xk = xk_ref[rows, :]
        lo = lo_ref[rows, :]
        w_log = -math.exp(-0.5) * _sigmoid(w0_ref[d] + _bdot(jnp.tanh(lo[:, 0:D_LORA]), wup_ref[d]))
        a_all = _sigmoid(a0_ref[d] + _bdot(lo[:, D_LORA:2 * D_LORA], aup_ref[d]))
        w1 = w_log.astype(BF16)
        r1 = w_log - w1.astype(F32)
        w2 = r1.astype(BF16)
        w3 = (r1 - w2.astype(F32)).astype(BF16)
        cum = jnp.dot(tri3, jnp.concatenate([w1, w2, w3], axis=0), preferred_element_type=F32)
        cum_end = cum[cs - 1:cs, :] if d == 0 else cum[0:1, :]
        return dict(earlier=dist > 0, upto=dist >= 0, xr=xr_ref[rows, :], xv=xv_ref[rows, :], a_all=a_all,
                    e_incl=jnp.exp(cum), e_excl=jnp.exp(cum - w_log), e_neg=jnp.exp(-cum),
                    e_end=jnp.exp(cum_end - cum), w_end=jnp.exp(cum_end), kk_all=xk * kk_ref[...],
                    keff_all=xk * (1.0 + (a_all - 1.0) * ka_ref[...]))

    in_refs = [(xrf_ref, xkf_ref, xvf_ref, lof_ref), (xrb_ref, xkb_ref, xvb_ref, lob_ref)]
    dirs = {(d, sub): direction(d, sub, *in_refs[d]) for d in range(2) for sub in range(nsub)}
    y_refs = [yf_ref, yb_ref]

    chains = [(d, sub, p) for d in range(2) for sub in range(nsub) for p in range(npair)]
    pairs = range(len(chains))
    sls = [slice(p * pw, (p + 1) * pw) for _, _, p in chains]
    pick = lambda name: [dirs[d, sub][name][:, sls[i]] for i, (d, sub, _) in enumerate(chains)]
    earlier = [dirs[d, sub]["earlier"] for d, sub, _ in chains]
    upto = [dirs[d, sub]["upto"] for d, sub, _ in chains]
    cat = lambda x, y: jnp.concatenate([x, y], axis=0)

    head_sum = _pair_head_sum

    e_incl, e_excl, e_neg, e_end, w_end = (pick(k) for k in ("e_incl", "e_excl", "e_neg", "e_end", "w_end"))
    kk = [x * lax.rsqrt(head_sum(x * x) + 1e-12) for x in pick("kk_all")]
    b = [kk[p] * a for p, a in zip(pairs, pick("a_all"))]
    keff = pick("keff_all")
    v = pick("xv")
    a_t = [-kk[p] * e_excl[p] for p in pairs]
    r_t = [xr * e_incl[p] for p, xr in zip(pairs, pick("xr"))]
    ar = [cat(a_t[p], r_t[p]).astype(BF16) for p in pairs]
    g = [_bdot_nt(ar[p], cat(bd(b[p] * e_neg[p]), bd(keff[p] * e_neg[p]))) for p in pairs]
    n_mat = [jnp.where(earlier[p], g[p][:cs, :pw], 0.0) for p in pairs]
    m_rb = [jnp.where(upto[p], g[p][cs:, :pw], 0.0) for p in pairs]
    a_ak = [jnp.where(earlier[p], g[p][:cs, pw:], 0.0) for p in pairs]
    m_rk = [jnp.where(upto[p], g[p][cs:, pw:], 0.0) for p in pairs]
    bd_s = lambda xs: (bd(xs[0]), bd(xs[1]))
    n_d = [jnp.where(sub_diag, n, 0.0) for n in n_mat]
    n_o = [jnp.where(sub_diag, 0.0, n) for n in n_mat]
    n_s = [_split2(n) for n in n_d]
    x = [eye + n for n in n_d]
    pk = [_mm3s(ns, bd_s(ns)) for ns in n_s]
    for _ in range(RWKV_SUB_DOUBLINGS - 1):
        xp = [_mm3s(_split2(cat(x[p], pk[p])), bd_s(_split2(pk[p]))) for p in pairs]
        x = [x[p] + xp[p][:cs] for p in pairs]
        pk = [r[cs:] for r in xp]
    t_d = [x[p] + _mm3s(_split2(x[p]), bd_s(_split2(pk[p]))) for p in pairs]
    lcat = lambda x, y: jnp.concatenate([x, y], axis=1)
    m_o = [_bdot(t_d[p], bd(n_o[p])) for p in pairs]
    mz = [_bdot(m_o[p], lcat(bd(t_d[p]), bd(m_o[p]))) for p in pairs]
    z = [t_d[p] + mz[p][:, :pw] for p in pairs]
    t_inv = [(z[p] + _bdot(mz[p][:, pw:], bd(z[p]))).astype(BF16) for p in pairs]
    av = [_bdot(cat(a_ak[p], m_rk[p]), bd(v[p])) for p in pairs]
    tau = [_bdot(t_inv[p], lcat(bd(a_t[p]), bd(av[p][:cs]))) for p in pairs]
    bk_end = [cat(b[p] * e_end[p], keff[p] * e_end[p]) for p in pairs]
    for step in range(nsub):
        now = [i for i, (d, sub, _) in enumerate(chains) if sub == (step if d == 0 else nsub - 1 - step)]
        s_prev = {i: st[chains[i][0], chains[i][2]] for i in now}
        hs = {i: _bdot_nt(cat(tau[i][:, :pw], r_t[i]), s_prev[i]) for i in now}
        u = {i: hs[i][:cs] + tau[i][:, pw:] for i in now}
        y = {i: hs[i][cs:] + _bdot(m_rb[i], bd(u[i])) + av[i][cs:] for i in now}
        for i in now:
            d, sub, p = chains[i]
            y_refs[d][pl.ds(sub * cs, cs), sls[i]] = y[i]
            st[d, p] = s_prev[i] * w_end[i] + jnp.where(same_head, _bdot_tn(cat(u[i], v[i]), bk_end[i]), 0.0)

    @pl.when(c == pl.num_programs(1) - 1)
    def _():
        _store_states(sf_ref, st)


def _rwkv_scan(xr, xk, xv, lora, w0, w_up, a0, a_up, k_k, k_a, state0, seq):
    n = xr.shape[0]
    nb = n // seq
    blk = RWKV_CHUNK * RWKV_CHUNKS_PER_STEP
    nc = seq // blk

    fwd_idx = lambda b, c: (b * nc + c, 0)
    bwd_idx = lambda b, c: (b * nc + nc - 1 - c, 0)
    blk_f = pl.BlockSpec((blk, D_GROUP), fwd_idx)
    blk_b = pl.BlockSpec((blk, D_GROUP), bwd_idx)
    lblk_f = pl.BlockSpec((blk, lora.shape[1]), fwd_idx)
    lblk_b = pl.BlockSpec((blk, lora.shape[1]), bwd_idx)
    dvec = pl.BlockSpec((2, 1, D_GROUP), lambda b, c: (0, 0, 0))
    dmat = pl.BlockSpec((2, D_LORA, D_GROUP), lambda b, c: (0, 0, 0))
    vec = pl.BlockSpec((1, D_GROUP), lambda b, c: (0, 0))
    npair, pw = N_HEADS // 2, 2 * HEAD_DIM
    st_in = ([] if state0 is None else
             [pl.BlockSpec((1, 2, npair, pw, pw), lambda b, c: (b, 0, 0, 0, 0))])
    st_out = pl.BlockSpec((1, 2, N_HEADS, HEAD_DIM, HEAD_DIM), lambda b, c: (b, 0, 0, 0, 0))
    shared = nc == 1
    bwd_specs, bwd_args = ([], []) if shared else ([blk_b, blk_b, blk_b, lblk_b], [xr, xk, xv, lora])
    return pl.pallas_call(
        functools.partial(_rwkv_kernel, zero_init=state0 is None, shared_block=shared),
        grid=(nb, nc),
        in_specs=[blk_f, blk_f, blk_f, lblk_f] + bwd_specs + [dvec, dmat, dvec, dmat, vec, vec] + st_in,
        out_specs=[blk_f, blk_b, st_out],
        out_shape=[jax.ShapeDtypeStruct((n, D_GROUP), F32), jax.ShapeDtypeStruct((n, D_GROUP), F32),
                   jax.ShapeDtypeStruct((nb, 2, N_HEADS, HEAD_DIM, HEAD_DIM), F32)],
        scratch_shapes=[pltpu.VMEM((2, npair, pw, pw), F32)],
        compiler_params=_cparams("arbitrary", "arbitrary"),
        name="rwkv7_scan",
    )(xr, xk, xv, lora, *bwd_args, w0.reshape(2, 1, D_GROUP), w_up, a0.reshape(2, 1, D_GROUP), a_up,
      k_k.reshape(1, D_GROUP), k_a.reshape(1, D_GROUP), *([] if state0 is None else [_states_to_pairs(state0)]))


ATTN_WIDTHS = (D_GROUP,) * 6
REC_WIDTHS = (D_GROUP,) * 7 + (2 * D_LORA + D_LORA_G,)
RET_BF16_GROUPS = (0, 2)


def kernel(x_prompt, x_sample, cache_na_k, cache_na_v, cache_diff_k, cache_diff_v, state_ret, state_rwkv, c, c_ctx, norm_mix_g, norm_ffn_g, norm_final_g, w_ada, b_ada, w_in_attn, w_out_attn, na_rpb, diff_lq1, diff_lk1, diff_lq2, diff_lk2, w_in_rec, w_out_rec, ret_decay_logit, rw_w0, rw_w_up, rw_a0, rw_a_up, rw_g_up, rw_k_k, rw_k_a, rw_r_k, rw_ln_g, rw_ln_b, w_ffn_in, w_ffn_out):
    bp, seq, d = x_prompt.shape
    bl, n_lat, _ = x_sample.shape
    depth = w_ada.shape[0]
    assert n_lat == GRID_ROWS * GRID_W and 1 + bl <= 8
    assert seq % (RWKV_CHUNK * RWKV_CHUNKS_PER_STEP) == 0 and seq % (RET_CHUNK * RET_CHUNKS_PER_STEP) == 0
    assert n_lat % (RWKV_CHUNK * RWKV_CHUNKS_PER_STEP) == 0 and n_lat % (RET_CHUNK * RET_CHUNKS_PER_STEP) == 0
    assert (bp * seq) % TAIL_TM == 0 and n_lat % TAIL_TM == 0 and n_lat % LAT_DIFF_TQ == 0
    assert w_in_attn.shape[2] == sum(ATTN_WIDTHS) and w_in_rec.shape[2] == sum(REC_WIDTHS)
    assert w_ffn_in.shape[2] == 2 * D_FF and w_ada.shape[2] % ADA_TN == 0
    ctx = x_prompt.reshape(bp * seq, d)
    lat = x_sample.reshape(bl * n_lat, d)

    cond8 = jnp.zeros((8, d), F32).at[0].set(c_ctx).at[1:1 + bl].set(c)
    mods = _ada_modulation(cond8, w_ada, b_ada).reshape(depth, 8, 6, d)
    w_ffn_in_bf16 = w_ffn_in.astype(BF16)
    w_ffn_out_bf16 = w_ffn_out.astype(BF16)

    outs = {}
    for layer in range(depth):
        m_ctx = [mods[layer, 0:1, j].reshape(1, 1, d) for j in range(6)]
        m_lat = [mods[layer, 1:1 + bl, j].reshape(bl, 1, d) for j in range(6)]
        if layer % 2 == 0:
            i = layer // 2
            lam_init = 0.8 - 0.6 * math.exp(-0.3 * layer)
            lam_params = jnp.stack([diff_lq1[i], diff_lk1[i], diff_lq2[i], diff_lk2[i]]).astype(F32)
            w_in = w_in_attn[i].astype(BF16)
            w_out = w_out_attn[i].astype(BF16)
            res = _ctx_attention(ctx, norm_mix_g[layer], m_ctx[0], m_ctx[1], w_in, lam_params, seq, lam_init)
            mix_ctx = res[:2]
            outs["na_k"], outs["na_v"], outs["df_k"], outs["df_v"] = res[2:]
            plat = _norm_mod_matmul(lat, norm_mix_g[layer], m_lat[0], m_lat[1], w_in, ATTN_WIDTHS, n_lat, PROJ_TM)
            past = cache_na_k.shape[2]
            tt = _na_bias_table(na_rpb[i])
            o_na = _lat_na_attention(plat[0], plat[1], plat[2],
                                     cache_na_k[:, i].reshape(bl, past, D_GROUP),
                                     cache_na_v[:, i].reshape(bl, past, D_GROUP), tt, n_lat)
            o_df = _lat_diff_attention(lam_params, plat[3], plat[4], plat[5],
                                       cache_diff_k[:, i].reshape(bl, past, D_GROUP),
                                       cache_diff_v[:, i].reshape(bl, past, D_GROUP), n_lat, lam_init)
            mix_lat = (o_na, o_df)
        else:
            j = layer // 2
            w_in = w_in_rec[j].astype(BF16)
            w_out = w_out_rec[j].astype(BF16)
            pc = _norm_mod_matmul(ctx, norm_mix_g[layer], m_ctx[0], m_ctx[1], w_in, REC_WIDTHS, bp * seq, PROJ_TM,
                                  RET_BF16_GROUPS)
            plat = _norm_mod_matmul(lat, norm_mix_g[layer], m_lat[0], m_lat[1], w_in, REC_WIDTHS, n_lat, PROJ_TM,
                                    RET_BF16_GROUPS)
            mixes = []
            for is_ctx, p, s_ret0, s_rw0, sq in ((True, pc, None, None, seq),
                                                 (False, plat, state_ret[:, j], state_rwkv[:, j], n_lat)):
                rq, rk, rv, rg, wr, wk, wv, lora = p
                o_f, o_b, s_ret = _retention_scan(ret_decay_logit[j], rq, rk, rv, s_ret0, sq)
                y_f, y_b, s_rw = _rwkv_scan(wr, wk, wv, lora, rw_w0[j], rw_w_up[j].astype(BF16), rw_a0[j],
                                            rw_a_up[j].astype(BF16), rw_k_k[j], rw_k_a[j], s_rw0, sq)
                mixes.append((o_f, o_b, rg, y_f, y_b, wr, wk, wv, lora, rw_r_k[j].reshape(1, D_GROUP),
                              rw_ln_g[j].reshape(1, D_GROUP), rw_ln_b[j].reshape(1, D_GROUP),
                              rw_g_up[j].astype(BF16)))
                if is_ctx:
                    outs["ret"], outs["rwkv"] = s_ret, s_rw
            mix_ctx, mix_lat = mixes
        final = layer == depth - 1
        ctx = _layer_tail(ctx, mix_ctx, w_out, m_ctx[2], norm_ffn_g[layer], m_ctx[3], m_ctx[4], m_ctx[5],
                          w_ffn_in_bf16, w_ffn_out_bf16, layer, norm_final_g, bp * seq, final)
        lat = _layer_tail(lat, mix_lat, w_out, m_lat[2], norm_ffn_g[layer], m_lat[3], m_lat[4], m_lat[5],
                          w_ffn_in_bf16, w_ffn_out_bf16, layer, norm_final_g, n_lat, final)

    y_prompt = ctx.reshape(bp, seq, d)
    y_sample = lat.reshape(bl, n_lat, d)
    new_cache_na_k = outs["na_k"].reshape(bp, 1, seq, N_HEADS, HEAD_DIM)
    new_cache_na_v = outs["na_v"].reshape(bp, 1, seq, N_HEADS, HEAD_DIM)
    new_cache_diff_k = outs["df_k"].reshape(bp, 1, seq, N_HEADS // 2, 2, HEAD_DIM)
    new_cache_diff_v = outs["df_v"].reshape(bp, 1, seq, N_HEADS // 2, 2 * HEAD_DIM)
    new_state_ret = outs["ret"].reshape(bp, 1, 2, N_HEADS, HEAD_DIM, HEAD_DIM)
    new_state_rwkv = outs["rwkv"].reshape(bp, 1, 2, N_HEADS, HEAD_DIM, HEAD_DIM)
    return (y_prompt, y_sample, new_cache_na_k, new_cache_na_v, new_cache_diff_k, new_cache_diff_v,
            new_state_ret, new_state_rwkv)
```
